```python
import jax, jax.numpy as jnp
from jax import lax
import numpy as np

D_MODEL = 2048
BATCH = 4
SEQ = 2048
DEPTH = 2

D_MIX = D_MODEL
D_BRANCH = D_MIX // 4
RWKV_HEAD = 64
RWKV_HEADS = D_BRANCH // RWKV_HEAD
RWKV_LORA_W = 64
RWKV_LORA_A = 64
RWKV_GN_EPS = 64e-5
ATT_HEAD = 64
ATT_Q_HEADS = D_BRANCH // ATT_HEAD
ATT_KV_HEADS = 2
ATT_GROUP = ATT_Q_HEADS // ATT_KV_HEADS
WINDOW = 128
ATT_BLOCK = WINDOW
NEG_INF = -1e30
POOL_WINDOWS = (2, 4, 8, 16)
POOL_GROUPS = len(POOL_WINDOWS)
POOL_CH = D_BRANCH // POOL_GROUPS
SGU_CHUNK = 128
SGU_GROUPS = 4
SGU_CH = D_BRANCH // SGU_GROUPS
LN_EPS = 1e-5
NORM_EPS = 1e-6

A_COLS = 3 * D_BRANCH + RWKV_LORA_W + RWKV_LORA_A
B_COLS = D_BRANCH + 2 * ATT_KV_HEADS * ATT_HEAD
C_COLS = D_BRANCH
D_COLS = 2 * D_BRANCH
G_COLS = D_MIX
D_IN = A_COLS + B_COLS + C_COLS + D_COLS + G_COLS
SPLITS = (A_COLS, A_COLS + B_COLS, A_COLS + B_COLS + C_COLS, A_COLS + B_COLS + C_COLS + D_COLS)

kernel_name = 'hybrid_parallel_rwkv7_swa_pool_sgu'


def rms_norm(x, w):
    xf = x.astype(jnp.float32)
    y = xf * lax.rsqrt(jnp.mean(xf * xf, axis=-1, keepdims=True) + NORM_EPS)
    return (y * w.astype(jnp.float32)).astype(x.dtype)


def token_shift(z, mu):
    z_prev = jnp.pad(z, ((0, 0), (1, 0), (0, 0)))[:, :-1]
    return z + mu * (z_prev - z)


def rwkv7_mixer(z, w0, w_up, a0, a_up, k_k, k_a, r_k, ln_w, ln_b):
    B_, T, _ = z.shape
    H, N = RWKV_HEADS, RWKV_HEAD
    zf = z.astype(jnp.float32)
    r, k, v, wd, ad = jnp.split(zf, [D_BRANCH, 2 * D_BRANCH, 3 * D_BRANCH, 3 * D_BRANCH + RWKV_LORA_W], axis=-1)
    w_log = -jax.nn.softplus(-(w0 + jnp.tanh(wd) @ w_up)) - 0.5
    decay = jnp.exp(-jnp.exp(w_log))
    a = jax.nn.sigmoid(a0 + ad @ a_up)
    heads = lambda t: t.reshape(B_, T, H, N)
    kk = heads(k * k_k)
    kk = kk / jnp.maximum(jnp.sqrt(jnp.sum(kk * kk, axis=-1, keepdims=True)), 1e-12)
    k = k * (1.0 + (a - 1.0) * k_a)
    r, k, v, decay, a = heads(r), heads(k), heads(v), heads(decay), heads(a)
    a_vec = -kk
    b_vec = kk * a

    def step(S, inp):
        r_t, w_t, k_t, v_t, a_t, b_t = inp
        sa = jnp.einsum('bhij,bhj->bhi', S, a_t)
        S = S * w_t[:, :, None, :] + sa[..., None] * b_t[:, :, None, :] + v_t[..., None] * k_t[:, :, None, :]
        y = jnp.einsum('bhij,bhj->bhi', S, r_t)
        return S, y

    xs = tuple(jnp.moveaxis(t, 1, 0) for t in (r, decay, k, v, a_vec, b_vec))
    S0 = jnp.zeros((B_, H, N, N), jnp.float32)
    _, y = lax.scan(step, S0, xs)
    y = jnp.moveaxis(y, 0, 1)
    mu = jnp.mean(y, axis=-1, keepdims=True)
    var = jnp.mean(jnp.square(y - mu), axis=-1, keepdims=True)
    y = ((y - mu) * lax.rsqrt(var + RWKV_GN_EPS)).reshape(B_, T, D_BRANCH) * ln_w + ln_b
    bonus = jnp.sum(r * k * r_k, axis=-1, keepdims=True) * v
    y = y + bonus.reshape(B_, T, D_BRANCH)
    return y.astype(z.dtype)


def alibi_slopes(n_heads):
    return jnp.exp2(-8.0 * (jnp.arange(n_heads) + 1) / n_heads).astype(jnp.float32)


def swa_sink_attention(z, sinks):
    B_, T, _ = z.shape
    NB = T // ATT_BLOCK
    KV, G, HD, BLK = ATT_KV_HEADS, ATT_GROUP, ATT_HEAD, ATT_BLOCK
    zf = z.astype(jnp.float32)
    q, k, v = jnp.split(zf, [D_BRANCH, D_BRANCH + KV * HD], axis=-1)
    q = q.reshape(B_, NB, BLK, KV, G, HD)
    k = k.reshape(B_, NB, BLK, KV, HD)
    v = v.reshape(B_, NB, BLK, KV, HD)

    def with_prev(t):
        prev = jnp.pad(t, ((0, 0), (1, 0), (0, 0), (0, 0), (0, 0)))[:, :-1]
        return jnp.concatenate([prev, t], axis=2)

    kw, vw = with_prev(k), with_prev(v)
    s = jnp.einsum('bnqkgd,bnskd->bnkgqs', q, kw) * (HD ** -0.5)
    qi = jnp.arange(BLK)[:, None]
    sj = jnp.arange(2 * BLK)[None, :]
    dist = qi + BLK - sj
    kpos = jnp.arange(NB)[:, None, None] * BLK + sj[None] - BLK
    valid = (dist >= 0)[None] & (dist < WINDOW)[None] & (kpos >= 0)
    slopes = alibi_slopes(ATT_Q_HEADS).reshape(KV, G)
    s = s - slopes[:, :, None, None] * dist.astype(jnp.float32)
    s = jnp.where(valid[None, :, None, None], s, NEG_INF)
    sink = jnp.broadcast_to(sinks.astype(jnp.float32).reshape(KV, G)[None, None, :, :, None, None], s.shape[:-1] + (1,))
    p = jax.nn.softmax(jnp.concatenate([s, sink], axis=-1), axis=-1)[..., :-1]
    o = jnp.einsum('bnkgqs,bnskd->bnqkgd', p, vw)
    return o.reshape(B_, T, D_BRANCH).astype(z.dtype)


def pool_mixer(z, pool_w, pool_scale):
    B_, T, _ = z.shape
    zg = z.astype(jnp.float32).reshape(B_, T, POOL_GROUPS, POOL_CH)
    csum = jnp.cumsum(zg, axis=1)
    pos = jnp.arange(T) + 1
    outs = []
    for g, w in enumerate(POOL_WINDOWS):
        c = csum[:, :, g]
        lag = jnp.pad(c, ((0, 0), (w, 0), (0, 0)))[:, :T]
        cnt = jnp.minimum(pos, w).astype(jnp.float32)[None, :, None]
        outs.append((c - lag) / cnt - zg[:, :, g])
    pooled = jnp.stack(outs, axis=2)
    y = jnp.einsum('btgc,gcd->btgd', pooled, pool_w)
    return (y.reshape(B_, T, D_BRANCH) * pool_scale).astype(z.dtype)


def spatial_gating(z, norm_w, sgu_w, sgu_b):
    B_, T, _ = z.shape
    NC = T // SGU_CHUNK
    zf = jax.nn.gelu(z.astype(jnp.float32), approximate=False)
    u, v = jnp.split(zf, 2, axis=-1)
    mu = jnp.mean(v, axis=-1, keepdims=True)
    var = jnp.mean(jnp.square(v - mu), axis=-1, keepdims=True)
    v = (v - mu) * lax.rsqrt(var + LN_EPS) * norm_w
    v = v.reshape(B_, NC, SGU_CHUNK, SGU_GROUPS, SGU_CH)
    causal = jnp.tril(jnp.ones((SGU_CHUNK, SGU_CHUNK), dtype=bool))
    w_s = jnp.where(causal[None], sgu_w, 0)
    s = jnp.einsum('gij,bcjgd->bcigd', w_s, v) + jnp.swapaxes(sgu_b, 0, 1)[:, :, None]
    return (u * s.reshape(B_, T, D_BRANCH)).astype(z.dtype)


def hybrid_layer(x, pre_w, post_w, w_in, mu, w0, w_up, a0, a_up, k_k, k_a, r_k, ln_w, ln_b,
                 sinks, pool_w, pool_scale, sgu_norm_w, sgu_w, sgu_b, w_out):
    h = rms_norm(x, pre_w)
    proj = h @ w_in
    za, zb, zc, zd, gate = jnp.split(proj, SPLITS, axis=-1)
    ya = rwkv7_mixer(token_shift(za, mu), w0, w_up, a0, a_up, k_k, k_a, r_k, ln_w, ln_b)
    yb = swa_sink_attention(zb, sinks)
    yc = pool_mixer(zc, pool_w, pool_scale)
    yd = spatial_gating(zd, sgu_norm_w, sgu_w, sgu_b)
    y = jnp.concatenate([ya, yb, yc, yd], axis=-1) * jax.nn.silu(gate)
    return x + rms_norm(y @ w_out, post_w)


def setup_inputs(seed: int = 0) -> dict:
    key = jax.random.key(seed)
    ks = jax.random.split(key, 24)
    L = DEPTH
    nrm = lambda k, shape, s: jax.random.normal(k, shape, jnp.float32) * s
    return {
        'x': nrm(ks[0], (BATCH, SEQ, D_MODEL), 1.0),
        'pre_norm_w': 1.0 + nrm(ks[1], (L, D_MODEL), 0.02),
        'post_norm_w': 1.0 + nrm(ks[2], (L, D_MODEL), 0.02),
        'w_in': nrm(ks[3], (L, D_MODEL, D_IN), D_MODEL ** -0.5),
        'shift_mu': jax.random.uniform(ks[4], (L, A_COLS), jnp.float32),
        'rwkv_w0': jax.random.uniform(ks[5], (L, D_BRANCH), jnp.float32, -4.0, 1.0),
        'rwkv_w_up': nrm(ks[6], (L, RWKV_LORA_W, D_BRANCH), 0.1),
        'rwkv_a0': nrm(ks[7], (L, D_BRANCH), 0.1),
        'rwkv_a_up': nrm(ks[8], (L, RWKV_LORA_A, D_BRANCH), 0.1),
        'rwkv_k_k': 0.85 + nrm(ks[9], (L, D_BRANCH), 0.02),
        'rwkv_k_a': 1.0 + nrm(ks[10], (L, D_BRANCH), 0.02),
        'rwkv_r_k': nrm(ks[11], (L, RWKV_HEADS, RWKV_HEAD), 0.1),
        'rwkv_ln_w': 1.0 + nrm(ks[12], (L, D_BRANCH), 0.02),
        'rwkv_ln_b': nrm(ks[13], (L, D_BRANCH), 0.02),
        'attn_sinks': nrm(ks[14], (L, ATT_Q_HEADS), 0.5),
        'pool_w': nrm(ks[15], (L, POOL_GROUPS, POOL_CH, POOL_CH), POOL_CH ** -0.5),
        'pool_scale': 1.0 + nrm(ks[16], (L, D_BRANCH), 0.02),
        'sgu_norm_w': 1.0 + nrm(ks[17], (L, D_BRANCH), 0.02),
        'sgu_w': nrm(ks[18], (L, SGU_GROUPS, SGU_CHUNK, SGU_CHUNK), 0.5 * SGU_CHUNK ** -0.5),
        'sgu_b': 1.0 + nrm(ks[19], (L, SGU_GROUPS, SGU_CHUNK), 0.02),
        'w_out': nrm(ks[20], (L, D_MIX, D_MODEL), D_MIX ** -0.5),
    }


def reference(x, pre_norm_w, post_norm_w, w_in, shift_mu, rwkv_w0, rwkv_w_up, rwkv_a0, rwkv_a_up,
              rwkv_k_k, rwkv_k_a, rwkv_r_k, rwkv_ln_w, rwkv_ln_b, attn_sinks, pool_w, pool_scale,
              sgu_norm_w, sgu_w, sgu_b, w_out):
    h = x
    for l in range(DEPTH):
        h = hybrid_layer(h, pre_norm_w[l], post_norm_w[l], w_in[l], shift_mu[l], rwkv_w0[l], rwkv_w_up[l],
                         rwkv_a0[l], rwkv_a_up[l], rwkv_k_k[l], rwkv_k_a[l], rwkv_r_k[l], rwkv_ln_w[l],
                         rwkv_ln_b[l], attn_sinks[l], pool_w[l], pool_scale[l], sgu_norm_w[l], sgu_w[l],
                         sgu_b[l], w_out[l])
    return h
```

```python
import functools

import jax
import jax.numpy as jnp
import numpy as np
from jax import lax
from jax.experimental import pallas as pl
from jax.experimental.pallas import tpu as pltpu

F32 = jnp.float32
BF16 = jnp.bfloat16
HIGHEST = lax.Precision.HIGHEST

D_MODEL = 2048
DEPTH = 2
D_BRANCH = 512
HEAD = 64
N_HEADS = D_BRANCH // HEAD
LORA = 64
KV_HEADS = 2
ATT_GROUP = N_HEADS // KV_HEADS
BLK = 128
POOL_WINDOWS = (2, 4, 8, 16)
POOL_HALO = 16
NEG_INF = -1e30
NORM_EPS = 1e-6
LN_EPS = 1e-5
GN_EPS = 64e-5

A_COLS = 3 * D_BRANCH + 2 * LORA
B_COLS = D_BRANCH + 2 * KV_HEADS * HEAD
OFF_B = A_COLS
OFF_C = OFF_B + B_COLS
OFF_D = OFF_C + D_BRANCH
OFF_G = OFF_D + 2 * D_BRANCH
D_IN = OFF_G + D_MODEL

P_GATE = 0
P_SGU = 2048
P_POOL = 3072
P_Q = 3584
P_R = 4096
P_K = 4608
P_V = 5120
P_KV = 5632
P_LORA = 5888
P_WIDTH = 6144

RW_CHUNK = 64
RW_TILE = 64

_VMEM_LIMIT = 56 * 1024 * 1024


def _perm_columns():
    seg = lambda start, n: np.arange(start, start + n)
    cols = np.concatenate([
        seg(OFF_G, D_MODEL),
        seg(OFF_D, 2 * D_BRANCH),
        seg(OFF_C, D_BRANCH),
        seg(OFF_B, D_BRANCH),
        seg(0, 3 * D_BRANCH),
        seg(OFF_B + D_BRANCH, 2 * KV_HEADS * HEAD),
        seg(3 * D_BRANCH, 2 * LORA),
    ])
    assert cols.shape[0] == D_IN
    return cols


IN_TM = 1024
IN_TN = 512


def _inproj_kernel(x_ref, pw_ref, w_ref, o_ref, h_scr):
    @pl.when(pl.program_id(1) == 0)
    def _():
        for r0 in range(0, IN_TM, 256):
            x = x_ref[r0:r0 + 256, :]
            ms = jnp.mean(x * x, axis=-1, keepdims=True)
            h_scr[r0:r0 + 256, :] = (x * lax.rsqrt(ms + NORM_EPS) * pw_ref[...]).astype(BF16)

    o_ref[...] = jnp.dot(h_scr[...], w_ref[...], preferred_element_type=F32)


def _inproj(x2d, pre_w, w_perm):
    m = x2d.shape[0]
    return pl.pallas_call(
        _inproj_kernel,
        grid=(m // IN_TM, P_WIDTH // IN_TN),
        in_specs=[
            pl.BlockSpec((IN_TM, D_MODEL), lambda i, j: (i, 0)),
            pl.BlockSpec((1, D_MODEL), lambda i, j: (0, 0)),
            pl.BlockSpec((D_MODEL, IN_TN), lambda i, j: (0, j)),
        ],
        out_specs=pl.BlockSpec((IN_TM, IN_TN), lambda i, j: (i, j)),
        out_shape=jax.ShapeDtypeStruct((m, P_WIDTH), F32),
        scratch_shapes=[pltpu.VMEM((IN_TM, D_MODEL), BF16)],
        compiler_params=pltpu.CompilerParams(
            dimension_semantics=("parallel", "arbitrary"), vmem_limit_bytes=_VMEM_LIMIT),
        name="inproj",
    )(x2d, pre_w, w_perm)


def _bdot(a, b):
    return jnp.dot(a.astype(BF16), b.astype(BF16), preferred_element_type=F32)


def _bdot_nt(a, b):
    return lax.dot_general(a.astype(BF16), b.astype(BF16), (((1,), (1,)), ((), ())),
                           preferred_element_type=F32)


def _bdot_tn(a, b):
    return lax.dot_general(a.astype(BF16), b.astype(BF16), (((0,), (0,)), ((), ())),
                           preferred_element_type=F32)


def _hdot(a, b):
    return jnp.dot(a, b, precision=HIGHEST, preferred_element_type=F32)


def _unit_lower_inverse(lmat, eye, masks):
    base_mask, level_masks = masks
    ld = jnp.where(base_mask, lmat, 0.0)
    l2 = _bdot(ld, ld)
    x = eye + ld
    x = x + _bdot(x, l2)
    l4 = _bdot(l2, l2)
    x = x + _bdot(x, l4)
    for lm in level_masks:
        loff = jnp.where(lm, lmat, 0.0)
        x = x + _bdot(_bdot(x, loff), x)
    return x


def _rwkv_kernel(r_ref, k_ref, v_ref, lo_ref, mur_ref, muk_ref, muv_ref, mulo_ref,
                 w0_ref, wup_ref, a0_ref, aup_ref, kkw_ref, kaw_ref, rkw_ref, lnw_ref, lnb_ref,
                 bd_ref, o_ref, s_scr, pr_scr, pk_scr, pv_scr, plo_scr):
    n = pl.program_id(1)
    tile = r_ref.shape[0]
    c = RW_CHUNK

    @pl.when(n == 0)
    def _():
        s_scr[...] = jnp.zeros_like(s_scr)
        pr_scr[...] = jnp.zeros_like(pr_scr)
        pk_scr[...] = jnp.zeros_like(pk_scr)
        pv_scr[...] = jnp.zeros_like(pv_scr)
        plo_scr[...] = jnp.zeros_like(plo_scr)

    row = lax.broadcasted_iota(jnp.int32, (tile, 1), 0)

    def shifted(z_ref, mu_ref, p_scr):
        z = z_ref[...]
        zp = jnp.where(row == 0, p_scr[0:1, :], pltpu.roll(z, 1, axis=0))
        p_scr[0:1, :] = z[tile - 1:tile, :]
        return z + mu_ref[...] * (zp - z)

    r = shifted(r_ref, mur_ref, pr_scr)
    k = shifted(k_ref, muk_ref, pk_scr)
    v = shifted(v_ref, muv_ref, pv_scr)
    lo = shifted(lo_ref, mulo_ref, plo_scr)

    bd = bd_ref[...]
    w_log = -jax.nn.softplus(-(w0_ref[...] + _hdot(jnp.tanh(lo[:, :LORA]), wup_ref[...]))) - 0.5
    logw = -jnp.exp(w_log)
    asig = jax.nn.sigmoid(a0_ref[...] + _hdot(lo[:, LORA:], aup_ref[...]))
    kk = k * kkw_ref[...]
    kk = kk / jnp.maximum(jnp.sqrt(_hdot(kk * kk, bd)), 1e-12)
    k2 = k * (1.0 + (asig - 1.0) * kaw_ref[...])
    bonus = _hdot(r * k2 * rkw_ref[...], bd) * v

    ci = lax.broadcasted_iota(jnp.int32, (c, c), 0)
    cj = lax.broadcasted_iota(jnp.int32, (c, c), 1)
    strict = ci > cj
    incl = ci >= cj
    eye = (ci == cj).astype(F32)
    tril = incl.astype(F32)
    same = lambda b: (ci // b) == (cj // b)
    masks = (same(8), tuple(same(2 * b) & ~same(b) for b in (8, 16, 32)))

    for c0 in range(0, tile, c):
        sl = slice(c0, c0 + c)
        lw = logw[sl]
        cum = _hdot(tril, lw)
        g_incl = jnp.exp(cum)
        g_prev = jnp.exp(cum - lw)
        g_inv = jnp.exp(-cum)
        total = cum[c - 1:c, :]
        g_tail = jnp.exp(total - cum)
        g_all = jnp.exp(total)
        kk_c, as_c, k_c, r_c, v_c = kk[sl], asig[sl], k2[sl], r[sl], v[sl]
        a_t = -kk_c * g_prev
        b_c = kk_c * as_c
        b_t = b_c * g_inv
        k_t = k_c * g_inv
        r_t = r_c * g_incl
        b_e = b_c * g_tail
        k_e = k_c * g_tail
        for h in range(N_HEADS):
            hs = slice(h * HEAD, (h + 1) * HEAD)
            ah, bh, kh, rh, vh = a_t[:, hs], b_t[:, hs], k_t[:, hs], r_t[:, hs], v_c[:, hs]
            l_ab = jnp.where(strict, _bdot_nt(ah, bh), 0.0)
            l_ak = jnp.where(strict, _bdot_nt(ah, kh), 0.0)
            m_rb = jnp.where(incl, _bdot_nt(rh, bh), 0.0)
            m_rk = jnp.where(incl, _bdot_nt(rh, kh), 0.0)
            tinv = _unit_lower_inverse(l_ab, eye, masks)
            s0 = s_scr[h]
            u = _bdot(tinv, _bdot_nt(ah, s0) + _bdot(l_ak, vh))
            y = _bdot_nt(rh, s0) + _bdot(m_rb, u) + _bdot(m_rk, vh)
            s_scr[h] = s0 * g_all[:, hs] + _bdot_tn(u, b_e[:, hs]) + _bdot_tn(vh, k_e[:, hs])
            o_ref[sl, hs] = y

    y = o_ref[...]
    mu = _hdot(y, bd) * (1.0 / HEAD)
    d = y - mu
    var = _hdot(d * d, bd) * (1.0 / HEAD)
    o_ref[...] = d * lax.rsqrt(var + GN_EPS) * lnw_ref[...] + lnb_ref[...] + bonus


def _rwkv(proj, vecs, wup, aup, bd, batch, seq):
    m = proj.shape[0]
    nt = seq // RW_TILE
    row = lambda b, n: b * nt + n
    wide = lambda cb: pl.BlockSpec((RW_TILE, D_BRANCH), lambda b, n: (row(b, n), cb))
    vec = lambda w: pl.BlockSpec((1, w), lambda b, n: (0, 0))
    full = lambda a: pl.BlockSpec(a.shape, lambda b, n: (0, 0))
    in_specs = [
        wide(P_R // D_BRANCH), wide(P_K // D_BRANCH), wide(P_V // D_BRANCH),
        pl.BlockSpec((RW_TILE, 2 * LORA), lambda b, n: (row(b, n), P_LORA // (2 * LORA))),
        vec(D_BRANCH), vec(D_BRANCH), vec(D_BRANCH), vec(2 * LORA),
        vec(D_BRANCH), full(wup), vec(D_BRANCH), full(aup),
        vec(D_BRANCH), vec(D_BRANCH), vec(D_BRANCH), vec(D_BRANCH), vec(D_BRANCH),
        full(bd),
    ]
    return pl.pallas_call(
        _rwkv_kernel,
        grid=(batch, nt),
        in_specs=in_specs,
        out_specs=pl.BlockSpec((RW_TILE, D_BRANCH), lambda b, n: (row(b, n), 0)),
        out_shape=jax.ShapeDtypeStruct((m, D_BRANCH), F32),
        scratch_shapes=[
            pltpu.VMEM((N_HEADS, HEAD, HEAD), F32),
            pltpu.VMEM((8, D_BRANCH), F32), pltpu.VMEM((8, D_BRANCH), F32),
            pltpu.VMEM((8, D_BRANCH), F32), pltpu.VMEM((8, 2 * LORA), F32),
        ],
        compiler_params=pltpu.CompilerParams(
            dimension_semantics=("parallel", "arbitrary"), vmem_limit_bytes=_VMEM_LIMIT),
        name="rwkv",
    )(proj, proj, proj, proj, *vecs[:4], vecs[4], wup, vecs[5], aup, *vecs[6:], bd)


_SLOPES = tuple(2.0 ** (-8.0 * (h + 1) / N_HEADS) for h in range(N_HEADS))
_SQRT_HALF = float(np.sqrt(0.5))


def _mixers_kernel(sinks_ref, q_ref, kvc_ref, kvp_ref, zc_ref, zcp_ref, zd_ref,
                   poolw_ref, pscale_ref, nw_ref, sw_ref, sb_ref, o_ref):
    n = pl.program_id(1)

    q = q_ref[...]
    kv = jnp.concatenate([kvp_ref[...], kvc_ref[...]], axis=0)
    qi = lax.broadcasted_iota(jnp.int32, (BLK, 2 * BLK), 0)
    sj = lax.broadcasted_iota(jnp.int32, (BLK, 2 * BLK), 1)
    dist = qi + BLK - sj
    valid = (dist >= 0) & (dist < BLK) & ((n > 0) | (sj >= BLK))
    distf = dist.astype(F32)
    for g in range(KV_HEADS):
        kg = kv[:, g * HEAD:(g + 1) * HEAD]
        vg = kv[:, (KV_HEADS + g) * HEAD:(KV_HEADS + g + 1) * HEAD]
        for j in range(ATT_GROUP):
            h = g * ATT_GROUP + j
            hs = slice(h * HEAD, (h + 1) * HEAD)
            s = _bdot_nt(q[:, hs], kg) * (HEAD ** -0.5)
            s = jnp.where(valid, s - _SLOPES[h] * distf, NEG_INF)
            sink = sinks_ref[h]
            mx = jnp.maximum(jnp.max(s, axis=-1, keepdims=True), sink)
            p = jnp.exp(s - mx)
            den = jnp.sum(p, axis=-1, keepdims=True) + jnp.exp(sink - mx)
            o_ref[:, hs] = _bdot(p, vg) / den

    zfull = jnp.concatenate([jnp.where(n > 0, zcp_ref[...], 0.0), zc_ref[...]], axis=0)
    pos = n * BLK + lax.broadcasted_iota(jnp.int32, (BLK, 1), 0) + 1
    for g, w in enumerate(POOL_WINDOWS):
        gs = slice(g * BLK, (g + 1) * BLK)
        zg = zfull[:, gs]
        acc = zg
        step = 1
        while step < w:
            acc = acc + pltpu.roll(acc, step, axis=0)
            step *= 2
        cnt = jnp.minimum(pos, w).astype(F32)
        pooled = acc[POOL_HALO:, :] / cnt - zg[POOL_HALO:, :]
        yg = _bdot(pooled, poolw_ref[g]) * pscale_ref[:, gs]
        o_ref[:, D_BRANCH + g * BLK:D_BRANCH + (g + 1) * BLK] = yg

    zd = zd_ref[...]
    gz = 0.5 * zd * (1.0 + lax.erf(zd * _SQRT_HALF))
    u = gz[:, :D_BRANCH]
    vv = gz[:, D_BRANCH:]
    mu = jnp.mean(vv, axis=-1, keepdims=True)
    dv = vv - mu
    var = jnp.mean(dv * dv, axis=-1, keepdims=True)
    vn = dv * lax.rsqrt(var + LN_EPS) * nw_ref[...]
    ri = lax.broadcasted_iota(jnp.int32, (BLK, BLK), 0)
    rj = lax.broadcasted_iota(jnp.int32, (BLK, BLK), 1)
    causal = ri >= rj
    for g in range(4):
        gs = slice(g * BLK, (g + 1) * BLK)
        ws = jnp.where(causal, sw_ref[g], 0.0)
        sg = _bdot(ws, vn[:, gs]) + sb_ref[:, gs]
        o_ref[:, 2 * D_BRANCH + g * BLK:2 * D_BRANCH + (g + 1) * BLK] = u[:, gs] * sg


def _mixers(proj, sinks, pool_w, pool_scale, norm_w, sgu_w, sgu_bias, batch, seq):
    m = proj.shape[0]
    nb = seq // BLK
    row = lambda b, n: b * nb + n
    halo = BLK // POOL_HALO
    in_specs = [
        pl.BlockSpec(memory_space=pltpu.SMEM),
        pl.BlockSpec((BLK, D_BRANCH), lambda b, n: (row(b, n), P_Q // D_BRANCH)),
        pl.BlockSpec((BLK, 256), lambda b, n: (row(b, n), P_KV // 256)),
        pl.BlockSpec((BLK, 256), lambda b, n: (row(b, jnp.maximum(n - 1, 0)), P_KV // 256)),
        pl.BlockSpec((BLK, D_BRANCH), lambda b, n: (row(b, n), P_POOL // D_BRANCH)),
        pl.BlockSpec((POOL_HALO, D_BRANCH),
                     lambda b, n: (jnp.maximum(row(b, n) * halo - 1, 0), P_POOL // D_BRANCH)),
        pl.BlockSpec((BLK, 2 * D_BRANCH), lambda b, n: (row(b, n), P_SGU // (2 * D_BRANCH))),
        pl.BlockSpec((4, BLK, BLK), lambda b, n: (0, 0, 0)),
        pl.BlockSpec((1, D_BRANCH), lambda b, n: (0, 0)),
        pl.BlockSpec((1, D_BRANCH), lambda b, n: (0, 0)),
        pl.BlockSpec((4, BLK, BLK), lambda b, n: (0, 0, 0)),
        pl.BlockSpec((BLK, D_BRANCH), lambda b, n: (0, 0)),
    ]
    return pl.pallas_call(
        _mixers_kernel,
        grid=(batch, nb),
        in_specs=in_specs,
        out_specs=pl.BlockSpec((BLK, 3 * D_BRANCH), lambda b, n: (row(b, n), 0)),
        out_shape=jax.ShapeDtypeStruct((m, 3 * D_BRANCH), F32),
        compiler_params=pltpu.CompilerParams(
            dimension_semantics=("parallel", "parallel"), vmem_limit_bytes=_VMEM_LIMIT),
        name="mixers",
    )(sinks, proj, proj, proj, proj, proj, proj, pool_w, pool_scale, norm_w, sgu_w, sgu_bias)


OUT_TM = 512


def _outproj_kernel(ya_ref, yb_ref, g_ref, x_ref, w_ref, pw_ref, o_ref):
    g = g_ref[...]
    sg = g * jax.nn.sigmoid(g)
    acc = jnp.dot((ya_ref[...] * sg[:, :D_BRANCH]).astype(BF16), w_ref[:D_BRANCH, :],
                  preferred_element_type=F32)
    acc = acc + jnp.dot((yb_ref[...] * sg[:, D_BRANCH:]).astype(BF16), w_ref[D_BRANCH:, :],
                        preferred_element_type=F32)
    ms = jnp.mean(acc * acc, axis=-1, keepdims=True)
    o_ref[...] = x_ref[...] + acc * lax.rsqrt(ms + NORM_EPS) * pw_ref[...]


def _outproj(ya, ybcd, proj, x2d, w_out, post_w):
    m = x2d.shape[0]
    return pl.pallas_call(
        _outproj_kernel,
        grid=(m // OUT_TM,),
        in_specs=[
            pl.BlockSpec((OUT_TM, D_BRANCH), lambda i: (i, 0)),
            pl.BlockSpec((OUT_TM, 3 * D_BRANCH), lambda i: (i, 0)),
            pl.BlockSpec((OUT_TM, D_MODEL), lambda i: (i, P_GATE // D_MODEL)),
            pl.BlockSpec((OUT_TM, D_MODEL), lambda i: (i, 0)),
            pl.BlockSpec((D_MODEL, D_MODEL), lambda i: (0, 0)),
            pl.BlockSpec((1, D_MODEL), lambda i: (0, 0)),
        ],
        out_specs=pl.BlockSpec((OUT_TM, D_MODEL), lambda i: (i, 0)),
        out_shape=jax.ShapeDtypeStruct((m, D_MODEL), F32),
        compiler_params=pltpu.CompilerParams(
            dimension_semantics=("parallel",), vmem_limit_bytes=_VMEM_LIMIT),
        name="outproj",
    )(ya, ybcd, proj, x2d, w_out, post_w)


def kernel(x, pre_norm_w, post_norm_w, w_in, shift_mu, rwkv_w0, rwkv_w_up, rwkv_a0, rwkv_a_up,
           rwkv_k_k, rwkv_k_a, rwkv_r_k, rwkv_ln_w, rwkv_ln_b, attn_sinks, pool_w, pool_scale,
           sgu_norm_w, sgu_w, sgu_b, w_out):
    batch, seq, _ = x.shape
    assert x.shape == (batch, seq, D_MODEL) and seq % BLK == 0 and seq % RW_TILE == 0
    m = batch * seq
    cols = _perm_columns()
    head_id = np.arange(D_BRANCH) // HEAD
    bd = jnp.asarray((head_id[:, None] == head_id[None, :]).astype(np.float32))
    row_vec = lambda a: a.reshape(1, -1)

    h = x.reshape(m, D_MODEL)
    for l in range(DEPTH):
        w_perm = jnp.pad(w_in[l][:, cols].astype(BF16), ((0, 0), (0, P_WIDTH - D_IN)))
        proj = _inproj(h, row_vec(pre_norm_w[l]), w_perm)

        mu = shift_mu[l]
        vecs = [row_vec(mu[0:D_BRANCH]), row_vec(mu[D_BRANCH:2 * D_BRANCH]),
                row_vec(mu[2 * D_BRANCH:3 * D_BRANCH]), row_vec(mu[3 * D_BRANCH:]),
                row_vec(rwkv_w0[l]), row_vec(rwkv_a0[l]), row_vec(rwkv_k_k[l]),
                row_vec(rwkv_k_a[l]), row_vec(rwkv_r_k[l]), row_vec(rwkv_ln_w[l]),
                row_vec(rwkv_ln_b[l])]
        ya = _rwkv(proj, vecs, rwkv_w_up[l], rwkv_a_up[l], bd, batch, seq)

        sgu_bias = jnp.broadcast_to(sgu_b[l].T[:, :, None], (BLK, 4, BLK)).reshape(BLK, D_BRANCH)
        ybcd = _mixers(proj, attn_sinks[l], pool_w[l], row_vec(pool_scale[l]),
                       row_vec(sgu_norm_w[l]), sgu_w[l], sgu_bias, batch, seq)

        h = _outproj(ya, ybcd, proj, h, w_out[l].astype(BF16), row_vec(post_norm_w[l]))
    return h.reshape(batch, seq, D_MODEL)
```

```python
import functools

import jax
import jax.numpy as jnp
import numpy as np
from jax import lax
from jax.experimental import pallas as pl
from jax.experimental.pallas import tpu as pltpu

F32 = jnp.float32
BF16 = jnp.bfloat16
HIGHEST = lax.Precision.HIGHEST

D_MODEL = 2048
DEPTH = 2
D_BRANCH = 512
HEAD = 64
N_HEADS = D_BRANCH // HEAD
LORA = 64
KV_HEADS = 2
ATT_GROUP = N_HEADS // KV_HEADS
BLK = 128
POOL_WINDOWS = (2, 4, 8, 16)
POOL_HALO = 16
NEG_INF = -1e30
NORM_EPS = 1e-6
LN_EPS = 1e-5
GN_EPS = 64e-5

A_COLS = 3 * D_BRANCH + 2 * LORA
B_COLS = D_BRANCH + 2 * KV_HEADS * HEAD
OFF_B = A_COLS
OFF_C = OFF_B + B_COLS
OFF_D = OFF_C + D_BRANCH
OFF_G = OFF_D + 2 * D_BRANCH
D_IN = OFF_G + D_MODEL

P_GATE = 0
P_SGU = 2048
P_POOL = 3072
P_Q = 3584
P_R = 4096
P_K = 4608
P_V = 5120
P_KV = 5632
P_LORA = 5888
P_WIDTH = 6144

RW_CHUNK = 64
RW_TILE = 128

_VMEM_LIMIT = 56 * 1024 * 1024


def _perm_columns():
    seg = lambda start, n: np.arange(start, start + n)
    cols = np.concatenate([
        seg(OFF_G, D_MODEL),
        seg(OFF_D, 2 * D_BRANCH),
        seg(OFF_C, D_BRANCH),
        seg(OFF_B, D_BRANCH),
        seg(0, 3 * D_BRANCH),
        seg(OFF_B + D_BRANCH, 2 * KV_HEADS * HEAD),
        seg(3 * D_BRANCH, 2 * LORA),
    ])
    assert cols.shape[0] == D_IN
    return cols


IN_TM = 1024
IN_TN = 512


def _inproj_kernel(x_ref, pw_ref, w_ref, o_ref, h_scr):
    @pl.when(pl.program_id(1) == 0)
    def _():
        for r0 in range(0, IN_TM, 256):
            x = x_ref[r0:r0 + 256, :]
            ms = jnp.mean(x * x, axis=-1, keepdims=True)
            h_scr[r0:r0 + 256, :] = (x * lax.rsqrt(ms + NORM_EPS) * pw_ref[...]).astype(BF16)

    o_ref[...] = jnp.dot(h_scr[...], w_ref[...], preferred_element_type=F32)


def _inproj(x2d, pre_w, w_perm):
    m = x2d.shape[0]
    return pl.pallas_call(
        _inproj_kernel,
        grid=(m // IN_TM, P_WIDTH // IN_TN),
        in_specs=[
            pl.BlockSpec((IN_TM, D_MODEL), lambda i, j: (i, 0)),
            pl.BlockSpec((1, D_MODEL), lambda i, j: (0, 0)),
            pl.BlockSpec((D_MODEL, IN_TN), lambda i, j: (0, j)),
        ],
        out_specs=pl.BlockSpec((IN_TM, IN_TN), lambda i, j: (i, j)),
        out_shape=jax.ShapeDtypeStruct((m, P_WIDTH), F32),
        scratch_shapes=[pltpu.VMEM((IN_TM, D_MODEL), BF16)],
        compiler_params=pltpu.CompilerParams(
            dimension_semantics=("parallel", "arbitrary"), vmem_limit_bytes=_VMEM_LIMIT),
        name="inproj",
    )(x2d, pre_w, w_perm)


def _bdot(a, b):
    return jnp.dot(a.astype(BF16), b.astype(BF16), preferred_element_type=F32)


def _bdot_nt(a, b):
    return lax.dot_general(a.astype(BF16), b.astype(BF16), (((1,), (1,)), ((), ())),
                           preferred_element_type=F32)


def _bdot_tn(a, b):
    return lax.dot_general(a.astype(BF16), b.astype(BF16), (((0,), (0,)), ((), ())),
                           preferred_element_type=F32)


def _hdot(a, b):
    return jnp.dot(a, b, precision=HIGHEST, preferred_element_type=F32)


def _split3(x):
    x1 = x.astype(BF16)
    r1 = x - x1.astype(F32)
    x2 = r1.astype(BF16)
    x3 = (r1 - x2.astype(F32)).astype(BF16)
    return x1, x2, x3


def _head_sums(xs, bd):
    rows = xs[0].shape[0]
    parts = [p for x in xs for p in _split3(x)]
    out = jnp.dot(jnp.concatenate(parts, axis=0), bd, preferred_element_type=F32)
    res = []
    for i in range(len(xs)):
        o = out[3 * i * rows:(3 * i + 3) * rows]
        res.append(o[:rows] + o[rows:2 * rows] + o[2 * rows:])
    return res


def _unit_lower_inverse_many(lmats, eye, masks):
    base_mask, level_masks = masks
    lds = [jnp.where(base_mask, l, 0.0) for l in lmats]
    l2s = [_bdot(ld, ld) for ld in lds]
    xs = [eye + ld for ld in lds]
    xs = [x + _bdot(x, l2) for x, l2 in zip(xs, l2s)]
    l4s = [_bdot(l2, l2) for l2 in l2s]
    xs = [x + _bdot(x, l4) for x, l4 in zip(xs, l4s)]
    for lm in level_masks:
        ts = [_bdot(x, jnp.where(lm, l, 0.0)) for x, l in zip(xs, lmats)]
        xs = [x + _bdot(t, x) for x, t in zip(xs, ts)]
    return xs


def _rwkv_kernel(r_ref, k_ref, v_ref, lo_ref, mur_ref, muk_ref, muv_ref, mulo_ref,
                 w0_ref, wup_ref, a0_ref, aup_ref, kkw_ref, kaw_ref, rkw_ref, lnw_ref, lnb_ref,
                 bd_ref, o_ref, s_scr, pr_scr, pk_scr, pv_scr, plo_scr):
    n = pl.program_id(1)
    tile = r_ref.shape[0]
    c = RW_CHUNK
    n_chunks = tile // c

    @pl.when(n == 0)
    def _():
        s_scr[...] = jnp.zeros_like(s_scr)
        pr_scr[...] = jnp.zeros_like(pr_scr)
        pk_scr[...] = jnp.zeros_like(pk_scr)
        pv_scr[...] = jnp.zeros_like(pv_scr)
        plo_scr[...] = jnp.zeros_like(plo_scr)

    row = lax.broadcasted_iota(jnp.int32, (tile, 1), 0)

    def shifted(z_ref, mu_ref, p_scr):
        z = z_ref[...]
        zp = jnp.where(row == 0, p_scr[0:1, :], pltpu.roll(z, 1, axis=0))
        p_scr[0:1, :] = z[tile - 1:tile, :]
        return z + mu_ref[...] * (zp - z)

    r = shifted(r_ref, mur_ref, pr_scr)
    k = shifted(k_ref, muk_ref, pk_scr)
    v = shifted(v_ref, muv_ref, pv_scr)
    lo = shifted(lo_ref, mulo_ref, plo_scr)

    bd = bd_ref[...]
    w_log = -jax.nn.softplus(-(w0_ref[...] + _hdot(jnp.tanh(lo[:, :LORA]), wup_ref[...]))) - 0.5
    logw = -jnp.exp(w_log)
    asig = jax.nn.sigmoid(a0_ref[...] + _hdot(lo[:, LORA:], aup_ref[...]))
    kk = k * kkw_ref[...]
    k2 = k * (1.0 + (asig - 1.0) * kaw_ref[...])
    kk_ss, rk_sum = _head_sums([kk * kk, r * k2 * rkw_ref[...]], bd)
    kk = kk / jnp.maximum(jnp.sqrt(kk_ss), 1e-12)
    bonus = rk_sum * v

    ci = lax.broadcasted_iota(jnp.int32, (c, c), 0)
    cj = lax.broadcasted_iota(jnp.int32, (c, c), 1)
    strict = ci > cj
    incl = ci >= cj
    eye = (ci == cj).astype(F32)
    same = lambda b: (ci // b) == (cj // b)
    masks = (same(8), tuple(same(2 * b) & ~same(b) for b in (8, 16, 32)))

    ti = lax.broadcasted_iota(jnp.int32, (tile, tile), 0)
    tj = lax.broadcasted_iota(jnp.int32, (tile, tile), 1)
    tril = ((ti >= tj) & ((ti // c) == (tj // c))).astype(BF16)
    cum = sum(jnp.dot(tril, p, preferred_element_type=F32) for p in _split3(logw))

    pairs = [(ck, h) for ck in range(n_chunks) for h in range(N_HEADS)]
    per_chunk = []
    for ck in range(n_chunks):
        sl = slice(ck * c, (ck + 1) * c)
        lw, cm = logw[sl], cum[sl]
        total = cm[c - 1:c, :]
        g_inv = jnp.exp(-cm)
        g_tail = jnp.exp(total - cm)
        b_c = kk[sl] * asig[sl]
        per_chunk.append(dict(
            a=-kk[sl] * jnp.exp(cm - lw), b=b_c * g_inv, k=k2[sl] * g_inv,
            r=r[sl] * jnp.exp(cm), v=v[sl], b_e=b_c * g_tail, k_e=k2[sl] * g_tail,
            g_all=jnp.exp(total)))

    def head(name, p):
        ck, h = p
        return per_chunk[ck][name][:, h * HEAD:(h + 1) * HEAD]

    ah = [head("a", p) for p in pairs]
    rh = [head("r", p) for p in pairs]
    vh = [head("v", p) for p in pairs]
    l_ab = [jnp.where(strict, _bdot_nt(a, head("b", p)), 0.0) for a, p in zip(ah, pairs)]
    l_ak = [jnp.where(strict, _bdot_nt(a, head("k", p)), 0.0) for a, p in zip(ah, pairs)]
    m_rb = [jnp.where(incl, _bdot_nt(x, head("b", p)), 0.0) for x, p in zip(rh, pairs)]
    m_rk = [jnp.where(incl, _bdot_nt(x, head("k", p)), 0.0) for x, p in zip(rh, pairs)]
    qv = [_bdot(l, x) for l, x in zip(l_ak, vh)]
    yv = [_bdot(m, x) for m, x in zip(m_rk, vh)]
    vk = [_bdot_tn(x, head("k_e", p)) for x, p in zip(vh, pairs)]
    tinv = _unit_lower_inverse_many(l_ab, eye, masks)
    wa = [_bdot(t, a) for t, a in zip(tinv, ah)]
    uv = [_bdot(t, q) for t, q in zip(tinv, qv)]
    uvt = [x.T for x in uv]

    state = [s_scr[h] for h in range(N_HEADS)]
    for ck in range(n_chunks):
        idx = [ck * N_HEADS + h for h in range(N_HEADS)]
        u = [_bdot_nt(wa[i], state[h]) + uv[i] for h, i in enumerate(idx)]
        ut = [_bdot_nt(state[h], wa[i]) + uvt[i] for h, i in enumerate(idx)]
        ys = [_bdot_nt(rh[i], state[h]) + yv[i] for h, i in enumerate(idx)]
        for h, i in enumerate(idx):
            hs = slice(h * HEAD, (h + 1) * HEAD)
            o_ref[ck * c:(ck + 1) * c, hs] = ys[h] + _bdot(m_rb[i], u[h])
        state = [state[h] * head("g_all", pairs[i]) + vk[i] + _bdot(ut[h], head("b_e", pairs[i]))
                 for h, i in enumerate(idx)]
    for h in range(N_HEADS):
        s_scr[h] = state[h]

    y = o_ref[...]
    mu = _head_sums([y], bd)[0] * (1.0 / HEAD)
    d = y - mu
    var = _head_sums([d * d], bd)[0] * (1.0 / HEAD)
    o_ref[...] = d * lax.rsqrt(var + GN_EPS) * lnw_ref[...] + lnb_ref[...] + bonus


def _rwkv(proj, vecs, wup, aup, bd, batch, seq):
    m = proj.shape[0]
    nt = seq // RW_TILE
    row = lambda b, n: b * nt + n
    wide = lambda cb: pl.BlockSpec((RW_TILE, D_BRANCH), lambda b, n: (row(b, n), cb))
    vec = lambda w: pl.BlockSpec((1, w), lambda b, n: (0, 0))
    full = lambda a: pl.BlockSpec(a.shape, lambda b, n: (0, 0))
    in_specs = [
        wide(P_R // D_BRANCH), wide(P_K // D_BRANCH), wide(P_V // D_BRANCH),
        pl.BlockSpec((RW_TILE, 2 * LORA), lambda b, n: (row(b, n), P_LORA // (2 * LORA))),
        vec(D_BRANCH), vec(D_BRANCH), vec(D_BRANCH), vec(2 * LORA),
        vec(D_BRANCH), full(wup), vec(D_BRANCH), full(aup),
        vec(D_BRANCH), vec(D_BRANCH), vec(D_BRANCH), vec(D_BRANCH), vec(D_BRANCH),
        full(bd),
    ]
    return pl.pallas_call(
        _rwkv_kernel,
        grid=(batch, nt),
        in_specs=in_specs,
        out_specs=pl.BlockSpec((RW_TILE, D_BRANCH), lambda b, n: (row(b, n), 0)),
        out_shape=jax.ShapeDtypeStruct((m, D_BRANCH), F32),
        scratch_shapes=[
            pltpu.VMEM((N_HEADS, HEAD, HEAD), F32),
            pltpu.VMEM((8, D_BRANCH), F32), pltpu.VMEM((8, D_BRANCH), F32),
            pltpu.VMEM((8, D_BRANCH), F32), pltpu.VMEM((8, 2 * LORA), F32),
        ],
        compiler_params=pltpu.CompilerParams(
            dimension_semantics=("parallel", "arbitrary"), vmem_limit_bytes=_VMEM_LIMIT),
        name="rwkv",
    )(proj, proj, proj, proj, *vecs[:4], vecs[4], wup, vecs[5], aup, *vecs[6:], bd)


_SLOPES = tuple(2.0 ** (-8.0 * (h + 1) / N_HEADS) for h in range(N_HEADS))
_SQRT_HALF = float(np.sqrt(0.5))


def _mixers_kernel(sinks_ref, q_ref, kvc_ref, kvp_ref, zc_ref, zcp_ref, zd_ref,
                   poolw_ref, pscale_ref, nw_ref, sw_ref, sb_ref, o_ref):
    n = pl.program_id(1)

    q = q_ref[...]
    kv = jnp.concatenate([kvp_ref[...], kvc_ref[...]], axis=0)
    qi = lax.broadcasted_iota(jnp.int32, (BLK, 2 * BLK), 0)
    sj = lax.broadcasted_iota(jnp.int32, (BLK, 2 * BLK), 1)
    dist = qi + BLK - sj
    valid = (dist >= 0) & (dist < BLK) & ((n > 0) | (sj >= BLK))
    distf = dist.astype(F32)
    for g in range(KV_HEADS):
        kg = kv[:, g * HEAD:(g + 1) * HEAD]
        vg = kv[:, (KV_HEADS + g) * HEAD:(KV_HEADS + g + 1) * HEAD]
        for j in range(ATT_GROUP):
            h = g * ATT_GROUP + j
            hs = slice(h * HEAD, (h + 1) * HEAD)
            s = _bdot_nt(q[:, hs], kg) * (HEAD ** -0.5)
            s = jnp.where(valid, s - _SLOPES[h] * distf, NEG_INF)
            sink = sinks_ref[h]
            mx = jnp.maximum(jnp.max(s, axis=-1, keepdims=True), sink)
            p = jnp.exp(s - mx)
            den = jnp.sum(p, axis=-1, keepdims=True) + jnp.exp(sink - mx)
            o_ref[:, hs] = _bdot(p, vg) / den

    zfull = jnp.concatenate([jnp.where(n > 0, zcp_ref[...], 0.0), zc_ref[...]], axis=0)
    pos = n * BLK + lax.broadcasted_iota(jnp.int32, (BLK, 1), 0) + 1
    for g, w in enumerate(POOL_WINDOWS):
        gs = slice(g * BLK, (g + 1) * BLK)
        zg = zfull[:, gs]
        acc = zg
        step = 1
        while step < w:
            acc = acc + pltpu.roll(acc, step, axis=0)
            step *= 2
        cnt = jnp.minimum(pos, w).astype(F32)
        pooled = acc[POOL_HALO:, :] / cnt - zg[POOL_HALO:, :]
        yg = _bdot(pooled, poolw_ref[g]) * pscale_ref[:, gs]
        o_ref[:, D_BRANCH + g * BLK:D_BRANCH + (g + 1) * BLK] = yg

    zd = zd_ref[...]
    gz = 0.5 * zd * (1.0 + lax.erf(zd * _SQRT_HALF))
    u = gz[:, :D_BRANCH]
    vv = gz[:, D_BRANCH:]
    mu = jnp.mean(vv, axis=-1, keepdims=True)
    dv = vv - mu
    var = jnp.mean(dv * dv, axis=-1, keepdims=True)
    vn = dv * lax.rsqrt(var + LN_EPS) * nw_ref[...]
    ri = lax.broadcasted_iota(jnp.int32, (BLK, BLK), 0)
    rj = lax.broadcasted_iota(jnp.int32, (BLK, BLK), 1)
    causal = ri >= rj
    for g in range(4):
        gs = slice(g * BLK, (g + 1) * BLK)
        ws = jnp.where(causal, sw_ref[g], 0.0)
        sg = _bdot(ws, vn[:, gs]) + sb_ref[:, gs]
        o_ref[:, 2 * D_BRANCH + g * BLK:2 * D_BRANCH + (g + 1) * BLK] = u[:, gs] * sg


def _mixers(proj, sinks, pool_w, pool_scale, norm_w, sgu_w, sgu_bias, batch, seq):
    m = proj.shape[0]
    nb = seq // BLK
    row = lambda b, n: b * nb + n
    halo = BLK // POOL_HALO
    in_specs = [
        pl.BlockSpec(memory_space=pltpu.SMEM),
        pl.BlockSpec((BLK, D_BRANCH), lambda b, n: (row(b, n), P_Q // D_BRANCH)),
        pl.BlockSpec((BLK, 256), lambda b, n: (row(b, n), P_KV // 256)),
        pl.BlockSpec((BLK, 256), lambda b, n: (row(b, jnp.maximum(n - 1, 0)), P_KV // 256)),
        pl.BlockSpec((BLK, D_BRANCH), lambda b, n: (row(b, n), P_POOL // D_BRANCH)),
        pl.BlockSpec((POOL_HALO, D_BRANCH),
                     lambda b, n: (jnp.maximum(row(b, n) * halo - 1, 0), P_POOL // D_BRANCH)),
        pl.BlockSpec((BLK, 2 * D_BRANCH), lambda b, n: (row(b, n), P_SGU // (2 * D_BRANCH))),
        pl.BlockSpec((4, BLK, BLK), lambda b, n: (0, 0, 0)),
        pl.BlockSpec((1, D_BRANCH), lambda b, n: (0, 0)),
        pl.BlockSpec((1, D_BRANCH), lambda b, n: (0, 0)),
        pl.BlockSpec((4, BLK, BLK), lambda b, n: (0, 0, 0)),
        pl.BlockSpec((BLK, D_BRANCH), lambda b, n: (0, 0)),
    ]
    return pl.pallas_call(
        _mixers_kernel,
        grid=(batch, nb),
        in_specs=in_specs,
        out_specs=pl.BlockSpec((BLK, 3 * D_BRANCH), lambda b, n: (row(b, n), 0)),
        out_shape=jax.ShapeDtypeStruct((m, 3 * D_BRANCH), F32),
        compiler_params=pltpu.CompilerParams(
            dimension_semantics=("parallel", "parallel"), vmem_limit_bytes=_VMEM_LIMIT),
        name="mixers",
    )(sinks, proj, proj, proj, proj, proj, proj, pool_w, pool_scale, norm_w, sgu_w, sgu_bias)


OUT_TM = 512


def _outproj_kernel(ya_ref, yb_ref, g_ref, x_ref, w_ref, pw_ref, o_ref):
    g = g_ref[...]
    sg = g * jax.nn.sigmoid(g)
    acc = jnp.dot((ya_ref[...] * sg[:, :D_BRANCH]).astype(BF16), w_ref[:D_BRANCH, :],
                  preferred_element_type=F32)
    acc = acc + jnp.dot((yb_ref[...] * sg[:, D_BRANCH:]).astype(BF16), w_ref[D_BRANCH:, :],
                        preferred_element_type=F32)
    ms = jnp.mean(acc * acc, axis=-1, keepdims=True)
    o_ref[...] = x_ref[...] + acc * lax.rsqrt(ms + NORM_EPS) * pw_ref[...]


def _outproj(ya, ybcd, proj, x2d, w_out, post_w):
    m = x2d.shape[0]
    return pl.pallas_call(
        _outproj_kernel,
        grid=(m // OUT_TM,),
        in_specs=[
            pl.BlockSpec((OUT_TM, D_BRANCH), lambda i: (i, 0)),
            pl.BlockSpec((OUT_TM, 3 * D_BRANCH), lambda i: (i, 0)),
            pl.BlockSpec((OUT_TM, D_MODEL), lambda i: (i, P_GATE // D_MODEL)),
            pl.BlockSpec((OUT_TM, D_MODEL), lambda i: (i, 0)),
            pl.BlockSpec((D_MODEL, D_MODEL), lambda i: (0, 0)),
            pl.BlockSpec((1, D_MODEL), lambda i: (0, 0)),
        ],
        out_specs=pl.BlockSpec((OUT_TM, D_MODEL), lambda i: (i, 0)),
        out_shape=jax.ShapeDtypeStruct((m, D_MODEL), F32),
        compiler_params=pltpu.CompilerParams(
            dimension_semantics=("parallel",), vmem_limit_bytes=_VMEM_LIMIT),
        name="outproj",
    )(ya, ybcd, proj, x2d, w_out, post_w)


def kernel(x, pre_norm_w, post_norm_w, w_in, shift_mu, rwkv_w0, rwkv_w_up, rwkv_a0, rwkv_a_up,
           rwkv_k_k, rwkv_k_a, rwkv_r_k, rwkv_ln_w, rwkv_ln_b, attn_sinks, pool_w, pool_scale,
           sgu_norm_w, sgu_w, sgu_b, w_out):
    batch, seq, _ = x.shape
    assert x.shape == (batch, seq, D_MODEL) and seq % BLK == 0 and seq % RW_TILE == 0
    m = batch * seq
    cols = _perm_columns()
    head_id = np.arange(D_BRANCH) // HEAD
    bd = jnp.asarray((head_id[:, None] == head_id[None, :]).astype(np.float32), dtype=BF16)
    row_vec = lambda a: a.reshape(1, -1)

    h = x.reshape(m, D_MODEL)
    for l in range(DEPTH):
        w_perm = jnp.pad(w_in[l][:, cols].astype(BF16), ((0, 0), (0, P_WIDTH - D_IN)))
        proj = _inproj(h, row_vec(pre_norm_w[l]), w_perm)

        mu = shift_mu[l]
        vecs = [row_vec(mu[0:D_BRANCH]), row_vec(mu[D_BRANCH:2 * D_BRANCH]),
                row_vec(mu[2 * D_BRANCH:3 * D_BRANCH]), row_vec(mu[3 * D_BRANCH:]),
                row_vec(rwkv_w0[l]), row_vec(rwkv_a0[l]), row_vec(rwkv_k_k[l]),
                row_vec(rwkv_k_a[l]), row_vec(rwkv_r_k[l]), row_vec(rwkv_ln_w[l]),
                row_vec(rwkv_ln_b[l])]
        ya = _rwkv(proj, vecs, rwkv_w_up[l], rwkv_a_up[l], bd, batch, seq)

        sgu_bias = jnp.broadcast_to(sgu_b[l].T[:, :, None], (BLK, 4, BLK)).reshape(BLK, D_BRANCH)
        ybcd = _mixers(proj, attn_sinks[l], pool_w[l], row_vec(pool_scale[l]),
                       row_vec(sgu_norm_w[l]), sgu_w[l], sgu_bias, batch, seq)

        h = _outproj(ya, ybcd, proj, h, w_out[l].astype(BF16), row_vec(post_norm_w[l]))
    return h.reshape(batch, seq, D_MODEL)
```

```python
import functools

import jax
import jax.numpy as jnp
import numpy as np
from jax import lax
from jax.experimental import pallas as pl
from jax.experimental.pallas import tpu as pltpu

F32 = jnp.float32
BF16 = jnp.bfloat16

D_MODEL = 2048
DEPTH = 2
D_BRANCH = 512
HEAD = 64
N_HEADS = D_BRANCH // HEAD
LORA = 64
KV_HEADS = 2
ATT_GROUP = N_HEADS // KV_HEADS
BLK = 128
POOL_WINDOWS = (2, 4, 8, 16)
POOL_HALO = 16
NEG_INF = -1e30
NORM_EPS = 1e-6
LN_EPS = 1e-5
GN_EPS = 64e-5

A_COLS = 3 * D_BRANCH + 2 * LORA
B_COLS = D_BRANCH + 2 * KV_HEADS * HEAD
OFF_B = A_COLS
OFF_C = OFF_B + B_COLS
OFF_D = OFF_C + D_BRANCH
OFF_G = OFF_D + 2 * D_BRANCH
D_IN = OFF_G + D_MODEL

P_GATE = 0
P_SGU = 2048
P_POOL = 3072
P_Q = 3584
P_R = 4096
P_K = 4608
P_V = 5120
P_KV = 5632
P_LORA = 5888
P_WIDTH = 6144

RW_CHUNK = 64
RW_TILE = 256

_VMEM_LIMIT = 56 * 1024 * 1024


def _permute_weight(w):
    seg = lambda start, n: w[:, start:start + n]
    parts = [
        seg(OFF_G, D_MODEL),
        seg(OFF_D, 2 * D_BRANCH),
        seg(OFF_C, D_BRANCH),
        seg(OFF_B, D_BRANCH),
        seg(0, 3 * D_BRANCH),
        seg(OFF_B + D_BRANCH, 2 * KV_HEADS * HEAD),
        seg(3 * D_BRANCH, 2 * LORA),
        jnp.zeros((w.shape[0], P_WIDTH - D_IN), w.dtype),
    ]
    return jnp.concatenate(parts, axis=1).astype(BF16)


IN_TM = 1024
IN_TN = 512


def _inproj_kernel(x_ref, pw_ref, w_ref, o_ref, h_scr):
    @pl.when(pl.program_id(1) == 0)
    def _():
        for r0 in range(0, IN_TM, 256):
            x = x_ref[r0:r0 + 256, :]
            ms = jnp.mean(x * x, axis=-1, keepdims=True)
            h_scr[r0:r0 + 256, :] = (x * lax.rsqrt(ms + NORM_EPS) * pw_ref[...]).astype(BF16)

    o_ref[...] = jnp.dot(h_scr[...], w_ref[...], preferred_element_type=F32)


def _inproj(x2d, pre_w, w_perm):
    m = x2d.shape[0]
    return pl.pallas_call(
        _inproj_kernel,
        grid=(m // IN_TM, P_WIDTH // IN_TN),
        in_specs=[
            pl.BlockSpec((IN_TM, D_MODEL), lambda i, j: (i, 0)),
            pl.BlockSpec((1, D_MODEL), lambda i, j: (0, 0)),
            pl.BlockSpec((D_MODEL, IN_TN), lambda i, j: (0, j)),
        ],
        out_specs=pl.BlockSpec((IN_TM, IN_TN), lambda i, j: (i, j)),
        out_shape=jax.ShapeDtypeStruct((m, P_WIDTH), F32),
        scratch_shapes=[pltpu.VMEM((IN_TM, D_MODEL), BF16)],
        compiler_params=pltpu.CompilerParams(
            dimension_semantics=("parallel", "arbitrary"), vmem_limit_bytes=_VMEM_LIMIT),
        name="inproj",
    )(x2d, pre_w, w_perm)


def _bdot(a, b):
    return jnp.dot(a.astype(BF16), b.astype(BF16), preferred_element_type=F32)


def _bdot_nt(a, b):
    return lax.dot_general(a.astype(BF16), b.astype(BF16), (((1,), (1,)), ((), ())),
                           preferred_element_type=F32)


def _bdot_tn(a, b):
    return lax.dot_general(a.astype(BF16), b.astype(BF16), (((0,), (0,)), ((), ())),
                           preferred_element_type=F32)


def _split3(x):
    x1 = x.astype(BF16)
    r1 = x - x1.astype(F32)
    x2 = r1.astype(BF16)
    x3 = (r1 - x2.astype(F32)).astype(BF16)
    return x1, x2, x3


def _head_sums(xs, bd):
    rows = xs[0].shape[0]
    parts = [p for x in xs for p in _split3(x)]
    out = jnp.dot(jnp.concatenate(parts, axis=0), bd, preferred_element_type=F32)
    res = []
    for i in range(len(xs)):
        o = out[3 * i * rows:(3 * i + 3) * rows]
        res.append(o[:rows] + o[rows:2 * rows] + o[2 * rows:])
    return res


def _unit_lower_inverse_many(lmats, eye, masks):
    base_mask, level_masks = masks
    lds = [jnp.where(base_mask, l, 0.0) for l in lmats]
    l2s = [_bdot(ld, ld) for ld in lds]
    xs = [eye + ld for ld in lds]
    xs = [x + _bdot(x, l2) for x, l2 in zip(xs, l2s)]
    l4s = [_bdot(l2, l2) for l2 in l2s]
    xs = [x + _bdot(x, l4) for x, l4 in zip(xs, l4s)]
    for lm in level_masks:
        ts = [_bdot(x, jnp.where(lm, l, 0.0)) for x, l in zip(xs, lmats)]
        xs = [x + _bdot(t, x) for x, t in zip(xs, ts)]
    return xs


def _rwkv_kernel(r_ref, k_ref, v_ref, lo_ref, mur_ref, muk_ref, muv_ref, mulo_ref,
                 w0_ref, wup_ref, a0_ref, aup_ref, kkw_ref, kaw_ref, rkw_ref, lnw_ref, lnb_ref,
                 bd_ref, o_ref, s_scr, pr_scr, pk_scr, pv_scr, plo_scr):
    n = pl.program_id(1)
    tile = r_ref.shape[0]
    c = RW_CHUNK
    n_chunks = tile // c

    @pl.when(n == 0)
    def _():
        s_scr[...] = jnp.zeros_like(s_scr)
        pr_scr[...] = jnp.zeros_like(pr_scr)
        pk_scr[...] = jnp.zeros_like(pk_scr)
        pv_scr[...] = jnp.zeros_like(pv_scr)
        plo_scr[...] = jnp.zeros_like(plo_scr)

    row = lax.broadcasted_iota(jnp.int32, (tile, 1), 0)

    def shifted(z_ref, mu_ref, p_scr):
        z = z_ref[...]
        zp = jnp.where(row == 0, p_scr[0:1, :], pltpu.roll(z, 1, axis=0))
        p_scr[0:1, :] = z[tile - 1:tile, :]
        return z + mu_ref[...] * (zp - z)

    r = shifted(r_ref, mur_ref, pr_scr)
    k = shifted(k_ref, muk_ref, pk_scr)
    v = shifted(v_ref, muv_ref, pv_scr)
    lo = shifted(lo_ref, mulo_ref, plo_scr)

    bd = bd_ref[...]
    w_log = -jax.nn.softplus(-(w0_ref[...] + _bdot(jnp.tanh(lo[:, :LORA]), wup_ref[...]))) - 0.5
    logw = -jnp.exp(w_log)
    asig = jax.nn.sigmoid(a0_ref[...] + _bdot(lo[:, LORA:], aup_ref[...]))
    kk = k * kkw_ref[...]
    k2 = k * (1.0 + (asig - 1.0) * kaw_ref[...])
    kk_ss, rk_sum = _head_sums([kk * kk, r * k2 * rkw_ref[...]], bd)
    kk = kk / jnp.maximum(jnp.sqrt(kk_ss), 1e-12)
    bonus = rk_sum * v

    ci = lax.broadcasted_iota(jnp.int32, (c, c), 0)
    cj = lax.broadcasted_iota(jnp.int32, (c, c), 1)
    strict = ci > cj
    incl = ci >= cj
    eye = (ci == cj).astype(F32)
    same = lambda b: (ci // b) == (cj // b)
    masks = (same(8), tuple(same(2 * b) & ~same(b) for b in (8, 16, 32)))

    ti = lax.broadcasted_iota(jnp.int32, (tile, tile), 0)
    tj = lax.broadcasted_iota(jnp.int32, (tile, tile), 1)
    tril = ((ti >= tj) & ((ti // c) == (tj // c))).astype(BF16)
    cum = sum(jnp.dot(tril, p, preferred_element_type=F32) for p in _split3(logw))

    pairs = [(ck, h) for ck in range(n_chunks) for h in range(N_HEADS)]
    per_chunk = []
    for ck in range(n_chunks):
        sl = slice(ck * c, (ck + 1) * c)
        lw, cm = logw[sl], cum[sl]
        total = cm[c - 1:c, :]
        g_inv = jnp.exp(-cm)
        g_tail = jnp.exp(total - cm)
        b_c = kk[sl] * asig[sl]
        per_chunk.append(dict(
            a=-kk[sl] * jnp.exp(cm - lw), b=b_c * g_inv, k=k2[sl] * g_inv,
            r=r[sl] * jnp.exp(cm), v=v[sl], b_e=b_c * g_tail, k_e=k2[sl] * g_tail,
            g_all=jnp.exp(total)))

    def head(name, p):
        ck, h = p
        return per_chunk[ck][name][:, h * HEAD:(h + 1) * HEAD]

    ah = [head("a", p) for p in pairs]
    rh = [head("r", p) for p in pairs]
    vh = [head("v", p) for p in pairs]
    l_ab = [jnp.where(strict, _bdot_nt(a, head("b", p)), 0.0) for a, p in zip(ah, pairs)]
    l_ak = [jnp.where(strict, _bdot_nt(a, head("k", p)), 0.0) for a, p in zip(ah, pairs)]
    m_rb = [jnp.where(incl, _bdot_nt(x, head("b", p)), 0.0) for x, p in zip(rh, pairs)]
    m_rk = [jnp.where(incl, _bdot_nt(x, head("k", p)), 0.0) for x, p in zip(rh, pairs)]
    qv = [_bdot(l, x) for l, x in zip(l_ak, vh)]
    yv = [_bdot(m, x) for m, x in zip(m_rk, vh)]
    vk = [_bdot_tn(x, head("k_e", p)) for x, p in zip(vh, pairs)]
    tinv = _unit_lower_inverse_many(l_ab, eye, masks)
    wa = [_bdot(t, a) for t, a in zip(tinv, ah)]
    uv = [_bdot(t, q) for t, q in zip(tinv, qv)]
    uvt = [x.T for x in uv]

    state = [s_scr[h] for h in range(N_HEADS)]
    for ck in range(n_chunks):
        idx = [ck * N_HEADS + h for h in range(N_HEADS)]
        u = [_bdot_nt(wa[i], state[h]) + uv[i] for h, i in enumerate(idx)]
        ut = [_bdot_nt(state[h], wa[i]) + uvt[i] for h, i in enumerate(idx)]
        ys = [_bdot_nt(rh[i], state[h]) + yv[i] for h, i in enumerate(idx)]
        for h, i in enumerate(idx):
            hs = slice(h * HEAD, (h + 1) * HEAD)
            o_ref[ck * c:(ck + 1) * c, hs] = ys[h] + _bdot(m_rb[i], u[h])
        state = [state[h] * head("g_all", pairs[i]) + vk[i] + _bdot(ut[h], head("b_e", pairs[i]))
                 for h, i in enumerate(idx)]
    for h in range(N_HEADS):
        s_scr[h] = state[h]

    y = o_ref[...]
    mu = _head_sums([y], bd)[0] * (1.0 / HEAD)
    d = y - mu
    var = _head_sums([d * d], bd)[0] * (1.0 / HEAD)
    o_ref[...] = d * lax.rsqrt(var + GN_EPS) * lnw_ref[...] + lnb_ref[...] + bonus


def _rwkv(proj, vecs, wup, aup, bd, batch, seq):
    m = proj.shape[0]
    nt = seq // RW_TILE
    row = lambda b, n: b * nt + n
    wide = lambda cb: pl.BlockSpec((RW_TILE, D_BRANCH), lambda b, n: (row(b, n), cb))
    vec = lambda w: pl.BlockSpec((1, w), lambda b, n: (0, 0))
    full = lambda a: pl.BlockSpec(a.shape, lambda b, n: (0, 0))
    in_specs = [
        wide(P_R // D_BRANCH), wide(P_K // D_BRANCH), wide(P_V // D_BRANCH),
        pl.BlockSpec((RW_TILE, 2 * LORA), lambda b, n: (row(b, n), P_LORA // (2 * LORA))),
        vec(D_BRANCH), vec(D_BRANCH), vec(D_BRANCH), vec(2 * LORA),
        vec(D_BRANCH), full(wup), vec(D_BRANCH), full(aup),
        vec(D_BRANCH), vec(D_BRANCH), vec(D_BRANCH), vec(D_BRANCH), vec(D_BRANCH),
        full(bd),
    ]
    return pl.pallas_call(
        _rwkv_kernel,
        grid=(batch, nt),
        in_specs=in_specs,
        out_specs=pl.BlockSpec((RW_TILE, D_BRANCH), lambda b, n: (row(b, n), 0)),
        out_shape=jax.ShapeDtypeStruct((m, D_BRANCH), F32),
        scratch_shapes=[
            pltpu.VMEM((N_HEADS, HEAD, HEAD), F32),
            pltpu.VMEM((8, D_BRANCH), F32), pltpu.VMEM((8, D_BRANCH), F32),
            pltpu.VMEM((8, D_BRANCH), F32), pltpu.VMEM((8, 2 * LORA), F32),
        ],
        compiler_params=pltpu.CompilerParams(
            dimension_semantics=("parallel", "arbitrary"), vmem_limit_bytes=_VMEM_LIMIT),
        name="rwkv",
    )(proj, proj, proj, proj, *vecs[:4], vecs[4], wup, vecs[5], aup, *vecs[6:], bd)


_SLOPES = tuple(2.0 ** (-8.0 * (h + 1) / N_HEADS) for h in range(N_HEADS))
_SQRT_HALF = float(np.sqrt(0.5))


def _mixers_kernel(sinks_ref, q_ref, kvc_ref, kvp_ref, zc_ref, zcp_ref, zd_ref,
                   poolw_ref, pscale_ref, nw_ref, sw_ref, sb_ref, o_ref):
    n = pl.program_id(1)

    q = q_ref[...]
    kv = jnp.concatenate([kvp_ref[...], kvc_ref[...]], axis=0)
    qi = lax.broadcasted_iota(jnp.int32, (BLK, 2 * BLK), 0)
    sj = lax.broadcasted_iota(jnp.int32, (BLK, 2 * BLK), 1)
    dist = qi + BLK - sj
    valid = (dist >= 0) & (dist < BLK) & ((n > 0) | (sj >= BLK))
    distf = dist.astype(F32)
    for g in range(KV_HEADS):
        kg = kv[:, g * HEAD:(g + 1) * HEAD]
        vg = kv[:, (KV_HEADS + g) * HEAD:(KV_HEADS + g + 1) * HEAD]
        for j in range(ATT_GROUP):
            h = g * ATT_GROUP + j
            hs = slice(h * HEAD, (h + 1) * HEAD)
            s = _bdot_nt(q[:, hs], kg) * (HEAD ** -0.5)
            s = jnp.where(valid, s - _SLOPES[h] * distf, NEG_INF)
            sink = sinks_ref[h]
            mx = jnp.maximum(jnp.max(s, axis=-1, keepdims=True), sink)
            p = jnp.exp(s - mx)
            den = jnp.sum(p, axis=-1, keepdims=True) + jnp.exp(sink - mx)
            o_ref[:, hs] = _bdot(p, vg) / den

    zfull = jnp.concatenate([jnp.where(n > 0, zcp_ref[...], 0.0), zc_ref[...]], axis=0)
    pos = n * BLK + lax.broadcasted_iota(jnp.int32, (BLK, 1), 0) + 1
    for g, w in enumerate(POOL_WINDOWS):
        gs = slice(g * BLK, (g + 1) * BLK)
        zg = zfull[:, gs]
        acc = zg
        step = 1
        while step < w:
            acc = acc + pltpu.roll(acc, step, axis=0)
            step *= 2
        cnt = jnp.minimum(pos, w).astype(F32)
        pooled = acc[POOL_HALO:, :] / cnt - zg[POOL_HALO:, :]
        yg = _bdot(pooled, poolw_ref[g]) * pscale_ref[:, gs]
        o_ref[:, D_BRANCH + g * BLK:D_BRANCH + (g + 1) * BLK] = yg

    zd = zd_ref[...]
    gz = 0.5 * zd * (1.0 + lax.erf(zd * _SQRT_HALF))
    u = gz[:, :D_BRANCH]
    vv = gz[:, D_BRANCH:]
    mu = jnp.mean(vv, axis=-1, keepdims=True)
    dv = vv - mu
    var = jnp.mean(dv * dv, axis=-1, keepdims=True)
    vn = dv * lax.rsqrt(var + LN_EPS) * nw_ref[...]
    ri = lax.broadcasted_iota(jnp.int32, (BLK, BLK), 0)
    rj = lax.broadcasted_iota(jnp.int32, (BLK, BLK), 1)
    causal = ri >= rj
    for g in range(4):
        gs = slice(g * BLK, (g + 1) * BLK)
        ws = jnp.where(causal, sw_ref[g], 0.0)
        sg = _bdot(ws, vn[:, gs]) + sb_ref[:, gs]
        o_ref[:, 2 * D_BRANCH + g * BLK:2 * D_BRANCH + (g + 1) * BLK] = u[:, gs] * sg


def _mixers(proj, sinks, pool_w, pool_scale, norm_w, sgu_w, sgu_bias, batch, seq):
    m = proj.shape[0]
    nb = seq // BLK
    row = lambda b, n: b * nb + n
    halo = BLK // POOL_HALO
    in_specs = [
        pl.BlockSpec(memory_space=pltpu.SMEM),
        pl.BlockSpec((BLK, D_BRANCH), lambda b, n: (row(b, n), P_Q // D_BRANCH)),
        pl.BlockSpec((BLK, 256), lambda b, n: (row(b, n), P_KV // 256)),
        pl.BlockSpec((BLK, 256), lambda b, n: (row(b, jnp.maximum(n - 1, 0)), P_KV // 256)),
        pl.BlockSpec((BLK, D_BRANCH), lambda b, n: (row(b, n), P_POOL // D_BRANCH)),
        pl.BlockSpec((POOL_HALO, D_BRANCH),
                     lambda b, n: (jnp.maximum(row(b, n) * halo - 1, 0), P_POOL // D_BRANCH)),
        pl.BlockSpec((BLK, 2 * D_BRANCH), lambda b, n: (row(b, n), P_SGU // (2 * D_BRANCH))),
        pl.BlockSpec((4, BLK, BLK), lambda b, n: (0, 0, 0)),
        pl.BlockSpec((1, D_BRANCH), lambda b, n: (0, 0)),
        pl.BlockSpec((1, D_BRANCH), lambda b, n: (0, 0)),
        pl.BlockSpec((4, BLK, BLK), lambda b, n: (0, 0, 0)),
        pl.BlockSpec((BLK, D_BRANCH), lambda b, n: (0, 0)),
    ]
    return pl.pallas_call(
        _mixers_kernel,
        grid=(batch, nb),
        in_specs=in_specs,
        out_specs=pl.BlockSpec((BLK, 3 * D_BRANCH), lambda b, n: (row(b, n), 0)),
        out_shape=jax.ShapeDtypeStruct((m, 3 * D_BRANCH), F32),
        compiler_params=pltpu.CompilerParams(
            dimension_semantics=("parallel", "parallel"), vmem_limit_bytes=_VMEM_LIMIT),
        name="mixers",
    )(sinks, proj, proj, proj, proj, proj, proj, pool_w, pool_scale, norm_w, sgu_w, sgu_bias)


OUT_TM = 512


def _outproj_kernel(ya_ref, yb_ref, g_ref, x_ref, w_ref, pw_ref, o_ref):
    g = g_ref[...]
    sg = g * jax.nn.sigmoid(g)
    acc = jnp.dot((ya_ref[...] * sg[:, :D_BRANCH]).astype(BF16), w_ref[:D_BRANCH, :],
                  preferred_element_type=F32)
    acc = acc + jnp.dot((yb_ref[...] * sg[:, D_BRANCH:]).astype(BF16), w_ref[D_BRANCH:, :],
                        preferred_element_type=F32)
    ms = jnp.mean(acc * acc, axis=-1, keepdims=True)
    o_ref[...] = x_ref[...] + acc * lax.rsqrt(ms + NORM_EPS) * pw_ref[...]


def _outproj(ya, ybcd, proj, x2d, w_out, post_w):
    m = x2d.shape[0]
    return pl.pallas_call(
        _outproj_kernel,
        grid=(m // OUT_TM,),
        in_specs=[
            pl.BlockSpec((OUT_TM, D_BRANCH), lambda i: (i, 0)),
            pl.BlockSpec((OUT_TM, 3 * D_BRANCH), lambda i: (i, 0)),
            pl.BlockSpec((OUT_TM, D_MODEL), lambda i: (i, P_GATE // D_MODEL)),
            pl.BlockSpec((OUT_TM, D_MODEL), lambda i: (i, 0)),
            pl.BlockSpec((D_MODEL, D_MODEL), lambda i: (0, 0)),
            pl.BlockSpec((1, D_MODEL), lambda i: (0, 0)),
        ],
        out_specs=pl.BlockSpec((OUT_TM, D_MODEL), lambda i: (i, 0)),
        out_shape=jax.ShapeDtypeStruct((m, D_MODEL), F32),
        compiler_params=pltpu.CompilerParams(
            dimension_semantics=("parallel",), vmem_limit_bytes=_VMEM_LIMIT),
        name="outproj",
    )(ya, ybcd, proj, x2d, w_out, post_w)


def kernel(x, pre_norm_w, post_norm_w, w_in, shift_mu, rwkv_w0, rwkv_w_up, rwkv_a0, rwkv_a_up,
           rwkv_k_k, rwkv_k_a, rwkv_r_k, rwkv_ln_w, rwkv_ln_b, attn_sinks, pool_w, pool_scale,
           sgu_norm_w, sgu_w, sgu_b, w_out):
    batch, seq, _ = x.shape
    assert x.shape == (batch, seq, D_MODEL) and seq % BLK == 0 and seq % RW_TILE == 0
    m = batch * seq
    head_id = np.arange(D_BRANCH) // HEAD
    bd = jnp.asarray((head_id[:, None] == head_id[None, :]).astype(np.float32), dtype=BF16)
    row_vec = lambda a: a.reshape(1, -1)

    h = x.reshape(m, D_MODEL)
    for l in range(DEPTH):
        proj = _inproj(h, row_vec(pre_norm_w[l]), _permute_weight(w_in[l]))

        mu = shift_mu[l]
        vecs = [row_vec(mu[0:D_BRANCH]), row_vec(mu[D_BRANCH:2 * D_BRANCH]),
                row_vec(mu[2 * D_BRANCH:3 * D_BRANCH]), row_vec(mu[3 * D_BRANCH:]),
                row_vec(rwkv_w0[l]), row_vec(rwkv_a0[l]), row_vec(rwkv_k_k[l]),
                row_vec(rwkv_k_a[l]), row_vec(rwkv_r_k[l]), row_vec(rwkv_ln_w[l]),
                row_vec(rwkv_ln_b[l])]
        ya = _rwkv(proj, vecs, rwkv_w_up[l], rwkv_a_up[l], bd, batch, seq)

        sgu_bias = jnp.broadcast_to(sgu_b[l].T[:, :, None], (BLK, 4, BLK)).reshape(BLK, D_BRANCH)
        ybcd = _mixers(proj, attn_sinks[l], pool_w[l], row_vec(pool_scale[l]),
                       row_vec(sgu_norm_w[l]), sgu_w[l], sgu_bias, batch, seq)

        h = _outproj(ya, ybcd, proj, h, w_out[l].astype(BF16), row_vec(post_norm_w[l]))
    return h.reshape(batch, seq, D_MODEL)
```

```python
import functools

import jax
import jax.numpy as jnp
import numpy as np
from jax import lax
from jax.experimental import pallas as pl
from jax.experimental.pallas import tpu as pltpu

F32 = jnp.float32
BF16 = jnp.bfloat16

D_MODEL = 2048
DEPTH = 2
D_BRANCH = 512
HEAD = 64
N_HEADS = D_BRANCH // HEAD
LORA = 64
KV_HEADS = 2
ATT_GROUP = N_HEADS // KV_HEADS
BLK = 128
POOL_WINDOWS = (2, 4, 8, 16)
POOL_HALO = 16
NEG_INF = -1e30
NORM_EPS = 1e-6
LN_EPS = 1e-5
GN_EPS = 64e-5

A_COLS = 3 * D_BRANCH + 2 * LORA
B_COLS = D_BRANCH + 2 * KV_HEADS * HEAD
OFF_B = A_COLS
OFF_C = OFF_B + B_COLS
OFF_D = OFF_C + D_BRANCH
OFF_G = OFF_D + 2 * D_BRANCH
D_IN = OFF_G + D_MODEL

P_GATE = 0
P_SGU = 2048
P_POOL = 3072
P_Q = 3584
P_R = 4096
P_K = 4608
P_V = 5120
P_KV = 5632
P_LORA = 5888
P_WIDTH = 6144

RW_CHUNK = 64
RW_TILE = 256

_VMEM_LIMIT = 56 * 1024 * 1024


def _permute_weight(w):
    seg = lambda start, n: w[:, start:start + n]
    parts = [
        seg(OFF_G, D_MODEL),
        seg(OFF_D, 2 * D_BRANCH),
        seg(OFF_C, D_BRANCH),
        seg(OFF_B, D_BRANCH),
        seg(0, 3 * D_BRANCH),
        seg(OFF_B + D_BRANCH, 2 * KV_HEADS * HEAD),
        seg(3 * D_BRANCH, 2 * LORA),
        jnp.zeros((w.shape[0], P_WIDTH - D_IN), w.dtype),
    ]
    return jnp.concatenate(parts, axis=1).astype(BF16)


IN_TM = 1024
IN_TN = 1024


def _inproj_kernel(x_ref, pw_ref, w_ref, o_ref, h_scr):
    @pl.when(pl.program_id(1) == 0)
    def _():
        for r0 in range(0, IN_TM, 256):
            x = x_ref[r0:r0 + 256, :]
            ms = jnp.mean(x * x, axis=-1, keepdims=True)
            h_scr[r0:r0 + 256, :] = (x * lax.rsqrt(ms + NORM_EPS) * pw_ref[...]).astype(BF16)

    o_ref[...] = jnp.dot(h_scr[...], w_ref[...], preferred_element_type=F32)


def _inproj(x2d, pre_w, w_perm):
    m = x2d.shape[0]
    return pl.pallas_call(
        _inproj_kernel,
        grid=(m // IN_TM, P_WIDTH // IN_TN),
        in_specs=[
            pl.BlockSpec((IN_TM, D_MODEL), lambda i, j: (i, 0)),
            pl.BlockSpec((1, D_MODEL), lambda i, j: (0, 0)),
            pl.BlockSpec((D_MODEL, IN_TN), lambda i, j: (0, j)),
        ],
        out_specs=pl.BlockSpec((IN_TM, IN_TN), lambda i, j: (i, j)),
        out_shape=jax.ShapeDtypeStruct((m, P_WIDTH), F32),
        scratch_shapes=[pltpu.VMEM((IN_TM, D_MODEL), BF16)],
        compiler_params=pltpu.CompilerParams(
            dimension_semantics=("parallel", "arbitrary"), vmem_limit_bytes=_VMEM_LIMIT),
        name="inproj",
    )(x2d, pre_w, w_perm)


def _bdot(a, b):
    return jnp.dot(a.astype(BF16), b.astype(BF16), preferred_element_type=F32)


def _bdot_nt(a, b):
    return lax.dot_general(a.astype(BF16), b.astype(BF16), (((1,), (1,)), ((), ())),
                           preferred_element_type=F32)


def _bdot_tn(a, b):
    return lax.dot_general(a.astype(BF16), b.astype(BF16), (((0,), (0,)), ((), ())),
                           preferred_element_type=F32)


def _split3(x):
    x1 = x.astype(BF16)
    r1 = x - x1.astype(F32)
    x2 = r1.astype(BF16)
    x3 = (r1 - x2.astype(F32)).astype(BF16)
    return x1, x2, x3


def _head_sums(xs, bd):
    rows = xs[0].shape[0]
    parts = [p for x in xs for p in _split3(x)]
    out = jnp.dot(jnp.concatenate(parts, axis=0), bd, preferred_element_type=F32)
    res = []
    for i in range(len(xs)):
        o = out[3 * i * rows:(3 * i + 3) * rows]
        res.append(o[:rows] + o[rows:2 * rows] + o[2 * rows:])
    return res


def _unit_lower_inverse_many(lmats, eye, masks):
    base_mask, level_masks = masks
    lds = [jnp.where(base_mask, l, 0.0) for l in lmats]
    l2s = [_bdot(ld, ld) for ld in lds]
    xs = [eye + ld for ld in lds]
    xs = [x + _bdot(x, l2) for x, l2 in zip(xs, l2s)]
    l4s = [_bdot(l2, l2) for l2 in l2s]
    xs = [x + _bdot(x, l4) for x, l4 in zip(xs, l4s)]
    for lm in level_masks:
        ts = [_bdot(x, jnp.where(lm, l, 0.0)) for x, l in zip(xs, lmats)]
        xs = [x + _bdot(t, x) for x, t in zip(xs, ts)]
    return xs


def _rwkv_kernel(r_ref, k_ref, v_ref, lo_ref, mur_ref, muk_ref, muv_ref, mulo_ref,
                 w0_ref, wup_ref, a0_ref, aup_ref, kkw_ref, kaw_ref, rkw_ref, lnw_ref, lnb_ref,
                 bd_ref, o_ref, s_scr, pr_scr, pk_scr, pv_scr, plo_scr):
    n = pl.program_id(1)
    tile = r_ref.shape[0]
    c = RW_CHUNK
    n_chunks = tile // c

    @pl.when(n == 0)
    def _():
        s_scr[...] = jnp.zeros_like(s_scr)
        pr_scr[...] = jnp.zeros_like(pr_scr)
        pk_scr[...] = jnp.zeros_like(pk_scr)
        pv_scr[...] = jnp.zeros_like(pv_scr)
        plo_scr[...] = jnp.zeros_like(plo_scr)

    row = lax.broadcasted_iota(jnp.int32, (tile, 1), 0)

    def shifted(z_ref, mu_ref, p_scr):
        z = z_ref[...]
        zp = jnp.where(row == 0, p_scr[0:1, :], pltpu.roll(z, 1, axis=0))
        p_scr[0:1, :] = z[tile - 1:tile, :]
        return z + mu_ref[...] * (zp - z)

    r = shifted(r_ref, mur_ref, pr_scr)
    k = shifted(k_ref, muk_ref, pk_scr)
    v = shifted(v_ref, muv_ref, pv_scr)
    lo = shifted(lo_ref, mulo_ref, plo_scr)

    bd = bd_ref[...]
    w_log = -jax.nn.softplus(-(w0_ref[...] + _bdot(jnp.tanh(lo[:, :LORA]), wup_ref[...]))) - 0.5
    logw = -jnp.exp(w_log)
    asig = jax.nn.sigmoid(a0_ref[...] + _bdot(lo[:, LORA:], aup_ref[...]))
    kk = k * kkw_ref[...]
    k2 = k * (1.0 + (asig - 1.0) * kaw_ref[...])
    kk_ss, rk_sum = _head_sums([kk * kk, r * k2 * rkw_ref[...]], bd)
    kk = kk / jnp.maximum(jnp.sqrt(kk_ss), 1e-12)
    bonus = rk_sum * v

    ci = lax.broadcasted_iota(jnp.int32, (c, c), 0)
    cj = lax.broadcasted_iota(jnp.int32, (c, c), 1)
    strict = ci > cj
    incl = ci >= cj
    eye = (ci == cj).astype(F32)
    same = lambda b: (ci // b) == (cj // b)
    masks = (same(8), tuple(same(2 * b) & ~same(b) for b in (8, 16, 32)))

    ti = lax.broadcasted_iota(jnp.int32, (tile, tile), 0)
    tj = lax.broadcasted_iota(jnp.int32, (tile, tile), 1)
    tril = ((ti >= tj) & ((ti // c) == (tj // c))).astype(BF16)
    cum = sum(jnp.dot(tril, p, preferred_element_type=F32) for p in _split3(logw))

    pairs = [(ck, h) for ck in range(n_chunks) for h in range(N_HEADS)]
    per_chunk = []
    for ck in range(n_chunks):
        sl = slice(ck * c, (ck + 1) * c)
        lw, cm = logw[sl], cum[sl]
        total = cm[c - 1:c, :]
        g_inv = jnp.exp(-cm)
        g_tail = jnp.exp(total - cm)
        b_c = kk[sl] * asig[sl]
        per_chunk.append(dict(
            a=-kk[sl] * jnp.exp(cm - lw), b=b_c * g_inv, k=k2[sl] * g_inv,
            r=r[sl] * jnp.exp(cm), v=v[sl], b_e=b_c * g_tail, k_e=k2[sl] * g_tail,
            g_all=jnp.exp(total)))

    def head(name, p):
        ck, h = p
        return per_chunk[ck][name][:, h * HEAD:(h + 1) * HEAD]

    ah = [head("a", p) for p in pairs]
    rh = [head("r", p) for p in pairs]
    vh = [head("v", p) for p in pairs]
    l_ab = [jnp.where(strict, _bdot_nt(a, head("b", p)), 0.0) for a, p in zip(ah, pairs)]
    l_ak = [jnp.where(strict, _bdot_nt(a, head("k", p)), 0.0) for a, p in zip(ah, pairs)]
    m_rb = [jnp.where(incl, _bdot_nt(x, head("b", p)), 0.0) for x, p in zip(rh, pairs)]
    m_rk = [jnp.where(incl, _bdot_nt(x, head("k", p)), 0.0) for x, p in zip(rh, pairs)]
    qv = [_bdot(l, x) for l, x in zip(l_ak, vh)]
    yv = [_bdot(m, x) for m, x in zip(m_rk, vh)]
    vk = [_bdot_tn(x, head("k_e", p)) for x, p in zip(vh, pairs)]
    tinv = _unit_lower_inverse_many(l_ab, eye, masks)
    wa = [_bdot(t, a) for t, a in zip(tinv, ah)]
    uv = [_bdot(t, q) for t, q in zip(tinv, qv)]
    uvt = [x.T for x in uv]

    state = [s_scr[h] for h in range(N_HEADS)]
    for ck in range(n_chunks):
        idx = [ck * N_HEADS + h for h in range(N_HEADS)]
        u = [_bdot_nt(wa[i], state[h]) + uv[i] for h, i in enumerate(idx)]
        ut = [_bdot_nt(state[h], wa[i]) + uvt[i] for h, i in enumerate(idx)]
        ys = [_bdot_nt(rh[i], state[h]) + yv[i] for h, i in enumerate(idx)]
        for h, i in enumerate(idx):
            hs = slice(h * HEAD, (h + 1) * HEAD)
            o_ref[ck * c:(ck + 1) * c, hs] = ys[h] + _bdot(m_rb[i], u[h])
        state = [state[h] * head("g_all", pairs[i]) + vk[i] + _bdot(ut[h], head("b_e", pairs[i]))
                 for h, i in enumerate(idx)]
    for h in range(N_HEADS):
        s_scr[h] = state[h]

    y = o_ref[...]
    mu = _head_sums([y], bd)[0] * (1.0 / HEAD)
    d = y - mu
    var = _head_sums([d * d], bd)[0] * (1.0 / HEAD)
    o_ref[...] = d * lax.rsqrt(var + GN_EPS) * lnw_ref[...] + lnb_ref[...] + bonus


def _rwkv(proj, vecs, wup, aup, bd, batch, seq):
    m = proj.shape[0]
    nt = seq // RW_TILE
    row = lambda b, n: b * nt + n
    wide = lambda cb: pl.BlockSpec((RW_TILE, D_BRANCH), lambda b, n: (row(b, n), cb))
    vec = lambda w: pl.BlockSpec((1, w), lambda b, n: (0, 0))
    full = lambda a: pl.BlockSpec(a.shape, lambda b, n: (0, 0))
    in_specs = [
        wide(P_R // D_BRANCH), wide(P_K // D_BRANCH), wide(P_V // D_BRANCH),
        pl.BlockSpec((RW_TILE, 2 * LORA), lambda b, n: (row(b, n), P_LORA // (2 * LORA))),
        vec(D_BRANCH), vec(D_BRANCH), vec(D_BRANCH), vec(2 * LORA),
        vec(D_BRANCH), full(wup), vec(D_BRANCH), full(aup),
        vec(D_BRANCH), vec(D_BRANCH), vec(D_BRANCH), vec(D_BRANCH), vec(D_BRANCH),
        full(bd),
    ]
    return pl.pallas_call(
        _rwkv_kernel,
        grid=(batch, nt),
        in_specs=in_specs,
        out_specs=pl.BlockSpec((RW_TILE, D_BRANCH), lambda b, n: (row(b, n), 0)),
        out_shape=jax.ShapeDtypeStruct((m, D_BRANCH), F32),
        scratch_shapes=[
            pltpu.VMEM((N_HEADS, HEAD, HEAD), F32),
            pltpu.VMEM((8, D_BRANCH), F32), pltpu.VMEM((8, D_BRANCH), F32),
            pltpu.VMEM((8, D_BRANCH), F32), pltpu.VMEM((8, 2 * LORA), F32),
        ],
        compiler_params=pltpu.CompilerParams(
            dimension_semantics=("parallel", "arbitrary"), vmem_limit_bytes=_VMEM_LIMIT),
        name="rwkv",
    )(proj, proj, proj, proj, *vecs[:4], vecs[4], wup, vecs[5], aup, *vecs[6:], bd)


_SLOPES = tuple(2.0 ** (-8.0 * (h + 1) / N_HEADS) for h in range(N_HEADS))
_SQRT_HALF = float(np.sqrt(0.5))


def _attention_bias():
    t = np.arange(BLK)[:, None]
    s = np.arange(2 * BLK)[None, :]
    dist = t + BLK - s
    out = np.empty((2, N_HEADS * BLK, 2 * BLK), np.float32)
    for first in (0, 1):
        valid = (dist >= 0) & (dist < BLK) & ((s >= BLK) | (first == 1))
        for h in range(N_HEADS):
            out[first, h * BLK:(h + 1) * BLK] = np.where(valid, -_SLOPES[h] * dist, NEG_INF)
    return out


def _mixers_kernel(sinks_ref, bias_ref, q_ref, kvc_ref, kvp_ref, zc_ref, zcp_ref, zd_ref,
                   poolw_ref, pscale_ref, nw_ref, sw_ref, sb_ref, o_ref):
    n = pl.program_id(1)

    q = q_ref[...] * (HEAD ** -0.5)
    kv = jnp.concatenate([kvp_ref[...], kvc_ref[...]], axis=0)
    rows = ATT_GROUP * BLK
    head_in_group = lax.broadcasted_iota(jnp.int32, (rows, 1), 0) // BLK

    def per_row(values):
        col = jnp.full((rows, 1), values[-1], F32)
        for j in range(ATT_GROUP - 2, -1, -1):
            col = jnp.where(head_in_group == j, values[j], col)
        return col

    scores = []
    for g in range(KV_HEADS):
        qg = jnp.concatenate(
            [q[:, (g * ATT_GROUP + j) * HEAD:(g * ATT_GROUP + j + 1) * HEAD]
             for j in range(ATT_GROUP)], axis=0)
        scores.append(_bdot_nt(qg, kv[:, g * HEAD:(g + 1) * HEAD]))
    probs, dens = [], []
    for g in range(KV_HEADS):
        heads = range(g * ATT_GROUP, (g + 1) * ATT_GROUP)
        sink = per_row([sinks_ref[h] for h in heads])
        s = scores[g] + bias_ref[g * rows:(g + 1) * rows, :]
        mx = jnp.maximum(jnp.max(s, axis=-1, keepdims=True), sink)
        p = jnp.exp(s - mx)
        probs.append(p)
        dens.append(jnp.sum(p, axis=-1, keepdims=True) + jnp.exp(sink - mx))
    for g in range(KV_HEADS):
        vg = kv[:, (KV_HEADS + g) * HEAD:(KV_HEADS + g + 1) * HEAD]
        og = _bdot(probs[g], vg) / dens[g]
        for j in range(ATT_GROUP):
            h = g * ATT_GROUP + j
            o_ref[:, h * HEAD:(h + 1) * HEAD] = og[j * BLK:(j + 1) * BLK, :]

    zfull = jnp.concatenate([jnp.where(n > 0, zcp_ref[...], 0.0), zc_ref[...]], axis=0)
    pos = n * BLK + lax.broadcasted_iota(jnp.int32, (BLK, 1), 0) + 1
    for g, w in enumerate(POOL_WINDOWS):
        gs = slice(g * BLK, (g + 1) * BLK)
        zg = zfull[:, gs]
        acc = zg
        step = 1
        while step < w:
            acc = acc + pltpu.roll(acc, step, axis=0)
            step *= 2
        cnt = jnp.minimum(pos, w).astype(F32)
        pooled = acc[POOL_HALO:, :] / cnt - zg[POOL_HALO:, :]
        yg = _bdot(pooled, poolw_ref[g]) * pscale_ref[:, gs]
        o_ref[:, D_BRANCH + g * BLK:D_BRANCH + (g + 1) * BLK] = yg

    zd = zd_ref[...]
    gz = 0.5 * zd * (1.0 + lax.erf(zd * _SQRT_HALF))
    u = gz[:, :D_BRANCH]
    vv = gz[:, D_BRANCH:]
    mu = jnp.mean(vv, axis=-1, keepdims=True)
    dv = vv - mu
    var = jnp.mean(dv * dv, axis=-1, keepdims=True)
    vn = dv * lax.rsqrt(var + LN_EPS) * nw_ref[...]
    ri = lax.broadcasted_iota(jnp.int32, (BLK, BLK), 0)
    rj = lax.broadcasted_iota(jnp.int32, (BLK, BLK), 1)
    causal = ri >= rj
    for g in range(4):
        gs = slice(g * BLK, (g + 1) * BLK)
        ws = jnp.where(causal, sw_ref[g], 0.0)
        sg = _bdot(ws, vn[:, gs]) + sb_ref[:, gs]
        o_ref[:, 2 * D_BRANCH + g * BLK:2 * D_BRANCH + (g + 1) * BLK] = u[:, gs] * sg


def _mixers(proj, sinks, pool_w, pool_scale, norm_w, sgu_w, sgu_bias, batch, seq):
    m = proj.shape[0]
    nb = seq // BLK
    row = lambda b, n: b * nb + n
    halo = BLK // POOL_HALO
    bias = jnp.asarray(_attention_bias())
    in_specs = [
        pl.BlockSpec(memory_space=pltpu.SMEM),
        pl.BlockSpec((None,) + bias.shape[1:], lambda b, n: (jnp.minimum(n, 1), 0, 0)),
        pl.BlockSpec((BLK, D_BRANCH), lambda b, n: (row(b, n), P_Q // D_BRANCH)),
        pl.BlockSpec((BLK, 256), lambda b, n: (row(b, n), P_KV // 256)),
        pl.BlockSpec((BLK, 256), lambda b, n: (row(b, jnp.maximum(n - 1, 0)), P_KV // 256)),
        pl.BlockSpec((BLK, D_BRANCH), lambda b, n: (row(b, n), P_POOL // D_BRANCH)),
        pl.BlockSpec((POOL_HALO, D_BRANCH),
                     lambda b, n: (jnp.maximum(row(b, n) * halo - 1, 0), P_POOL // D_BRANCH)),
        pl.BlockSpec((BLK, 2 * D_BRANCH), lambda b, n: (row(b, n), P_SGU // (2 * D_BRANCH))),
        pl.BlockSpec((4, BLK, BLK), lambda b, n: (0, 0, 0)),
        pl.BlockSpec((1, D_BRANCH), lambda b, n: (0, 0)),
        pl.BlockSpec((1, D_BRANCH), lambda b, n: (0, 0)),
        pl.BlockSpec((4, BLK, BLK), lambda b, n: (0, 0, 0)),
        pl.BlockSpec((BLK, D_BRANCH), lambda b, n: (0, 0)),
    ]
    return pl.pallas_call(
        _mixers_kernel,
        grid=(batch, nb),
        in_specs=in_specs,
        out_specs=pl.BlockSpec((BLK, 3 * D_BRANCH), lambda b, n: (row(b, n), 0)),
        out_shape=jax.ShapeDtypeStruct((m, 3 * D_BRANCH), F32),
        compiler_params=pltpu.CompilerParams(
            dimension_semantics=("parallel", "parallel"), vmem_limit_bytes=_VMEM_LIMIT),
        name="mixers",
    )(sinks, bias, proj, proj, proj, proj, proj, proj, pool_w, pool_scale, norm_w, sgu_w,
      sgu_bias)


OUT_TM = 512


def _outproj_kernel(ya_ref, yb_ref, g_ref, x_ref, w_ref, pw_ref, o_ref):
    g = g_ref[...]
    sg = g * jax.nn.sigmoid(g)
    acc = jnp.dot((ya_ref[...] * sg[:, :D_BRANCH]).astype(BF16), w_ref[:D_BRANCH, :],
                  preferred_element_type=F32)
    acc = acc + jnp.dot((yb_ref[...] * sg[:, D_BRANCH:]).astype(BF16), w_ref[D_BRANCH:, :],
                        preferred_element_type=F32)
    ms = jnp.mean(acc * acc, axis=-1, keepdims=True)
    o_ref[...] = x_ref[...] + acc * lax.rsqrt(ms + NORM_EPS) * pw_ref[...]


def _outproj(ya, ybcd, proj, x2d, w_out, post_w):
    m = x2d.shape[0]
    return pl.pallas_call(
        _outproj_kernel,
        grid=(m // OUT_TM,),
        in_specs=[
            pl.BlockSpec((OUT_TM, D_BRANCH), lambda i: (i, 0)),
            pl.BlockSpec((OUT_TM, 3 * D_BRANCH), lambda i: (i, 0)),
            pl.BlockSpec((OUT_TM, D_MODEL), lambda i: (i, P_GATE // D_MODEL)),
            pl.BlockSpec((OUT_TM, D_MODEL), lambda i: (i, 0)),
            pl.BlockSpec((D_MODEL, D_MODEL), lambda i: (0, 0)),
            pl.BlockSpec((1, D_MODEL), lambda i: (0, 0)),
        ],
        out_specs=pl.BlockSpec((OUT_TM, D_MODEL), lambda i: (i, 0)),
        out_shape=jax.ShapeDtypeStruct((m, D_MODEL), F32),
        compiler_params=pltpu.CompilerParams(
            dimension_semantics=("parallel",), vmem_limit_bytes=_VMEM_LIMIT),
        name="outproj",
    )(ya, ybcd, proj, x2d, w_out, post_w)


def kernel(x, pre_norm_w, post_norm_w, w_in, shift_mu, rwkv_w0, rwkv_w_up, rwkv_a0, rwkv_a_up,
           rwkv_k_k, rwkv_k_a, rwkv_r_k, rwkv_ln_w, rwkv_ln_b, attn_sinks, pool_w, pool_scale,
           sgu_norm_w, sgu_w, sgu_b, w_out):
    batch, seq, _ = x.shape
    assert x.shape == (batch, seq, D_MODEL) and seq % BLK == 0 and seq % RW_TILE == 0
    m = batch * seq
    head_id = np.arange(D_BRANCH) // HEAD
    bd = jnp.asarray((head_id[:, None] == head_id[None, :]).astype(np.float32), dtype=BF16)
    row_vec = lambda a: a.reshape(1, -1)

    h = x.reshape(m, D_MODEL)
    for l in range(DEPTH):
        proj = _inproj(h, row_vec(pre_norm_w[l]), _permute_weight(w_in[l]))

        mu = shift_mu[l]
        vecs = [row_vec(mu[0:D_BRANCH]), row_vec(mu[D_BRANCH:2 * D_BRANCH]),
                row_vec(mu[2 * D_BRANCH:3 * D_BRANCH]), row_vec(mu[3 * D_BRANCH:]),
                row_vec(rwkv_w0[l]), row_vec(rwkv_a0[l]), row_vec(rwkv_k_k[l]),
                row_vec(rwkv_k_a[l]), row_vec(rwkv_r_k[l]), row_vec(rwkv_ln_w[l]),
                row_vec(rwkv_ln_b[l])]
        ya = _rwkv(proj, vecs, rwkv_w_up[l], rwkv_a_up[l], bd, batch, seq)

        sgu_bias = jnp.broadcast_to(sgu_b[l].T[:, :, None], (BLK, 4, BLK)).reshape(BLK, D_BRANCH)
        ybcd = _mixers(proj, attn_sinks[l], pool_w[l], row_vec(pool_scale[l]),
                       row_vec(sgu_norm_w[l]), sgu_w[l], sgu_bias, batch, seq)

        h = _outproj(ya, ybcd, proj, h, w_out[l].astype(BF16), row_vec(post_norm_w[l]))
    return h.reshape(batch, seq, D_MODEL)
```

```python
import functools

import jax
import jax.numpy as jnp
import numpy as np
from jax import lax
from jax.experimental import pallas as pl
from jax.experimental.pallas import tpu as pltpu

F32 = jnp.float32
BF16 = jnp.bfloat16

D_MODEL = 2048
DEPTH = 2
D_BRANCH = 512
HEAD = 64
N_HEADS = D_BRANCH // HEAD
LORA = 64
KV_HEADS = 2
ATT_GROUP = N_HEADS // KV_HEADS
BLK = 128
POOL_WINDOWS = (2, 4, 8, 16)
POOL_HALO = 16
NEG_INF = -1e30
NORM_EPS = 1e-6
LN_EPS = 1e-5
GN_EPS = 64e-5

A_COLS = 3 * D_BRANCH + 2 * LORA
B_COLS = D_BRANCH + 2 * KV_HEADS * HEAD
OFF_B = A_COLS
OFF_C = OFF_B + B_COLS
OFF_D = OFF_C + D_BRANCH
OFF_G = OFF_D + 2 * D_BRANCH
D_IN = OFF_G + D_MODEL

P_GATE = 0
P_SGU = 2048
P_POOL = 3072
P_Q = 3584
P_R = 4096
P_K = 4608
P_V = 5120
P_KV = 5632
P_LORA = 5888
P_WIDTH = 6144

RW_CHUNK = 64
RW_TILE = 256

_VMEM_LIMIT = 56 * 1024 * 1024


def _permute_weight(w):
    seg = lambda start, n: w[:, start:start + n]
    parts = [
        seg(OFF_G, D_MODEL),
        seg(OFF_D, 2 * D_BRANCH),
        seg(OFF_C, D_BRANCH),
        seg(OFF_B, D_BRANCH),
        seg(0, 3 * D_BRANCH),
        seg(OFF_B + D_BRANCH, 2 * KV_HEADS * HEAD),
        seg(3 * D_BRANCH, 2 * LORA),
        jnp.zeros((w.shape[0], P_WIDTH - D_IN), w.dtype),
    ]
    return jnp.concatenate(parts, axis=1).astype(BF16)


IN_TM = 1024
IN_TN = 1024


def _inproj_kernel(x_ref, pw_ref, w_ref, o_ref, h_scr):
    @pl.when(pl.program_id(1) == 0)
    def _():
        for r0 in range(0, IN_TM, 256):
            x = x_ref[r0:r0 + 256, :]
            ms = jnp.mean(x * x, axis=-1, keepdims=True)
            h_scr[r0:r0 + 256, :] = (x * lax.rsqrt(ms + NORM_EPS) * pw_ref[...]).astype(BF16)

    o_ref[...] = jnp.dot(h_scr[...], w_ref[...], preferred_element_type=F32)


def _inproj(x2d, pre_w, w_perm):
    m = x2d.shape[0]
    return pl.pallas_call(
        _inproj_kernel,
        grid=(m // IN_TM, P_WIDTH // IN_TN),
        in_specs=[
            pl.BlockSpec((IN_TM, D_MODEL), lambda i, j: (i, 0)),
            pl.BlockSpec((1, D_MODEL), lambda i, j: (0, 0)),
            pl.BlockSpec((D_MODEL, IN_TN), lambda i, j: (0, j)),
        ],
        out_specs=pl.BlockSpec((IN_TM, IN_TN), lambda i, j: (i, j)),
        out_shape=jax.ShapeDtypeStruct((m, P_WIDTH), F32),
        scratch_shapes=[pltpu.VMEM((IN_TM, D_MODEL), BF16)],
        compiler_params=pltpu.CompilerParams(
            dimension_semantics=("parallel", "arbitrary"), vmem_limit_bytes=_VMEM_LIMIT),
        name="inproj",
    )(x2d, pre_w, w_perm)


def _bdot(a, b):
    return jnp.dot(a.astype(BF16), b.astype(BF16), preferred_element_type=F32)


def _bdot_nt(a, b):
    return lax.dot_general(a.astype(BF16), b.astype(BF16), (((1,), (1,)), ((), ())),
                           preferred_element_type=F32)


def _bdot_tn(a, b):
    return lax.dot_general(a.astype(BF16), b.astype(BF16), (((0,), (0,)), ((), ())),
                           preferred_element_type=F32)


def _split3(x):
    x1 = x.astype(BF16)
    r1 = x - x1.astype(F32)
    x2 = r1.astype(BF16)
    x3 = (r1 - x2.astype(F32)).astype(BF16)
    return x1, x2, x3


def _head_sums(xs, bd):
    rows = xs[0].shape[0]
    half = bd.shape[0]
    x = jnp.concatenate(xs, axis=0).astype(BF16)
    out = jnp.concatenate(
        [jnp.dot(x[:, :half], bd, preferred_element_type=F32),
         jnp.dot(x[:, half:], bd, preferred_element_type=F32)], axis=1)
    return [out[i * rows:(i + 1) * rows] for i in range(len(xs))]


def _mm(a16, b16):
    return jnp.dot(a16, b16, preferred_element_type=F32)


def _mm_nt(a16, b16):
    return lax.dot_general(a16, b16, (((1,), (1,)), ((), ())), preferred_element_type=F32)


def _block_diag(m16, bdm16):
    return jnp.concatenate([m16, m16], axis=0) * bdm16


def _unit_lower_inverse_many(lmats, eye, masks, bdm16):
    base_mask, level_masks = masks
    c = lmats[0].shape[0]
    bd_of = lambda m: _block_diag(m.astype(BF16), bdm16)
    lds = [jnp.where(base_mask, l, 0.0) for l in lmats]
    l2s = [_mm(ld.astype(BF16), bd_of(ld)) for ld in lds]
    xs = [eye + ld for ld in lds]
    both = [_mm(jnp.concatenate([x, l2], axis=0).astype(BF16), bd_of(l2))
            for x, l2 in zip(xs, l2s)]
    xs = [x + b[:c] for x, b in zip(xs, both)]
    xs = [x + _mm(x.astype(BF16), bd_of(b[c:])) for x, b in zip(xs, both)]
    for lm in level_masks:
        ts = [_mm(x.astype(BF16), bd_of(jnp.where(lm, l, 0.0))) for x, l in zip(xs, lmats)]
        xs = [x + _mm(t.astype(BF16), bd_of(x)) for x, t in zip(xs, ts)]
    return xs


def _rwkv_kernel(r_ref, k_ref, v_ref, lo_ref, mur_ref, muk_ref, muv_ref, mulo_ref,
                 w0_ref, wup_ref, a0_ref, aup_ref, kkw_ref, kaw_ref, rkw_ref, lnw_ref, lnb_ref,
                 bd_ref, o_ref, s_scr, pr_scr, pk_scr, pv_scr, plo_scr):
    n = pl.program_id(1)
    tile = r_ref.shape[0]
    c = RW_CHUNK
    n_chunks = tile // c

    @pl.when(n == 0)
    def _():
        s_scr[...] = jnp.zeros_like(s_scr)
        pr_scr[...] = jnp.zeros_like(pr_scr)
        pk_scr[...] = jnp.zeros_like(pk_scr)
        pv_scr[...] = jnp.zeros_like(pv_scr)
        plo_scr[...] = jnp.zeros_like(plo_scr)

    row = lax.broadcasted_iota(jnp.int32, (tile, 1), 0)

    def shifted(z_ref, mu_ref, p_scr):
        z = z_ref[...]
        zp = jnp.where(row == 0, p_scr[0:1, :], pltpu.roll(z, 1, axis=0))
        p_scr[0:1, :] = z[tile - 1:tile, :]
        return z + mu_ref[...] * (zp - z)

    r = shifted(r_ref, mur_ref, pr_scr)
    k = shifted(k_ref, muk_ref, pk_scr)
    v = shifted(v_ref, muv_ref, pv_scr)
    lo = shifted(lo_ref, mulo_ref, plo_scr)

    bd = bd_ref[...]
    w_log = -jax.nn.softplus(-(w0_ref[...] + _bdot(jnp.tanh(lo[:, :LORA]), wup_ref[...]))) - 0.5
    logw = -jnp.exp(w_log)
    asig = jax.nn.sigmoid(a0_ref[...] + _bdot(lo[:, LORA:], aup_ref[...]))
    kk = k * kkw_ref[...]
    k2 = k * (1.0 + (asig - 1.0) * kaw_ref[...])
    kk_ss, rk_sum = _head_sums([kk * kk, r * k2 * rkw_ref[...]], bd)
    kk = kk * lax.rsqrt(jnp.maximum(kk_ss, 1e-24))
    bonus = rk_sum * v

    pw = 2 * HEAD
    ci = lax.broadcasted_iota(jnp.int32, (c, pw), 0)
    cj = lax.broadcasted_iota(jnp.int32, (c, pw), 1) % HEAD
    eye = (ci == cj).astype(F32)
    same = lambda b: (ci // b) == (cj // b)
    masks = (same(8), tuple(same(2 * b) & ~same(b) for b in (8, 16, 32)))
    si = lax.broadcasted_iota(jnp.int32, (2 * c, pw), 0)
    sj = lax.broadcasted_iota(jnp.int32, (2 * c, pw), 1) % HEAD
    tri2 = ((si < c) & (si > sj)) | ((si >= c) & ((si - c) >= sj))
    bi = lax.broadcasted_iota(jnp.int32, (2 * pw, pw), 0)
    bj = lax.broadcasted_iota(jnp.int32, (2 * pw, pw), 1)
    bdm_bool = ((bi // HEAD) % 2) == (bj // HEAD)
    bdm4_16 = bdm_bool.astype(BF16)
    bdm16 = bdm4_16[:pw]
    bdm = bdm_bool[:pw]

    ti = lax.broadcasted_iota(jnp.int32, (tile, tile), 0)
    tj = lax.broadcasted_iota(jnp.int32, (tile, tile), 1)
    tril = ((ti >= tj) & ((ti // c) == (tj // c))).astype(BF16)
    cum = sum(jnp.dot(tril, p, preferred_element_type=F32) for p in _split3(logw))

    n_pairs = N_HEADS // 2
    items = [(ck, p) for ck in range(n_chunks) for p in range(n_pairs)]
    per_chunk = []
    for ck in range(n_chunks):
        sl = slice(ck * c, (ck + 1) * c)
        lw, cm = logw[sl], cum[sl]
        total = cm[c - 1:c, :]
        g_inv = jnp.exp(-cm)
        g_all = jnp.exp(total)
        g_tail = g_all * g_inv
        b_c = kk[sl] * asig[sl]
        per_chunk.append(dict(
            a=(-kk[sl] * jnp.exp(cm - lw)).astype(BF16), b=(b_c * g_inv).astype(BF16),
            k=(k2[sl] * g_inv).astype(BF16), r=(r[sl] * jnp.exp(cm)).astype(BF16),
            v=v[sl].astype(BF16), b_e=(b_c * g_tail).astype(BF16),
            k_e=(k2[sl] * g_tail).astype(BF16), g_all=g_all))

    def part(name, it):
        ck, p = it
        return per_chunk[ck][name][:, p * pw:(p + 1) * pw]

    a2 = [part("a", it) for it in items]
    r2 = [part("r", it) for it in items]
    ar2 = [jnp.concatenate([a, x], axis=0) for a, x in zip(a2, r2)]
    v_bd = [_block_diag(part("v", it), bdm16) for it in items]
    prods = []
    for x, it in zip(ar2, items):
        b, k_ = part("b", it), part("k", it)
        prods.append(_mm_nt(x, jnp.concatenate([b, b, k_, k_], axis=0) * bdm4_16))
    ab = [jnp.where(tri2, pr[:, :pw], 0.0) for pr in prods]
    ak16 = [jnp.where(tri2, pr[:, pw:], 0.0).astype(BF16) for pr in prods]
    qy = [_mm(x, vb) for x, vb in zip(ak16, v_bd)]
    vk = [jnp.where(bdm, lax.dot_general(part("v", it), part("k_e", it), (((0,), (0,)), ((), ())),
                                         preferred_element_type=F32), 0.0) for it in items]
    tinv = _unit_lower_inverse_many([x[:c] for x in ab], eye, masks, bdm16)
    wu = [_mm(t.astype(BF16),
              jnp.concatenate([_block_diag(a, bdm16), _block_diag(q[:c].astype(BF16), bdm16)], axis=1))
          for t, a, q in zip(tinv, a2, qy)]
    wa16 = [x[:, :pw].astype(BF16) for x in wu]
    uv = [x[:, pw:] for x in wu]
    uvt = [x.T for x in uv]
    war = [jnp.concatenate([w, x], axis=0) for w, x in zip(wa16, r2)]
    m_rb16 = [x[c:].astype(BF16) for x in ab]

    state = [s_scr[p] for p in range(n_pairs)]
    for ck in range(n_chunks):
        idx = [ck * n_pairs + p for p in range(n_pairs)]
        s16 = [s.astype(BF16) for s in state]
        uy = [_mm_nt(war[i], s16[p]) for p, i in enumerate(idx)]
        ut = [_mm_nt(s16[p], wa16[i]) + uvt[i] for p, i in enumerate(idx)]
        for p, i in enumerate(idx):
            u16 = (uy[p][:c] + uv[i]).astype(BF16)
            y = uy[p][c:] + qy[i][c:] + _mm(m_rb16[i], _block_diag(u16, bdm16))
            o_ref[ck * c:(ck + 1) * c, p * pw:(p + 1) * pw] = y
        state = [state[p] * part("g_all", items[i]) + vk[i]
                 + jnp.where(bdm, _mm(ut[p].astype(BF16), part("b_e", items[i])), 0.0)
                 for p, i in enumerate(idx)]
    for p in range(n_pairs):
        s_scr[p] = state[p]

    y = o_ref[...]
    mu = _head_sums([y], bd)[0] * (1.0 / HEAD)
    d = y - mu
    var = _head_sums([d * d], bd)[0] * (1.0 / HEAD)
    o_ref[...] = d * lax.rsqrt(var + GN_EPS) * lnw_ref[...] + lnb_ref[...] + bonus


def _rwkv(proj, vecs, wup, aup, bd, batch, seq):
    m = proj.shape[0]
    nt = seq // RW_TILE
    row = lambda b, n: b * nt + n
    wide = lambda cb: pl.BlockSpec((RW_TILE, D_BRANCH), lambda b, n: (row(b, n), cb))
    vec = lambda w: pl.BlockSpec((1, w), lambda b, n: (0, 0))
    full = lambda a: pl.BlockSpec(a.shape, lambda b, n: (0, 0))
    in_specs = [
        wide(P_R // D_BRANCH), wide(P_K // D_BRANCH), wide(P_V // D_BRANCH),
        pl.BlockSpec((RW_TILE, 2 * LORA), lambda b, n: (row(b, n), P_LORA // (2 * LORA))),
        vec(D_BRANCH), vec(D_BRANCH), vec(D_BRANCH), vec(2 * LORA),
        vec(D_BRANCH), full(wup), vec(D_BRANCH), full(aup),
        vec(D_BRANCH), vec(D_BRANCH), vec(D_BRANCH), vec(D_BRANCH), vec(D_BRANCH),
        full(bd),
    ]
    return pl.pallas_call(
        _rwkv_kernel,
        grid=(batch, nt),
        in_specs=in_specs,
        out_specs=pl.BlockSpec((RW_TILE, D_BRANCH), lambda b, n: (row(b, n), 0)),
        out_shape=jax.ShapeDtypeStruct((m, D_BRANCH), F32),
        scratch_shapes=[
            pltpu.VMEM((N_HEADS // 2, 2 * HEAD, 2 * HEAD), F32),
            pltpu.VMEM((8, D_BRANCH), F32), pltpu.VMEM((8, D_BRANCH), F32),
            pltpu.VMEM((8, D_BRANCH), F32), pltpu.VMEM((8, 2 * LORA), F32),
        ],
        compiler_params=pltpu.CompilerParams(
            dimension_semantics=("parallel", "arbitrary"), vmem_limit_bytes=_VMEM_LIMIT),
        name="rwkv",
    )(proj, proj, proj, proj, *vecs[:4], vecs[4], wup, vecs[5], aup, *vecs[6:], bd)


_SLOPES = tuple(2.0 ** (-8.0 * (h + 1) / N_HEADS) for h in range(N_HEADS))
_SQRT_HALF = float(np.sqrt(0.5))


def _attention_bias():
    t = np.arange(BLK)[:, None]
    s = np.arange(2 * BLK)[None, :]
    dist = t + BLK - s
    out = np.empty((2, N_HEADS * BLK, 2 * BLK), np.float32)
    for first in (0, 1):
        valid = (dist >= 0) & (dist < BLK) & ((s >= BLK) | (first == 1))
        for h in range(N_HEADS):
            out[first, h * BLK:(h + 1) * BLK] = np.where(valid, -_SLOPES[h] * dist, NEG_INF)
    return out


def _mixers_kernel(sinks_ref, bias_ref, q_ref, kvc_ref, kvp_ref, zc_ref, zcp_ref, zd_ref,
                   poolw_ref, pscale_ref, nw_ref, sw_ref, sb_ref, o_ref):
    n = pl.program_id(1)

    q = q_ref[...] * (HEAD ** -0.5)
    kv = jnp.concatenate([kvp_ref[...], kvc_ref[...]], axis=0)
    rows = ATT_GROUP * BLK
    head_in_group = lax.broadcasted_iota(jnp.int32, (rows, 1), 0) // BLK

    def per_row(values):
        col = jnp.full((rows, 1), values[-1], F32)
        for j in range(ATT_GROUP - 2, -1, -1):
            col = jnp.where(head_in_group == j, values[j], col)
        return col

    scores = []
    for g in range(KV_HEADS):
        qg = jnp.concatenate(
            [q[:, (g * ATT_GROUP + j) * HEAD:(g * ATT_GROUP + j + 1) * HEAD]
             for j in range(ATT_GROUP)], axis=0)
        scores.append(_bdot_nt(qg, kv[:, g * HEAD:(g + 1) * HEAD]))
    probs, dens = [], []
    for g in range(KV_HEADS):
        heads = range(g * ATT_GROUP, (g + 1) * ATT_GROUP)
        sink = per_row([sinks_ref[h] for h in heads])
        s = scores[g] + bias_ref[g * rows:(g + 1) * rows, :]
        mx = jnp.maximum(jnp.max(s, axis=-1, keepdims=True), sink)
        p = jnp.exp(s - mx)
        probs.append(p)
        dens.append(jnp.sum(p, axis=-1, keepdims=True) + jnp.exp(sink - mx))
    for g in range(KV_HEADS):
        vg = kv[:, (KV_HEADS + g) * HEAD:(KV_HEADS + g + 1) * HEAD]
        og = _bdot(probs[g], vg) / dens[g]
        for j in range(ATT_GROUP):
            h = g * ATT_GROUP + j
            o_ref[:, h * HEAD:(h + 1) * HEAD] = og[j * BLK:(j + 1) * BLK, :]

    zfull = jnp.concatenate([jnp.where(n > 0, zcp_ref[...], 0.0), zc_ref[...]], axis=0)
    pos = n * BLK + lax.broadcasted_iota(jnp.int32, (BLK, 1), 0) + 1
    for g, w in enumerate(POOL_WINDOWS):
        gs = slice(g * BLK, (g + 1) * BLK)
        zg = zfull[:, gs]
        acc = zg
        step = 1
        while step < w:
            acc = acc + pltpu.roll(acc, step, axis=0)
            step *= 2
        cnt = jnp.minimum(pos, w).astype(F32)
        pooled = acc[POOL_HALO:, :] / cnt - zg[POOL_HALO:, :]
        yg = _bdot(pooled, poolw_ref[g]) * pscale_ref[:, gs]
        o_ref[:, D_BRANCH + g * BLK:D_BRANCH + (g + 1) * BLK] = yg

    zd = zd_ref[...]
    gz = 0.5 * zd * (1.0 + lax.erf(zd * _SQRT_HALF))
    u = gz[:, :D_BRANCH]
    vv = gz[:, D_BRANCH:]
    mu = jnp.mean(vv, axis=-1, keepdims=True)
    dv = vv - mu
    var = jnp.mean(dv * dv, axis=-1, keepdims=True)
    vn = dv * lax.rsqrt(var + LN_EPS) * nw_ref[...]
    ri = lax.broadcasted_iota(jnp.int32, (BLK, BLK), 0)
    rj = lax.broadcasted_iota(jnp.int32, (BLK, BLK), 1)
    causal = ri >= rj
    for g in range(4):
        gs = slice(g * BLK, (g + 1) * BLK)
        ws = jnp.where(causal, sw_ref[g], 0.0)
        sg = _bdot(ws, vn[:, gs]) + sb_ref[:, gs]
        o_ref[:, 2 * D_BRANCH + g * BLK:2 * D_BRANCH + (g + 1) * BLK] = u[:, gs] * sg


def _mixers(proj, sinks, pool_w, pool_scale, norm_w, sgu_w, sgu_bias, batch, seq):
    m = proj.shape[0]
    nb = seq // BLK
    row = lambda b, n: b * nb + n
    halo = BLK // POOL_HALO
    bias = jnp.asarray(_attention_bias())
    in_specs = [
        pl.BlockSpec(memory_space=pltpu.SMEM),
        pl.BlockSpec((None,) + bias.shape[1:], lambda b, n: (jnp.minimum(n, 1), 0, 0)),
        pl.BlockSpec((BLK, D_BRANCH), lambda b, n: (row(b, n), P_Q // D_BRANCH)),
        pl.BlockSpec((BLK, 256), lambda b, n: (row(b, n), P_KV // 256)),
        pl.BlockSpec((BLK, 256), lambda b, n: (row(b, jnp.maximum(n - 1, 0)), P_KV // 256)),
        pl.BlockSpec((BLK, D_BRANCH), lambda b, n: (row(b, n), P_POOL // D_BRANCH)),
        pl.BlockSpec((POOL_HALO, D_BRANCH),
                     lambda b, n: (jnp.maximum(row(b, n) * halo - 1, 0), P_POOL // D_BRANCH)),
        pl.BlockSpec((BLK, 2 * D_BRANCH), lambda b, n: (row(b, n), P_SGU // (2 * D_BRANCH))),
        pl.BlockSpec((4, BLK, BLK), lambda b, n: (0, 0, 0)),
        pl.BlockSpec((1, D_BRANCH), lambda b, n: (0, 0)),
        pl.BlockSpec((1, D_BRANCH), lambda b, n: (0, 0)),
        pl.BlockSpec((4, BLK, BLK), lambda b, n: (0, 0, 0)),
        pl.BlockSpec((BLK, D_BRANCH), lambda b, n: (0, 0)),
    ]
    return pl.pallas_call(
        _mixers_kernel,
        grid=(batch, nb),
        in_specs=in_specs,
        out_specs=pl.BlockSpec((BLK, 3 * D_BRANCH), lambda b, n: (row(b, n), 0)),
        out_shape=jax.ShapeDtypeStruct((m, 3 * D_BRANCH), F32),
        compiler_params=pltpu.CompilerParams(
            dimension_semantics=("parallel", "parallel"), vmem_limit_bytes=_VMEM_LIMIT),
        name="mixers",
    )(sinks, bias, proj, proj, proj, proj, proj, proj, pool_w, pool_scale, norm_w, sgu_w,
      sgu_bias)


OUT_TM = 512


def _outproj_kernel(ya_ref, yb_ref, g_ref, x_ref, w_ref, pw_ref, o_ref):
    g = g_ref[...]
    sg = g * jax.nn.sigmoid(g)
    acc = jnp.dot((ya_ref[...] * sg[:, :D_BRANCH]).astype(BF16), w_ref[:D_BRANCH, :],
                  preferred_element_type=F32)
    acc = acc + jnp.dot((yb_ref[...] * sg[:, D_BRANCH:]).astype(BF16), w_ref[D_BRANCH:, :],
                        preferred_element_type=F32)
    ms = jnp.mean(acc * acc, axis=-1, keepdims=True)
    o_ref[...] = x_ref[...] + acc * lax.rsqrt(ms + NORM_EPS) * pw_ref[...]


def _outproj(ya, ybcd, proj, x2d, w_out, post_w):
    m = x2d.shape[0]
    return pl.pallas_call(
        _outproj_kernel,
        grid=(m // OUT_TM,),
        in_specs=[
            pl.BlockSpec((OUT_TM, D_BRANCH), lambda i: (i, 0)),
            pl.BlockSpec((OUT_TM, 3 * D_BRANCH), lambda i: (i, 0)),
            pl.BlockSpec((OUT_TM, D_MODEL), lambda i: (i, P_GATE // D_MODEL)),
            pl.BlockSpec((OUT_TM, D_MODEL), lambda i: (i, 0)),
            pl.BlockSpec((D_MODEL, D_MODEL), lambda i: (0, 0)),
            pl.BlockSpec((1, D_MODEL), lambda i: (0, 0)),
        ],
        out_specs=pl.BlockSpec((OUT_TM, D_MODEL), lambda i: (i, 0)),
        out_shape=jax.ShapeDtypeStruct((m, D_MODEL), F32),
        compiler_params=pltpu.CompilerParams(
            dimension_semantics=("parallel",), vmem_limit_bytes=_VMEM_LIMIT),
        name="outproj",
    )(ya, ybcd, proj, x2d, w_out, post_w)


def kernel(x, pre_norm_w, post_norm_w, w_in, shift_mu, rwkv_w0, rwkv_w_up, rwkv_a0, rwkv_a_up,
           rwkv_k_k, rwkv_k_a, rwkv_r_k, rwkv_ln_w, rwkv_ln_b, attn_sinks, pool_w, pool_scale,
           sgu_norm_w, sgu_w, sgu_b, w_out):
    batch, seq, _ = x.shape
    assert x.shape == (batch, seq, D_MODEL) and seq % BLK == 0 and seq % RW_TILE == 0
    m = batch * seq
    head_id = np.arange(D_BRANCH // 2) // HEAD
    bd = jnp.asarray((head_id[:, None] == head_id[None, :]).astype(np.float32), dtype=BF16)
    row_vec = lambda a: a.reshape(1, -1)

    h = x.reshape(m, D_MODEL)
    for l in range(DEPTH):
        proj = _inproj(h, row_vec(pre_norm_w[l]), _permute_weight(w_in[l]))

        mu = shift_mu[l]
        vecs = [row_vec(mu[0:D_BRANCH]), row_vec(mu[D_BRANCH:2 * D_BRANCH]),
                row_vec(mu[2 * D_BRANCH:3 * D_BRANCH]), row_vec(mu[3 * D_BRANCH:]),
                row_vec(rwkv_w0[l]), row_vec(rwkv_a0[l]), row_vec(rwkv_k_k[l]),
                row_vec(rwkv_k_a[l]), row_vec(rwkv_r_k[l]), row_vec(rwkv_ln_w[l]),
                row_vec(rwkv_ln_b[l])]
        ya = _rwkv(proj, vecs, rwkv_w_up[l], rwkv_a_up[l], bd, batch, seq)

        sgu_bias = jnp.broadcast_to(sgu_b[l].T[:, :, None], (BLK, 4, BLK)).reshape(BLK, D_BRANCH)
        ybcd = _mixers(proj, attn_sinks[l], pool_w[l], row_vec(pool_scale[l]),
                       row_vec(sgu_norm_w[l]), sgu_w[l], sgu_bias, batch, seq)

        h = _outproj(ya, ybcd, proj, h, w_out[l].astype(BF16), row_vec(post_norm_w[l]))
    return h.reshape(batch, seq, D_MODEL)
```

```python
import functools

import jax
import jax.numpy as jnp
import numpy as np
from jax import lax
from jax.experimental import pallas as pl
from jax.experimental.pallas import tpu as pltpu

F32 = jnp.float32
BF16 = jnp.bfloat16

D_MODEL = 2048
DEPTH = 2
D_BRANCH = 512
HEAD = 64
N_HEADS = D_BRANCH // HEAD
LORA = 64
KV_HEADS = 2
ATT_GROUP = N_HEADS // KV_HEADS
BLK = 128
POOL_WINDOWS = (2, 4, 8, 16)
POOL_HALO = 16
NEG_INF = -1e30
NORM_EPS = 1e-6
LN_EPS = 1e-5
GN_EPS = 64e-5

A_COLS = 3 * D_BRANCH + 2 * LORA
B_COLS = D_BRANCH + 2 * KV_HEADS * HEAD
OFF_B = A_COLS
OFF_C = OFF_B + B_COLS
OFF_D = OFF_C + D_BRANCH
OFF_G = OFF_D + 2 * D_BRANCH
D_IN = OFF_G + D_MODEL

P_GATE = 0
P_SGU = 2048
P_POOL = 3072
P_Q = 3584
P_R = 4096
P_K = 4608
P_V = 5120
P_KV = 5632
P_LORA = 5888
P_WIDTH = 6144

RW_CHUNK = 64
RW_TILE = 256

_VMEM_LIMIT = 56 * 1024 * 1024


def _tail_weight(w_in, layer):
    seg = lambda start, n: w_in[layer, :, start:start + n]
    parts = [
        seg(OFF_B + D_BRANCH, 2 * KV_HEADS * HEAD),
        seg(3 * D_BRANCH, 2 * LORA),
        jnp.zeros((D_MODEL, P_WIDTH - D_IN), w_in.dtype),
    ]
    return jnp.concatenate(parts, axis=1).astype(BF16)


IN_TM = 1024
IN_TN = 1024


IN_HALF = IN_TN // 2
_CHUNK_SRC = (OFF_G, OFF_G + 512, OFF_G + 1024, OFF_G + 1536, OFF_D, OFF_D + 512, OFF_C, OFF_B,
              0, 512, 1024, 0)
assert len(_CHUNK_SRC) * IN_HALF == P_WIDTH


def _inproj_kernel(src_ref, x_ref, pw_ref, wa_ref, wb_ref, wt_ref, o_ref, h_scr):
    j = pl.program_id(1)
    last = pl.num_programs(1) - 1

    @pl.when(j == 0)
    def _():
        for r0 in range(0, IN_TM, 256):
            x = x_ref[r0:r0 + 256, :]
            ms = jnp.mean(x * x, axis=-1, keepdims=True)
            h_scr[r0:r0 + 256, :] = (x * lax.rsqrt(ms + NORM_EPS) * pw_ref[...]).astype(BF16)

    o_ref[:, :IN_HALF] = jnp.dot(h_scr[...], wa_ref[...].astype(BF16),
                                 preferred_element_type=F32)

    @pl.when(j < last)
    def _():
        o_ref[:, IN_HALF:] = jnp.dot(h_scr[...], wb_ref[...].astype(BF16),
                                     preferred_element_type=F32)

    @pl.when(j == last)
    def _():
        o_ref[:, IN_HALF:] = jnp.dot(h_scr[...], wt_ref[...], preferred_element_type=F32)


def _inproj(x2d, pre_w, w_in, layer, w_tail):
    m = x2d.shape[0]
    src = jnp.asarray(_CHUNK_SRC, jnp.int32)
    w_spec = lambda half: pl.BlockSpec(
        (None, pl.Element(D_MODEL), pl.Element(IN_HALF)),
        lambda i, j, src_ref: (layer, 0, pl.multiple_of(src_ref[2 * j + half], 128)))
    grid_spec = pltpu.PrefetchScalarGridSpec(
        num_scalar_prefetch=1,
        grid=(m // IN_TM, P_WIDTH // IN_TN),
        in_specs=[
            pl.BlockSpec((IN_TM, D_MODEL), lambda i, j, s: (i, 0)),
            pl.BlockSpec((1, D_MODEL), lambda i, j, s: (0, 0)),
            w_spec(0), w_spec(1),
            pl.BlockSpec((D_MODEL, IN_HALF), lambda i, j, s: (0, 0)),
        ],
        out_specs=pl.BlockSpec((IN_TM, IN_TN), lambda i, j, s: (i, j)),
        scratch_shapes=[pltpu.VMEM((IN_TM, D_MODEL), BF16)],
    )
    return pl.pallas_call(
        _inproj_kernel,
        grid_spec=grid_spec,
        out_shape=jax.ShapeDtypeStruct((m, P_WIDTH), F32),
        compiler_params=pltpu.CompilerParams(
            dimension_semantics=("parallel", "arbitrary"), vmem_limit_bytes=_VMEM_LIMIT),
        name="inproj",
    )(src, x2d, pre_w, w_in, w_in, w_tail)


def _bdot(a, b):
    return jnp.dot(a.astype(BF16), b.astype(BF16), preferred_element_type=F32)


def _bdot_nt(a, b):
    return lax.dot_general(a.astype(BF16), b.astype(BF16), (((1,), (1,)), ((), ())),
                           preferred_element_type=F32)


def _bdot_tn(a, b):
    return lax.dot_general(a.astype(BF16), b.astype(BF16), (((0,), (0,)), ((), ())),
                           preferred_element_type=F32)


def _split3(x):
    x1 = x.astype(BF16)
    r1 = x - x1.astype(F32)
    x2 = r1.astype(BF16)
    x3 = (r1 - x2.astype(F32)).astype(BF16)
    return x1, x2, x3


def _head_sums(xs, bd):
    rows = xs[0].shape[0]
    half = bd.shape[0]
    x = jnp.concatenate(xs, axis=0).astype(BF16)
    out = jnp.concatenate(
        [jnp.dot(x[:, :half], bd, preferred_element_type=F32),
         jnp.dot(x[:, half:], bd, preferred_element_type=F32)], axis=1)
    return [out[i * rows:(i + 1) * rows] for i in range(len(xs))]


def _mm(a16, b16):
    return jnp.dot(a16, b16, preferred_element_type=F32)


def _mm_nt(a16, b16):
    return lax.dot_general(a16, b16, (((1,), (1,)), ((), ())), preferred_element_type=F32)


def _block_diag(m16, bdm16):
    return jnp.concatenate([m16, m16], axis=0) * bdm16


def _unit_lower_inverse_many(lmats, eye, masks, bdm16):
    base_mask, level_masks = masks
    c = lmats[0].shape[0]
    bd_of = lambda m: _block_diag(m.astype(BF16), bdm16)
    lds = [jnp.where(base_mask, l, 0.0) for l in lmats]
    l2s = [_mm(ld.astype(BF16), bd_of(ld)) for ld in lds]
    xs = [eye + ld for ld in lds]
    both = [_mm(jnp.concatenate([x, l2], axis=0).astype(BF16), bd_of(l2))
            for x, l2 in zip(xs, l2s)]
    xs = [x + b[:c] for x, b in zip(xs, both)]
    xs = [x + _mm(x.astype(BF16), bd_of(b[c:])) for x, b in zip(xs, both)]
    for lm in level_masks:
        ts = [_mm(x.astype(BF16), bd_of(jnp.where(lm, l, 0.0))) for x, l in zip(xs, lmats)]
        xs = [x + _mm(t.astype(BF16), bd_of(x)) for x, t in zip(xs, ts)]
    return xs


def _rwkv_kernel(r_ref, k_ref, v_ref, lo_ref, mur_ref, muk_ref, muv_ref, mulo_ref,
                 w0_ref, wup_ref, a0_ref, aup_ref, kkw_ref, kaw_ref, rkw_ref, lnw_ref, lnb_ref,
                 bd_ref, o_ref, s_scr, pr_scr, pk_scr, pv_scr, plo_scr):
    n = pl.program_id(1)
    tile = r_ref.shape[0]
    c = RW_CHUNK
    n_chunks = tile // c

    @pl.when(n == 0)
    def _():
        s_scr[...] = jnp.zeros_like(s_scr)
        pr_scr[...] = jnp.zeros_like(pr_scr)
        pk_scr[...] = jnp.zeros_like(pk_scr)
        pv_scr[...] = jnp.zeros_like(pv_scr)
        plo_scr[...] = jnp.zeros_like(plo_scr)

    row = lax.broadcasted_iota(jnp.int32, (tile, 1), 0)

    def shifted(z_ref, mu_ref, p_scr):
        z = z_ref[...]
        zp = jnp.where(row == 0, p_scr[0:1, :], pltpu.roll(z, 1, axis=0))
        p_scr[0:1, :] = z[tile - 1:tile, :]
        return z + mu_ref[...] * (zp - z)

    r = shifted(r_ref, mur_ref, pr_scr)
    k = shifted(k_ref, muk_ref, pk_scr)
    v = shifted(v_ref, muv_ref, pv_scr)
    lo = shifted(lo_ref, mulo_ref, plo_scr)

    bd = bd_ref[...]
    w_log = -jax.nn.softplus(-(w0_ref[...] + _bdot(jnp.tanh(lo[:, :LORA]), wup_ref[...]))) - 0.5
    logw = -jnp.exp(w_log)
    asig = jax.nn.sigmoid(a0_ref[...] + _bdot(lo[:, LORA:], aup_ref[...]))
    kk = k * kkw_ref[...]
    k2 = k * (1.0 + (asig - 1.0) * kaw_ref[...])
    kk_ss, rk_sum = _head_sums([kk * kk, r * k2 * rkw_ref[...]], bd)
    kk = kk * lax.rsqrt(jnp.maximum(kk_ss, 1e-24))
    bonus = rk_sum * v

    pw = 2 * HEAD
    ci = lax.broadcasted_iota(jnp.int32, (c, pw), 0)
    cj = lax.broadcasted_iota(jnp.int32, (c, pw), 1) % HEAD
    eye = (ci == cj).astype(F32)
    same = lambda b: (ci // b) == (cj // b)
    masks = (same(8), tuple(same(2 * b) & ~same(b) for b in (8, 16, 32)))
    si = lax.broadcasted_iota(jnp.int32, (2 * c, pw), 0)
    sj = lax.broadcasted_iota(jnp.int32, (2 * c, pw), 1) % HEAD
    tri2 = ((si < c) & (si > sj)) | ((si >= c) & ((si - c) >= sj))
    bi = lax.broadcasted_iota(jnp.int32, (2 * pw, pw), 0)
    bj = lax.broadcasted_iota(jnp.int32, (2 * pw, pw), 1)
    bdm_bool = ((bi // HEAD) % 2) == (bj // HEAD)
    bdm4_16 = bdm_bool.astype(BF16)
    bdm16 = bdm4_16[:pw]
    bdm = bdm_bool[:pw]

    ti = lax.broadcasted_iota(jnp.int32, (tile, tile), 0)
    tj = lax.broadcasted_iota(jnp.int32, (tile, tile), 1)
    tril = ((ti >= tj) & ((ti // c) == (tj // c))).astype(BF16)
    cum = sum(jnp.dot(tril, p, preferred_element_type=F32) for p in _split3(logw))

    n_pairs = N_HEADS // 2
    items = [(ck, p) for ck in range(n_chunks) for p in range(n_pairs)]
    per_chunk = []
    for ck in range(n_chunks):
        sl = slice(ck * c, (ck + 1) * c)
        lw, cm = logw[sl], cum[sl]
        total = cm[c - 1:c, :]
        g_inv = jnp.exp(-cm)
        g_all = jnp.exp(total)
        g_tail = g_all * g_inv
        b_c = kk[sl] * asig[sl]
        per_chunk.append(dict(
            a=(-kk[sl] * jnp.exp(cm - lw)).astype(BF16), b=(b_c * g_inv).astype(BF16),
            k=(k2[sl] * g_inv).astype(BF16), r=(r[sl] * jnp.exp(cm)).astype(BF16),
            v=v[sl].astype(BF16), b_e=(b_c * g_tail).astype(BF16),
            k_e=(k2[sl] * g_tail).astype(BF16), g_all=g_all))

    def part(name, it):
        ck, p = it
        return per_chunk[ck][name][:, p * pw:(p + 1) * pw]

    a2 = [part("a", it) for it in items]
    r2 = [part("r", it) for it in items]
    ar2 = [jnp.concatenate([a, x], axis=0) for a, x in zip(a2, r2)]
    v_bd = [_block_diag(part("v", it), bdm16) for it in items]
    prods = []
    for x, it in zip(ar2, items):
        b, k_ = part("b", it), part("k", it)
        prods.append(_mm_nt(x, jnp.concatenate([b, b, k_, k_], axis=0) * bdm4_16))
    ab = [jnp.where(tri2, pr[:, :pw], 0.0) for pr in prods]
    ak16 = [jnp.where(tri2, pr[:, pw:], 0.0).astype(BF16) for pr in prods]
    qy = [_mm(x, vb) for x, vb in zip(ak16, v_bd)]
    vk = [jnp.where(bdm, lax.dot_general(part("v", it), part("k_e", it), (((0,), (0,)), ((), ())),
                                         preferred_element_type=F32), 0.0) for it in items]
    tinv = _unit_lower_inverse_many([x[:c] for x in ab], eye, masks, bdm16)
    wu = [_mm(t.astype(BF16),
              jnp.concatenate([_block_diag(a, bdm16), _block_diag(q[:c].astype(BF16), bdm16)], axis=1))
          for t, a, q in zip(tinv, a2, qy)]
    wa16 = [x[:, :pw].astype(BF16) for x in wu]
    uv = [x[:, pw:] for x in wu]
    uvt = [x.T for x in uv]
    war = [jnp.concatenate([w, x], axis=0) for w, x in zip(wa16, r2)]
    m_rb16 = [x[c:].astype(BF16) for x in ab]

    state = [s_scr[p] for p in range(n_pairs)]
    for ck in range(n_chunks):
        idx = [ck * n_pairs + p for p in range(n_pairs)]
        s16 = [s.astype(BF16) for s in state]
        uy = [_mm_nt(war[i], s16[p]) for p, i in enumerate(idx)]
        ut = [_mm_nt(s16[p], wa16[i]) + uvt[i] for p, i in enumerate(idx)]
        for p, i in enumerate(idx):
            u16 = (uy[p][:c] + uv[i]).astype(BF16)
            y = uy[p][c:] + qy[i][c:] + _mm(m_rb16[i], _block_diag(u16, bdm16))
            o_ref[ck * c:(ck + 1) * c, p * pw:(p + 1) * pw] = y
        state = [state[p] * part("g_all", items[i]) + vk[i]
                 + jnp.where(bdm, _mm(ut[p].astype(BF16), part("b_e", items[i])), 0.0)
                 for p, i in enumerate(idx)]
    for p in range(n_pairs):
        s_scr[p] = state[p]

    y = o_ref[...]
    mu = _head_sums([y], bd)[0] * (1.0 / HEAD)
    d = y - mu
    var = _head_sums([d * d], bd)[0] * (1.0 / HEAD)
    o_ref[...] = d * lax.rsqrt(var + GN_EPS) * lnw_ref[...] + lnb_ref[...] + bonus


def _rwkv(proj, vecs, wup, aup, bd, batch, seq):
    m = proj.shape[0]
    nt = seq // RW_TILE
    row = lambda b, n: b * nt + n
    wide = lambda cb: pl.BlockSpec((RW_TILE, D_BRANCH), lambda b, n: (row(b, n), cb))
    vec = lambda w: pl.BlockSpec((1, w), lambda b, n: (0, 0))
    full = lambda a: pl.BlockSpec(a.shape, lambda b, n: (0, 0))
    in_specs = [
        wide(P_R // D_BRANCH), wide(P_K // D_BRANCH), wide(P_V // D_BRANCH),
        pl.BlockSpec((RW_TILE, 2 * LORA), lambda b, n: (row(b, n), P_LORA // (2 * LORA))),
        vec(D_BRANCH), vec(D_BRANCH), vec(D_BRANCH), vec(2 * LORA),
        vec(D_BRANCH), full(wup), vec(D_BRANCH), full(aup),
        vec(D_BRANCH), vec(D_BRANCH), vec(D_BRANCH), vec(D_BRANCH), vec(D_BRANCH),
        full(bd),
    ]
    return pl.pallas_call(
        _rwkv_kernel,
        grid=(batch, nt),
        in_specs=in_specs,
        out_specs=pl.BlockSpec((RW_TILE, D_BRANCH), lambda b, n: (row(b, n), 0)),
        out_shape=jax.ShapeDtypeStruct((m, D_BRANCH), F32),
        scratch_shapes=[
            pltpu.VMEM((N_HEADS // 2, 2 * HEAD, 2 * HEAD), F32),
            pltpu.VMEM((8, D_BRANCH), F32), pltpu.VMEM((8, D_BRANCH), F32),
            pltpu.VMEM((8, D_BRANCH), F32), pltpu.VMEM((8, 2 * LORA), F32),
        ],
        compiler_params=pltpu.CompilerParams(
            dimension_semantics=("parallel", "arbitrary"), vmem_limit_bytes=_VMEM_LIMIT),
        name="rwkv",
    )(proj, proj, proj, proj, *vecs[:4], vecs[4], wup, vecs[5], aup, *vecs[6:], bd)


_SLOPES = tuple(2.0 ** (-8.0 * (h + 1) / N_HEADS) for h in range(N_HEADS))
_SQRT_HALF = float(np.sqrt(0.5))


def _attention_bias():
    t = np.arange(BLK)[:, None]
    s = np.arange(2 * BLK)[None, :]
    dist = t + BLK - s
    out = np.empty((2, N_HEADS * BLK, 2 * BLK), np.float32)
    for first in (0, 1):
        valid = (dist >= 0) & (dist < BLK) & ((s >= BLK) | (first == 1))
        for h in range(N_HEADS):
            out[first, h * BLK:(h + 1) * BLK] = np.where(valid, -_SLOPES[h] * dist, NEG_INF)
    return out


def _mixers_kernel(sinks_ref, bias_ref, q_ref, kvc_ref, kvp_ref, zc_ref, zcp_ref, zd_ref,
                   poolw_ref, pscale_ref, nw_ref, sw_ref, sb_ref, o_ref):
    n = pl.program_id(1)

    q = q_ref[...] * (HEAD ** -0.5)
    kv = jnp.concatenate([kvp_ref[...], kvc_ref[...]], axis=0)
    rows = ATT_GROUP * BLK
    head_in_group = lax.broadcasted_iota(jnp.int32, (rows, 1), 0) // BLK

    def per_row(values):
        col = jnp.full((rows, 1), values[-1], F32)
        for j in range(ATT_GROUP - 2, -1, -1):
            col = jnp.where(head_in_group == j, values[j], col)
        return col

    scores = []
    for g in range(KV_HEADS):
        qg = jnp.concatenate(
            [q[:, (g * ATT_GROUP + j) * HEAD:(g * ATT_GROUP + j + 1) * HEAD]
             for j in range(ATT_GROUP)], axis=0)
        scores.append(_bdot_nt(qg, kv[:, g * HEAD:(g + 1) * HEAD]))
    probs, dens = [], []
    for g in range(KV_HEADS):
        heads = range(g * ATT_GROUP, (g + 1) * ATT_GROUP)
        sink = per_row([sinks_ref[h] for h in heads])
        s = scores[g] + bias_ref[g * rows:(g + 1) * rows, :]
        mx = jnp.maximum(jnp.max(s, axis=-1, keepdims=True), sink)
        p = jnp.exp(s - mx)
        probs.append(p)
        dens.append(jnp.sum(p, axis=-1, keepdims=True) + jnp.exp(sink - mx))
    for g in range(KV_HEADS):
        vg = kv[:, (KV_HEADS + g) * HEAD:(KV_HEADS + g + 1) * HEAD]
        og = _bdot(probs[g], vg) / dens[g]
        for j in range(ATT_GROUP):
            h = g * ATT_GROUP + j
            o_ref[:, h * HEAD:(h + 1) * HEAD] = og[j * BLK:(j + 1) * BLK, :]

    zfull = jnp.concatenate([jnp.where(n > 0, zcp_ref[...], 0.0), zc_ref[...]], axis=0)
    pos = n * BLK + lax.broadcasted_iota(jnp.int32, (BLK, 1), 0) + 1
    for g, w in enumerate(POOL_WINDOWS):
        gs = slice(g * BLK, (g + 1) * BLK)
        zg = zfull[:, gs]
        acc = zg
        step = 1
        while step < w:
            acc = acc + pltpu.roll(acc, step, axis=0)
            step *= 2
        cnt = jnp.minimum(pos, w).astype(F32)
        pooled = acc[POOL_HALO:, :] / cnt - zg[POOL_HALO:, :]
        yg = _bdot(pooled, poolw_ref[g]) * pscale_ref[:, gs]
        o_ref[:, D_BRANCH + g * BLK:D_BRANCH + (g + 1) * BLK] = yg

    zd = zd_ref[...]
    gz = 0.5 * zd * (1.0 + lax.erf(zd * _SQRT_HALF))
    u = gz[:, :D_BRANCH]
    vv = gz[:, D_BRANCH:]
    mu = jnp.mean(vv, axis=-1, keepdims=True)
    dv = vv - mu
    var = jnp.mean(dv * dv, axis=-1, keepdims=True)
    vn = dv * lax.rsqrt(var + LN_EPS) * nw_ref[...]
    ri = lax.broadcasted_iota(jnp.int32, (BLK, BLK), 0)
    rj = lax.broadcasted_iota(jnp.int32, (BLK, BLK), 1)
    causal = ri >= rj
    for g in range(4):
        gs = slice(g * BLK, (g + 1) * BLK)
        ws = jnp.where(causal, sw_ref[g], 0.0)
        sg = _bdot(ws, vn[:, gs]) + sb_ref[:, gs]
        o_ref[:, 2 * D_BRANCH + g * BLK:2 * D_BRANCH + (g + 1) * BLK] = u[:, gs] * sg


def _mixers(proj, sinks, pool_w, pool_scale, norm_w, sgu_w, sgu_bias, batch, seq):
    m = proj.shape[0]
    nb = seq // BLK
    row = lambda b, n: b * nb + n
    halo = BLK // POOL_HALO
    bias = jnp.asarray(_attention_bias())
    in_specs = [
        pl.BlockSpec(memory_space=pltpu.SMEM),
        pl.BlockSpec((None,) + bias.shape[1:], lambda b, n: (jnp.minimum(n, 1), 0, 0)),
        pl.BlockSpec((BLK, D_BRANCH), lambda b, n: (row(b, n), P_Q // D_BRANCH)),
        pl.BlockSpec((BLK, 256), lambda b, n: (row(b, n), P_KV // 256)),
        pl.BlockSpec((BLK, 256), lambda b, n: (row(b, jnp.maximum(n - 1, 0)), P_KV // 256)),
        pl.BlockSpec((BLK, D_BRANCH), lambda b, n: (row(b, n), P_POOL // D_BRANCH)),
        pl.BlockSpec((POOL_HALO, D_BRANCH),
                     lambda b, n: (jnp.maximum(row(b, n) * halo - 1, 0), P_POOL // D_BRANCH)),
        pl.BlockSpec((BLK, 2 * D_BRANCH), lambda b, n: (row(b, n), P_SGU // (2 * D_BRANCH))),
        pl.BlockSpec((4, BLK, BLK), lambda b, n: (0, 0, 0)),
        pl.BlockSpec((1, D_BRANCH), lambda b, n: (0, 0)),
        pl.BlockSpec((1, D_BRANCH), lambda b, n: (0, 0)),
        pl.BlockSpec((4, BLK, BLK), lambda b, n: (0, 0, 0)),
        pl.BlockSpec((BLK, D_BRANCH), lambda b, n: (0, 0)),
    ]
    return pl.pallas_call(
        _mixers_kernel,
        grid=(batch, nb),
        in_specs=in_specs,
        out_specs=pl.BlockSpec((BLK, 3 * D_BRANCH), lambda b, n: (row(b, n), 0)),
        out_shape=jax.ShapeDtypeStruct((m, 3 * D_BRANCH), F32),
        compiler_params=pltpu.CompilerParams(
            dimension_semantics=("parallel", "parallel"), vmem_limit_bytes=_VMEM_LIMIT),
        name="mixers",
    )(sinks, bias, proj, proj, proj, proj, proj, proj, pool_w, pool_scale, norm_w, sgu_w,
      sgu_bias)


OUT_TM = 512


def _outproj_kernel(ya_ref, yb_ref, g_ref, x_ref, w_ref, pw_ref, o_ref):
    g = g_ref[...]
    sg = g * jax.nn.sigmoid(g)
    acc = jnp.dot((ya_ref[...] * sg[:, :D_BRANCH]).astype(BF16), w_ref[:D_BRANCH, :],
                  preferred_element_type=F32)
    acc = acc + jnp.dot((yb_ref[...] * sg[:, D_BRANCH:]).astype(BF16), w_ref[D_BRANCH:, :],
                        preferred_element_type=F32)
    ms = jnp.mean(acc * acc, axis=-1, keepdims=True)
    o_ref[...] = x_ref[...] + acc * lax.rsqrt(ms + NORM_EPS) * pw_ref[...]


def _outproj(ya, ybcd, proj, x2d, w_out, post_w):
    m = x2d.shape[0]
    return pl.pallas_call(
        _outproj_kernel,
        grid=(m // OUT_TM,),
        in_specs=[
            pl.BlockSpec((OUT_TM, D_BRANCH), lambda i: (i, 0)),
            pl.BlockSpec((OUT_TM, 3 * D_BRANCH), lambda i: (i, 0)),
            pl.BlockSpec((OUT_TM, D_MODEL), lambda i: (i, P_GATE // D_MODEL)),
            pl.BlockSpec((OUT_TM, D_MODEL), lambda i: (i, 0)),
            pl.BlockSpec((D_MODEL, D_MODEL), lambda i: (0, 0)),
            pl.BlockSpec((1, D_MODEL), lambda i: (0, 0)),
        ],
        out_specs=pl.BlockSpec((OUT_TM, D_MODEL), lambda i: (i, 0)),
        out_shape=jax.ShapeDtypeStruct((m, D_MODEL), F32),
        compiler_params=pltpu.CompilerParams(
            dimension_semantics=("parallel",), vmem_limit_bytes=_VMEM_LIMIT),
        name="outproj",
    )(ya, ybcd, proj, x2d, w_out, post_w)


def kernel(x, pre_norm_w, post_norm_w, w_in, shift_mu, rwkv_w0, rwkv_w_up, rwkv_a0, rwkv_a_up,
           rwkv_k_k, rwkv_k_a, rwkv_r_k, rwkv_ln_w, rwkv_ln_b, attn_sinks, pool_w, pool_scale,
           sgu_norm_w, sgu_w, sgu_b, w_out):
    batch, seq, _ = x.shape
    assert x.shape == (batch, seq, D_MODEL) and seq % BLK == 0 and seq % RW_TILE == 0
    m = batch * seq
    head_id = np.arange(D_BRANCH // 2) // HEAD
    bd = jnp.asarray((head_id[:, None] == head_id[None, :]).astype(np.float32), dtype=BF16)
    row_vec = lambda a: a.reshape(1, -1)

    h = x.reshape(m, D_MODEL)
    for l in range(DEPTH):
        proj = _inproj(h, row_vec(pre_norm_w[l]), w_in, l, _tail_weight(w_in, l))

        mu = shift_mu[l]
        vecs = [row_vec(mu[0:D_BRANCH]), row_vec(mu[D_BRANCH:2 * D_BRANCH]),
                row_vec(mu[2 * D_BRANCH:3 * D_BRANCH]), row_vec(mu[3 * D_BRANCH:]),
                row_vec(rwkv_w0[l]), row_vec(rwkv_a0[l]), row_vec(rwkv_k_k[l]),
                row_vec(rwkv_k_a[l]), row_vec(rwkv_r_k[l]), row_vec(rwkv_ln_w[l]),
                row_vec(rwkv_ln_b[l])]
        ya = _rwkv(proj, vecs, rwkv_w_up[l], rwkv_a_up[l], bd, batch, seq)

        sgu_bias = jnp.broadcast_to(sgu_b[l].T[:, :, None], (BLK, 4, BLK)).reshape(BLK, D_BRANCH)
        ybcd = _mixers(proj, attn_sinks[l], pool_w[l], row_vec(pool_scale[l]),
                       row_vec(sgu_norm_w[l]), sgu_w[l], sgu_bias, batch, seq)

        h = _outproj(ya, ybcd, proj, h, w_out[l].astype(BF16), row_vec(post_norm_w[l]))
    return h.reshape(batch, seq, D_MODEL)
```

```python
import functools

import jax
import jax.numpy as jnp
import numpy as np
from jax import lax
from jax.experimental import pallas as pl
from jax.experimental.pallas import tpu as pltpu

F32 = jnp.float32
BF16 = jnp.bfloat16

D_MODEL = 2048
DEPTH = 2
D_BRANCH = 512
HEAD = 64
N_HEADS = D_BRANCH // HEAD
LORA = 64
KV_HEADS = 2
ATT_GROUP = N_HEADS // KV_HEADS
BLK = 128
POOL_WINDOWS = (2, 4, 8, 16)
POOL_HALO = 16
NEG_INF = -1e30
NORM_EPS = 1e-6
LN_EPS = 1e-5
GN_EPS = 64e-5

A_COLS = 3 * D_BRANCH + 2 * LORA
B_COLS = D_BRANCH + 2 * KV_HEADS * HEAD
OFF_B = A_COLS
OFF_C = OFF_B + B_COLS
OFF_D = OFF_C + D_BRANCH
OFF_G = OFF_D + 2 * D_BRANCH
D_IN = OFF_G + D_MODEL

P_GATE = 0
P_SGU = 2048
P_POOL = 3072
P_Q = 3584
P_R = 4096
P_K = 4608
P_V = 5120
P_KV = 5632
P_LORA = 5888
P_WIDTH = 6144

RW_CHUNK = 64
RW_TILE = 256

_VMEM_LIMIT = 56 * 1024 * 1024


def _tail_weight(w16, layer):
    seg = lambda start, n: w16[layer, :, start:start + n]
    parts = [
        seg(OFF_B + D_BRANCH, 2 * KV_HEADS * HEAD),
        seg(3 * D_BRANCH, 2 * LORA),
        jnp.zeros((D_MODEL, P_WIDTH - D_IN), w16.dtype),
    ]
    return jnp.concatenate(parts, axis=1)


IN_TM = 1024
IN_TN = 1024


IN_HALF = IN_TN // 2
_CHUNK_SRC = (OFF_G, OFF_G + 512, OFF_G + 1024, OFF_G + 1536, OFF_D, OFF_D + 512, OFF_C, OFF_B,
              0, 512, 1024, 0)
assert len(_CHUNK_SRC) * IN_HALF == P_WIDTH


def _inproj_kernel(src_ref, x_ref, pw_ref, wa_ref, wb_ref, wt_ref, o_ref, h_scr):
    j = pl.program_id(1)
    last = pl.num_programs(1) - 1

    @pl.when(j == 0)
    def _():
        for r0 in range(0, IN_TM, 256):
            x = x_ref[r0:r0 + 256, :]
            ms = jnp.mean(x * x, axis=-1, keepdims=True)
            h_scr[r0:r0 + 256, :] = (x * lax.rsqrt(ms + NORM_EPS) * pw_ref[...]).astype(BF16)

    o_ref[:, :IN_HALF] = jnp.dot(h_scr[...], wa_ref[...], preferred_element_type=F32)

    @pl.when(j < last)
    def _():
        o_ref[:, IN_HALF:] = jnp.dot(h_scr[...], wb_ref[...], preferred_element_type=F32)

    @pl.when(j == last)
    def _():
        o_ref[:, IN_HALF:] = jnp.dot(h_scr[...], wt_ref[...], preferred_element_type=F32)


def _inproj(x2d, pre_w, w_in, layer, w_tail):
    m = x2d.shape[0]
    src = jnp.asarray(_CHUNK_SRC, jnp.int32)
    w_spec = lambda half: pl.BlockSpec(
        (None, pl.Element(D_MODEL), pl.Element(IN_HALF)),
        lambda i, j, src_ref: (layer, 0, pl.multiple_of(src_ref[2 * j + half], 128)))
    grid_spec = pltpu.PrefetchScalarGridSpec(
        num_scalar_prefetch=1,
        grid=(m // IN_TM, P_WIDTH // IN_TN),
        in_specs=[
            pl.BlockSpec((IN_TM, D_MODEL), lambda i, j, s: (i, 0)),
            pl.BlockSpec((1, D_MODEL), lambda i, j, s: (0, 0)),
            w_spec(0), w_spec(1),
            pl.BlockSpec((D_MODEL, IN_HALF), lambda i, j, s: (0, 0)),
        ],
        out_specs=pl.BlockSpec((IN_TM, IN_TN), lambda i, j, s: (i, j)),
        scratch_shapes=[pltpu.VMEM((IN_TM, D_MODEL), BF16)],
    )
    return pl.pallas_call(
        _inproj_kernel,
        grid_spec=grid_spec,
        out_shape=jax.ShapeDtypeStruct((m, P_WIDTH), F32),
        compiler_params=pltpu.CompilerParams(
            dimension_semantics=("parallel", "arbitrary"), vmem_limit_bytes=_VMEM_LIMIT),
        name="inproj",
    )(src, x2d, pre_w, w_in, w_in, w_tail)


def _bdot(a, b):
    return jnp.dot(a.astype(BF16), b.astype(BF16), preferred_element_type=F32)


def _bdot_nt(a, b):
    return lax.dot_general(a.astype(BF16), b.astype(BF16), (((1,), (1,)), ((), ())),
                           preferred_element_type=F32)


def _bdot_tn(a, b):
    return lax.dot_general(a.astype(BF16), b.astype(BF16), (((0,), (0,)), ((), ())),
                           preferred_element_type=F32)


def _split3(x):
    x1 = x.astype(BF16)
    r1 = x - x1.astype(F32)
    x2 = r1.astype(BF16)
    x3 = (r1 - x2.astype(F32)).astype(BF16)
    return x1, x2, x3


def _head_sums(xs, bd):
    rows = xs[0].shape[0]
    half = bd.shape[0]
    x = jnp.concatenate(xs, axis=0).astype(BF16)
    out = jnp.concatenate(
        [jnp.dot(x[:, :half], bd, preferred_element_type=F32),
         jnp.dot(x[:, half:], bd, preferred_element_type=F32)], axis=1)
    return [out[i * rows:(i + 1) * rows] for i in range(len(xs))]


def _mm(a16, b16):
    return jnp.dot(a16, b16, preferred_element_type=F32)


def _mm_nt(a16, b16):
    return lax.dot_general(a16, b16, (((1,), (1,)), ((), ())), preferred_element_type=F32)


def _block_diag(m16, bdm16):
    return jnp.concatenate([m16, m16], axis=0) * bdm16


def _unit_lower_inverse_many(lmats, eye, masks, bdm16):
    base_mask, level_masks = masks
    c = lmats[0].shape[0]
    bd_of = lambda m: _block_diag(m.astype(BF16), bdm16)
    lds = [jnp.where(base_mask, l, 0.0) for l in lmats]
    l2s = [_mm(ld.astype(BF16), bd_of(ld)) for ld in lds]
    xs = [eye + ld for ld in lds]
    both = [_mm(jnp.concatenate([x, l2], axis=0).astype(BF16), bd_of(l2))
            for x, l2 in zip(xs, l2s)]
    xs = [x + b[:c] for x, b in zip(xs, both)]
    xs = [x + _mm(x.astype(BF16), bd_of(b[c:])) for x, b in zip(xs, both)]
    for lm in level_masks:
        ts = [_mm(x.astype(BF16), bd_of(jnp.where(lm, l, 0.0))) for x, l in zip(xs, lmats)]
        xs = [x + _mm(t.astype(BF16), bd_of(x)) for x, t in zip(xs, ts)]
    return xs


def _rwkv_kernel(r_ref, k_ref, v_ref, lo_ref, mur_ref, muk_ref, muv_ref, mulo_ref,
                 w0_ref, wup_ref, a0_ref, aup_ref, kkw_ref, kaw_ref, rkw_ref, lnw_ref, lnb_ref,
                 bd_ref, o_ref, s_scr, pr_scr, pk_scr, pv_scr, plo_scr):
    n = pl.program_id(1)
    tile = r_ref.shape[0]
    c = RW_CHUNK
    n_chunks = tile // c

    @pl.when(n == 0)
    def _():
        s_scr[...] = jnp.zeros_like(s_scr)
        pr_scr[...] = jnp.zeros_like(pr_scr)
        pk_scr[...] = jnp.zeros_like(pk_scr)
        pv_scr[...] = jnp.zeros_like(pv_scr)
        plo_scr[...] = jnp.zeros_like(plo_scr)

    row = lax.broadcasted_iota(jnp.int32, (tile, 1), 0)

    def shifted(z_ref, mu_ref, p_scr):
        z = z_ref[...]
        zp = jnp.where(row == 0, p_scr[0:1, :], pltpu.roll(z, 1, axis=0))
        p_scr[0:1, :] = z[tile - 1:tile, :]
        return z + mu_ref[...] * (zp - z)

    r = shifted(r_ref, mur_ref, pr_scr)
    k = shifted(k_ref, muk_ref, pk_scr)
    v = shifted(v_ref, muv_ref, pv_scr)
    lo = shifted(lo_ref, mulo_ref, plo_scr)

    bd = bd_ref[...]
    w_log = -jax.nn.softplus(-(w0_ref[...] + _bdot(jnp.tanh(lo[:, :LORA]), wup_ref[...]))) - 0.5
    logw = -jnp.exp(w_log)
    asig = jax.nn.sigmoid(a0_ref[...] + _bdot(lo[:, LORA:], aup_ref[...]))
    kk = k * kkw_ref[...]
    k2 = k * (1.0 + (asig - 1.0) * kaw_ref[...])
    kk_ss, rk_sum = _head_sums([kk * kk, r * k2 * rkw_ref[...]], bd)
    kk = kk * lax.rsqrt(jnp.maximum(kk_ss, 1e-24))
    bonus = rk_sum * v

    pw = 2 * HEAD
    ci = lax.broadcasted_iota(jnp.int32, (c, pw), 0)
    cj = lax.broadcasted_iota(jnp.int32, (c, pw), 1) % HEAD
    eye = (ci == cj).astype(F32)
    same = lambda b: (ci // b) == (cj // b)
    masks = (same(8), tuple(same(2 * b) & ~same(b) for b in (8, 16, 32)))
    si = lax.broadcasted_iota(jnp.int32, (2 * c, pw), 0)
    sj = lax.broadcasted_iota(jnp.int32, (2 * c, pw), 1) % HEAD
    tri2 = ((si < c) & (si > sj)) | ((si >= c) & ((si - c) >= sj))
    bi = lax.broadcasted_iota(jnp.int32, (2 * pw, pw), 0)
    bj = lax.broadcasted_iota(jnp.int32, (2 * pw, pw), 1)
    bdm_bool = ((bi // HEAD) % 2) == (bj // HEAD)
    bdm4_16 = bdm_bool.astype(BF16)
    bdm16 = bdm4_16[:pw]
    bdm = bdm_bool[:pw]

    ti = lax.broadcasted_iota(jnp.int32, (tile, tile), 0)
    tj = lax.broadcasted_iota(jnp.int32, (tile, tile), 1)
    tril = ((ti >= tj) & ((ti // c) == (tj // c))).astype(BF16)
    cum = sum(jnp.dot(tril, p, preferred_element_type=F32) for p in _split3(logw))

    n_pairs = N_HEADS // 2
    items = [(ck, p) for ck in range(n_chunks) for p in range(n_pairs)]
    per_chunk = []
    for ck in range(n_chunks):
        sl = slice(ck * c, (ck + 1) * c)
        lw, cm = logw[sl], cum[sl]
        total = cm[c - 1:c, :]
        g_inv = jnp.exp(-cm)
        g_all = jnp.exp(total)
        g_tail = g_all * g_inv
        b_c = kk[sl] * asig[sl]
        per_chunk.append(dict(
            a=(-kk[sl] * jnp.exp(cm - lw)).astype(BF16), b=(b_c * g_inv).astype(BF16),
            k=(k2[sl] * g_inv).astype(BF16), r=(r[sl] * jnp.exp(cm)).astype(BF16),
            v=v[sl].astype(BF16), b_e=(b_c * g_tail).astype(BF16),
            k_e=(k2[sl] * g_tail).astype(BF16), g_all=g_all))

    def part(name, it):
        ck, p = it
        return per_chunk[ck][name][:, p * pw:(p + 1) * pw]

    a2 = [part("a", it) for it in items]
    r2 = [part("r", it) for it in items]
    ar2 = [jnp.concatenate([a, x], axis=0) for a, x in zip(a2, r2)]
    v_bd = [_block_diag(part("v", it), bdm16) for it in items]
    prods = []
    for x, it in zip(ar2, items):
        b, k_ = part("b", it), part("k", it)
        prods.append(_mm_nt(x, jnp.concatenate([b, b, k_, k_], axis=0) * bdm4_16))
    ab = [jnp.where(tri2, pr[:, :pw], 0.0) for pr in prods]
    ak16 = [jnp.where(tri2, pr[:, pw:], 0.0).astype(BF16) for pr in prods]
    qy = [_mm(x, vb) for x, vb in zip(ak16, v_bd)]
    vk = [jnp.where(bdm, lax.dot_general(part("v", it), part("k_e", it), (((0,), (0,)), ((), ())),
                                         preferred_element_type=F32), 0.0) for it in items]
    tinv = _unit_lower_inverse_many([x[:c] for x in ab], eye, masks, bdm16)
    wu = [_mm(t.astype(BF16),
              jnp.concatenate([_block_diag(a, bdm16), _block_diag(q[:c].astype(BF16), bdm16)], axis=1))
          for t, a, q in zip(tinv, a2, qy)]
    wa16 = [x[:, :pw].astype(BF16) for x in wu]
    uv = [x[:, pw:] for x in wu]
    uvt = [x.T for x in uv]
    war = [jnp.concatenate([w, x], axis=0) for w, x in zip(wa16, r2)]
    m_rb16 = [x[c:].astype(BF16) for x in ab]

    state = [s_scr[p] for p in range(n_pairs)]
    for ck in range(n_chunks):
        idx = [ck * n_pairs + p for p in range(n_pairs)]
        s16 = [s.astype(BF16) for s in state]
        uy = [_mm_nt(war[i], s16[p]) for p, i in enumerate(idx)]
        ut = [_mm_nt(s16[p], wa16[i]) + uvt[i] for p, i in enumerate(idx)]
        for p, i in enumerate(idx):
            u16 = (uy[p][:c] + uv[i]).astype(BF16)
            y = uy[p][c:] + qy[i][c:] + _mm(m_rb16[i], _block_diag(u16, bdm16))
            o_ref[ck * c:(ck + 1) * c, p * pw:(p + 1) * pw] = y
        state = [state[p] * part("g_all", items[i]) + vk[i]
                 + jnp.where(bdm, _mm(ut[p].astype(BF16), part("b_e", items[i])), 0.0)
                 for p, i in enumerate(idx)]
    for p in range(n_pairs):
        s_scr[p] = state[p]

    y = o_ref[...]
    mu = _head_sums([y], bd)[0] * (1.0 / HEAD)
    d = y - mu
    var = _head_sums([d * d], bd)[0] * (1.0 / HEAD)
    o_ref[...] = d * lax.rsqrt(var + GN_EPS) * lnw_ref[...] + lnb_ref[...] + bonus


def _rwkv(proj, vecs, wup, aup, bd, batch, seq):
    m = proj.shape[0]
    nt = seq // RW_TILE
    row = lambda b, n: b * nt + n
    wide = lambda cb: pl.BlockSpec((RW_TILE, D_BRANCH), lambda b, n: (row(b, n), cb))
    vec = lambda w: pl.BlockSpec((1, w), lambda b, n: (0, 0))
    full = lambda a: pl.BlockSpec(a.shape, lambda b, n: (0, 0))
    in_specs = [
        wide(P_R // D_BRANCH), wide(P_K // D_BRANCH), wide(P_V // D_BRANCH),
        pl.BlockSpec((RW_TILE, 2 * LORA), lambda b, n: (row(b, n), P_LORA // (2 * LORA))),
        vec(D_BRANCH), vec(D_BRANCH), vec(D_BRANCH), vec(2 * LORA),
        vec(D_BRANCH), full(wup), vec(D_BRANCH), full(aup),
        vec(D_BRANCH), vec(D_BRANCH), vec(D_BRANCH), vec(D_BRANCH), vec(D_BRANCH),
        full(bd),
    ]
    return pl.pallas_call(
        _rwkv_kernel,
        grid=(batch, nt),
        in_specs=in_specs,
        out_specs=pl.BlockSpec((RW_TILE, D_BRANCH), lambda b, n: (row(b, n), 0)),
        out_shape=jax.ShapeDtypeStruct((m, D_BRANCH), F32),
        scratch_shapes=[
            pltpu.VMEM((N_HEADS // 2, 2 * HEAD, 2 * HEAD), F32),
            pltpu.VMEM((8, D_BRANCH), F32), pltpu.VMEM((8, D_BRANCH), F32),
            pltpu.VMEM((8, D_BRANCH), F32), pltpu.VMEM((8, 2 * LORA), F32),
        ],
        compiler_params=pltpu.CompilerParams(
            dimension_semantics=("parallel", "arbitrary"), vmem_limit_bytes=_VMEM_LIMIT),
        name="rwkv",
    )(proj, proj, proj, proj, *vecs[:4], vecs[4], wup, vecs[5], aup, *vecs[6:], bd)


_SLOPES = tuple(2.0 ** (-8.0 * (h + 1) / N_HEADS) for h in range(N_HEADS))
_SQRT_HALF = float(np.sqrt(0.5))


def _attention_bias():
    t = np.arange(BLK)[:, None]
    s = np.arange(2 * BLK)[None, :]
    dist = t + BLK - s
    out = np.empty((2, N_HEADS * BLK, 2 * BLK), np.float32)
    for first in (0, 1):
        valid = (dist >= 0) & (dist < BLK) & ((s >= BLK) | (first == 1))
        for h in range(N_HEADS):
            out[first, h * BLK:(h + 1) * BLK] = np.where(valid, -_SLOPES[h] * dist, NEG_INF)
    return out


def _mixers_kernel(sinks_ref, bias_ref, q_ref, kvc_ref, kvp_ref, zc_ref, zcp_ref, zd_ref,
                   poolw_ref, pscale_ref, nw_ref, sw_ref, sb_ref, o_ref):
    n = pl.program_id(1)

    q = q_ref[...] * (HEAD ** -0.5)
    kv = jnp.concatenate([kvp_ref[...], kvc_ref[...]], axis=0)
    rows = ATT_GROUP * BLK
    head_in_group = lax.broadcasted_iota(jnp.int32, (rows, 1), 0) // BLK

    def per_row(values):
        col = jnp.full((rows, 1), values[-1], F32)
        for j in range(ATT_GROUP - 2, -1, -1):
            col = jnp.where(head_in_group == j, values[j], col)
        return col

    scores = []
    for g in range(KV_HEADS):
        qg = jnp.concatenate(
            [q[:, (g * ATT_GROUP + j) * HEAD:(g * ATT_GROUP + j + 1) * HEAD]
             for j in range(ATT_GROUP)], axis=0)
        scores.append(_bdot_nt(qg, kv[:, g * HEAD:(g + 1) * HEAD]))
    probs, dens = [], []
    for g in range(KV_HEADS):
        heads = range(g * ATT_GROUP, (g + 1) * ATT_GROUP)
        sink = per_row([sinks_ref[h] for h in heads])
        s = scores[g] + bias_ref[g * rows:(g + 1) * rows, :]
        mx = jnp.maximum(jnp.max(s, axis=-1, keepdims=True), sink)
        p = jnp.exp(s - mx)
        probs.append(p)
        dens.append(jnp.sum(p, axis=-1, keepdims=True) + jnp.exp(sink - mx))
    for g in range(KV_HEADS):
        vg = kv[:, (KV_HEADS + g) * HEAD:(KV_HEADS + g + 1) * HEAD]
        og = _bdot(probs[g], vg) / dens[g]
        for j in range(ATT_GROUP):
            h = g * ATT_GROUP + j
            o_ref[:, h * HEAD:(h + 1) * HEAD] = og[j * BLK:(j + 1) * BLK, :]

    zfull = jnp.concatenate([jnp.where(n > 0, zcp_ref[...], 0.0), zc_ref[...]], axis=0)
    pos = n * BLK + lax.broadcasted_iota(jnp.int32, (BLK, 1), 0) + 1
    for g, w in enumerate(POOL_WINDOWS):
        gs = slice(g * BLK, (g + 1) * BLK)
        zg = zfull[:, gs]
        acc = zg
        step = 1
        while step < w:
            acc = acc + pltpu.roll(acc, step, axis=0)
            step *= 2
        cnt = jnp.minimum(pos, w).astype(F32)
        pooled = acc[POOL_HALO:, :] / cnt - zg[POOL_HALO:, :]
        yg = _bdot(pooled, poolw_ref[g]) * pscale_ref[:, gs]
        o_ref[:, D_BRANCH + g * BLK:D_BRANCH + (g + 1) * BLK] = yg

    zd = zd_ref[...]
    gz = 0.5 * zd * (1.0 + lax.erf(zd * _SQRT_HALF))
    u = gz[:, :D_BRANCH]
    vv = gz[:, D_BRANCH:]
    mu = jnp.mean(vv, axis=-1, keepdims=True)
    dv = vv - mu
    var = jnp.mean(dv * dv, axis=-1, keepdims=True)
    vn = dv * lax.rsqrt(var + LN_EPS) * nw_ref[...]
    ri = lax.broadcasted_iota(jnp.int32, (BLK, BLK), 0)
    rj = lax.broadcasted_iota(jnp.int32, (BLK, BLK), 1)
    causal = ri >= rj
    for g in range(4):
        gs = slice(g * BLK, (g + 1) * BLK)
        ws = jnp.where(causal, sw_ref[g], 0.0)
        sg = _bdot(ws, vn[:, gs]) + sb_ref[:, gs]
        o_ref[:, 2 * D_BRANCH + g * BLK:2 * D_BRANCH + (g + 1) * BLK] = u[:, gs] * sg


def _mixers(proj, sinks, pool_w, pool_scale, norm_w, sgu_w, sgu_bias, batch, seq):
    m = proj.shape[0]
    nb = seq // BLK
    row = lambda b, n: b * nb + n
    halo = BLK // POOL_HALO
    bias = jnp.asarray(_attention_bias())
    in_specs = [
        pl.BlockSpec(memory_space=pltpu.SMEM),
        pl.BlockSpec((None,) + bias.shape[1:], lambda b, n: (jnp.minimum(n, 1), 0, 0)),
        pl.BlockSpec((BLK, D_BRANCH), lambda b, n: (row(b, n), P_Q // D_BRANCH)),
        pl.BlockSpec((BLK, 256), lambda b, n: (row(b, n), P_KV // 256)),
        pl.BlockSpec((BLK, 256), lambda b, n: (row(b, jnp.maximum(n - 1, 0)), P_KV // 256)),
        pl.BlockSpec((BLK, D_BRANCH), lambda b, n: (row(b, n), P_POOL // D_BRANCH)),
        pl.BlockSpec((POOL_HALO, D_BRANCH),
                     lambda b, n: (jnp.maximum(row(b, n) * halo - 1, 0), P_POOL // D_BRANCH)),
        pl.BlockSpec((BLK, 2 * D_BRANCH), lambda b, n: (row(b, n), P_SGU // (2 * D_BRANCH))),
        pl.BlockSpec((4, BLK, BLK), lambda b, n: (0, 0, 0)),
        pl.BlockSpec((1, D_BRANCH), lambda b, n: (0, 0)),
        pl.BlockSpec((1, D_BRANCH), lambda b, n: (0, 0)),
        pl.BlockSpec((4, BLK, BLK), lambda b, n: (0, 0, 0)),
        pl.BlockSpec((BLK, D_BRANCH), lambda b, n: (0, 0)),
    ]
    return pl.pallas_call(
        _mixers_kernel,
        grid=(batch, nb),
        in_specs=in_specs,
        out_specs=pl.BlockSpec((BLK, 3 * D_BRANCH), lambda b, n: (row(b, n), 0)),
        out_shape=jax.ShapeDtypeStruct((m, 3 * D_BRANCH), F32),
        compiler_params=pltpu.CompilerParams(
            dimension_semantics=("parallel", "parallel"), vmem_limit_bytes=_VMEM_LIMIT),
        name="mixers",
    )(sinks, bias, proj, proj, proj, proj, proj, proj, pool_w, pool_scale, norm_w, sgu_w,
      sgu_bias)


OUT_TM = 512


def _outproj_kernel(ya_ref, yb_ref, g_ref, x_ref, w_ref, pw_ref, o_ref):
    g = g_ref[...]
    sg = g * jax.nn.sigmoid(g)
    acc = jnp.dot((ya_ref[...] * sg[:, :D_BRANCH]).astype(BF16), w_ref[:D_BRANCH, :],
                  preferred_element_type=F32)
    acc = acc + jnp.dot((yb_ref[...] * sg[:, D_BRANCH:]).astype(BF16), w_ref[D_BRANCH:, :],
                        preferred_element_type=F32)
    ms = jnp.mean(acc * acc, axis=-1, keepdims=True)
    o_ref[...] = x_ref[...] + acc * lax.rsqrt(ms + NORM_EPS) * pw_ref[...]


def _outproj(ya, ybcd, proj, x2d, w_out16, layer, post_w):
    m = x2d.shape[0]
    return pl.pallas_call(
        _outproj_kernel,
        grid=(m // OUT_TM,),
        in_specs=[
            pl.BlockSpec((OUT_TM, D_BRANCH), lambda i: (i, 0)),
            pl.BlockSpec((OUT_TM, 3 * D_BRANCH), lambda i: (i, 0)),
            pl.BlockSpec((OUT_TM, D_MODEL), lambda i: (i, P_GATE // D_MODEL)),
            pl.BlockSpec((OUT_TM, D_MODEL), lambda i: (i, 0)),
            pl.BlockSpec((None, D_MODEL, D_MODEL), lambda i: (layer, 0, 0)),
            pl.BlockSpec((1, D_MODEL), lambda i: (0, 0)),
        ],
        out_specs=pl.BlockSpec((OUT_TM, D_MODEL), lambda i: (i, 0)),
        out_shape=jax.ShapeDtypeStruct((m, D_MODEL), F32),
        compiler_params=pltpu.CompilerParams(
            dimension_semantics=("parallel",), vmem_limit_bytes=_VMEM_LIMIT),
        name="outproj",
    )(ya, ybcd, proj, x2d, w_out16, post_w)


def kernel(x, pre_norm_w, post_norm_w, w_in, shift_mu, rwkv_w0, rwkv_w_up, rwkv_a0, rwkv_a_up,
           rwkv_k_k, rwkv_k_a, rwkv_r_k, rwkv_ln_w, rwkv_ln_b, attn_sinks, pool_w, pool_scale,
           sgu_norm_w, sgu_w, sgu_b, w_out):
    batch, seq, _ = x.shape
    assert x.shape == (batch, seq, D_MODEL) and seq % BLK == 0 and seq % RW_TILE == 0
    m = batch * seq
    head_id = np.arange(D_BRANCH // 2) // HEAD
    bd = jnp.asarray((head_id[:, None] == head_id[None, :]).astype(np.float32), dtype=BF16)
    row_vec = lambda a: a.reshape(1, -1)

    w_in16 = w_in.astype(BF16)
    w_out16 = w_out.astype(BF16)
    h = x.reshape(m, D_MODEL)
    for l in range(DEPTH):
        proj = _inproj(h, row_vec(pre_norm_w[l]), w_in16, l, _tail_weight(w_in16, l))

        mu = shift_mu[l]
        vecs = [row_vec(mu[0:D_BRANCH]), row_vec(mu[D_BRANCH:2 * D_BRANCH]),
                row_vec(mu[2 * D_BRANCH:3 * D_BRANCH]), row_vec(mu[3 * D_BRANCH:]),
                row_vec(rwkv_w0[l]), row_vec(rwkv_a0[l]), row_vec(rwkv_k_k[l]),
                row_vec(rwkv_k_a[l]), row_vec(rwkv_r_k[l]), row_vec(rwkv_ln_w[l]),
                row_vec(rwkv_ln_b[l])]
        ya = _rwkv(proj, vecs, rwkv_w_up[l], rwkv_a_up[l], bd, batch, seq)

        sgu_bias = jnp.broadcast_to(sgu_b[l].T[:, :, None], (BLK, 4, BLK)).reshape(BLK, D_BRANCH)
        ybcd = _mixers(proj, attn_sinks[l], pool_w[l], row_vec(pool_scale[l]),
                       row_vec(sgu_norm_w[l]), sgu_w[l], sgu_bias, batch, seq)

        h = _outproj(ya, ybcd, proj, h, w_out16, l, row_vec(post_norm_w[l]))
    return h.reshape(batch, seq, D_MODEL)
```

```python
import functools

import jax
import jax.numpy as jnp
import numpy as np
from jax import lax
from jax.experimental import pallas as pl
from jax.experimental.pallas import tpu as pltpu

F32 = jnp.float32
BF16 = jnp.bfloat16
ACT = BF16

D_MODEL = 2048
DEPTH = 2
D_BRANCH = 512
HEAD = 64
N_HEADS = D_BRANCH // HEAD
LORA = 64
KV_HEADS = 2
ATT_GROUP = N_HEADS // KV_HEADS
BLK = 128
POOL_WINDOWS = (2, 4, 8, 16)
POOL_HALO = 16
NEG_INF = -1e30
NORM_EPS = 1e-6
LN_EPS = 1e-5
GN_EPS = 64e-5

A_COLS = 3 * D_BRANCH + 2 * LORA
B_COLS = D_BRANCH + 2 * KV_HEADS * HEAD
OFF_B = A_COLS
OFF_C = OFF_B + B_COLS
OFF_D = OFF_C + D_BRANCH
OFF_G = OFF_D + 2 * D_BRANCH
D_IN = OFF_G + D_MODEL

P_GATE = 0
P_SGU = 2048
P_POOL = 3072
P_Q = 3584
P_R = 4096
P_K = 4608
P_V = 5120
P_KV = 5632
P_LORA = 5888
P_WIDTH = 6144

RW_CHUNK = 64
RW_TILE = 256

_VMEM_LIMIT = 56 * 1024 * 1024


def _tail_weight(w16, layer):
    seg = lambda start, n: w16[layer, :, start:start + n]
    parts = [
        seg(OFF_B + D_BRANCH, 2 * KV_HEADS * HEAD),
        seg(3 * D_BRANCH, 2 * LORA),
        jnp.zeros((D_MODEL, P_WIDTH - D_IN), w16.dtype),
    ]
    return jnp.concatenate(parts, axis=1)


IN_TM = 1024
IN_TN = 1024


IN_HALF = IN_TN // 2
_CHUNK_SRC = (OFF_G, OFF_G + 512, OFF_G + 1024, OFF_G + 1536, OFF_D, OFF_D + 512, OFF_C, OFF_B,
              0, 512, 1024, 0)
assert len(_CHUNK_SRC) * IN_HALF == P_WIDTH


def _inproj_kernel(src_ref, x_ref, pw_ref, wa_ref, wb_ref, wt_ref, o_ref, h_scr):
    j = pl.program_id(1)
    last = pl.num_programs(1) - 1

    @pl.when(j == 0)
    def _():
        for r0 in range(0, IN_TM, 256):
            x = x_ref[r0:r0 + 256, :]
            ms = jnp.mean(x * x, axis=-1, keepdims=True)
            h_scr[r0:r0 + 256, :] = (x * lax.rsqrt(ms + NORM_EPS) * pw_ref[...]).astype(BF16)

    def project(w_ref):
        return jnp.dot(h_scr[...], w_ref[...], preferred_element_type=F32).astype(o_ref.dtype)

    o_ref[:, :IN_HALF] = project(wa_ref)

    @pl.when(j < last)
    def _():
        o_ref[:, IN_HALF:] = project(wb_ref)

    @pl.when(j == last)
    def _():
        o_ref[:, IN_HALF:] = project(wt_ref)


def _inproj(x2d, pre_w, w_in, layer, w_tail):
    m = x2d.shape[0]
    src = jnp.asarray(_CHUNK_SRC, jnp.int32)
    w_spec = lambda half: pl.BlockSpec(
        (None, pl.Element(D_MODEL), pl.Element(IN_HALF)),
        lambda i, j, src_ref: (layer, 0, pl.multiple_of(src_ref[2 * j + half], 128)))
    grid_spec = pltpu.PrefetchScalarGridSpec(
        num_scalar_prefetch=1,
        grid=(m // IN_TM, P_WIDTH // IN_TN),
        in_specs=[
            pl.BlockSpec((IN_TM, D_MODEL), lambda i, j, s: (i, 0)),
            pl.BlockSpec((1, D_MODEL), lambda i, j, s: (0, 0)),
            w_spec(0), w_spec(1),
            pl.BlockSpec((D_MODEL, IN_HALF), lambda i, j, s: (0, 0)),
        ],
        out_specs=pl.BlockSpec((IN_TM, IN_TN), lambda i, j, s: (i, j)),
        scratch_shapes=[pltpu.VMEM((IN_TM, D_MODEL), BF16)],
    )
    return pl.pallas_call(
        _inproj_kernel,
        grid_spec=grid_spec,
        out_shape=jax.ShapeDtypeStruct((m, P_WIDTH), ACT),
        compiler_params=pltpu.CompilerParams(
            dimension_semantics=("parallel", "arbitrary"), vmem_limit_bytes=_VMEM_LIMIT),
        name="inproj",
    )(src, x2d, pre_w, w_in, w_in, w_tail)


def _bdot(a, b):
    return jnp.dot(a.astype(BF16), b.astype(BF16), preferred_element_type=F32)


def _bdot_nt(a, b):
    return lax.dot_general(a.astype(BF16), b.astype(BF16), (((1,), (1,)), ((), ())),
                           preferred_element_type=F32)


def _bdot_tn(a, b):
    return lax.dot_general(a.astype(BF16), b.astype(BF16), (((0,), (0,)), ((), ())),
                           preferred_element_type=F32)


def _split3(x):
    x1 = x.astype(BF16)
    r1 = x - x1.astype(F32)
    x2 = r1.astype(BF16)
    x3 = (r1 - x2.astype(F32)).astype(BF16)
    return x1, x2, x3


def _head_sums(xs, bd):
    rows = xs[0].shape[0]
    half = bd.shape[0]
    x = jnp.concatenate(xs, axis=0).astype(BF16)
    out = jnp.concatenate(
        [jnp.dot(x[:, :half], bd, preferred_element_type=F32),
         jnp.dot(x[:, half:], bd, preferred_element_type=F32)], axis=1)
    return [out[i * rows:(i + 1) * rows] for i in range(len(xs))]


def _mm(a16, b16):
    return jnp.dot(a16, b16, preferred_element_type=F32)


def _mm_nt(a16, b16):
    return lax.dot_general(a16, b16, (((1,), (1,)), ((), ())), preferred_element_type=F32)


def _block_diag(m16, bdm16):
    return jnp.concatenate([m16, m16], axis=0) * bdm16


def _unit_lower_inverse_many(lmats, eye, masks, bdm16):
    base_mask, level_masks = masks
    c = lmats[0].shape[0]
    bd_of = lambda m: _block_diag(m.astype(BF16), bdm16)
    lds = [jnp.where(base_mask, l, 0.0) for l in lmats]
    l2s = [_mm(ld.astype(BF16), bd_of(ld)) for ld in lds]
    xs = [eye + ld for ld in lds]
    both = [_mm(jnp.concatenate([x, l2], axis=0).astype(BF16), bd_of(l2))
            for x, l2 in zip(xs, l2s)]
    xs = [x + b[:c] for x, b in zip(xs, both)]
    xs = [x + _mm(x.astype(BF16), bd_of(b[c:])) for x, b in zip(xs, both)]
    for lm in level_masks:
        ts = [_mm(x.astype(BF16), bd_of(jnp.where(lm, l, 0.0))) for x, l in zip(xs, lmats)]
        xs = [x + _mm(t.astype(BF16), bd_of(x)) for x, t in zip(xs, ts)]
    return xs


def _rwkv_kernel(r_ref, k_ref, v_ref, lo_ref, mur_ref, muk_ref, muv_ref, mulo_ref,
                 w0_ref, wup_ref, a0_ref, aup_ref, kkw_ref, kaw_ref, rkw_ref, lnw_ref, lnb_ref,
                 bd_ref, o_ref, s_scr, pr_scr, pk_scr, pv_scr, plo_scr, y_scr):
    n = pl.program_id(1)
    tile = r_ref.shape[0]
    c = RW_CHUNK
    n_chunks = tile // c

    @pl.when(n == 0)
    def _():
        s_scr[...] = jnp.zeros_like(s_scr)
        pr_scr[...] = jnp.zeros_like(pr_scr)
        pk_scr[...] = jnp.zeros_like(pk_scr)
        pv_scr[...] = jnp.zeros_like(pv_scr)
        plo_scr[...] = jnp.zeros_like(plo_scr)

    row = lax.broadcasted_iota(jnp.int32, (tile, 1), 0)

    def shifted(z_ref, mu_ref, p_scr):
        z = z_ref[...].astype(F32)
        zp = jnp.where(row == 0, p_scr[0:1, :], pltpu.roll(z, 1, axis=0))
        p_scr[0:1, :] = z[tile - 1:tile, :]
        return z + mu_ref[...] * (zp - z)

    r = shifted(r_ref, mur_ref, pr_scr)
    k = shifted(k_ref, muk_ref, pk_scr)
    v = shifted(v_ref, muv_ref, pv_scr)
    lo = shifted(lo_ref, mulo_ref, plo_scr)

    bd = bd_ref[...]
    w_log = -jax.nn.softplus(-(w0_ref[...] + _bdot(jnp.tanh(lo[:, :LORA]), wup_ref[...]))) - 0.5
    logw = -jnp.exp(w_log)
    asig = jax.nn.sigmoid(a0_ref[...] + _bdot(lo[:, LORA:], aup_ref[...]))
    kk = k * kkw_ref[...]
    k2 = k * (1.0 + (asig - 1.0) * kaw_ref[...])
    kk_ss, rk_sum = _head_sums([kk * kk, r * k2 * rkw_ref[...]], bd)
    kk = kk * lax.rsqrt(jnp.maximum(kk_ss, 1e-24))
    bonus = rk_sum * v

    pw = 2 * HEAD
    ci = lax.broadcasted_iota(jnp.int32, (c, pw), 0)
    cj = lax.broadcasted_iota(jnp.int32, (c, pw), 1) % HEAD
    eye = (ci == cj).astype(F32)
    same = lambda b: (ci // b) == (cj // b)
    masks = (same(8), tuple(same(2 * b) & ~same(b) for b in (8, 16, 32)))
    si = lax.broadcasted_iota(jnp.int32, (2 * c, pw), 0)
    sj = lax.broadcasted_iota(jnp.int32, (2 * c, pw), 1) % HEAD
    tri2 = ((si < c) & (si > sj)) | ((si >= c) & ((si - c) >= sj))
    bi = lax.broadcasted_iota(jnp.int32, (2 * pw, pw), 0)
    bj = lax.broadcasted_iota(jnp.int32, (2 * pw, pw), 1)
    bdm_bool = ((bi // HEAD) % 2) == (bj // HEAD)
    bdm4_16 = bdm_bool.astype(BF16)
    bdm16 = bdm4_16[:pw]
    bdm = bdm_bool[:pw]

    ti = lax.broadcasted_iota(jnp.int32, (tile, tile), 0)
    tj = lax.broadcasted_iota(jnp.int32, (tile, tile), 1)
    tril = ((ti >= tj) & ((ti // c) == (tj // c))).astype(BF16)
    cum = sum(jnp.dot(tril, p, preferred_element_type=F32) for p in _split3(logw))

    n_pairs = N_HEADS // 2
    items = [(ck, p) for ck in range(n_chunks) for p in range(n_pairs)]
    per_chunk = []
    for ck in range(n_chunks):
        sl = slice(ck * c, (ck + 1) * c)
        lw, cm = logw[sl], cum[sl]
        total = cm[c - 1:c, :]
        g_inv = jnp.exp(-cm)
        g_all = jnp.exp(total)
        g_tail = g_all * g_inv
        b_c = kk[sl] * asig[sl]
        per_chunk.append(dict(
            a=(-kk[sl] * jnp.exp(cm - lw)).astype(BF16), b=(b_c * g_inv).astype(BF16),
            k=(k2[sl] * g_inv).astype(BF16), r=(r[sl] * jnp.exp(cm)).astype(BF16),
            v=v[sl].astype(BF16), b_e=(b_c * g_tail).astype(BF16),
            k_e=(k2[sl] * g_tail).astype(BF16), g_all=g_all))

    def part(name, it):
        ck, p = it
        return per_chunk[ck][name][:, p * pw:(p + 1) * pw]

    a2 = [part("a", it) for it in items]
    r2 = [part("r", it) for it in items]
    ar2 = [jnp.concatenate([a, x], axis=0) for a, x in zip(a2, r2)]
    v_bd = [_block_diag(part("v", it), bdm16) for it in items]
    prods = []
    for x, it in zip(ar2, items):
        b, k_ = part("b", it), part("k", it)
        prods.append(_mm_nt(x, jnp.concatenate([b, b, k_, k_], axis=0) * bdm4_16))
    ab = [jnp.where(tri2, pr[:, :pw], 0.0) for pr in prods]
    ak16 = [jnp.where(tri2, pr[:, pw:], 0.0).astype(BF16) for pr in prods]
    qy = [_mm(x, vb) for x, vb in zip(ak16, v_bd)]
    vk = [jnp.where(bdm, lax.dot_general(part("v", it), part("k_e", it), (((0,), (0,)), ((), ())),
                                         preferred_element_type=F32), 0.0) for it in items]
    tinv = _unit_lower_inverse_many([x[:c] for x in ab], eye, masks, bdm16)
    wu = [_mm(t.astype(BF16),
              jnp.concatenate([_block_diag(a, bdm16), _block_diag(q[:c].astype(BF16), bdm16)], axis=1))
          for t, a, q in zip(tinv, a2, qy)]
    wa16 = [x[:, :pw].astype(BF16) for x in wu]
    uv = [x[:, pw:] for x in wu]
    uvt = [x.T for x in uv]
    war = [jnp.concatenate([w, x], axis=0) for w, x in zip(wa16, r2)]
    m_rb16 = [x[c:].astype(BF16) for x in ab]

    state = [s_scr[p] for p in range(n_pairs)]
    for ck in range(n_chunks):
        idx = [ck * n_pairs + p for p in range(n_pairs)]
        s16 = [s.astype(BF16) for s in state]
        uy = [_mm_nt(war[i], s16[p]) for p, i in enumerate(idx)]
        ut = [_mm_nt(s16[p], wa16[i]) + uvt[i] for p, i in enumerate(idx)]
        for p, i in enumerate(idx):
            u16 = (uy[p][:c] + uv[i]).astype(BF16)
            y = uy[p][c:] + qy[i][c:] + _mm(m_rb16[i], _block_diag(u16, bdm16))
            y_scr[ck * c:(ck + 1) * c, p * pw:(p + 1) * pw] = y
        state = [state[p] * part("g_all", items[i]) + vk[i]
                 + jnp.where(bdm, _mm(ut[p].astype(BF16), part("b_e", items[i])), 0.0)
                 for p, i in enumerate(idx)]
    for p in range(n_pairs):
        s_scr[p] = state[p]

    y = y_scr[...]
    mu = _head_sums([y], bd)[0] * (1.0 / HEAD)
    d = y - mu
    var = _head_sums([d * d], bd)[0] * (1.0 / HEAD)
    out = d * lax.rsqrt(var + GN_EPS) * lnw_ref[...] + lnb_ref[...] + bonus
    o_ref[...] = out.astype(o_ref.dtype)


def _rwkv(proj, vecs, wup, aup, bd, batch, seq):
    m = proj.shape[0]
    nt = seq // RW_TILE
    row = lambda b, n: b * nt + n
    wide = lambda cb: pl.BlockSpec((RW_TILE, D_BRANCH), lambda b, n: (row(b, n), cb))
    vec = lambda w: pl.BlockSpec((1, w), lambda b, n: (0, 0))
    full = lambda a: pl.BlockSpec(a.shape, lambda b, n: (0, 0))
    in_specs = [
        wide(P_R // D_BRANCH), wide(P_K // D_BRANCH), wide(P_V // D_BRANCH),
        pl.BlockSpec((RW_TILE, 2 * LORA), lambda b, n: (row(b, n), P_LORA // (2 * LORA))),
        vec(D_BRANCH), vec(D_BRANCH), vec(D_BRANCH), vec(2 * LORA),
        vec(D_BRANCH), full(wup), vec(D_BRANCH), full(aup),
        vec(D_BRANCH), vec(D_BRANCH), vec(D_BRANCH), vec(D_BRANCH), vec(D_BRANCH),
        full(bd),
    ]
    return pl.pallas_call(
        _rwkv_kernel,
        grid=(batch, nt),
        in_specs=in_specs,
        out_specs=pl.BlockSpec((RW_TILE, D_BRANCH), lambda b, n: (row(b, n), 0)),
        out_shape=jax.ShapeDtypeStruct((m, D_BRANCH), ACT),
        scratch_shapes=[
            pltpu.VMEM((N_HEADS // 2, 2 * HEAD, 2 * HEAD), F32),
            pltpu.VMEM((8, D_BRANCH), F32), pltpu.VMEM((8, D_BRANCH), F32),
            pltpu.VMEM((8, D_BRANCH), F32), pltpu.VMEM((8, 2 * LORA), F32),
            pltpu.VMEM((RW_TILE, D_BRANCH), F32),
        ],
        compiler_params=pltpu.CompilerParams(
            dimension_semantics=("parallel", "arbitrary"), vmem_limit_bytes=_VMEM_LIMIT),
        name="rwkv",
    )(proj, proj, proj, proj, *vecs[:4], vecs[4], wup, vecs[5], aup, *vecs[6:], bd)


_SLOPES = tuple(2.0 ** (-8.0 * (h + 1) / N_HEADS) for h in range(N_HEADS))
_SQRT_HALF = float(np.sqrt(0.5))


def _attention_bias():
    t = np.arange(BLK)[:, None]
    s = np.arange(2 * BLK)[None, :]
    dist = t + BLK - s
    out = np.empty((2, N_HEADS * BLK, 2 * BLK), np.float32)
    for first in (0, 1):
        valid = (dist >= 0) & (dist < BLK) & ((s >= BLK) | (first == 1))
        for h in range(N_HEADS):
            out[first, h * BLK:(h + 1) * BLK] = np.where(valid, -_SLOPES[h] * dist, NEG_INF)
    return out


def _mixers_kernel(sinks_ref, bias_ref, q_ref, kvc_ref, kvp_ref, zc_ref, zcp_ref, zd_ref,
                   poolw_ref, pscale_ref, nw_ref, sw_ref, sb_ref, o_ref, y_scr):
    n = pl.program_id(1)

    q = q_ref[...] * (HEAD ** -0.5)
    kv = jnp.concatenate([kvp_ref[...], kvc_ref[...]], axis=0)
    rows = ATT_GROUP * BLK
    head_in_group = lax.broadcasted_iota(jnp.int32, (rows, 1), 0) // BLK

    def per_row(values):
        col = jnp.full((rows, 1), values[-1], F32)
        for j in range(ATT_GROUP - 2, -1, -1):
            col = jnp.where(head_in_group == j, values[j], col)
        return col

    scores = []
    for g in range(KV_HEADS):
        qg = jnp.concatenate(
            [q[:, (g * ATT_GROUP + j) * HEAD:(g * ATT_GROUP + j + 1) * HEAD]
             for j in range(ATT_GROUP)], axis=0)
        scores.append(_bdot_nt(qg, kv[:, g * HEAD:(g + 1) * HEAD]))
    probs, dens = [], []
    for g in range(KV_HEADS):
        heads = range(g * ATT_GROUP, (g + 1) * ATT_GROUP)
        sink = per_row([sinks_ref[h] for h in heads])
        s = scores[g] + bias_ref[g * rows:(g + 1) * rows, :]
        mx = jnp.maximum(jnp.max(s, axis=-1, keepdims=True), sink)
        p = jnp.exp(s - mx)
        probs.append(p)
        dens.append(jnp.sum(p, axis=-1, keepdims=True) + jnp.exp(sink - mx))
    for g in range(KV_HEADS):
        vg = kv[:, (KV_HEADS + g) * HEAD:(KV_HEADS + g + 1) * HEAD]
        og = _bdot(probs[g], vg) / dens[g]
        for j in range(ATT_GROUP):
            h = g * ATT_GROUP + j
            y_scr[:, h * HEAD:(h + 1) * HEAD] = og[j * BLK:(j + 1) * BLK, :]

    zfull = jnp.concatenate([jnp.where(n > 0, zcp_ref[...], 0.0), zc_ref[...]],
                            axis=0).astype(F32)
    pos = n * BLK + lax.broadcasted_iota(jnp.int32, (BLK, 1), 0) + 1
    for g, w in enumerate(POOL_WINDOWS):
        gs = slice(g * BLK, (g + 1) * BLK)
        zg = zfull[:, gs]
        acc = zg
        step = 1
        while step < w:
            acc = acc + pltpu.roll(acc, step, axis=0)
            step *= 2
        cnt = jnp.minimum(pos, w).astype(F32)
        pooled = acc[POOL_HALO:, :] / cnt - zg[POOL_HALO:, :]
        yg = _bdot(pooled, poolw_ref[g]) * pscale_ref[:, gs]
        y_scr[:, D_BRANCH + g * BLK:D_BRANCH + (g + 1) * BLK] = yg

    zd = zd_ref[...].astype(F32)
    gz = 0.5 * zd * (1.0 + lax.erf(zd * _SQRT_HALF))
    u = gz[:, :D_BRANCH]
    vv = gz[:, D_BRANCH:]
    mu = jnp.mean(vv, axis=-1, keepdims=True)
    dv = vv - mu
    var = jnp.mean(dv * dv, axis=-1, keepdims=True)
    vn = dv * lax.rsqrt(var + LN_EPS) * nw_ref[...]
    ri = lax.broadcasted_iota(jnp.int32, (BLK, BLK), 0)
    rj = lax.broadcasted_iota(jnp.int32, (BLK, BLK), 1)
    causal = ri >= rj
    for g in range(4):
        gs = slice(g * BLK, (g + 1) * BLK)
        ws = jnp.where(causal, sw_ref[g], 0.0)
        sg = _bdot(ws, vn[:, gs]) + sb_ref[:, gs]
        y_scr[:, 2 * D_BRANCH + g * BLK:2 * D_BRANCH + (g + 1) * BLK] = u[:, gs] * sg
    o_ref[...] = y_scr[...].astype(o_ref.dtype)


def _mixers(proj, sinks, pool_w, pool_scale, norm_w, sgu_w, sgu_bias, batch, seq):
    m = proj.shape[0]
    nb = seq // BLK
    row = lambda b, n: b * nb + n
    halo = BLK // POOL_HALO
    bias = jnp.asarray(_attention_bias())
    in_specs = [
        pl.BlockSpec(memory_space=pltpu.SMEM),
        pl.BlockSpec((None,) + bias.shape[1:], lambda b, n: (jnp.minimum(n, 1), 0, 0)),
        pl.BlockSpec((BLK, D_BRANCH), lambda b, n: (row(b, n), P_Q // D_BRANCH)),
        pl.BlockSpec((BLK, 256), lambda b, n: (row(b, n), P_KV // 256)),
        pl.BlockSpec((BLK, 256), lambda b, n: (row(b, jnp.maximum(n - 1, 0)), P_KV // 256)),
        pl.BlockSpec((BLK, D_BRANCH), lambda b, n: (row(b, n), P_POOL // D_BRANCH)),
        pl.BlockSpec((POOL_HALO, D_BRANCH),
                     lambda b, n: (jnp.maximum(row(b, n) * halo - 1, 0), P_POOL // D_BRANCH)),
        pl.BlockSpec((BLK, 2 * D_BRANCH), lambda b, n: (row(b, n), P_SGU // (2 * D_BRANCH))),
        pl.BlockSpec((4, BLK, BLK), lambda b, n: (0, 0, 0)),
        pl.BlockSpec((1, D_BRANCH), lambda b, n: (0, 0)),
        pl.BlockSpec((1, D_BRANCH), lambda b, n: (0, 0)),
        pl.BlockSpec((4, BLK, BLK), lambda b, n: (0, 0, 0)),
        pl.BlockSpec((BLK, D_BRANCH), lambda b, n: (0, 0)),
    ]
    return pl.pallas_call(
        _mixers_kernel,
        grid=(batch, nb),
        in_specs=in_specs,
        out_specs=pl.BlockSpec((BLK, 3 * D_BRANCH), lambda b, n: (row(b, n), 0)),
        out_shape=jax.ShapeDtypeStruct((m, 3 * D_BRANCH), ACT),
        scratch_shapes=[pltpu.VMEM((BLK, 3 * D_BRANCH), F32)],
        compiler_params=pltpu.CompilerParams(
            dimension_semantics=("parallel", "parallel"), vmem_limit_bytes=_VMEM_LIMIT),
        name="mixers",
    )(sinks, bias, proj, proj, proj, proj, proj, proj, pool_w, pool_scale, norm_w, sgu_w,
      sgu_bias)


OUT_TM = 512


def _outproj_kernel(ya_ref, yb_ref, g_ref, x_ref, w_ref, pw_ref, o_ref):
    g = g_ref[...].astype(F32)
    sg = g * jax.nn.sigmoid(g)
    ya = ya_ref[...].astype(F32)
    yb = yb_ref[...].astype(F32)
    acc = jnp.dot((ya * sg[:, :D_BRANCH]).astype(BF16), w_ref[:D_BRANCH, :],
                  preferred_element_type=F32)
    acc = acc + jnp.dot((yb * sg[:, D_BRANCH:]).astype(BF16), w_ref[D_BRANCH:, :],
                        preferred_element_type=F32)
    ms = jnp.mean(acc * acc, axis=-1, keepdims=True)
    o_ref[...] = x_ref[...] + acc * lax.rsqrt(ms + NORM_EPS) * pw_ref[...]


def _outproj(ya, ybcd, proj, x2d, w_out16, layer, post_w):
    m = x2d.shape[0]
    return pl.pallas_call(
        _outproj_kernel,
        grid=(m // OUT_TM,),
        in_specs=[
            pl.BlockSpec((OUT_TM, D_BRANCH), lambda i: (i, 0)),
            pl.BlockSpec((OUT_TM, 3 * D_BRANCH), lambda i: (i, 0)),
            pl.BlockSpec((OUT_TM, D_MODEL), lambda i: (i, P_GATE // D_MODEL)),
            pl.BlockSpec((OUT_TM, D_MODEL), lambda i: (i, 0)),
            pl.BlockSpec((None, D_MODEL, D_MODEL), lambda i: (layer, 0, 0)),
            pl.BlockSpec((1, D_MODEL), lambda i: (0, 0)),
        ],
        out_specs=pl.BlockSpec((OUT_TM, D_MODEL), lambda i: (i, 0)),
        out_shape=jax.ShapeDtypeStruct((m, D_MODEL), F32),
        compiler_params=pltpu.CompilerParams(
            dimension_semantics=("parallel",), vmem_limit_bytes=_VMEM_LIMIT),
        name="outproj",
    )(ya, ybcd, proj, x2d, w_out16, post_w)


def kernel(x, pre_norm_w, post_norm_w, w_in, shift_mu, rwkv_w0, rwkv_w_up, rwkv_a0, rwkv_a_up,
           rwkv_k_k, rwkv_k_a, rwkv_r_k, rwkv_ln_w, rwkv_ln_b, attn_sinks, pool_w, pool_scale,
           sgu_norm_w, sgu_w, sgu_b, w_out):
    batch, seq, _ = x.shape
    assert x.shape == (batch, seq, D_MODEL) and seq % BLK == 0 and seq % RW_TILE == 0
    m = batch * seq
    head_id = np.arange(D_BRANCH // 2) // HEAD
    bd = jnp.asarray((head_id[:, None] == head_id[None, :]).astype(np.float32), dtype=BF16)
    row_vec = lambda a: a.reshape(1, -1)

    w_in16 = w_in.astype(BF16)
    w_out16 = w_out.astype(BF16)
    h = x.reshape(m, D_MODEL)
    for l in range(DEPTH):
        proj = _inproj(h, row_vec(pre_norm_w[l]), w_in16, l, _tail_weight(w_in16, l))

        mu = shift_mu[l]
        vecs = [row_vec(mu[0:D_BRANCH]), row_vec(mu[D_BRANCH:2 * D_BRANCH]),
                row_vec(mu[2 * D_BRANCH:3 * D_BRANCH]), row_vec(mu[3 * D_BRANCH:]),
                row_vec(rwkv_w0[l]), row_vec(rwkv_a0[l]), row_vec(rwkv_k_k[l]),
                row_vec(rwkv_k_a[l]), row_vec(rwkv_r_k[l]), row_vec(rwkv_ln_w[l]),
                row_vec(rwkv_ln_b[l])]
        ya = _rwkv(proj, vecs, rwkv_w_up[l], rwkv_a_up[l], bd, batch, seq)

        sgu_bias = jnp.broadcast_to(sgu_b[l].T[:, :, None], (BLK, 4, BLK)).reshape(BLK, D_BRANCH)
        ybcd = _mixers(proj, attn_sinks[l], pool_w[l], row_vec(pool_scale[l]),
                       row_vec(sgu_norm_w[l]), sgu_w[l], sgu_bias, batch, seq)

        h = _outproj(ya, ybcd, proj, h, w_out16, l, row_vec(post_norm_w[l]))
    return h.reshape(batch, seq, D_MODEL)
```

```python
import itertools

import jax
import jax.numpy as jnp
import numpy as np
from jax import lax
from jax.experimental import pallas as pl
from jax.experimental.pallas import tpu as pltpu

F32 = jnp.float32
BF16 = jnp.bfloat16
ACT = BF16

D_MODEL = 2048
DEPTH = 2
D_BRANCH = 512
HEAD = 64
N_HEADS = D_BRANCH // HEAD
LORA = 64
KV_HEADS = 2
ATT_GROUP = N_HEADS // KV_HEADS
BLK = 128
POOL_WINDOWS = (2, 4, 8, 16)
POOL_HALO = 16
NEG_INF = -1e30
NORM_EPS = 1e-6
LN_EPS = 1e-5
GN_EPS = 64e-5

A_COLS = 3 * D_BRANCH + 2 * LORA
B_COLS = D_BRANCH + 2 * KV_HEADS * HEAD
OFF_B = A_COLS
OFF_C = OFF_B + B_COLS
OFF_D = OFF_C + D_BRANCH
OFF_G = OFF_D + 2 * D_BRANCH
D_IN = OFF_G + D_MODEL

P_GATE = 0
P_SGU = 2048
P_POOL = 3072
P_Q = 3584
P_R = 4096
P_K = 4608
P_V = 5120
P_KV = 5632
P_LORA = 5888
P_WIDTH = 6144

RW_CHUNK = 64
RW_TILE = 256
RW_STEP = 2 * RW_TILE

_VMEM_LIMIT = 56 * 1024 * 1024


def _tail_weight(w16, layer):
    seg = lambda start, n: w16[layer, :, start:start + n]
    parts = [
        seg(OFF_B + D_BRANCH, 2 * KV_HEADS * HEAD),
        seg(3 * D_BRANCH, 2 * LORA),
        jnp.zeros((D_MODEL, P_WIDTH - D_IN), w16.dtype),
    ]
    return jnp.concatenate(parts, axis=1)


IN_TM = 1024
IN_TN = 1024
IN_HALF = IN_TN // 2
_CHUNK_SRC = (OFF_G, OFF_G + 512, OFF_G + 1024, OFF_G + 1536, OFF_D, OFF_D + 512, OFF_C, OFF_B,
              0, 512, 1024, 0)
assert len(_CHUNK_SRC) * IN_HALF == P_WIDTH


def _inproj_kernel(src_ref, x_ref, pw_ref, wa_ref, wb_ref, wt_ref, o_ref, h_scr):
    j = pl.program_id(1)
    last = pl.num_programs(1) - 1

    @pl.when(j == 0)
    def _():
        for r0 in range(0, IN_TM, 256):
            x = x_ref[r0:r0 + 256, :]
            ms = jnp.mean(x * x, axis=-1, keepdims=True)
            h_scr[r0:r0 + 256, :] = (x * lax.rsqrt(ms + NORM_EPS) * pw_ref[...]).astype(BF16)

    def project(w_ref):
        return jnp.dot(h_scr[...], w_ref[...], preferred_element_type=F32).astype(o_ref.dtype)

    o_ref[:, :IN_HALF] = project(wa_ref)

    @pl.when(j < last)
    def _():
        o_ref[:, IN_HALF:] = project(wb_ref)

    @pl.when(j == last)
    def _():
        o_ref[:, IN_HALF:] = project(wt_ref)


def _inproj(x2d, pre_w, w_in, layer, w_tail):
    m = x2d.shape[0]
    src = jnp.asarray(_CHUNK_SRC, jnp.int32)
    w_spec = lambda half: pl.BlockSpec(
        (None, pl.Element(D_MODEL), pl.Element(IN_HALF)),
        lambda i, j, src_ref: (layer, 0, pl.multiple_of(src_ref[2 * j + half], 128)))
    grid_spec = pltpu.PrefetchScalarGridSpec(
        num_scalar_prefetch=1,
        grid=(m // IN_TM, P_WIDTH // IN_TN),
        in_specs=[
            pl.BlockSpec((IN_TM, D_MODEL), lambda i, j, s: (i, 0)),
            pl.BlockSpec((1, D_MODEL), lambda i, j, s: (0, 0)),
            w_spec(0), w_spec(1),
            pl.BlockSpec((D_MODEL, IN_HALF), lambda i, j, s: (0, 0)),
        ],
        out_specs=pl.BlockSpec((IN_TM, IN_TN), lambda i, j, s: (i, j)),
        scratch_shapes=[pltpu.VMEM((IN_TM, D_MODEL), BF16)],
    )
    return pl.pallas_call(
        _inproj_kernel,
        grid_spec=grid_spec,
        out_shape=jax.ShapeDtypeStruct((m, P_WIDTH), ACT),
        compiler_params=pltpu.CompilerParams(
            dimension_semantics=("parallel", "arbitrary"), vmem_limit_bytes=_VMEM_LIMIT),
        name="inproj",
    )(src, x2d, pre_w, w_in, w_in, w_tail)


def _bdot(a, b):
    return jnp.dot(a.astype(BF16), b.astype(BF16), preferred_element_type=F32)


def _bdot_nt(a, b):
    return lax.dot_general(a.astype(BF16), b.astype(BF16), (((1,), (1,)), ((), ())),
                           preferred_element_type=F32)


def _split3(x):
    x1 = x.astype(BF16)
    r1 = x - x1.astype(F32)
    x2 = r1.astype(BF16)
    x3 = (r1 - x2.astype(F32)).astype(BF16)
    return x1, x2, x3


def _head_sums(xs, bd):
    rows = xs[0].shape[0]
    half = bd.shape[0]
    x = jnp.concatenate(xs, axis=0).astype(BF16)
    out = jnp.concatenate(
        [jnp.dot(x[:, :half], bd, preferred_element_type=F32),
         jnp.dot(x[:, half:], bd, preferred_element_type=F32)], axis=1)
    return [out[i * rows:(i + 1) * rows] for i in range(len(xs))]


def _mm(a16, b16):
    return jnp.dot(a16, b16, preferred_element_type=F32)


def _mm_nt(a16, b16):
    return lax.dot_general(a16, b16, (((1,), (1,)), ((), ())), preferred_element_type=F32)


def _block_diag(m16, bdm16):
    return jnp.concatenate([m16, m16], axis=0) * bdm16


def _unit_lower_inverse_many(lmats, eye, masks, bdm16):
    base_mask, level_masks = masks
    c = lmats[0].shape[0]
    bd_of = lambda m: _block_diag(m.astype(BF16), bdm16)
    lds = [jnp.where(base_mask, l, 0.0) for l in lmats]
    l2s = [_mm(ld.astype(BF16), bd_of(ld)) for ld in lds]
    yield
    xs = [eye + ld for ld in lds]
    both = [_mm(jnp.concatenate([x, l2], axis=0).astype(BF16), bd_of(l2))
            for x, l2 in zip(xs, l2s)]
    yield
    xs = [x + b[:c] for x, b in zip(xs, both)]
    xs = [x + _mm(x.astype(BF16), bd_of(b[c:])) for x, b in zip(xs, both)]
    yield
    for lm in level_masks:
        ts = [_mm(x.astype(BF16), bd_of(jnp.where(lm, l, 0.0))) for x, l in zip(xs, lmats)]
        yield
        xs = [x + _mm(t.astype(BF16), bd_of(x)) for x, t in zip(xs, ts)]
        yield
    return xs


_SET_BF16 = ("a", "b", "k", "r", "v", "b_e", "k_e")
_SET_NAMES = _SET_BF16 + ("g_all", "bonus")
_PREP_MXU_DELAY = 4


def _rwkv_prep(z_refs, row0, w, carries, dst):
    tile, c = RW_TILE, RW_CHUNK
    row = lax.broadcasted_iota(jnp.int32, (tile, 1), 0)

    def shifted(z_ref, mu_ref, p_scr):
        z = z_ref[row0:row0 + tile, :].astype(F32)
        zp = jnp.where(row == 0, p_scr[0:1, :], pltpu.roll(z, 1, axis=0))
        p_scr[0:1, :] = z[tile - 1:tile, :]
        return z + mu_ref[...] * (zp - z)

    lo = shifted(z_refs[3], w["mu_lo"], carries[3])
    lora_w = _bdot(jnp.tanh(lo[:, :LORA]), w["wup"][...])
    lora_a = _bdot(lo[:, LORA:], w["aup"][...])
    yield
    r = shifted(z_refs[0], w["mu_r"], carries[0])
    k = shifted(z_refs[1], w["mu_k"], carries[1])
    v = shifted(z_refs[2], w["mu_v"], carries[2])
    w_log = -jax.nn.softplus(-(w["w0"][...] + lora_w)) - 0.5
    logw = -jnp.exp(w_log)
    asig = jax.nn.sigmoid(w["a0"][...] + lora_a)
    logw_terms = _split3(logw)
    kk = k * w["kkw"][...]
    k2 = k * (1.0 + (asig - 1.0) * w["kaw"][...])
    sum_terms = [kk * kk, r * k2 * w["rkw"][...]]
    for _ in range(_PREP_MXU_DELAY):
        yield
    tril = w["tril"][...]
    cum = sum(jnp.dot(tril, p, preferred_element_type=F32) for p in logw_terms)
    kk_ss, rk_sum = _head_sums(sum_terms, w["bd"][...])
    yield
    kk = kk * lax.rsqrt(jnp.maximum(kk_ss, 1e-24))
    dst["bonus"][...] = rk_sum * v
    for ck in range(tile // c):
        sl = slice(ck * c, (ck + 1) * c)
        lw, cm = logw[sl], cum[sl]
        total = cm[c - 1:c, :]
        g_inv = jnp.exp(-cm)
        g_all = jnp.exp(total)
        g_tail = g_all * g_inv
        b_c = kk[sl] * asig[sl]
        dst["a"][sl, :] = (-kk[sl] * jnp.exp(cm - lw)).astype(BF16)
        dst["b"][sl, :] = (b_c * g_inv).astype(BF16)
        dst["k"][sl, :] = (k2[sl] * g_inv).astype(BF16)
        dst["r"][sl, :] = (r[sl] * jnp.exp(cm)).astype(BF16)
        dst["v"][sl, :] = v[sl].astype(BF16)
        dst["b_e"][sl, :] = (b_c * g_tail).astype(BF16)
        dst["k_e"][sl, :] = (k2[sl] * g_tail).astype(BF16)
        dst["g_all"][ck:ck + 1, :] = g_all
    yield


def _rwkv_main(src, w, consts, s_scr, y_scr, o_ref, row0):
    tile, c, pw = RW_TILE, RW_CHUNK, 2 * HEAD
    eye, masks, tri2, bdm4_16, bdm16, bdm = consts
    n_chunks, n_pairs = tile // c, N_HEADS // 2
    items = [(ck, p) for ck in range(n_chunks) for p in range(n_pairs)]

    def part(name, it):
        ck, p = it
        return src[name][ck * c:(ck + 1) * c, p * pw:(p + 1) * pw]

    a2 = [part("a", it) for it in items]
    r2 = [part("r", it) for it in items]
    prods = []
    for a, x, it in zip(a2, r2, items):
        b, k_ = part("b", it), part("k", it)
        prods.append(_mm_nt(jnp.concatenate([a, x], axis=0),
                            jnp.concatenate([b, b, k_, k_], axis=0) * bdm4_16))
    yield
    ab = [jnp.where(tri2, pr[:, :pw], 0.0) for pr in prods]
    ak16 = [jnp.where(tri2, pr[:, pw:], 0.0).astype(BF16) for pr in prods]
    qy = [_mm(x, _block_diag(part("v", it), bdm16))
          for x, it in zip(ak16, items)]
    vk = [jnp.where(bdm, lax.dot_general(part("v", it), part("k_e", it), (((0,), (0,)), ((), ())),
                                         preferred_element_type=F32), 0.0) for it in items]
    yield
    tinv = yield from _unit_lower_inverse_many([x[:c] for x in ab], eye, masks, bdm16)
    wu = [_mm(t.astype(BF16),
              jnp.concatenate([_block_diag(a, bdm16), _block_diag(q[:c].astype(BF16), bdm16)], axis=1))
          for t, a, q in zip(tinv, a2, qy)]
    wa16 = [x[:, :pw].astype(BF16) for x in wu]
    uv = [x[:, pw:] for x in wu]
    uvt = [x.T for x in uv]
    war = [jnp.concatenate([x, y], axis=0) for x, y in zip(wa16, r2)]
    m_rb16 = [x[c:].astype(BF16) for x in ab]
    yield

    state = [s_scr[p] for p in range(n_pairs)]
    for ck in range(n_chunks):
        idx = [ck * n_pairs + p for p in range(n_pairs)]
        s16 = [s.astype(BF16) for s in state]
        uy = [_mm_nt(war[i], s16[p]) for p, i in enumerate(idx)]
        ut = [_mm_nt(s16[p], wa16[i]) + uvt[i] for p, i in enumerate(idx)]
        for p, i in enumerate(idx):
            u16 = (uy[p][:c] + uv[i]).astype(BF16)
            y = uy[p][c:] + qy[i][c:] + _mm(m_rb16[i], _block_diag(u16, bdm16))
            y_scr[ck * c:(ck + 1) * c, p * pw:(p + 1) * pw] = y
        state = [state[p] * src["g_all"][ck:ck + 1, p * pw:(p + 1) * pw] + vk[i]
                 + jnp.where(bdm, _mm(ut[p].astype(BF16), part("b_e", items[i])), 0.0)
                 for p, i in enumerate(idx)]
        yield
    for p in range(n_pairs):
        s_scr[p] = state[p]

    y = y_scr[...]
    bd = w["bd"][...]
    mu = _head_sums([y], bd)[0] * (1.0 / HEAD)
    d = y - mu
    var = _head_sums([d * d], bd)[0] * (1.0 / HEAD)
    out = d * lax.rsqrt(var + GN_EPS) * w["lnw"][...] + w["lnb"][...] + src["bonus"][...]
    o_ref[row0:row0 + tile, :] = out.astype(o_ref.dtype)
    yield


def _interleave(*streams):
    for _ in itertools.zip_longest(*streams):
        pass


_W_NAMES = ("mu_r", "mu_k", "mu_v", "mu_lo", "w0", "wup", "a0", "aup", "kkw", "kaw", "rkw",
            "lnw", "lnb", "bd", "tril")


def _rwkv_kernel(*refs):
    cur, nxt = refs[0:4], refs[4:8]
    w = dict(zip(_W_NAMES, refs[8:8 + len(_W_NAMES)]))
    rest = refs[8 + len(_W_NAMES):]
    o_ref, s_scr, y_scr = rest[0], rest[1], rest[2]
    carries = rest[3:7]
    n_set = len(_SET_NAMES)
    set0 = dict(zip(_SET_NAMES, rest[7:7 + n_set]))
    set1 = dict(zip(_SET_NAMES, rest[7 + n_set:7 + 2 * n_set]))
    n = pl.program_id(1)

    @pl.when(n == 0)
    def _():
        s_scr[...] = jnp.zeros_like(s_scr)
        for p_scr in carries:
            p_scr[...] = jnp.zeros_like(p_scr)
        _interleave(_rwkv_prep(cur, 0, w, carries, set0))

    c, pw = RW_CHUNK, 2 * HEAD
    ci = lax.broadcasted_iota(jnp.int32, (c, pw), 0)
    cj = lax.broadcasted_iota(jnp.int32, (c, pw), 1) % HEAD
    eye = (ci == cj).astype(F32)
    same = lambda b: (ci // b) == (cj // b)
    masks = (same(8), tuple(same(2 * b) & ~same(b) for b in (8, 16, 32)))
    si = lax.broadcasted_iota(jnp.int32, (2 * c, pw), 0)
    sj = lax.broadcasted_iota(jnp.int32, (2 * c, pw), 1) % HEAD
    tri2 = ((si < c) & (si > sj)) | ((si >= c) & ((si - c) >= sj))
    bi = lax.broadcasted_iota(jnp.int32, (2 * pw, pw), 0)
    bj = lax.broadcasted_iota(jnp.int32, (2 * pw, pw), 1)
    bdm_bool = ((bi // HEAD) % 2) == (bj // HEAD)
    bdm4_16 = bdm_bool.astype(BF16)
    consts = (eye, masks, tri2, bdm4_16, bdm4_16[:pw], bdm_bool[:pw])

    _interleave(_rwkv_prep(cur, RW_TILE, w, carries, set1),
                _rwkv_main(set0, w, consts, s_scr, y_scr, o_ref, 0))
    _interleave(_rwkv_prep(nxt, 0, w, carries, set0),
                _rwkv_main(set1, w, consts, s_scr, y_scr, o_ref, RW_TILE))


def _rwkv(proj, vecs, wup, aup, bd, tril, batch, seq):
    m = proj.shape[0]
    ns = seq // RW_STEP
    row = lambda b, n: b * ns + n
    nxt = lambda b, n: b * ns + jnp.minimum(n + 1, ns - 1)
    vec = lambda width: pl.BlockSpec((1, width), lambda b, n: (0, 0))
    full = lambda a: pl.BlockSpec(a.shape, lambda b, n: (0, 0))

    def token_specs(row_fn):
        wide = lambda cb: pl.BlockSpec((RW_STEP, D_BRANCH), lambda b, n: (row_fn(b, n), cb))
        return [wide(P_R // D_BRANCH), wide(P_K // D_BRANCH), wide(P_V // D_BRANCH),
                pl.BlockSpec((RW_STEP, 2 * LORA), lambda b, n: (row_fn(b, n), P_LORA // (2 * LORA)))]

    in_specs = token_specs(row) + token_specs(nxt) + [
        vec(D_BRANCH), vec(D_BRANCH), vec(D_BRANCH), vec(2 * LORA),
        vec(D_BRANCH), full(wup), vec(D_BRANCH), full(aup),
        vec(D_BRANCH), vec(D_BRANCH), vec(D_BRANCH), vec(D_BRANCH), vec(D_BRANCH),
        full(bd), full(tril),
    ]
    prep_set = ([pltpu.VMEM((RW_TILE, D_BRANCH), BF16) for _ in _SET_BF16]
                + [pltpu.VMEM((8, D_BRANCH), F32), pltpu.VMEM((RW_TILE, D_BRANCH), F32)])
    return pl.pallas_call(
        _rwkv_kernel,
        grid=(batch, ns),
        in_specs=in_specs,
        out_specs=pl.BlockSpec((RW_STEP, D_BRANCH), lambda b, n: (row(b, n), 0)),
        out_shape=jax.ShapeDtypeStruct((m, D_BRANCH), ACT),
        scratch_shapes=[
            pltpu.VMEM((N_HEADS // 2, 2 * HEAD, 2 * HEAD), F32),
            pltpu.VMEM((RW_TILE, D_BRANCH), F32),
            pltpu.VMEM((8, D_BRANCH), F32), pltpu.VMEM((8, D_BRANCH), F32),
            pltpu.VMEM((8, D_BRANCH), F32), pltpu.VMEM((8, 2 * LORA), F32),
        ] + prep_set + prep_set,
        compiler_params=pltpu.CompilerParams(
            dimension_semantics=("parallel", "arbitrary"), vmem_limit_bytes=_VMEM_LIMIT),
        name="rwkv",
    )(*([proj] * 8), *vecs[:4], vecs[4], wup, vecs[5], aup, *vecs[6:], bd, tril)


_SLOPES = tuple(2.0 ** (-8.0 * (h + 1) / N_HEADS) for h in range(N_HEADS))
_SQRT_HALF = float(np.sqrt(0.5))


def _attention_bias():
    t = np.arange(BLK)[:, None]
    s = np.arange(2 * BLK)[None, :]
    dist = t + BLK - s
    out = np.empty((2, N_HEADS * BLK, 2 * BLK), np.float32)
    for first in (0, 1):
        valid = (dist >= 0) & (dist < BLK) & ((s >= BLK) | (first == 1))
        for h in range(N_HEADS):
            out[first, h * BLK:(h + 1) * BLK] = np.where(valid, -_SLOPES[h] * dist, NEG_INF)
    return out


def _mixers_kernel(sinks_ref, bias_ref, q_ref, kvc_ref, kvp_ref, zc_ref, zcp_ref, zd_ref,
                   poolw_ref, pscale_ref, nw_ref, sw_ref, sb_ref, o_ref, y_scr):
    n = pl.program_id(1)

    q = q_ref[...] * (HEAD ** -0.5)
    kv = jnp.concatenate([kvp_ref[...], kvc_ref[...]], axis=0)
    rows = ATT_GROUP * BLK
    head_in_group = lax.broadcasted_iota(jnp.int32, (rows, 1), 0) // BLK

    def per_row(values):
        col = jnp.full((rows, 1), values[-1], F32)
        for j in range(ATT_GROUP - 2, -1, -1):
            col = jnp.where(head_in_group == j, values[j], col)
        return col

    scores = []
    for g in range(KV_HEADS):
        qg = jnp.concatenate(
            [q[:, (g * ATT_GROUP + j) * HEAD:(g * ATT_GROUP + j + 1) * HEAD]
             for j in range(ATT_GROUP)], axis=0)
        scores.append(_bdot_nt(qg, kv[:, g * HEAD:(g + 1) * HEAD]))
    probs, dens = [], []
    for g in range(KV_HEADS):
        heads = range(g * ATT_GROUP, (g + 1) * ATT_GROUP)
        sink = per_row([sinks_ref[h] for h in heads])
        s = scores[g] + bias_ref[g * rows:(g + 1) * rows, :]
        mx = jnp.maximum(jnp.max(s, axis=-1, keepdims=True), sink)
        p = jnp.exp(s - mx)
        probs.append(p)
        dens.append(jnp.sum(p, axis=-1, keepdims=True) + jnp.exp(sink - mx))
    for g in range(KV_HEADS):
        vg = kv[:, (KV_HEADS + g) * HEAD:(KV_HEADS + g + 1) * HEAD]
        og = _bdot(probs[g], vg) / dens[g]
        for j in range(ATT_GROUP):
            h = g * ATT_GROUP + j
            y_scr[:, h * HEAD:(h + 1) * HEAD] = og[j * BLK:(j + 1) * BLK, :]

    zfull = jnp.concatenate([jnp.where(n > 0, zcp_ref[...], 0.0), zc_ref[...]],
                            axis=0).astype(F32)
    pos = n * BLK + lax.broadcasted_iota(jnp.int32, (BLK, 1), 0) + 1
    for g, w in enumerate(POOL_WINDOWS):
        gs = slice(g * BLK, (g + 1) * BLK)
        zg = zfull[:, gs]
        acc = zg
        step = 1
        while step < w:
            acc = acc + pltpu.roll(acc, step, axis=0)
            step *= 2
        cnt = jnp.minimum(pos, w).astype(F32)
        pooled = acc[POOL_HALO:, :] / cnt - zg[POOL_HALO:, :]
        yg = _bdot(pooled, poolw_ref[g]) * pscale_ref[:, gs]
        y_scr[:, D_BRANCH + g * BLK:D_BRANCH + (g + 1) * BLK] = yg

    zd = zd_ref[...].astype(F32)
    gz = 0.5 * zd * (1.0 + lax.erf(zd * _SQRT_HALF))
    u = gz[:, :D_BRANCH]
    vv = gz[:, D_BRANCH:]
    mu = jnp.mean(vv, axis=-1, keepdims=True)
    dv = vv - mu
    var = jnp.mean(dv * dv, axis=-1, keepdims=True)
    vn = dv * lax.rsqrt(var + LN_EPS) * nw_ref[...]
    ri = lax.broadcasted_iota(jnp.int32, (BLK, BLK), 0)
    rj = lax.broadcasted_iota(jnp.int32, (BLK, BLK), 1)
    causal = ri >= rj
    for g in range(4):
        gs = slice(g * BLK, (g + 1) * BLK)
        ws = jnp.where(causal, sw_ref[g], 0.0)
        sg = _bdot(ws, vn[:, gs]) + sb_ref[:, gs]
        y_scr[:, 2 * D_BRANCH + g * BLK:2 * D_BRANCH + (g + 1) * BLK] = u[:, gs] * sg
    o_ref[...] = y_scr[...].astype(o_ref.dtype)


def _mixers(proj, sinks, pool_w, pool_scale, norm_w, sgu_w, sgu_bias, batch, seq):
    m = proj.shape[0]
    nb = seq // BLK
    row = lambda b, n: b * nb + n
    halo = BLK // POOL_HALO
    bias = jnp.asarray(_attention_bias())
    in_specs = [
        pl.BlockSpec(memory_space=pltpu.SMEM),
        pl.BlockSpec((None,) + bias.shape[1:], lambda b, n: (jnp.minimum(n, 1), 0, 0)),
        pl.BlockSpec((BLK, D_BRANCH), lambda b, n: (row(b, n), P_Q // D_BRANCH)),
        pl.BlockSpec((BLK, 256), lambda b, n: (row(b, n), P_KV // 256)),
        pl.BlockSpec((BLK, 256), lambda b, n: (row(b, jnp.maximum(n - 1, 0)), P_KV // 256)),
        pl.BlockSpec((BLK, D_BRANCH), lambda b, n: (row(b, n), P_POOL // D_BRANCH)),
        pl.BlockSpec((POOL_HALO, D_BRANCH),
                     lambda b, n: (jnp.maximum(row(b, n) * halo - 1, 0), P_POOL // D_BRANCH)),
        pl.BlockSpec((BLK, 2 * D_BRANCH), lambda b, n: (row(b, n), P_SGU // (2 * D_BRANCH))),
        pl.BlockSpec((4, BLK, BLK), lambda b, n: (0, 0, 0)),
        pl.BlockSpec((1, D_BRANCH), lambda b, n: (0, 0)),
        pl.BlockSpec((1, D_BRANCH), lambda b, n: (0, 0)),
        pl.BlockSpec((4, BLK, BLK), lambda b, n: (0, 0, 0)),
        pl.BlockSpec((BLK, D_BRANCH), lambda b, n: (0, 0)),
    ]
    return pl.pallas_call(
        _mixers_kernel,
        grid=(batch, nb),
        in_specs=in_specs,
        out_specs=pl.BlockSpec((BLK, 3 * D_BRANCH), lambda b, n: (row(b, n), 0)),
        out_shape=jax.ShapeDtypeStruct((m, 3 * D_BRANCH), ACT),
        scratch_shapes=[pltpu.VMEM((BLK, 3 * D_BRANCH), F32)],
        compiler_params=pltpu.CompilerParams(
            dimension_semantics=("parallel", "parallel"), vmem_limit_bytes=_VMEM_LIMIT),
        name="mixers",
    )(sinks, bias, proj, proj, proj, proj, proj, proj, pool_w, pool_scale, norm_w, sgu_w,
      sgu_bias)


OUT_TM = 512


def _outproj_kernel(ya_ref, yb_ref, g_ref, x_ref, w_ref, pw_ref, o_ref):
    g = g_ref[...].astype(F32)
    sg = g * jax.nn.sigmoid(g)
    ya = ya_ref[...].astype(F32)
    yb = yb_ref[...].astype(F32)
    acc = jnp.dot((ya * sg[:, :D_BRANCH]).astype(BF16), w_ref[:D_BRANCH, :],
                  preferred_element_type=F32)
    acc = acc + jnp.dot((yb * sg[:, D_BRANCH:]).astype(BF16), w_ref[D_BRANCH:, :],
                        preferred_element_type=F32)
    ms = jnp.mean(acc * acc, axis=-1, keepdims=True)
    o_ref[...] = x_ref[...] + acc * lax.rsqrt(ms + NORM_EPS) * pw_ref[...]


def _outproj(ya, ybcd, proj, x2d, w_out16, layer, post_w):
    m = x2d.shape[0]
    return pl.pallas_call(
        _outproj_kernel,
        grid=(m // OUT_TM,),
        in_specs=[
            pl.BlockSpec((OUT_TM, D_BRANCH), lambda i: (i, 0)),
            pl.BlockSpec((OUT_TM, 3 * D_BRANCH), lambda i: (i, 0)),
            pl.BlockSpec((OUT_TM, D_MODEL), lambda i: (i, P_GATE // D_MODEL)),
            pl.BlockSpec((OUT_TM, D_MODEL), lambda i: (i, 0)),
            pl.BlockSpec((None, D_MODEL, D_MODEL), lambda i: (layer, 0, 0)),
            pl.BlockSpec((1, D_MODEL), lambda i: (0, 0)),
        ],
        out_specs=pl.BlockSpec((OUT_TM, D_MODEL), lambda i: (i, 0)),
        out_shape=jax.ShapeDtypeStruct((m, D_MODEL), F32),
        compiler_params=pltpu.CompilerParams(
            dimension_semantics=("parallel",), vmem_limit_bytes=_VMEM_LIMIT),
        name="outproj",
    )(ya, ybcd, proj, x2d, w_out16, post_w)


def kernel(x, pre_norm_w, post_norm_w, w_in, shift_mu, rwkv_w0, rwkv_w_up, rwkv_a0, rwkv_a_up,
           rwkv_k_k, rwkv_k_a, rwkv_r_k, rwkv_ln_w, rwkv_ln_b, attn_sinks, pool_w, pool_scale,
           sgu_norm_w, sgu_w, sgu_b, w_out):
    batch, seq, _ = x.shape
    assert x.shape == (batch, seq, D_MODEL) and seq % BLK == 0 and seq % RW_STEP == 0
    m = batch * seq
    head_id = np.arange(D_BRANCH // 2) // HEAD
    bd = jnp.asarray((head_id[:, None] == head_id[None, :]).astype(np.float32), dtype=BF16)
    t_id = np.arange(RW_TILE)
    tril = jnp.asarray(((t_id[:, None] >= t_id[None, :])
                        & (t_id[:, None] // RW_CHUNK == t_id[None, :] // RW_CHUNK)).astype(np.float32),
                       dtype=BF16)
    row_vec = lambda a: a.reshape(1, -1)

    w_in16 = w_in.astype(BF16)
    w_out16 = w_out.astype(BF16)
    h = x.reshape(m, D_MODEL)
    for l in range(DEPTH):
        proj = _inproj(h, row_vec(pre_norm_w[l]), w_in16, l, _tail_weight(w_in16, l))

        mu = shift_mu[l]
        vecs = [row_vec(mu[0:D_BRANCH]), row_vec(mu[D_BRANCH:2 * D_BRANCH]),
                row_vec(mu[2 * D_BRANCH:3 * D_BRANCH]), row_vec(mu[3 * D_BRANCH:]),
                row_vec(rwkv_w0[l]), row_vec(rwkv_a0[l]), row_vec(rwkv_k_k[l]),
                row_vec(rwkv_k_a[l]), row_vec(rwkv_r_k[l]), row_vec(rwkv_ln_w[l]),
                row_vec(rwkv_ln_b[l])]
        ya = _rwkv(proj, vecs, rwkv_w_up[l], rwkv_a_up[l], bd, tril, batch, seq)

        sgu_bias = jnp.broadcast_to(sgu_b[l].T[:, :, None], (BLK, 4, BLK)).reshape(BLK, D_BRANCH)
        ybcd = _mixers(proj, attn_sinks[l], pool_w[l], row_vec(pool_scale[l]),
                       row_vec(sgu_norm_w[l]), sgu_w[l], sgu_bias, batch, seq)

        h = _outproj(ya, ybcd, proj, h, w_out16, l, row_vec(post_norm_w[l]))
    return h.reshape(batch, seq, D_MODEL)
```

```python
import itertools

import jax
import jax.numpy as jnp
import numpy as np
from jax import lax
from jax.experimental import pallas as pl
from jax.experimental.pallas import tpu as pltpu

F32 = jnp.float32
BF16 = jnp.bfloat16
ACT = BF16

D_MODEL = 2048
DEPTH = 2
D_BRANCH = 512
HEAD = 64
N_HEADS = D_BRANCH // HEAD
LORA = 64
KV_HEADS = 2
ATT_GROUP = N_HEADS // KV_HEADS
BLK = 128
POOL_WINDOWS = (2, 4, 8, 16)
POOL_HALO = 16
NEG_INF = -1e30
NORM_EPS = 1e-6
LN_EPS = 1e-5
GN_EPS = 64e-5

A_COLS = 3 * D_BRANCH + 2 * LORA
B_COLS = D_BRANCH + 2 * KV_HEADS * HEAD
OFF_B = A_COLS
OFF_C = OFF_B + B_COLS
OFF_D = OFF_C + D_BRANCH
OFF_G = OFF_D + 2 * D_BRANCH
D_IN = OFF_G + D_MODEL

P_GATE = 0
P_SGU = 2048
P_POOL = 3072
P_Q = 3584
P_R = 4096
P_K = 4608
P_V = 5120
P_KV = 5632
P_LORA = 5888
P_WIDTH = 6144

RW_CHUNK = 64
RW_TILE = 256
RW_STEP = 2 * RW_TILE

_VMEM_LIMIT = 56 * 1024 * 1024


def _tail_weight(w16, layer):
    seg = lambda start, n: w16[layer, :, start:start + n]
    parts = [
        seg(OFF_B + D_BRANCH, 2 * KV_HEADS * HEAD),
        seg(3 * D_BRANCH, 2 * LORA),
        jnp.zeros((D_MODEL, P_WIDTH - D_IN), w16.dtype),
    ]
    return jnp.concatenate(parts, axis=1)


IN_TM = 1024
IN_TN = 1024
IN_HALF = IN_TN // 2
_CHUNK_SRC = (OFF_G, OFF_G + 512, OFF_G + 1024, OFF_G + 1536, OFF_D, OFF_D + 512, OFF_C, OFF_B,
              0, 512, 1024, 0)
assert len(_CHUNK_SRC) * IN_HALF == P_WIDTH


def _inproj_kernel(src_ref, x_ref, pw_ref, wa_ref, wb_ref, wt_ref, o_ref, h_scr):
    j = pl.program_id(1)
    last = pl.num_programs(1) - 1

    @pl.when(j == 0)
    def _():
        for r0 in range(0, IN_TM, 256):
            x = x_ref[r0:r0 + 256, :]
            ms = jnp.mean(x * x, axis=-1, keepdims=True)
            h_scr[r0:r0 + 256, :] = (x * lax.rsqrt(ms + NORM_EPS) * pw_ref[...]).astype(BF16)

    def project(w_ref):
        return jnp.dot(h_scr[...], w_ref[...], preferred_element_type=F32).astype(o_ref.dtype)

    o_ref[:, :IN_HALF] = project(wa_ref)

    @pl.when(j < last)
    def _():
        o_ref[:, IN_HALF:] = project(wb_ref)

    @pl.when(j == last)
    def _():
        o_ref[:, IN_HALF:] = project(wt_ref)


def _inproj(x2d, pre_w, w_in, layer, w_tail):
    m = x2d.shape[0]
    src = jnp.asarray(_CHUNK_SRC, jnp.int32)
    w_spec = lambda half: pl.BlockSpec(
        (None, pl.Element(D_MODEL), pl.Element(IN_HALF)),
        lambda i, j, src_ref: (layer, 0, pl.multiple_of(src_ref[2 * j + half], 128)))
    grid_spec = pltpu.PrefetchScalarGridSpec(
        num_scalar_prefetch=1,
        grid=(m // IN_TM, P_WIDTH // IN_TN),
        in_specs=[
            pl.BlockSpec((IN_TM, D_MODEL), lambda i, j, s: (i, 0)),
            pl.BlockSpec((1, D_MODEL), lambda i, j, s: (0, 0)),
            w_spec(0), w_spec(1),
            pl.BlockSpec((D_MODEL, IN_HALF), lambda i, j, s: (0, 0)),
        ],
        out_specs=pl.BlockSpec((IN_TM, IN_TN), lambda i, j, s: (i, j)),
        scratch_shapes=[pltpu.VMEM((IN_TM, D_MODEL), BF16)],
    )
    return pl.pallas_call(
        _inproj_kernel,
        grid_spec=grid_spec,
        out_shape=jax.ShapeDtypeStruct((m, P_WIDTH), ACT),
        compiler_params=pltpu.CompilerParams(
            dimension_semantics=("parallel", "arbitrary"), vmem_limit_bytes=_VMEM_LIMIT),
        name="inproj",
    )(src, x2d, pre_w, w_in, w_in, w_tail)


def _bdot(a, b):
    return jnp.dot(a.astype(BF16), b.astype(BF16), preferred_element_type=F32)


def _bdot_nt(a, b):
    return lax.dot_general(a.astype(BF16), b.astype(BF16), (((1,), (1,)), ((), ())),
                           preferred_element_type=F32)


def _split3(x):
    x1 = x.astype(BF16)
    r1 = x - x1.astype(F32)
    x2 = r1.astype(BF16)
    x3 = (r1 - x2.astype(F32)).astype(BF16)
    return x1, x2, x3


def _head_sums(xs, bd):
    rows = xs[0].shape[0]
    half = bd.shape[0]
    x = jnp.concatenate(xs, axis=0).astype(BF16)
    out = jnp.concatenate(
        [jnp.dot(x[:, :half], bd, preferred_element_type=F32),
         jnp.dot(x[:, half:], bd, preferred_element_type=F32)], axis=1)
    return [out[i * rows:(i + 1) * rows] for i in range(len(xs))]


def _mm(a16, b16):
    return jnp.dot(a16, b16, preferred_element_type=F32)


def _mm_nt(a16, b16):
    return lax.dot_general(a16, b16, (((1,), (1,)), ((), ())), preferred_element_type=F32)


def _block_diag(m16, bdm16):
    return jnp.concatenate([m16, m16], axis=0) * bdm16


def _unit_lower_inverse_many(lmats, eye, masks, bdm16):
    base_mask, base_bd16, level_bd16 = masks
    c = lmats[0].shape[0]
    l16s = [l.astype(BF16) for l in lmats]
    stack2 = lambda m16: jnp.concatenate([m16, m16], axis=0)
    lds = [jnp.where(base_mask, l, 0.0) for l in lmats]
    l2s = [_mm(ld.astype(BF16), stack2(l16) * base_bd16).astype(BF16)
           for ld, l16 in zip(lds, l16s)]
    yield
    xs = [eye + ld for ld in lds]
    both = [_mm(jnp.concatenate([x.astype(BF16), l2], axis=0), stack2(l2) * bdm16)
            for x, l2 in zip(xs, l2s)]
    xs = [x + b[:c] for x, b in zip(xs, both)]
    l4s = [b[c:].astype(BF16) for b in both]
    yield
    xs = [x + _mm(x.astype(BF16), stack2(l4) * bdm16) for x, l4 in zip(xs, l4s)]
    yield
    for lvl16 in level_bd16:
        ts = [_mm(x.astype(BF16), stack2(l16) * lvl16).astype(BF16) for x, l16 in zip(xs, l16s)]
        yield
        xs = [x + _mm(t, stack2(x.astype(BF16)) * bdm16) for x, t in zip(xs, ts)]
        yield
    return xs


_SET_BF16 = ("a", "b", "k", "r", "v", "b_e", "k_e")
_SET_NAMES = _SET_BF16 + ("g_all", "bonus")
_PREP_MXU_DELAY = 4
_EXP_NEG_HALF = float(np.exp(-0.5))


def _rwkv_prep(z_refs, row0, w, carries, dst):
    tile, c = RW_TILE, RW_CHUNK
    row = lax.broadcasted_iota(jnp.int32, (tile, 1), 0)

    def shifted(z_ref, mu_ref, p_scr):
        z = z_ref[row0:row0 + tile, :].astype(F32)
        zp = jnp.where(row == 0, p_scr[0:1, :], pltpu.roll(z, 1, axis=0))
        p_scr[0:1, :] = z[tile - 1:tile, :]
        return z + mu_ref[...] * (zp - z)

    lo = shifted(z_refs[3], w["mu_lo"], carries[3])
    lora_w = _bdot(jnp.tanh(lo[:, :LORA]), w["wup"][...])
    lora_a = _bdot(lo[:, LORA:], w["aup"][...])
    yield
    r = shifted(z_refs[0], w["mu_r"], carries[0])
    k = shifted(z_refs[1], w["mu_k"], carries[1])
    v = shifted(z_refs[2], w["mu_v"], carries[2])
    logw = -_EXP_NEG_HALF * jax.nn.sigmoid(w["w0"][...] + lora_w)
    asig = jax.nn.sigmoid(w["a0"][...] + lora_a)
    logw_terms = _split3(logw)
    kk = k * w["kkw"][...]
    k2 = k * (1.0 + (asig - 1.0) * w["kaw"][...])
    sum_terms = [kk * kk, r * k2 * w["rkw"][...]]
    for _ in range(_PREP_MXU_DELAY):
        yield
    tril = w["tril"][...]
    cum = sum(jnp.dot(tril, p, preferred_element_type=F32) for p in logw_terms)
    kk_ss, rk_sum = _head_sums(sum_terms, w["bd"][...])
    yield
    kk = kk * lax.rsqrt(jnp.maximum(kk_ss, 1e-24))
    dst["bonus"][...] = rk_sum * v
    for ck in range(tile // c):
        sl = slice(ck * c, (ck + 1) * c)
        lw, cm = logw[sl], cum[sl]
        total = cm[c - 1:c, :]
        g_inv = jnp.exp(-cm)
        g_all = jnp.exp(total)
        g_tail = g_all * g_inv
        b_c = kk[sl] * asig[sl]
        dst["a"][sl, :] = (-kk[sl] * jnp.exp(cm - lw)).astype(BF16)
        dst["b"][sl, :] = (b_c * g_inv).astype(BF16)
        dst["k"][sl, :] = (k2[sl] * g_inv).astype(BF16)
        dst["r"][sl, :] = (r[sl] * jnp.exp(cm)).astype(BF16)
        dst["v"][sl, :] = v[sl].astype(BF16)
        dst["b_e"][sl, :] = (b_c * g_tail).astype(BF16)
        dst["k_e"][sl, :] = (k2[sl] * g_tail).astype(BF16)
        dst["g_all"][ck:ck + 1, :] = g_all
    yield


def _rwkv_main(src, w, consts, s_scr, y_scr, o_ref, row0):
    tile, c, pw = RW_TILE, RW_CHUNK, 2 * HEAD
    eye, masks, tri2, bdm4_16, bdm16, bdm = consts
    n_chunks, n_pairs = tile // c, N_HEADS // 2
    items = [(ck, p) for ck in range(n_chunks) for p in range(n_pairs)]

    def part(name, it):
        ck, p = it
        return src[name][ck * c:(ck + 1) * c, p * pw:(p + 1) * pw]

    a2 = [part("a", it) for it in items]
    r2 = [part("r", it) for it in items]
    l_ab, m_rb16, ak16 = [], [], []
    for a, x, it in zip(a2, r2, items):
        b, k_ = part("b", it), part("k", it)
        pr = _mm_nt(jnp.concatenate([a, x], axis=0),
                    jnp.concatenate([b, b, k_, k_], axis=0) * bdm4_16)
        ab = jnp.where(tri2, pr[:, :pw], 0.0)
        l_ab.append(ab[:c])
        m_rb16.append(ab[c:].astype(BF16))
        ak16.append(jnp.where(tri2, pr[:, pw:], 0.0).astype(BF16))
    yield
    qy = [_mm(x, _block_diag(part("v", it), bdm16))
          for x, it in zip(ak16, items)]
    qv16 = [q[:c].astype(BF16) for q in qy]
    yv = [q[c:] for q in qy]
    vk = [jnp.where(bdm, lax.dot_general(part("v", it), part("k_e", it), (((0,), (0,)), ((), ())),
                                         preferred_element_type=F32), 0.0) for it in items]
    yield
    tinv = yield from _unit_lower_inverse_many(l_ab, eye, masks, bdm16)
    wu = [_mm(t.astype(BF16),
              jnp.concatenate([_block_diag(a, bdm16), _block_diag(q, bdm16)], axis=1))
          for t, a, q in zip(tinv, a2, qv16)]
    wa16 = [x[:, :pw].astype(BF16) for x in wu]
    uv = [x[:, pw:] for x in wu]
    uvt = [x.T for x in uv]
    war = [jnp.concatenate([x, y], axis=0) for x, y in zip(wa16, r2)]
    yield

    state = [s_scr[p] for p in range(n_pairs)]
    for ck in range(n_chunks):
        idx = [ck * n_pairs + p for p in range(n_pairs)]
        s16 = [s.astype(BF16) for s in state]
        uy = [_mm_nt(war[i], s16[p]) for p, i in enumerate(idx)]
        ut = [_mm_nt(s16[p], wa16[i]) + uvt[i] for p, i in enumerate(idx)]
        for p, i in enumerate(idx):
            u16 = (uy[p][:c] + uv[i]).astype(BF16)
            y = uy[p][c:] + yv[i] + _mm(m_rb16[i], _block_diag(u16, bdm16))
            y_scr[ck * c:(ck + 1) * c, p * pw:(p + 1) * pw] = y
        state = [state[p] * src["g_all"][ck:ck + 1, p * pw:(p + 1) * pw] + vk[i]
                 + jnp.where(bdm, _mm(ut[p].astype(BF16), part("b_e", items[i])), 0.0)
                 for p, i in enumerate(idx)]
        yield
    for p in range(n_pairs):
        s_scr[p] = state[p]

    y = y_scr[...]
    bd = w["bd"][...]
    mu = _head_sums([y], bd)[0] * (1.0 / HEAD)
    d = y - mu
    var = _head_sums([d * d], bd)[0] * (1.0 / HEAD)
    out = d * lax.rsqrt(var + GN_EPS) * w["lnw"][...] + w["lnb"][...] + src["bonus"][...]
    o_ref[row0:row0 + tile, :] = out.astype(o_ref.dtype)
    yield


def _interleave(*streams):
    for _ in itertools.zip_longest(*streams):
        pass


_W_NAMES = ("mu_r", "mu_k", "mu_v", "mu_lo", "w0", "wup", "a0", "aup", "kkw", "kaw", "rkw",
            "lnw", "lnb", "bd", "tril")


def _rwkv_kernel(*refs):
    cur, nxt = refs[0:4], refs[4:8]
    w = dict(zip(_W_NAMES, refs[8:8 + len(_W_NAMES)]))
    rest = refs[8 + len(_W_NAMES):]
    o_ref, s_scr, y_scr = rest[0], rest[1], rest[2]
    carries = rest[3:7]
    n_set = len(_SET_NAMES)
    set0 = dict(zip(_SET_NAMES, rest[7:7 + n_set]))
    set1 = dict(zip(_SET_NAMES, rest[7 + n_set:7 + 2 * n_set]))
    n = pl.program_id(1)

    @pl.when(n == 0)
    def _():
        s_scr[...] = jnp.zeros_like(s_scr)
        for p_scr in carries:
            p_scr[...] = jnp.zeros_like(p_scr)
        _interleave(_rwkv_prep(cur, 0, w, carries, set0))

    c, pw = RW_CHUNK, 2 * HEAD
    ci = lax.broadcasted_iota(jnp.int32, (c, pw), 0)
    cj = lax.broadcasted_iota(jnp.int32, (c, pw), 1) % HEAD
    eye = (ci == cj).astype(F32)
    si = lax.broadcasted_iota(jnp.int32, (2 * c, pw), 0)
    sj = lax.broadcasted_iota(jnp.int32, (2 * c, pw), 1) % HEAD
    tri2 = ((si < c) & (si > sj)) | ((si >= c) & ((si - c) >= sj))
    bi = lax.broadcasted_iota(jnp.int32, (2 * pw, pw), 0)
    bj = lax.broadcasted_iota(jnp.int32, (2 * pw, pw), 1)
    bdm_bool = ((bi // HEAD) % 2) == (bj // HEAD)
    bdm4_16 = bdm_bool.astype(BF16)
    hi = lax.broadcasted_iota(jnp.int32, (pw, pw), 0)
    hj = lax.broadcasted_iota(jnp.int32, (pw, pw), 1)
    di, dj = hi % HEAD, hj % HEAD
    same_head = (hi // HEAD) == (hj // HEAD)
    near = lambda b: same_head & ((di // b) == (dj // b))
    ring = lambda b: same_head & ((di // (2 * b)) == (dj // (2 * b))) & ((di // b) != (dj // b))
    masks = ((ci // 8) == (cj // 8), near(8).astype(BF16),
             tuple(ring(b).astype(BF16) for b in (8, 16, 32)))
    consts = (eye, masks, tri2, bdm4_16, bdm4_16[:pw], bdm_bool[:pw])

    _interleave(_rwkv_prep(cur, RW_TILE, w, carries, set1),
                _rwkv_main(set0, w, consts, s_scr, y_scr, o_ref, 0))
    _interleave(_rwkv_prep(nxt, 0, w, carries, set0),
                _rwkv_main(set1, w, consts, s_scr, y_scr, o_ref, RW_TILE))


def _rwkv(proj, vecs, wup, aup, bd, tril, batch, seq):
    m = proj.shape[0]
    ns = seq // RW_STEP
    row = lambda b, n: b * ns + n
    nxt = lambda b, n: b * ns + jnp.minimum(n + 1, ns - 1)
    vec = lambda width: pl.BlockSpec((1, width), lambda b, n: (0, 0))
    full = lambda a: pl.BlockSpec(a.shape, lambda b, n: (0, 0))

    def token_specs(row_fn):
        wide = lambda cb: pl.BlockSpec((RW_STEP, D_BRANCH), lambda b, n: (row_fn(b, n), cb))
        return [wide(P_R // D_BRANCH), wide(P_K // D_BRANCH), wide(P_V // D_BRANCH),
                pl.BlockSpec((RW_STEP, 2 * LORA), lambda b, n: (row_fn(b, n), P_LORA // (2 * LORA)))]

    in_specs = token_specs(row) + token_specs(nxt) + [
        vec(D_BRANCH), vec(D_BRANCH), vec(D_BRANCH), vec(2 * LORA),
        vec(D_BRANCH), full(wup), vec(D_BRANCH), full(aup),
        vec(D_BRANCH), vec(D_BRANCH), vec(D_BRANCH), vec(D_BRANCH), vec(D_BRANCH),
        full(bd), full(tril),
    ]
    prep_set = ([pltpu.VMEM((RW_TILE, D_BRANCH), BF16) for _ in _SET_BF16]
                + [pltpu.VMEM((8, D_BRANCH), F32), pltpu.VMEM((RW_TILE, D_BRANCH), F32)])
    return pl.pallas_call(
        _rwkv_kernel,
        grid=(batch, ns),
        in_specs=in_specs,
        out_specs=pl.BlockSpec((RW_STEP, D_BRANCH), lambda b, n: (row(b, n), 0)),
        out_shape=jax.ShapeDtypeStruct((m, D_BRANCH), ACT),
        scratch_shapes=[
            pltpu.VMEM((N_HEADS // 2, 2 * HEAD, 2 * HEAD), F32),
            pltpu.VMEM((RW_TILE, D_BRANCH), F32),
            pltpu.VMEM((8, D_BRANCH), F32), pltpu.VMEM((8, D_BRANCH), F32),
            pltpu.VMEM((8, D_BRANCH), F32), pltpu.VMEM((8, 2 * LORA), F32),
        ] + prep_set + prep_set,
        compiler_params=pltpu.CompilerParams(
            dimension_semantics=("parallel", "arbitrary"), vmem_limit_bytes=_VMEM_LIMIT),
        name="rwkv",
    )(*([proj] * 8), *vecs[:4], vecs[4], wup, vecs[5], aup, *vecs[6:], bd, tril)


_SLOPES = tuple(2.0 ** (-8.0 * (h + 1) / N_HEADS) for h in range(N_HEADS))
_SQRT_HALF = float(np.sqrt(0.5))


def _attention_bias():
    t = np.arange(BLK)[:, None]
    s = np.arange(2 * BLK)[None, :]
    dist = t + BLK - s
    out = np.empty((2, N_HEADS * BLK, 2 * BLK), np.float32)
    for first in (0, 1):
        valid = (dist >= 0) & (dist < BLK) & ((s >= BLK) | (first == 1))
        for h in range(N_HEADS):
            out[first, h * BLK:(h + 1) * BLK] = np.where(valid, -_SLOPES[h] * dist, NEG_INF)
    return out


def _mixers_kernel(sinks_ref, bias_ref, q_ref, kvc_ref, kvp_ref, zc_ref, zcp_ref, zd_ref,
                   poolw_ref, pscale_ref, nw_ref, sw_ref, sb_ref, o_ref, y_scr):
    n = pl.program_id(1)

    q = q_ref[...] * (HEAD ** -0.5)
    kv = jnp.concatenate([kvp_ref[...], kvc_ref[...]], axis=0)
    rows = ATT_GROUP * BLK
    head_in_group = lax.broadcasted_iota(jnp.int32, (rows, 1), 0) // BLK

    def per_row(values):
        col = jnp.full((rows, 1), values[-1], F32)
        for j in range(ATT_GROUP - 2, -1, -1):
            col = jnp.where(head_in_group == j, values[j], col)
        return col

    scores = []
    for g in range(KV_HEADS):
        qg = jnp.concatenate(
            [q[:, (g * ATT_GROUP + j) * HEAD:(g * ATT_GROUP + j + 1) * HEAD]
             for j in range(ATT_GROUP)], axis=0)
        scores.append(_bdot_nt(qg, kv[:, g * HEAD:(g + 1) * HEAD]))
    probs, dens = [], []
    for g in range(KV_HEADS):
        heads = range(g * ATT_GROUP, (g + 1) * ATT_GROUP)
        sink = per_row([sinks_ref[h] for h in heads])
        s = scores[g] + bias_ref[g * rows:(g + 1) * rows, :]
        mx = jnp.maximum(jnp.max(s, axis=-1, keepdims=True), sink)
        p = jnp.exp(s - mx)
        probs.append(p)
        dens.append(jnp.sum(p, axis=-1, keepdims=True) + jnp.exp(sink - mx))
    for g in range(KV_HEADS):
        vg = kv[:, (KV_HEADS + g) * HEAD:(KV_HEADS + g + 1) * HEAD]
        og = _bdot(probs[g], vg) / dens[g]
        for j in range(ATT_GROUP):
            h = g * ATT_GROUP + j
            y_scr[:, h * HEAD:(h + 1) * HEAD] = og[j * BLK:(j + 1) * BLK, :]

    zfull = jnp.concatenate([jnp.where(n > 0, zcp_ref[...], 0.0), zc_ref[...]],
                            axis=0).astype(F32)
    pos = n * BLK + lax.broadcasted_iota(jnp.int32, (BLK, 1), 0) + 1
    for g, w in enumerate(POOL_WINDOWS):
        gs = slice(g * BLK, (g + 1) * BLK)
        zg = zfull[:, gs]
        acc = zg
        step = 1
        while step < w:
            acc = acc + pltpu.roll(acc, step, axis=0)
            step *= 2
        cnt = jnp.minimum(pos, w).astype(F32)
        pooled = acc[POOL_HALO:, :] / cnt - zg[POOL_HALO:, :]
        yg = _bdot(pooled, poolw_ref[g]) * pscale_ref[:, gs]
        y_scr[:, D_BRANCH + g * BLK:D_BRANCH + (g + 1) * BLK] = yg

    zd = zd_ref[...].astype(F32)
    gz = 0.5 * zd * (1.0 + lax.erf(zd * _SQRT_HALF))
    u = gz[:, :D_BRANCH]
    vv = gz[:, D_BRANCH:]
    mu = jnp.mean(vv, axis=-1, keepdims=True)
    dv = vv - mu
    var = jnp.mean(dv * dv, axis=-1, keepdims=True)
    vn = dv * lax.rsqrt(var + LN_EPS) * nw_ref[...]
    ri = lax.broadcasted_iota(jnp.int32, (BLK, BLK), 0)
    rj = lax.broadcasted_iota(jnp.int32, (BLK, BLK), 1)
    causal = ri >= rj
    for g in range(4):
        gs = slice(g * BLK, (g + 1) * BLK)
        ws = jnp.where(causal, sw_ref[g], 0.0)
        sg = _bdot(ws, vn[:, gs]) + sb_ref[:, gs]
        y_scr[:, 2 * D_BRANCH + g * BLK:2 * D_BRANCH + (g + 1) * BLK] = u[:, gs] * sg
    o_ref[...] = y_scr[...].astype(o_ref.dtype)


def _mixers(proj, sinks, pool_w, pool_scale, norm_w, sgu_w, sgu_bias, batch, seq):
    m = proj.shape[0]
    nb = seq // BLK
    row = lambda b, n: b * nb + n
    halo = BLK // POOL_HALO
    bias = jnp.asarray(_attention_bias())
    in_specs = [
        pl.BlockSpec(memory_space=pltpu.SMEM),
        pl.BlockSpec((None,) + bias.shape[1:], lambda b, n: (jnp.minimum(n, 1), 0, 0)),
        pl.BlockSpec((BLK, D_BRANCH), lambda b, n: (row(b, n), P_Q // D_BRANCH)),
        pl.BlockSpec((BLK, 256), lambda b, n: (row(b, n), P_KV // 256)),
        pl.BlockSpec((BLK, 256), lambda b, n: (row(b, jnp.maximum(n - 1, 0)), P_KV // 256)),
        pl.BlockSpec((BLK, D_BRANCH), lambda b, n: (row(b, n), P_POOL // D_BRANCH)),
        pl.BlockSpec((POOL_HALO, D_BRANCH),
                     lambda b, n: (jnp.maximum(row(b, n) * halo - 1, 0), P_POOL // D_BRANCH)),
        pl.BlockSpec((BLK, 2 * D_BRANCH), lambda b, n: (row(b, n), P_SGU // (2 * D_BRANCH))),
        pl.BlockSpec((4, BLK, BLK), lambda b, n: (0, 0, 0)),
        pl.BlockSpec((1, D_BRANCH), lambda b, n: (0, 0)),
        pl.BlockSpec((1, D_BRANCH), lambda b, n: (0, 0)),
        pl.BlockSpec((4, BLK, BLK), lambda b, n: (0, 0, 0)),
        pl.BlockSpec((BLK, D_BRANCH), lambda b, n: (0, 0)),
    ]
    return pl.pallas_call(
        _mixers_kernel,
        grid=(batch, nb),
        in_specs=in_specs,
        out_specs=pl.BlockSpec((BLK, 3 * D_BRANCH), lambda b, n: (row(b, n), 0)),
        out_shape=jax.ShapeDtypeStruct((m, 3 * D_BRANCH), ACT),
        scratch_shapes=[pltpu.VMEM((BLK, 3 * D_BRANCH), F32)],
        compiler_params=pltpu.CompilerParams(
            dimension_semantics=("parallel", "parallel"), vmem_limit_bytes=_VMEM_LIMIT),
        name="mixers",
    )(sinks, bias, proj, proj, proj, proj, proj, proj, pool_w, pool_scale, norm_w, sgu_w,
      sgu_bias)


OUT_TM = 512


def _outproj_kernel(ya_ref, yb_ref, g_ref, x_ref, w_ref, pw_ref, o_ref):
    g = g_ref[...].astype(F32)
    sg = g * jax.nn.sigmoid(g)
    ya = ya_ref[...].astype(F32)
    yb = yb_ref[...].astype(F32)
    acc = jnp.dot((ya * sg[:, :D_BRANCH]).astype(BF16), w_ref[:D_BRANCH, :],
                  preferred_element_type=F32)
    acc = acc + jnp.dot((yb * sg[:, D_BRANCH:]).astype(BF16), w_ref[D_BRANCH:, :],
                        preferred_element_type=F32)
    ms = jnp.mean(acc * acc, axis=-1, keepdims=True)
    o_ref[...] = x_ref[...] + acc * lax.rsqrt(ms + NORM_EPS) * pw_ref[...]


def _outproj(ya, ybcd, proj, x2d, w_out16, layer, post_w):
    m = x2d.shape[0]
    return pl.pallas_call(
        _outproj_kernel,
        grid=(m // OUT_TM,),
        in_specs=[
            pl.BlockSpec((OUT_TM, D_BRANCH), lambda i: (i, 0)),
            pl.BlockSpec((OUT_TM, 3 * D_BRANCH), lambda i: (i, 0)),
            pl.BlockSpec((OUT_TM, D_MODEL), lambda i: (i, P_GATE // D_MODEL)),
            pl.BlockSpec((OUT_TM, D_MODEL), lambda i: (i, 0)),
            pl.BlockSpec((None, D_MODEL, D_MODEL), lambda i: (layer, 0, 0)),
            pl.BlockSpec((1, D_MODEL), lambda i: (0, 0)),
        ],
        out_specs=pl.BlockSpec((OUT_TM, D_MODEL), lambda i: (i, 0)),
        out_shape=jax.ShapeDtypeStruct((m, D_MODEL), F32),
        compiler_params=pltpu.CompilerParams(
            dimension_semantics=("parallel",), vmem_limit_bytes=_VMEM_LIMIT),
        name="outproj",
    )(ya, ybcd, proj, x2d, w_out16, post_w)


def kernel(x, pre_norm_w, post_norm_w, w_in, shift_mu, rwkv_w0, rwkv_w_up, rwkv_a0, rwkv_a_up,
           rwkv_k_k, rwkv_k_a, rwkv_r_k, rwkv_ln_w, rwkv_ln_b, attn_sinks, pool_w, pool_scale,
           sgu_norm_w, sgu_w, sgu_b, w_out):
    batch, seq, _ = x.shape
    assert x.shape == (batch, seq, D_MODEL) and seq % BLK == 0 and seq % RW_STEP == 0
    m = batch * seq
    head_id = np.arange(D_BRANCH // 2) // HEAD
    bd = jnp.asarray((head_id[:, None] == head_id[None, :]).astype(np.float32), dtype=BF16)
    t_id = np.arange(RW_TILE)
    tril = jnp.asarray(((t_id[:, None] >= t_id[None, :])
                        & (t_id[:, None] // RW_CHUNK == t_id[None, :] // RW_CHUNK)).astype(np.float32),
                       dtype=BF16)
    row_vec = lambda a: a.reshape(1, -1)

    w_in16 = w_in.astype(BF16)
    w_out16 = w_out.astype(BF16)
    h = x.reshape(m, D_MODEL)
    for l in range(DEPTH):
        proj = _inproj(h, row_vec(pre_norm_w[l]), w_in16, l, _tail_weight(w_in16, l))

        mu = shift_mu[l]
        vecs = [row_vec(mu[0:D_BRANCH]), row_vec(mu[D_BRANCH:2 * D_BRANCH]),
                row_vec(mu[2 * D_BRANCH:3 * D_BRANCH]), row_vec(mu[3 * D_BRANCH:]),
                row_vec(rwkv_w0[l]), row_vec(rwkv_a0[l]), row_vec(rwkv_k_k[l]),
                row_vec(rwkv_k_a[l]), row_vec(rwkv_r_k[l]), row_vec(rwkv_ln_w[l]),
                row_vec(rwkv_ln_b[l])]
        ya = _rwkv(proj, vecs, rwkv_w_up[l], rwkv_a_up[l], bd, tril, batch, seq)

        sgu_bias = jnp.broadcast_to(sgu_b[l].T[:, :, None], (BLK, 4, BLK)).reshape(BLK, D_BRANCH)
        ybcd = _mixers(proj, attn_sinks[l], pool_w[l], row_vec(pool_scale[l]),
                       row_vec(sgu_norm_w[l]), sgu_w[l], sgu_bias, batch, seq)

        h = _outproj(ya, ybcd, proj, h, w_out16, l, row_vec(post_norm_w[l]))
    return h.reshape(batch, seq, D_MODEL)
```

```python
import functools
import itertools

import jax
import jax.numpy as jnp
import numpy as np
from jax import lax
from jax.experimental import pallas as pl
from jax.experimental.pallas import tpu as pltpu

F32 = jnp.float32
BF16 = jnp.bfloat16
ACT = BF16

D_MODEL = 2048
DEPTH = 2
D_BRANCH = 512
HEAD = 64
N_HEADS = D_BRANCH // HEAD
LORA = 64
KV_HEADS = 2
ATT_GROUP = N_HEADS // KV_HEADS
BLK = 128
POOL_WINDOWS = (2, 4, 8, 16)
POOL_HALO = 16
NEG_INF = -1e30
NORM_EPS = 1e-6
LN_EPS = 1e-5
GN_EPS = 64e-5

A_COLS = 3 * D_BRANCH + 2 * LORA
B_COLS = D_BRANCH + 2 * KV_HEADS * HEAD
OFF_B = A_COLS
OFF_C = OFF_B + B_COLS
OFF_D = OFF_C + D_BRANCH
OFF_G = OFF_D + 2 * D_BRANCH
D_IN = OFF_G + D_MODEL

P_GATE = 0
P_SGU = 2048
P_POOL = 3072
P_Q = 3584
P_R = 4096
P_K = 4608
P_V = 5120
P_KV = 5632
P_LORA = 5888
P_WIDTH = 6144

RW_CHUNK = 64
RW_TILE = 256
RW_STEP = 2 * RW_TILE

_VMEM_LIMIT = 56 * 1024 * 1024


IN_TM = 512
IN_TN = 1024
IN_STAGE = 256
_W_RUNS = ((OFF_G, D_MODEL, P_GATE), (OFF_D, 2 * D_BRANCH, P_SGU), (OFF_C, D_BRANCH, P_POOL),
           (OFF_B, D_BRANCH, P_Q), (0, 3 * D_BRANCH, P_R),
           (OFF_B + D_BRANCH, 2 * KV_HEADS * HEAD, P_KV), (3 * D_BRANCH, 2 * LORA, P_LORA))
_W_PIECES = tuple((src + o, min(IN_STAGE, width - o), dst + o)
                  for src, width, dst in _W_RUNS for o in range(0, width, IN_STAGE))
assert sum(p[1] for p in _W_PIECES) == D_IN and P_LORA + 2 * LORA == D_IN


def _load_weight(w_hbm, layer, w_scr, stage, sem):
    def copy(idx):
        src, width, _ = _W_PIECES[idx]
        slot = idx % 2
        return pltpu.make_async_copy(w_hbm.at[layer, :, pl.ds(src, width)],
                                     stage.at[slot, :, pl.ds(0, width)], sem.at[slot])

    copy(0).start()
    for idx, (_, width, dst) in enumerate(_W_PIECES):
        if idx + 1 < len(_W_PIECES):
            copy(idx + 1).start()
        copy(idx).wait()
        w_scr[:, dst:dst + width] = stage[idx % 2, :, 0:width].astype(BF16)
    w_scr[:, D_IN:] = jnp.zeros((D_MODEL, P_WIDTH - D_IN), BF16)


def _inproj_kernel(x_ref, pw_ref, w_hbm, o_ref, w_scr, stage, sem, h_scr, *, layer):
    @pl.when(pl.program_id(0) == 0)
    def _():
        _load_weight(w_hbm, layer, w_scr, stage, sem)

    for r0 in range(0, IN_TM, 256):
        x = x_ref[r0:r0 + 256, :]
        ms = jnp.mean(x * x, axis=-1, keepdims=True)
        h_scr[r0:r0 + 256, :] = (x * lax.rsqrt(ms + NORM_EPS) * pw_ref[...]).astype(BF16)
    for c0 in range(0, P_WIDTH, IN_TN):
        o_ref[:, c0:c0 + IN_TN] = jnp.dot(h_scr[...], w_scr[:, c0:c0 + IN_TN],
                                          preferred_element_type=F32).astype(o_ref.dtype)


def _inproj(x2d, pre_w, w_in, layer):
    m = x2d.shape[0]
    return pl.pallas_call(
        functools.partial(_inproj_kernel, layer=layer),
        grid=(m // IN_TM,),
        in_specs=[
            pl.BlockSpec((IN_TM, D_MODEL), lambda i: (i, 0)),
            pl.BlockSpec((1, D_MODEL), lambda i: (0, 0)),
            pl.BlockSpec(memory_space=pl.ANY),
        ],
        out_specs=pl.BlockSpec((IN_TM, P_WIDTH), lambda i: (i, 0)),
        out_shape=jax.ShapeDtypeStruct((m, P_WIDTH), ACT),
        scratch_shapes=[
            pltpu.VMEM((D_MODEL, P_WIDTH), BF16),
            pltpu.VMEM((2, D_MODEL, IN_STAGE), F32),
            pltpu.SemaphoreType.DMA((2,)),
            pltpu.VMEM((IN_TM, D_MODEL), BF16),
        ],
        compiler_params=pltpu.CompilerParams(
            dimension_semantics=("arbitrary",), vmem_limit_bytes=_VMEM_LIMIT),
        name="inproj",
    )(x2d, pre_w, w_in)


def _bdot(a, b):
    return jnp.dot(a.astype(BF16), b.astype(BF16), preferred_element_type=F32)


def _bdot_nt(a, b):
    return lax.dot_general(a.astype(BF16), b.astype(BF16), (((1,), (1,)), ((), ())),
                           preferred_element_type=F32)


def _split3(x):
    x1 = x.astype(BF16)
    r1 = x - x1.astype(F32)
    x2 = r1.astype(BF16)
    x3 = (r1 - x2.astype(F32)).astype(BF16)
    return x1, x2, x3


def _head_sums(xs, bd):
    rows = xs[0].shape[0]
    half = bd.shape[0]
    x = jnp.concatenate(xs, axis=0).astype(BF16)
    out = jnp.concatenate(
        [jnp.dot(x[:, :half], bd, preferred_element_type=F32),
         jnp.dot(x[:, half:], bd, preferred_element_type=F32)], axis=1)
    return [out[i * rows:(i + 1) * rows] for i in range(len(xs))]


def _mm(a16, b16):
    return jnp.dot(a16, b16, preferred_element_type=F32)


def _mm_nt(a16, b16):
    return lax.dot_general(a16, b16, (((1,), (1,)), ((), ())), preferred_element_type=F32)


def _block_diag(m16, bdm16):
    return jnp.concatenate([m16, m16], axis=0) * bdm16


def _unit_lower_inverse_many(lmats, eye, masks, bdm16):
    base_mask, base_bd16, level_bd16 = masks
    c = lmats[0].shape[0]
    l16s = [l.astype(BF16) for l in lmats]
    stack2 = lambda m16: jnp.concatenate([m16, m16], axis=0)
    lds = [jnp.where(base_mask, l, 0.0) for l in lmats]
    l2s = [_mm(ld.astype(BF16), stack2(l16) * base_bd16).astype(BF16)
           for ld, l16 in zip(lds, l16s)]
    yield
    xs = [eye + ld for ld in lds]
    both = [_mm(jnp.concatenate([x.astype(BF16), l2], axis=0), stack2(l2) * bdm16)
            for x, l2 in zip(xs, l2s)]
    xs = [x + b[:c] for x, b in zip(xs, both)]
    l4s = [b[c:].astype(BF16) for b in both]
    yield
    xs = [x + _mm(x.astype(BF16), stack2(l4) * bdm16) for x, l4 in zip(xs, l4s)]
    yield
    for lvl16 in level_bd16:
        ts = [_mm(x.astype(BF16), stack2(l16) * lvl16).astype(BF16) for x, l16 in zip(xs, l16s)]
        yield
        xs = [x + _mm(t, stack2(x.astype(BF16)) * bdm16) for x, t in zip(xs, ts)]
        yield
    return xs


_SET_BF16 = ("a", "b", "k", "r", "v", "b_e", "k_e")
_SET_NAMES = _SET_BF16 + ("g_all", "bonus")
_PREP_MXU_DELAY = 4
_EXP_NEG_HALF = float(np.exp(-0.5))


def _rwkv_prep(z_refs, row0, w, carries, dst):
    tile, c = RW_TILE, RW_CHUNK
    row = lax.broadcasted_iota(jnp.int32, (tile, 1), 0)

    def shifted(z_ref, mu_ref, p_scr):
        z = z_ref[row0:row0 + tile, :].astype(F32)
        zp = jnp.where(row == 0, p_scr[0:1, :], pltpu.roll(z, 1, axis=0))
        p_scr[0:1, :] = z[tile - 1:tile, :]
        return z + mu_ref[...] * (zp - z)

    lo = shifted(z_refs[3], w["mu_lo"], carries[3])
    lora_w = _bdot(jnp.tanh(lo[:, :LORA]), w["wup"][...])
    lora_a = _bdot(lo[:, LORA:], w["aup"][...])
    yield
    r = shifted(z_refs[0], w["mu_r"], carries[0])
    k = shifted(z_refs[1], w["mu_k"], carries[1])
    v = shifted(z_refs[2], w["mu_v"], carries[2])
    logw = -_EXP_NEG_HALF * jax.nn.sigmoid(w["w0"][...] + lora_w)
    asig = jax.nn.sigmoid(w["a0"][...] + lora_a)
    logw_terms = _split3(logw)
    kk = k * w["kkw"][...]
    k2 = k * (1.0 + (asig - 1.0) * w["kaw"][...])
    sum_terms = [kk * kk, r * k2 * w["rkw"][...]]
    for _ in range(_PREP_MXU_DELAY):
        yield
    tril = w["tril"][...]
    cum = sum(jnp.dot(tril, p, preferred_element_type=F32) for p in logw_terms)
    kk_ss, rk_sum = _head_sums(sum_terms, w["bd"][...])
    yield
    kk = kk * lax.rsqrt(jnp.maximum(kk_ss, 1e-24))
    dst["bonus"][...] = rk_sum * v
    for ck in range(tile // c):
        sl = slice(ck * c, (ck + 1) * c)
        lw, cm = logw[sl], cum[sl]
        total = cm[c - 1:c, :]
        g_inv = jnp.exp(-cm)
        g_all = jnp.exp(total)
        g_tail = g_all * g_inv
        b_c = kk[sl] * asig[sl]
        dst["a"][sl, :] = (-kk[sl] * jnp.exp(cm - lw)).astype(BF16)
        dst["b"][sl, :] = (b_c * g_inv).astype(BF16)
        dst["k"][sl, :] = (k2[sl] * g_inv).astype(BF16)
        dst["r"][sl, :] = (r[sl] * jnp.exp(cm)).astype(BF16)
        dst["v"][sl, :] = v[sl].astype(BF16)
        dst["b_e"][sl, :] = (b_c * g_tail).astype(BF16)
        dst["k_e"][sl, :] = (k2[sl] * g_tail).astype(BF16)
        dst["g_all"][ck:ck + 1, :] = g_all
    yield


def _rwkv_main(src, w, consts, s_scr, y_scr, o_ref, row0):
    tile, c, pw = RW_TILE, RW_CHUNK, 2 * HEAD
    eye, masks, tri2, bdm4_16, bdm16, bdm = consts
    n_chunks, n_pairs = tile // c, N_HEADS // 2
    items = [(ck, p) for ck in range(n_chunks) for p in range(n_pairs)]

    def part(name, it):
        ck, p = it
        return src[name][ck * c:(ck + 1) * c, p * pw:(p + 1) * pw]

    a2 = [part("a", it) for it in items]
    r2 = [part("r", it) for it in items]
    l_ab, m_rb16, ak16 = [], [], []
    for a, x, it in zip(a2, r2, items):
        b, k_ = part("b", it), part("k", it)
        pr = _mm_nt(jnp.concatenate([a, x], axis=0),
                    jnp.concatenate([b, b, k_, k_], axis=0) * bdm4_16)
        ab = jnp.where(tri2, pr[:, :pw], 0.0)
        l_ab.append(ab[:c])
        m_rb16.append(ab[c:].astype(BF16))
        ak16.append(jnp.where(tri2, pr[:, pw:], 0.0).astype(BF16))
    yield
    qy = [_mm(x, _block_diag(part("v", it), bdm16))
          for x, it in zip(ak16, items)]
    qv16 = [q[:c].astype(BF16) for q in qy]
    yv = [q[c:] for q in qy]
    vk = [jnp.where(bdm, lax.dot_general(part("v", it), part("k_e", it), (((0,), (0,)), ((), ())),
                                         preferred_element_type=F32), 0.0) for it in items]
    yield
    tinv = yield from _unit_lower_inverse_many(l_ab, eye, masks, bdm16)
    wu = [_mm(t.astype(BF16),
              jnp.concatenate([_block_diag(a, bdm16), _block_diag(q, bdm16)], axis=1))
          for t, a, q in zip(tinv, a2, qv16)]
    wa16 = [x[:, :pw].astype(BF16) for x in wu]
    uv = [x[:, pw:] for x in wu]
    uvt = [x.T for x in uv]
    war = [jnp.concatenate([x, y], axis=0) for x, y in zip(wa16, r2)]
    yield

    state = [s_scr[p] for p in range(n_pairs)]
    for ck in range(n_chunks):
        idx = [ck * n_pairs + p for p in range(n_pairs)]
        s16 = [s.astype(BF16) for s in state]
        uy = [_mm_nt(war[i], s16[p]) for p, i in enumerate(idx)]
        ut = [_mm_nt(s16[p], wa16[i]) + uvt[i] for p, i in enumerate(idx)]
        for p, i in enumerate(idx):
            u16 = (uy[p][:c] + uv[i]).astype(BF16)
            y = uy[p][c:] + yv[i] + _mm(m_rb16[i], _block_diag(u16, bdm16))
            y_scr[ck * c:(ck + 1) * c, p * pw:(p + 1) * pw] = y
        state = [state[p] * src["g_all"][ck:ck + 1, p * pw:(p + 1) * pw] + vk[i]
                 + jnp.where(bdm, _mm(ut[p].astype(BF16), part("b_e", items[i])), 0.0)
                 for p, i in enumerate(idx)]
        yield
    for p in range(n_pairs):
        s_scr[p] = state[p]

    y = y_scr[...]
    bd = w["bd"][...]
    mu = _head_sums([y], bd)[0] * (1.0 / HEAD)
    d = y - mu
    var = _head_sums([d * d], bd)[0] * (1.0 / HEAD)
    out = d * lax.rsqrt(var + GN_EPS) * w["lnw"][...] + w["lnb"][...] + src["bonus"][...]
    o_ref[row0:row0 + tile, :] = out.astype(o_ref.dtype)
    yield


def _interleave(*streams):
    for _ in itertools.zip_longest(*streams):
        pass


_W_NAMES = ("mu_r", "mu_k", "mu_v", "mu_lo", "w0", "wup", "a0", "aup", "kkw", "kaw", "rkw",
            "lnw", "lnb", "bd", "tril")


def _rwkv_kernel(*refs):
    cur, nxt = refs[0:4], refs[4:8]
    w = dict(zip(_W_NAMES, refs[8:8 + len(_W_NAMES)]))
    rest = refs[8 + len(_W_NAMES):]
    o_ref, s_scr, y_scr = rest[0], rest[1], rest[2]
    carries = rest[3:7]
    n_set = len(_SET_NAMES)
    set0 = dict(zip(_SET_NAMES, rest[7:7 + n_set]))
    set1 = dict(zip(_SET_NAMES, rest[7 + n_set:7 + 2 * n_set]))
    n = pl.program_id(1)

    @pl.when(n == 0)
    def _():
        s_scr[...] = jnp.zeros_like(s_scr)
        for p_scr in carries:
            p_scr[...] = jnp.zeros_like(p_scr)
        _interleave(_rwkv_prep(cur, 0, w, carries, set0))

    c, pw = RW_CHUNK, 2 * HEAD
    ci = lax.broadcasted_iota(jnp.int32, (c, pw), 0)
    cj = lax.broadcasted_iota(jnp.int32, (c, pw), 1) % HEAD
    eye = (ci == cj).astype(F32)
    si = lax.broadcasted_iota(jnp.int32, (2 * c, pw), 0)
    sj = lax.broadcasted_iota(jnp.int32, (2 * c, pw), 1) % HEAD
    tri2 = ((si < c) & (si > sj)) | ((si >= c) & ((si - c) >= sj))
    bi = lax.broadcasted_iota(jnp.int32, (2 * pw, pw), 0)
    bj = lax.broadcasted_iota(jnp.int32, (2 * pw, pw), 1)
    bdm_bool = ((bi // HEAD) % 2) == (bj // HEAD)
    bdm4_16 = bdm_bool.astype(BF16)
    hi = lax.broadcasted_iota(jnp.int32, (pw, pw), 0)
    hj = lax.broadcasted_iota(jnp.int32, (pw, pw), 1)
    di, dj = hi % HEAD, hj % HEAD
    same_head = (hi // HEAD) == (hj // HEAD)
    near = lambda b: same_head & ((di // b) == (dj // b))
    ring = lambda b: same_head & ((di // (2 * b)) == (dj // (2 * b))) & ((di // b) != (dj // b))
    masks = ((ci // 8) == (cj // 8), near(8).astype(BF16),
             tuple(ring(b).astype(BF16) for b in (8, 16, 32)))
    consts = (eye, masks, tri2, bdm4_16, bdm4_16[:pw], bdm_bool[:pw])

    _interleave(_rwkv_prep(cur, RW_TILE, w, carries, set1),
                _rwkv_main(set0, w, consts, s_scr, y_scr, o_ref, 0))
    _interleave(_rwkv_prep(nxt, 0, w, carries, set0),
                _rwkv_main(set1, w, consts, s_scr, y_scr, o_ref, RW_TILE))


def _rwkv(proj, vecs, wup, aup, bd, tril, batch, seq):
    m = proj.shape[0]
    ns = seq // RW_STEP
    row = lambda b, n: b * ns + n
    nxt = lambda b, n: b * ns + jnp.minimum(n + 1, ns - 1)
    vec = lambda width: pl.BlockSpec((1, width), lambda b, n: (0, 0))
    full = lambda a: pl.BlockSpec(a.shape, lambda b, n: (0, 0))

    def token_specs(row_fn):
        wide = lambda cb: pl.BlockSpec((RW_STEP, D_BRANCH), lambda b, n: (row_fn(b, n), cb))
        return [wide(P_R // D_BRANCH), wide(P_K // D_BRANCH), wide(P_V // D_BRANCH),
                pl.BlockSpec((RW_STEP, 2 * LORA), lambda b, n: (row_fn(b, n), P_LORA // (2 * LORA)))]

    in_specs = token_specs(row) + token_specs(nxt) + [
        vec(D_BRANCH), vec(D_BRANCH), vec(D_BRANCH), vec(2 * LORA),
        vec(D_BRANCH), full(wup), vec(D_BRANCH), full(aup),
        vec(D_BRANCH), vec(D_BRANCH), vec(D_BRANCH), vec(D_BRANCH), vec(D_BRANCH),
        full(bd), full(tril),
    ]
    prep_set = ([pltpu.VMEM((RW_TILE, D_BRANCH), BF16) for _ in _SET_BF16]
                + [pltpu.VMEM((8, D_BRANCH), F32), pltpu.VMEM((RW_TILE, D_BRANCH), F32)])
    return pl.pallas_call(
        _rwkv_kernel,
        grid=(batch, ns),
        in_specs=in_specs,
        out_specs=pl.BlockSpec((RW_STEP, D_BRANCH), lambda b, n: (row(b, n), 0)),
        out_shape=jax.ShapeDtypeStruct((m, D_BRANCH), ACT),
        scratch_shapes=[
            pltpu.VMEM((N_HEADS // 2, 2 * HEAD, 2 * HEAD), F32),
            pltpu.VMEM((RW_TILE, D_BRANCH), F32),
            pltpu.VMEM((8, D_BRANCH), F32), pltpu.VMEM((8, D_BRANCH), F32),
            pltpu.VMEM((8, D_BRANCH), F32), pltpu.VMEM((8, 2 * LORA), F32),
        ] + prep_set + prep_set,
        compiler_params=pltpu.CompilerParams(
            dimension_semantics=("parallel", "arbitrary"), vmem_limit_bytes=_VMEM_LIMIT),
        name="rwkv",
    )(*([proj] * 8), *vecs[:4], vecs[4], wup, vecs[5], aup, *vecs[6:], bd, tril)


_SLOPES = tuple(2.0 ** (-8.0 * (h + 1) / N_HEADS) for h in range(N_HEADS))
_SQRT_HALF = float(np.sqrt(0.5))


def _attention_bias():
    t = np.arange(BLK)[:, None]
    s = np.arange(2 * BLK)[None, :]
    dist = t + BLK - s
    out = np.empty((2, N_HEADS * BLK, 2 * BLK), np.float32)
    for first in (0, 1):
        valid = (dist >= 0) & (dist < BLK) & ((s >= BLK) | (first == 1))
        for h in range(N_HEADS):
            out[first, h * BLK:(h + 1) * BLK] = np.where(valid, -_SLOPES[h] * dist, NEG_INF)
    return out


def _mixers_kernel(sinks_ref, bias_ref, q_ref, kvc_ref, kvp_ref, zc_ref, zcp_ref, zd_ref,
                   poolw_ref, pscale_ref, nw_ref, sw_ref, sb_ref, o_ref, y_scr):
    n = pl.program_id(1)

    q = q_ref[...] * (HEAD ** -0.5)
    kv = jnp.concatenate([kvp_ref[...], kvc_ref[...]], axis=0)
    rows = ATT_GROUP * BLK
    head_in_group = lax.broadcasted_iota(jnp.int32, (rows, 1), 0) // BLK

    def per_row(values):
        col = jnp.full((rows, 1), values[-1], F32)
        for j in range(ATT_GROUP - 2, -1, -1):
            col = jnp.where(head_in_group == j, values[j], col)
        return col

    scores = []
    for g in range(KV_HEADS):
        qg = jnp.concatenate(
            [q[:, (g * ATT_GROUP + j) * HEAD:(g * ATT_GROUP + j + 1) * HEAD]
             for j in range(ATT_GROUP)], axis=0)
        scores.append(_bdot_nt(qg, kv[:, g * HEAD:(g + 1) * HEAD]))
    probs, dens = [], []
    for g in range(KV_HEADS):
        heads = range(g * ATT_GROUP, (g + 1) * ATT_GROUP)
        sink = per_row([sinks_ref[h] for h in heads])
        s = scores[g] + bias_ref[g * rows:(g + 1) * rows, :]
        mx = jnp.maximum(jnp.max(s, axis=-1, keepdims=True), sink)
        p = jnp.exp(s - mx)
        probs.append(p)
        dens.append(jnp.sum(p, axis=-1, keepdims=True) + jnp.exp(sink - mx))
    for g in range(KV_HEADS):
        vg = kv[:, (KV_HEADS + g) * HEAD:(KV_HEADS + g + 1) * HEAD]
        og = _bdot(probs[g], vg) / dens[g]
        for j in range(ATT_GROUP):
            h = g * ATT_GROUP + j
            y_scr[:, h * HEAD:(h + 1) * HEAD] = og[j * BLK:(j + 1) * BLK, :]

    zfull = jnp.concatenate([jnp.where(n > 0, zcp_ref[...], 0.0), zc_ref[...]],
                            axis=0).astype(F32)
    pos = n * BLK + lax.broadcasted_iota(jnp.int32, (BLK, 1), 0) + 1
    for g, w in enumerate(POOL_WINDOWS):
        gs = slice(g * BLK, (g + 1) * BLK)
        zg = zfull[:, gs]
        acc = zg
        step = 1
        while step < w:
            acc = acc + pltpu.roll(acc, step, axis=0)
            step *= 2
        cnt = jnp.minimum(pos, w).astype(F32)
        pooled = acc[POOL_HALO:, :] / cnt - zg[POOL_HALO:, :]
        yg = _bdot(pooled, poolw_ref[g]) * pscale_ref[:, gs]
        y_scr[:, D_BRANCH + g * BLK:D_BRANCH + (g + 1) * BLK] = yg

    zd = zd_ref[...].astype(F32)
    gz = 0.5 * zd * (1.0 + lax.erf(zd * _SQRT_HALF))
    u = gz[:, :D_BRANCH]
    vv = gz[:, D_BRANCH:]
    mu = jnp.mean(vv, axis=-1, keepdims=True)
    dv = vv - mu
    var = jnp.mean(dv * dv, axis=-1, keepdims=True)
    vn = dv * lax.rsqrt(var + LN_EPS) * nw_ref[...]
    ri = lax.broadcasted_iota(jnp.int32, (BLK, BLK), 0)
    rj = lax.broadcasted_iota(jnp.int32, (BLK, BLK), 1)
    causal = ri >= rj
    for g in range(4):
        gs = slice(g * BLK, (g + 1) * BLK)
        ws = jnp.where(causal, sw_ref[g], 0.0)
        sg = _bdot(ws, vn[:, gs]) + sb_ref[:, gs]
        y_scr[:, 2 * D_BRANCH + g * BLK:2 * D_BRANCH + (g + 1) * BLK] = u[:, gs] * sg
    o_ref[...] = y_scr[...].astype(o_ref.dtype)


def _mixers(proj, sinks, pool_w, pool_scale, norm_w, sgu_w, sgu_bias, batch, seq):
    m = proj.shape[0]
    nb = seq // BLK
    row = lambda b, n: b * nb + n
    halo = BLK // POOL_HALO
    bias = jnp.asarray(_attention_bias())
    in_specs = [
        pl.BlockSpec(memory_space=pltpu.SMEM),
        pl.BlockSpec((None,) + bias.shape[1:], lambda b, n: (jnp.minimum(n, 1), 0, 0)),
        pl.BlockSpec((BLK, D_BRANCH), lambda b, n: (row(b, n), P_Q // D_BRANCH)),
        pl.BlockSpec((BLK, 256), lambda b, n: (row(b, n), P_KV // 256)),
        pl.BlockSpec((BLK, 256), lambda b, n: (row(b, jnp.maximum(n - 1, 0)), P_KV // 256)),
        pl.BlockSpec((BLK, D_BRANCH), lambda b, n: (row(b, n), P_POOL // D_BRANCH)),
        pl.BlockSpec((POOL_HALO, D_BRANCH),
                     lambda b, n: (jnp.maximum(row(b, n) * halo - 1, 0), P_POOL // D_BRANCH)),
        pl.BlockSpec((BLK, 2 * D_BRANCH), lambda b, n: (row(b, n), P_SGU // (2 * D_BRANCH))),
        pl.BlockSpec((4, BLK, BLK), lambda b, n: (0, 0, 0)),
        pl.BlockSpec((1, D_BRANCH), lambda b, n: (0, 0)),
        pl.BlockSpec((1, D_BRANCH), lambda b, n: (0, 0)),
        pl.BlockSpec((4, BLK, BLK), lambda b, n: (0, 0, 0)),
        pl.BlockSpec((BLK, D_BRANCH), lambda b, n: (0, 0)),
    ]
    return pl.pallas_call(
        _mixers_kernel,
        grid=(batch, nb),
        in_specs=in_specs,
        out_specs=pl.BlockSpec((BLK, 3 * D_BRANCH), lambda b, n: (row(b, n), 0)),
        out_shape=jax.ShapeDtypeStruct((m, 3 * D_BRANCH), ACT),
        scratch_shapes=[pltpu.VMEM((BLK, 3 * D_BRANCH), F32)],
        compiler_params=pltpu.CompilerParams(
            dimension_semantics=("parallel", "parallel"), vmem_limit_bytes=_VMEM_LIMIT),
        name="mixers",
    )(sinks, bias, proj, proj, proj, proj, proj, proj, pool_w, pool_scale, norm_w, sgu_w,
      sgu_bias)


OUT_TM = 512


def _outproj_kernel(ya_ref, yb_ref, g_ref, x_ref, w_ref, pw_ref, o_ref):
    g = g_ref[...].astype(F32)
    sg = g * jax.nn.sigmoid(g)
    ya = ya_ref[...].astype(F32)
    yb = yb_ref[...].astype(F32)
    acc = jnp.dot((ya * sg[:, :D_BRANCH]).astype(BF16), w_ref[:D_BRANCH, :],
                  preferred_element_type=F32)
    acc = acc + jnp.dot((yb * sg[:, D_BRANCH:]).astype(BF16), w_ref[D_BRANCH:, :],
                        preferred_element_type=F32)
    ms = jnp.mean(acc * acc, axis=-1, keepdims=True)
    o_ref[...] = x_ref[...] + acc * lax.rsqrt(ms + NORM_EPS) * pw_ref[...]


def _outproj(ya, ybcd, proj, x2d, w_out16, layer, post_w):
    m = x2d.shape[0]
    return pl.pallas_call(
        _outproj_kernel,
        grid=(m // OUT_TM,),
        in_specs=[
            pl.BlockSpec((OUT_TM, D_BRANCH), lambda i: (i, 0)),
            pl.BlockSpec((OUT_TM, 3 * D_BRANCH), lambda i: (i, 0)),
            pl.BlockSpec((OUT_TM, D_MODEL), lambda i: (i, P_GATE // D_MODEL)),
            pl.BlockSpec((OUT_TM, D_MODEL), lambda i: (i, 0)),
            pl.BlockSpec((None, D_MODEL, D_MODEL), lambda i: (layer, 0, 0)),
            pl.BlockSpec((1, D_MODEL), lambda i: (0, 0)),
        ],
        out_specs=pl.BlockSpec((OUT_TM, D_MODEL), lambda i: (i, 0)),
        out_shape=jax.ShapeDtypeStruct((m, D_MODEL), F32),
        compiler_params=pltpu.CompilerParams(
            dimension_semantics=("parallel",), vmem_limit_bytes=_VMEM_LIMIT),
        name="outproj",
    )(ya, ybcd, proj, x2d, w_out16, post_w)


def kernel(x, pre_norm_w, post_norm_w, w_in, shift_mu, rwkv_w0, rwkv_w_up, rwkv_a0, rwkv_a_up,
           rwkv_k_k, rwkv_k_a, rwkv_r_k, rwkv_ln_w, rwkv_ln_b, attn_sinks, pool_w, pool_scale,
           sgu_norm_w, sgu_w, sgu_b, w_out):
    batch, seq, _ = x.shape
    assert x.shape == (batch, seq, D_MODEL) and seq % BLK == 0 and seq % RW_STEP == 0
    m = batch * seq
    head_id = np.arange(D_BRANCH // 2) // HEAD
    bd = jnp.asarray((head_id[:, None] == head_id[None, :]).astype(np.float32), dtype=BF16)
    t_id = np.arange(RW_TILE)
    tril = jnp.asarray(((t_id[:, None] >= t_id[None, :])
                        & (t_id[:, None] // RW_CHUNK == t_id[None, :] // RW_CHUNK)).astype(np.float32),
                       dtype=BF16)
    row_vec = lambda a: a.reshape(1, -1)

    w_out16 = w_out.astype(BF16)
    h = x.reshape(m, D_MODEL)
    for l in range(DEPTH):
        proj = _inproj(h, row_vec(pre_norm_w[l]), w_in, l)

        mu = shift_mu[l]
        vecs = [row_vec(mu[0:D_BRANCH]), row_vec(mu[D_BRANCH:2 * D_BRANCH]),
                row_vec(mu[2 * D_BRANCH:3 * D_BRANCH]), row_vec(mu[3 * D_BRANCH:]),
                row_vec(rwkv_w0[l]), row_vec(rwkv_a0[l]), row_vec(rwkv_k_k[l]),
                row_vec(rwkv_k_a[l]), row_vec(rwkv_r_k[l]), row_vec(rwkv_ln_w[l]),
                row_vec(rwkv_ln_b[l])]
        ya = _rwkv(proj, vecs, rwkv_w_up[l], rwkv_a_up[l], bd, tril, batch, seq)

        sgu_bias = jnp.broadcast_to(sgu_b[l].T[:, :, None], (BLK, 4, BLK)).reshape(BLK, D_BRANCH)
        ybcd = _mixers(proj, attn_sinks[l], pool_w[l], row_vec(pool_scale[l]),
                       row_vec(sgu_norm_w[l]), sgu_w[l], sgu_bias, batch, seq)

        h = _outproj(ya, ybcd, proj, h, w_out16, l, row_vec(post_norm_w[l]))
    return h.reshape(batch, seq, D_MODEL)
```

```python
import functools
import itertools

import jax
import jax.numpy as jnp
import numpy as np
from jax import lax
from jax.experimental import pallas as pl
from jax.experimental.pallas import tpu as pltpu

F32 = jnp.float32
BF16 = jnp.bfloat16
ACT = BF16

D_MODEL = 2048
DEPTH = 2
D_BRANCH = 512
HEAD = 64
N_HEADS = D_BRANCH // HEAD
LORA = 64
KV_HEADS = 2
ATT_GROUP = N_HEADS // KV_HEADS
BLK = 128
POOL_WINDOWS = (2, 4, 8, 16)
POOL_HALO = 16
NEG_INF = -1e30
NORM_EPS = 1e-6
LN_EPS = 1e-5
GN_EPS = 64e-5

A_COLS = 3 * D_BRANCH + 2 * LORA
B_COLS = D_BRANCH + 2 * KV_HEADS * HEAD
OFF_B = A_COLS
OFF_C = OFF_B + B_COLS
OFF_D = OFF_C + D_BRANCH
OFF_G = OFF_D + 2 * D_BRANCH
D_IN = OFF_G + D_MODEL

P_GATE = 0
P_SGU = 2048
P_POOL = 3072
P_Q = 3584
P_R = 4096
P_K = 4608
P_V = 5120
P_KV = 5632
P_LORA = 5888
P_WIDTH = 6144

RW_CHUNK = 64
RW_TILE = 256
RW_STEP = 2 * RW_TILE

_VMEM_LIMIT = 56 * 1024 * 1024


IN_TM = 512
IN_TN = 1024
IN_STAGE = 256
_W_RUNS = ((OFF_G, D_MODEL, P_GATE), (OFF_D, 2 * D_BRANCH, P_SGU), (OFF_C, D_BRANCH, P_POOL),
           (OFF_B, D_BRANCH, P_Q), (0, 3 * D_BRANCH, P_R),
           (OFF_B + D_BRANCH, 2 * KV_HEADS * HEAD, P_KV), (3 * D_BRANCH, 2 * LORA, P_LORA))
_W_PIECES = tuple((src + o, min(IN_STAGE, width - o), dst + o)
                  for src, width, dst in _W_RUNS for o in range(0, width, IN_STAGE))
assert sum(p[1] for p in _W_PIECES) == D_IN and P_LORA + 2 * LORA == D_IN


def _load_weight(w_hbm, layer, w_scr, stage, sem):
    def copy(idx):
        src, width, _ = _W_PIECES[idx]
        slot = idx % 2
        return pltpu.make_async_copy(w_hbm.at[layer, :, pl.ds(src, width)],
                                     stage.at[slot, :, pl.ds(0, width)], sem.at[slot])

    copy(0).start()
    for idx, (_, width, dst) in enumerate(_W_PIECES):
        if idx + 1 < len(_W_PIECES):
            copy(idx + 1).start()
        copy(idx).wait()
        w_scr[:, dst:dst + width] = stage[idx % 2, :, 0:width].astype(BF16)
    w_scr[:, D_IN:] = jnp.zeros((D_MODEL, P_WIDTH - D_IN), BF16)


def _inproj_kernel(x_ref, pw_ref, w_hbm, o_ref, w_scr, stage, sem, h_scr, *, layer):
    @pl.when(pl.program_id(0) == 0)
    def _():
        _load_weight(w_hbm, layer, w_scr, stage, sem)

    for r0 in range(0, IN_TM, 256):
        x = x_ref[r0:r0 + 256, :]
        ms = jnp.mean(x * x, axis=-1, keepdims=True)
        h_scr[r0:r0 + 256, :] = (x * lax.rsqrt(ms + NORM_EPS) * pw_ref[...]).astype(BF16)
    for c0 in range(0, P_WIDTH, IN_TN):
        o_ref[:, c0:c0 + IN_TN] = jnp.dot(h_scr[...], w_scr[:, c0:c0 + IN_TN],
                                          preferred_element_type=F32).astype(o_ref.dtype)


def _inproj(x2d, pre_w, w_in, layer):
    m = x2d.shape[0]
    return pl.pallas_call(
        functools.partial(_inproj_kernel, layer=layer),
        grid=(m // IN_TM,),
        in_specs=[
            pl.BlockSpec((IN_TM, D_MODEL), lambda i: (i, 0)),
            pl.BlockSpec((1, D_MODEL), lambda i: (0, 0)),
            pl.BlockSpec(memory_space=pl.ANY),
        ],
        out_specs=pl.BlockSpec((IN_TM, P_WIDTH), lambda i: (i, 0)),
        out_shape=jax.ShapeDtypeStruct((m, P_WIDTH), ACT),
        scratch_shapes=[
            pltpu.VMEM((D_MODEL, P_WIDTH), BF16),
            pltpu.VMEM((2, D_MODEL, IN_STAGE), F32),
            pltpu.SemaphoreType.DMA((2,)),
            pltpu.VMEM((IN_TM, D_MODEL), BF16),
        ],
        compiler_params=pltpu.CompilerParams(
            dimension_semantics=("arbitrary",), vmem_limit_bytes=_VMEM_LIMIT),
        name="inproj",
    )(x2d, pre_w, w_in)


def _bdot(a, b):
    return jnp.dot(a.astype(BF16), b.astype(BF16), preferred_element_type=F32)


def _bdot_nt(a, b):
    return lax.dot_general(a.astype(BF16), b.astype(BF16), (((1,), (1,)), ((), ())),
                           preferred_element_type=F32)


def _split3(x):
    x1 = x.astype(BF16)
    r1 = x - x1.astype(F32)
    x2 = r1.astype(BF16)
    x3 = (r1 - x2.astype(F32)).astype(BF16)
    return x1, x2, x3


def _head_sums(xs, bd):
    rows = xs[0].shape[0]
    half = bd.shape[0]
    x = jnp.concatenate(xs, axis=0).astype(BF16)
    out = jnp.concatenate(
        [jnp.dot(x[:, :half], bd, preferred_element_type=F32),
         jnp.dot(x[:, half:], bd, preferred_element_type=F32)], axis=1)
    return [out[i * rows:(i + 1) * rows] for i in range(len(xs))]


def _mm(a16, b16):
    return jnp.dot(a16, b16, preferred_element_type=F32)


def _mm_nt(a16, b16):
    return lax.dot_general(a16, b16, (((1,), (1,)), ((), ())), preferred_element_type=F32)


def _block_diag(m16, bdm16):
    return jnp.concatenate([m16, m16], axis=0) * bdm16


def _unit_lower_inverse_many(lmats, eye, masks, bdm16):
    base_mask, base_bd16, level_bd16 = masks
    c = lmats[0].shape[0]
    l16s = [l.astype(BF16) for l in lmats]
    stack2 = lambda m16: jnp.concatenate([m16, m16], axis=0)
    lds = [jnp.where(base_mask, l, 0.0) for l in lmats]
    l2s = [_mm(ld.astype(BF16), stack2(l16) * base_bd16).astype(BF16)
           for ld, l16 in zip(lds, l16s)]
    yield
    xs = [eye + ld for ld in lds]
    both = [_mm(jnp.concatenate([x.astype(BF16), l2], axis=0), stack2(l2) * bdm16)
            for x, l2 in zip(xs, l2s)]
    xs = [x + b[:c] for x, b in zip(xs, both)]
    l4s = [b[c:].astype(BF16) for b in both]
    yield
    xs = [x + _mm(x.astype(BF16), stack2(l4) * bdm16) for x, l4 in zip(xs, l4s)]
    yield
    for lvl16 in level_bd16:
        ts = [_mm(x.astype(BF16), stack2(l16) * lvl16).astype(BF16) for x, l16 in zip(xs, l16s)]
        yield
        xs = [x + _mm(t, stack2(x.astype(BF16)) * bdm16) for x, t in zip(xs, ts)]
        yield
    return xs


_SET_BF16 = ("a", "b", "k", "r", "v", "b_e", "k_e")
_SET_NAMES = _SET_BF16 + ("g_all", "bonus")
_PREP_MXU_DELAY = 4
_EXP_NEG_HALF = float(np.exp(-0.5))


def _rwkv_prep(z_refs, row0, w, carries, dst):
    tile, c = RW_TILE, RW_CHUNK
    row = lax.broadcasted_iota(jnp.int32, (tile, 1), 0)

    def shifted(z_ref, mu_ref, p_scr):
        z = z_ref[row0:row0 + tile, :].astype(F32)
        zp = jnp.where(row == 0, p_scr[0:1, :], pltpu.roll(z, 1, axis=0))
        p_scr[0:1, :] = z[tile - 1:tile, :]
        return z + mu_ref[...] * (zp - z)

    lo = shifted(z_refs[3], w["mu_lo"], carries[3])
    lora_w = _bdot(jnp.tanh(lo[:, :LORA]), w["wup"][...])
    lora_a = _bdot(lo[:, LORA:], w["aup"][...])
    yield
    r = shifted(z_refs[0], w["mu_r"], carries[0])
    k = shifted(z_refs[1], w["mu_k"], carries[1])
    v = shifted(z_refs[2], w["mu_v"], carries[2])
    logw = -_EXP_NEG_HALF * jax.nn.sigmoid(w["w0"][...] + lora_w)
    asig = jax.nn.sigmoid(w["a0"][...] + lora_a)
    logw_terms = _split3(logw)
    kk = k * w["kkw"][...]
    k2 = k * (1.0 + (asig - 1.0) * w["kaw"][...])
    sum_terms = [kk * kk, r * k2 * w["rkw"][...]]
    for _ in range(_PREP_MXU_DELAY):
        yield
    tril = w["tril"][...]
    cum = sum(jnp.dot(tril, p, preferred_element_type=F32) for p in logw_terms)
    kk_ss, rk_sum = _head_sums(sum_terms, w["bd"][...])
    yield
    kk = kk * lax.rsqrt(jnp.maximum(kk_ss, 1e-24))
    dst["bonus"][...] = rk_sum * v
    for ck in range(tile // c):
        sl = slice(ck * c, (ck + 1) * c)
        lw, cm = logw[sl], cum[sl]
        total = cm[c - 1:c, :]
        g_inv = jnp.exp(-cm)
        g_all = jnp.exp(total)
        g_tail = g_all * g_inv
        b_c = kk[sl] * asig[sl]
        dst["a"][sl, :] = (-kk[sl] * jnp.exp(cm - lw)).astype(BF16)
        dst["b"][sl, :] = (b_c * g_inv).astype(BF16)
        dst["k"][sl, :] = (k2[sl] * g_inv).astype(BF16)
        dst["r"][sl, :] = (r[sl] * jnp.exp(cm)).astype(BF16)
        dst["v"][sl, :] = v[sl].astype(BF16)
        dst["b_e"][sl, :] = (b_c * g_tail).astype(BF16)
        dst["k_e"][sl, :] = (k2[sl] * g_tail).astype(BF16)
        dst["g_all"][ck:ck + 1, :] = g_all
    yield


def _rwkv_main(src, w, consts, s_scr, y_scr, o_ref, row0):
    tile, c, pw = RW_TILE, RW_CHUNK, 2 * HEAD
    eye, masks, tri2, bdm4_16, bdm16, bdm = consts
    n_chunks, n_pairs = tile // c, N_HEADS // 2
    items = [(ck, p) for ck in range(n_chunks) for p in range(n_pairs)]

    def part(name, it):
        ck, p = it
        return src[name][ck * c:(ck + 1) * c, p * pw:(p + 1) * pw]

    a2 = [part("a", it) for it in items]
    r2 = [part("r", it) for it in items]
    l_ab, m_rb16, ak16 = [], [], []
    for a, x, it in zip(a2, r2, items):
        b, k_ = part("b", it), part("k", it)
        pr = _mm_nt(jnp.concatenate([a, x], axis=0),
                    jnp.concatenate([b, b, k_, k_], axis=0) * bdm4_16)
        ab = jnp.where(tri2, pr[:, :pw], 0.0)
        l_ab.append(ab[:c])
        m_rb16.append(ab[c:].astype(BF16))
        ak16.append(jnp.where(tri2, pr[:, pw:], 0.0).astype(BF16))
    yield
    qy = [_mm(x, _block_diag(part("v", it), bdm16))
          for x, it in zip(ak16, items)]
    qv16 = [q[:c].astype(BF16) for q in qy]
    yv = [q[c:] for q in qy]
    vk = [jnp.where(bdm, lax.dot_general(part("v", it), part("k_e", it), (((0,), (0,)), ((), ())),
                                         preferred_element_type=F32), 0.0) for it in items]
    yield
    tinv = yield from _unit_lower_inverse_many(l_ab, eye, masks, bdm16)
    wu = [_mm(t.astype(BF16),
              jnp.concatenate([_block_diag(a, bdm16), _block_diag(q, bdm16)], axis=1))
          for t, a, q in zip(tinv, a2, qv16)]
    wa16 = [x[:, :pw].astype(BF16) for x in wu]
    uv = [x[:, pw:] for x in wu]
    uvt = [x.T for x in uv]
    war = [jnp.concatenate([x, y], axis=0) for x, y in zip(wa16, r2)]
    yield

    state = [s_scr[p] for p in range(n_pairs)]
    for ck in range(n_chunks):
        idx = [ck * n_pairs + p for p in range(n_pairs)]
        s16 = [s.astype(BF16) for s in state]
        uy = [_mm_nt(war[i], s16[p]) for p, i in enumerate(idx)]
        ut = [_mm_nt(s16[p], wa16[i]) + uvt[i] for p, i in enumerate(idx)]
        for p, i in enumerate(idx):
            u16 = (uy[p][:c] + uv[i]).astype(BF16)
            y = uy[p][c:] + yv[i] + _mm(m_rb16[i], _block_diag(u16, bdm16))
            y_scr[ck * c:(ck + 1) * c, p * pw:(p + 1) * pw] = y
        state = [state[p] * src["g_all"][ck:ck + 1, p * pw:(p + 1) * pw] + vk[i]
                 + jnp.where(bdm, _mm(ut[p].astype(BF16), part("b_e", items[i])), 0.0)
                 for p, i in enumerate(idx)]
        yield
    for p in range(n_pairs):
        s_scr[p] = state[p]

    y = y_scr[...]
    bd = w["bd"][...]
    mu = _head_sums([y], bd)[0] * (1.0 / HEAD)
    d = y - mu
    var = _head_sums([d * d], bd)[0] * (1.0 / HEAD)
    out = d * lax.rsqrt(var + GN_EPS) * w["lnw"][...] + w["lnb"][...] + src["bonus"][...]
    o_ref[row0:row0 + tile, :] = out.astype(o_ref.dtype)
    yield


def _interleave(*streams):
    for _ in itertools.zip_longest(*streams):
        pass


_W_NAMES = ("mu_r", "mu_k", "mu_v", "mu_lo", "w0", "wup", "a0", "aup", "kkw", "kaw", "rkw",
            "lnw", "lnb", "bd", "tril")


def _rwkv_kernel(*refs):
    cur, nxt = refs[0:4], refs[4:8]
    w = dict(zip(_W_NAMES, refs[8:8 + len(_W_NAMES)]))
    rest = refs[8 + len(_W_NAMES):]
    o_ref, s_scr, y_scr = rest[0], rest[1], rest[2]
    carries = rest[3:7]
    n_set = len(_SET_NAMES)
    set0 = dict(zip(_SET_NAMES, rest[7:7 + n_set]))
    set1 = dict(zip(_SET_NAMES, rest[7 + n_set:7 + 2 * n_set]))
    n = pl.program_id(1)

    @pl.when(n == 0)
    def _():
        s_scr[...] = jnp.zeros_like(s_scr)
        for p_scr in carries:
            p_scr[...] = jnp.zeros_like(p_scr)
        _interleave(_rwkv_prep(cur, 0, w, carries, set0))

    c, pw = RW_CHUNK, 2 * HEAD
    ci = lax.broadcasted_iota(jnp.int32, (c, pw), 0)
    cj = lax.broadcasted_iota(jnp.int32, (c, pw), 1) % HEAD
    eye = (ci == cj).astype(F32)
    si = lax.broadcasted_iota(jnp.int32, (2 * c, pw), 0)
    sj = lax.broadcasted_iota(jnp.int32, (2 * c, pw), 1) % HEAD
    tri2 = ((si < c) & (si > sj)) | ((si >= c) & ((si - c) >= sj))
    bi = lax.broadcasted_iota(jnp.int32, (2 * pw, pw), 0)
    bj = lax.broadcasted_iota(jnp.int32, (2 * pw, pw), 1)
    bdm_bool = ((bi // HEAD) % 2) == (bj // HEAD)
    bdm4_16 = bdm_bool.astype(BF16)
    hi = lax.broadcasted_iota(jnp.int32, (pw, pw), 0)
    hj = lax.broadcasted_iota(jnp.int32, (pw, pw), 1)
    di, dj = hi % HEAD, hj % HEAD
    same_head = (hi // HEAD) == (hj // HEAD)
    near = lambda b: same_head & ((di // b) == (dj // b))
    ring = lambda b: same_head & ((di // (2 * b)) == (dj // (2 * b))) & ((di // b) != (dj // b))
    masks = ((ci // 8) == (cj // 8), near(8).astype(BF16),
             tuple(ring(b).astype(BF16) for b in (8, 16, 32)))
    consts = (eye, masks, tri2, bdm4_16, bdm4_16[:pw], bdm_bool[:pw])

    _interleave(_rwkv_prep(cur, RW_TILE, w, carries, set1),
                _rwkv_main(set0, w, consts, s_scr, y_scr, o_ref, 0))
    _interleave(_rwkv_prep(nxt, 0, w, carries, set0),
                _rwkv_main(set1, w, consts, s_scr, y_scr, o_ref, RW_TILE))


def _rwkv(proj, vecs, wup, aup, bd, tril, batch, seq):
    m = proj.shape[0]
    ns = seq // RW_STEP
    row = lambda b, n: b * ns + n
    nxt = lambda b, n: b * ns + jnp.minimum(n + 1, ns - 1)
    vec = lambda width: pl.BlockSpec((1, width), lambda b, n: (0, 0))
    full = lambda a: pl.BlockSpec(a.shape, lambda b, n: (0, 0))

    def token_specs(row_fn):
        wide = lambda cb: pl.BlockSpec((RW_STEP, D_BRANCH), lambda b, n: (row_fn(b, n), cb))
        return [wide(P_R // D_BRANCH), wide(P_K // D_BRANCH), wide(P_V // D_BRANCH),
                pl.BlockSpec((RW_STEP, 2 * LORA), lambda b, n: (row_fn(b, n), P_LORA // (2 * LORA)))]

    in_specs = token_specs(row) + token_specs(nxt) + [
        vec(D_BRANCH), vec(D_BRANCH), vec(D_BRANCH), vec(2 * LORA),
        vec(D_BRANCH), full(wup), vec(D_BRANCH), full(aup),
        vec(D_BRANCH), vec(D_BRANCH), vec(D_BRANCH), vec(D_BRANCH), vec(D_BRANCH),
        full(bd), full(tril),
    ]
    prep_set = ([pltpu.VMEM((RW_TILE, D_BRANCH), BF16) for _ in _SET_BF16]
                + [pltpu.VMEM((8, D_BRANCH), F32), pltpu.VMEM((RW_TILE, D_BRANCH), F32)])
    return pl.pallas_call(
        _rwkv_kernel,
        grid=(batch, ns),
        in_specs=in_specs,
        out_specs=pl.BlockSpec((RW_STEP, D_BRANCH), lambda b, n: (row(b, n), 0)),
        out_shape=jax.ShapeDtypeStruct((m, D_BRANCH), ACT),
        scratch_shapes=[
            pltpu.VMEM((N_HEADS // 2, 2 * HEAD, 2 * HEAD), F32),
            pltpu.VMEM((RW_TILE, D_BRANCH), F32),
            pltpu.VMEM((8, D_BRANCH), F32), pltpu.VMEM((8, D_BRANCH), F32),
            pltpu.VMEM((8, D_BRANCH), F32), pltpu.VMEM((8, 2 * LORA), F32),
        ] + prep_set + prep_set,
        compiler_params=pltpu.CompilerParams(
            dimension_semantics=("parallel", "arbitrary"), vmem_limit_bytes=_VMEM_LIMIT),
        name="rwkv",
    )(*([proj] * 8), *vecs[:4], vecs[4], wup, vecs[5], aup, *vecs[6:], bd, tril)


_SLOPES = tuple(2.0 ** (-8.0 * (h + 1) / N_HEADS) for h in range(N_HEADS))
_SQRT_HALF = float(np.sqrt(0.5))


def _attention_bias():
    t = np.arange(BLK)[:, None]
    s = np.arange(2 * BLK)[None, :]
    dist = t + BLK - s
    out = np.empty((2, N_HEADS * BLK, 2 * BLK), np.float32)
    for first in (0, 1):
        valid = (dist >= 0) & (dist < BLK) & ((s >= BLK) | (first == 1))
        for h in range(N_HEADS):
            out[first, h * BLK:(h + 1) * BLK] = np.where(valid, -_SLOPES[h] * dist, NEG_INF)
    return out


def _mixers_kernel(sinks_ref, bias_ref, q_ref, kvc_ref, kvp_ref, zc_ref, zcp_ref, zd_ref,
                   poolw_ref, pscale_ref, nw_ref, sw_ref, sb_ref, o_ref, y_scr):
    n = pl.program_id(1)

    q = q_ref[...] * (HEAD ** -0.5)
    kv = jnp.concatenate([kvp_ref[...], kvc_ref[...]], axis=0)
    rows = ATT_GROUP * BLK
    head_in_group = lax.broadcasted_iota(jnp.int32, (rows, 1), 0) // BLK

    def per_row(values):
        col = jnp.full((rows, 1), values[-1], F32)
        for j in range(ATT_GROUP - 2, -1, -1):
            col = jnp.where(head_in_group == j, values[j], col)
        return col

    scores = []
    for g in range(KV_HEADS):
        qg = jnp.concatenate(
            [q[:, (g * ATT_GROUP + j) * HEAD:(g * ATT_GROUP + j + 1) * HEAD]
             for j in range(ATT_GROUP)], axis=0)
        scores.append(_bdot_nt(qg, kv[:, g * HEAD:(g + 1) * HEAD]))
    probs, dens = [], []
    for g in range(KV_HEADS):
        heads = range(g * ATT_GROUP, (g + 1) * ATT_GROUP)
        sink = per_row([sinks_ref[h] for h in heads])
        s = scores[g] + bias_ref[g * rows:(g + 1) * rows, :]
        mx = jnp.maximum(jnp.max(s, axis=-1, keepdims=True), sink)
        p = jnp.exp(s - mx)
        probs.append(p)
        dens.append(jnp.sum(p, axis=-1, keepdims=True) + jnp.exp(sink - mx))
    for g in range(KV_HEADS):
        vg = kv[:, (KV_HEADS + g) * HEAD:(KV_HEADS + g + 1) * HEAD]
        og = _bdot(probs[g], vg) / dens[g]
        for j in range(ATT_GROUP):
            h = g * ATT_GROUP + j
            y_scr[:, h * HEAD:(h + 1) * HEAD] = og[j * BLK:(j + 1) * BLK, :]

    zfull = jnp.concatenate([jnp.where(n > 0, zcp_ref[...], 0.0), zc_ref[...]],
                            axis=0).astype(F32)
    pos = n * BLK + lax.broadcasted_iota(jnp.int32, (BLK, 1), 0) + 1
    for g, w in enumerate(POOL_WINDOWS):
        gs = slice(g * BLK, (g + 1) * BLK)
        zg = zfull[:, gs]
        acc = zg
        step = 1
        while step < w:
            acc = acc + pltpu.roll(acc, step, axis=0)
            step *= 2
        cnt = jnp.minimum(pos, w).astype(F32)
        pooled = acc[POOL_HALO:, :] / cnt - zg[POOL_HALO:, :]
        yg = _bdot(pooled, poolw_ref[g]) * pscale_ref[:, gs]
        y_scr[:, D_BRANCH + g * BLK:D_BRANCH + (g + 1) * BLK] = yg

    zd = zd_ref[...].astype(F32)
    gz = 0.5 * zd * (1.0 + lax.erf(zd * _SQRT_HALF))
    u = gz[:, :D_BRANCH]
    vv = gz[:, D_BRANCH:]
    mu = jnp.mean(vv, axis=-1, keepdims=True)
    dv = vv - mu
    var = jnp.mean(dv * dv, axis=-1, keepdims=True)
    vn = dv * lax.rsqrt(var + LN_EPS) * nw_ref[...]
    ri = lax.broadcasted_iota(jnp.int32, (BLK, BLK), 0)
    rj = lax.broadcasted_iota(jnp.int32, (BLK, BLK), 1)
    causal = ri >= rj
    for g in range(4):
        gs = slice(g * BLK, (g + 1) * BLK)
        ws = jnp.where(causal, sw_ref[g], 0.0)
        sg = _bdot(ws, vn[:, gs]) + sb_ref[:, gs]
        y_scr[:, 2 * D_BRANCH + g * BLK:2 * D_BRANCH + (g + 1) * BLK] = u[:, gs] * sg
    o_ref[...] = y_scr[...].astype(o_ref.dtype)


def _mixers(proj, sinks, pool_w, pool_scale, norm_w, sgu_w, sgu_bias, batch, seq):
    m = proj.shape[0]
    nb = seq // BLK
    row = lambda b, n: b * nb + n
    halo = BLK // POOL_HALO
    bias = jnp.asarray(_attention_bias())
    in_specs = [
        pl.BlockSpec(memory_space=pltpu.SMEM),
        pl.BlockSpec((None,) + bias.shape[1:], lambda b, n: (jnp.minimum(n, 1), 0, 0)),
        pl.BlockSpec((BLK, D_BRANCH), lambda b, n: (row(b, n), P_Q // D_BRANCH)),
        pl.BlockSpec((BLK, 256), lambda b, n: (row(b, n), P_KV // 256)),
        pl.BlockSpec((BLK, 256), lambda b, n: (row(b, jnp.maximum(n - 1, 0)), P_KV // 256)),
        pl.BlockSpec((BLK, D_BRANCH), lambda b, n: (row(b, n), P_POOL // D_BRANCH)),
        pl.BlockSpec((POOL_HALO, D_BRANCH),
                     lambda b, n: (jnp.maximum(row(b, n) * halo - 1, 0), P_POOL // D_BRANCH)),
        pl.BlockSpec((BLK, 2 * D_BRANCH), lambda b, n: (row(b, n), P_SGU // (2 * D_BRANCH))),
        pl.BlockSpec((4, BLK, BLK), lambda b, n: (0, 0, 0)),
        pl.BlockSpec((1, D_BRANCH), lambda b, n: (0, 0)),
        pl.BlockSpec((1, D_BRANCH), lambda b, n: (0, 0)),
        pl.BlockSpec((4, BLK, BLK), lambda b, n: (0, 0, 0)),
        pl.BlockSpec((BLK, D_BRANCH), lambda b, n: (0, 0)),
    ]
    return pl.pallas_call(
        _mixers_kernel,
        grid=(batch, nb),
        in_specs=in_specs,
        out_specs=pl.BlockSpec((BLK, 3 * D_BRANCH), lambda b, n: (row(b, n), 0)),
        out_shape=jax.ShapeDtypeStruct((m, 3 * D_BRANCH), ACT),
        scratch_shapes=[pltpu.VMEM((BLK, 3 * D_BRANCH), F32)],
        compiler_params=pltpu.CompilerParams(
            dimension_semantics=("parallel", "parallel"), vmem_limit_bytes=_VMEM_LIMIT),
        name="mixers",
    )(sinks, bias, proj, proj, proj, proj, proj, proj, pool_w, pool_scale, norm_w, sgu_w,
      sgu_bias)


OUT_TM = 512
OUT_SUB = 256
OUT_STAGE = 512


def _outproj_kernel(ya_ref, yb_ref, g_ref, x_ref, w_hbm, pw_ref, o_ref, w_scr, stage, sem, *,
                    layer):
    @pl.when(pl.program_id(0) == 0)
    def _():
        def copy(idx):
            slot = idx % 2
            return pltpu.make_async_copy(w_hbm.at[layer, pl.ds(idx * OUT_STAGE, OUT_STAGE), :],
                                         stage.at[slot], sem.at[slot])

        n_pieces = D_MODEL // OUT_STAGE
        copy(0).start()
        for idx in range(n_pieces):
            if idx + 1 < n_pieces:
                copy(idx + 1).start()
            copy(idx).wait()
            w_scr[idx * OUT_STAGE:(idx + 1) * OUT_STAGE, :] = stage[idx % 2].astype(BF16)

    for r0 in range(0, OUT_TM, OUT_SUB):
        rows = slice(r0, r0 + OUT_SUB)
        g = g_ref[rows, :].astype(F32)
        y = jnp.concatenate([ya_ref[rows, :], yb_ref[rows, :]], axis=1).astype(F32)
        gated = (y * (g * jax.nn.sigmoid(g))).astype(BF16)
        acc = jnp.dot(gated, w_scr[...], preferred_element_type=F32)
        ms = jnp.mean(acc * acc, axis=-1, keepdims=True)
        o_ref[rows, :] = x_ref[rows, :] + acc * lax.rsqrt(ms + NORM_EPS) * pw_ref[...]


def _outproj(ya, ybcd, proj, x2d, w_out, layer, post_w):
    m = x2d.shape[0]
    return pl.pallas_call(
        functools.partial(_outproj_kernel, layer=layer),
        grid=(m // OUT_TM,),
        in_specs=[
            pl.BlockSpec((OUT_TM, D_BRANCH), lambda i: (i, 0)),
            pl.BlockSpec((OUT_TM, 3 * D_BRANCH), lambda i: (i, 0)),
            pl.BlockSpec((OUT_TM, D_MODEL), lambda i: (i, P_GATE // D_MODEL)),
            pl.BlockSpec((OUT_TM, D_MODEL), lambda i: (i, 0)),
            pl.BlockSpec(memory_space=pl.ANY),
            pl.BlockSpec((1, D_MODEL), lambda i: (0, 0)),
        ],
        out_specs=pl.BlockSpec((OUT_TM, D_MODEL), lambda i: (i, 0)),
        out_shape=jax.ShapeDtypeStruct((m, D_MODEL), F32),
        scratch_shapes=[
            pltpu.VMEM((D_MODEL, D_MODEL), BF16),
            pltpu.VMEM((2, OUT_STAGE, D_MODEL), F32),
            pltpu.SemaphoreType.DMA((2,)),
        ],
        compiler_params=pltpu.CompilerParams(
            dimension_semantics=("arbitrary",), vmem_limit_bytes=_VMEM_LIMIT),
        name="outproj",
    )(ya, ybcd, proj, x2d, w_out, post_w)


def kernel(x, pre_norm_w, post_norm_w, w_in, shift_mu, rwkv_w0, rwkv_w_up, rwkv_a0, rwkv_a_up,
           rwkv_k_k, rwkv_k_a, rwkv_r_k, rwkv_ln_w, rwkv_ln_b, attn_sinks, pool_w, pool_scale,
           sgu_norm_w, sgu_w, sgu_b, w_out):
    batch, seq, _ = x.shape
    assert x.shape == (batch, seq, D_MODEL) and seq % BLK == 0 and seq % RW_STEP == 0
    m = batch * seq
    head_id = np.arange(D_BRANCH // 2) // HEAD
    bd = jnp.asarray((head_id[:, None] == head_id[None, :]).astype(np.float32), dtype=BF16)
    t_id = np.arange(RW_TILE)
    tril = jnp.asarray(((t_id[:, None] >= t_id[None, :])
                        & (t_id[:, None] // RW_CHUNK == t_id[None, :] // RW_CHUNK)).astype(np.float32),
                       dtype=BF16)
    row_vec = lambda a: a.reshape(1, -1)

    h = x.reshape(m, D_MODEL)
    for l in range(DEPTH):
        proj = _inproj(h, row_vec(pre_norm_w[l]), w_in, l)

        mu = shift_mu[l]
        vecs = [row_vec(mu[0:D_BRANCH]), row_vec(mu[D_BRANCH:2 * D_BRANCH]),
                row_vec(mu[2 * D_BRANCH:3 * D_BRANCH]), row_vec(mu[3 * D_BRANCH:]),
                row_vec(rwkv_w0[l]), row_vec(rwkv_a0[l]), row_vec(rwkv_k_k[l]),
                row_vec(rwkv_k_a[l]), row_vec(rwkv_r_k[l]), row_vec(rwkv_ln_w[l]),
                row_vec(rwkv_ln_b[l])]
        ya = _rwkv(proj, vecs, rwkv_w_up[l], rwkv_a_up[l], bd, tril, batch, seq)

        sgu_bias = jnp.broadcast_to(sgu_b[l].T[:, :, None], (BLK, 4, BLK)).reshape(BLK, D_BRANCH)
        ybcd = _mixers(proj, attn_sinks[l], pool_w[l], row_vec(pool_scale[l]),
                       row_vec(sgu_norm_w[l]), sgu_w[l], sgu_bias, batch, seq)

        h = _outproj(ya, ybcd, proj, h, w_out, l, row_vec(post_norm_w[l]))
    return h.reshape(batch, seq, D_MODEL)
```

```python
import functools
import itertools

import jax
import jax.numpy as jnp
import numpy as np
from jax import lax
from jax.experimental import pallas as pl
from jax.experimental.pallas import tpu as pltpu

F32 = jnp.float32
BF16 = jnp.bfloat16
ACT = BF16

D_MODEL = 2048
DEPTH = 2
D_BRANCH = 512
HEAD = 64
N_HEADS = D_BRANCH // HEAD
LORA = 64
KV_HEADS = 2
ATT_GROUP = N_HEADS // KV_HEADS
BLK = 128
POOL_WINDOWS = (2, 4, 8, 16)
POOL_HALO = 16
NEG_INF = -1e30
NORM_EPS = 1e-6
LN_EPS = 1e-5
GN_EPS = 64e-5

A_COLS = 3 * D_BRANCH + 2 * LORA
B_COLS = D_BRANCH + 2 * KV_HEADS * HEAD
OFF_B = A_COLS
OFF_C = OFF_B + B_COLS
OFF_D = OFF_C + D_BRANCH
OFF_G = OFF_D + 2 * D_BRANCH
D_IN = OFF_G + D_MODEL

P_GATE = 0
P_SGU = 2048
P_POOL = 3072
P_Q = 3584
P_R = 4096
P_K = 4608
P_V = 5120
P_KV = 5632
P_LORA = 5888
P_WIDTH = 6144

RW_CHUNK = 64
RW_TILE = 256
RW_STEP = 2 * RW_TILE

_VMEM_LIMIT = 56 * 1024 * 1024


IN_TM = 512
IN_TN = 1024
IN_STAGE = 256
_W_RUNS = ((OFF_G, D_MODEL, P_GATE), (OFF_D, 2 * D_BRANCH, P_SGU), (OFF_C, D_BRANCH, P_POOL),
           (OFF_B, D_BRANCH, P_Q), (0, 3 * D_BRANCH, P_R),
           (OFF_B + D_BRANCH, 2 * KV_HEADS * HEAD, P_KV), (3 * D_BRANCH, 2 * LORA, P_LORA))
_W_PIECES = tuple((src + o, min(IN_STAGE, width - o), dst + o)
                  for src, width, dst in _W_RUNS for o in range(0, width, IN_STAGE))
assert sum(p[1] for p in _W_PIECES) == D_IN and P_LORA + 2 * LORA == D_IN


def _load_weight(w_hbm, layer, w_scr, stage, sem):
    def copy(idx):
        src, width, _ = _W_PIECES[idx]
        slot = idx % 2
        return pltpu.make_async_copy(w_hbm.at[layer, :, pl.ds(src, width)],
                                     stage.at[slot, :, pl.ds(0, width)], sem.at[slot])

    copy(0).start()
    for idx, (_, width, dst) in enumerate(_W_PIECES):
        if idx + 1 < len(_W_PIECES):
            copy(idx + 1).start()
        copy(idx).wait()
        w_scr[:, dst:dst + width] = stage[idx % 2, :, 0:width].astype(BF16)
    w_scr[:, D_IN:] = jnp.zeros((D_MODEL, P_WIDTH - D_IN), BF16)


def _inproj_kernel(x_ref, pw_ref, mu_ref, w_hbm, o_ref, w_scr, stage, sem, h_scr, carry_scr, *,
                   layer, tiles_per_seq):
    i = pl.program_id(0)

    @pl.when(i == 0)
    def _():
        carry_scr[...] = jnp.zeros_like(carry_scr)
        _load_weight(w_hbm, layer, w_scr, stage, sem)

    for r0 in range(0, IN_TM, 256):
        x = x_ref[r0:r0 + 256, :]
        ms = jnp.mean(x * x, axis=-1, keepdims=True)
        h_scr[r0:r0 + 256, :] = (x * lax.rsqrt(ms + NORM_EPS) * pw_ref[...]).astype(BF16)
    row = lax.broadcasted_iota(jnp.int32, (IN_TM, 1), 0)
    starts_sequence = (i % tiles_per_seq) == 0
    for c0 in range(0, P_WIDTH, IN_TN):
        acc = jnp.dot(h_scr[...], w_scr[:, c0:c0 + IN_TN], preferred_element_type=F32)
        if c0 >= P_R:
            cs = slice(c0 - P_R, c0 - P_R + IN_TN)
            before = jnp.where(starts_sequence, 0.0, carry_scr[0:1, cs])
            prev = jnp.where(row == 0, before, pltpu.roll(acc, 1, axis=0))
            carry_scr[0:1, cs] = acc[IN_TM - 1:IN_TM, :]
            acc = acc + mu_ref[:, cs] * (prev - acc)
        o_ref[:, c0:c0 + IN_TN] = acc.astype(o_ref.dtype)


def _inproj(x2d, pre_w, mu_p, w_in, layer, seq):
    m = x2d.shape[0]
    assert P_R % IN_TN == 0 and seq % IN_TM == 0 and mu_p.shape == (1, P_WIDTH - P_R)
    return pl.pallas_call(
        functools.partial(_inproj_kernel, layer=layer, tiles_per_seq=seq // IN_TM),
        grid=(m // IN_TM,),
        in_specs=[
            pl.BlockSpec((IN_TM, D_MODEL), lambda i: (i, 0)),
            pl.BlockSpec((1, D_MODEL), lambda i: (0, 0)),
            pl.BlockSpec((1, P_WIDTH - P_R), lambda i: (0, 0)),
            pl.BlockSpec(memory_space=pl.ANY),
        ],
        out_specs=pl.BlockSpec((IN_TM, P_WIDTH), lambda i: (i, 0)),
        out_shape=jax.ShapeDtypeStruct((m, P_WIDTH), ACT),
        scratch_shapes=[
            pltpu.VMEM((D_MODEL, P_WIDTH), BF16),
            pltpu.VMEM((2, D_MODEL, IN_STAGE), F32),
            pltpu.SemaphoreType.DMA((2,)),
            pltpu.VMEM((IN_TM, D_MODEL), BF16),
            pltpu.VMEM((8, P_WIDTH - P_R), F32),
        ],
        compiler_params=pltpu.CompilerParams(
            dimension_semantics=("arbitrary",), vmem_limit_bytes=_VMEM_LIMIT),
        name="inproj",
    )(x2d, pre_w, mu_p, w_in)


def _bdot(a, b):
    return jnp.dot(a.astype(BF16), b.astype(BF16), preferred_element_type=F32)


def _bdot_nt(a, b):
    return lax.dot_general(a.astype(BF16), b.astype(BF16), (((1,), (1,)), ((), ())),
                           preferred_element_type=F32)


def _split3(x):
    x1 = x.astype(BF16)
    r1 = x - x1.astype(F32)
    x2 = r1.astype(BF16)
    x3 = (r1 - x2.astype(F32)).astype(BF16)
    return x1, x2, x3


def _head_sums(xs, bd):
    rows = xs[0].shape[0]
    half = bd.shape[0]
    x = jnp.concatenate(xs, axis=0).astype(BF16)
    out = jnp.concatenate(
        [jnp.dot(x[:, :half], bd, preferred_element_type=F32),
         jnp.dot(x[:, half:], bd, preferred_element_type=F32)], axis=1)
    return [out[i * rows:(i + 1) * rows] for i in range(len(xs))]


def _mm(a16, b16):
    return jnp.dot(a16, b16, preferred_element_type=F32)


def _mm_nt(a16, b16):
    return lax.dot_general(a16, b16, (((1,), (1,)), ((), ())), preferred_element_type=F32)


def _block_diag(m16, bdm16):
    return jnp.concatenate([m16, m16], axis=0) * bdm16


def _unit_lower_inverse_many(lmats, eye, masks, bdm16):
    base_mask, base_bd16, level_bd16 = masks
    c = lmats[0].shape[0]
    l16s = [l.astype(BF16) for l in lmats]
    stack2 = lambda m16: jnp.concatenate([m16, m16], axis=0)
    lds = [jnp.where(base_mask, l, 0.0) for l in lmats]
    l2s = [_mm(ld.astype(BF16), stack2(l16) * base_bd16).astype(BF16)
           for ld, l16 in zip(lds, l16s)]
    yield
    xs = [eye + ld for ld in lds]
    both = [_mm(jnp.concatenate([x.astype(BF16), l2], axis=0), stack2(l2) * bdm16)
            for x, l2 in zip(xs, l2s)]
    xs = [x + b[:c] for x, b in zip(xs, both)]
    l4s = [b[c:].astype(BF16) for b in both]
    yield
    xs = [x + _mm(x.astype(BF16), stack2(l4) * bdm16) for x, l4 in zip(xs, l4s)]
    yield
    for lvl16 in level_bd16:
        ts = [_mm(x.astype(BF16), stack2(l16) * lvl16).astype(BF16) for x, l16 in zip(xs, l16s)]
        yield
        xs = [x + _mm(t, stack2(x.astype(BF16)) * bdm16) for x, t in zip(xs, ts)]
        yield
    return xs


_SET_BF16 = ("a", "b", "k", "r", "v", "b_e", "k_e")
_SET_NAMES = _SET_BF16 + ("g_all", "bonus")
_PREP_MXU_DELAY = 4
_EXP_NEG_HALF = float(np.exp(-0.5))


def _rwkv_prep(z_refs, row0, w, dst):
    tile, c = RW_TILE, RW_CHUNK
    rows = lambda z_ref: z_ref[row0:row0 + tile, :].astype(F32)

    lo = rows(z_refs[3])
    lora_w = _bdot(jnp.tanh(lo[:, :LORA]), w["wup"][...])
    lora_a = _bdot(lo[:, LORA:], w["aup"][...])
    yield
    r, k, v = rows(z_refs[0]), rows(z_refs[1]), rows(z_refs[2])
    logw = -_EXP_NEG_HALF * jax.nn.sigmoid(w["w0"][...] + lora_w)
    asig = jax.nn.sigmoid(w["a0"][...] + lora_a)
    logw_terms = _split3(logw)
    kk = k * w["kkw"][...]
    k2 = k * (1.0 + (asig - 1.0) * w["kaw"][...])
    sum_terms = [kk * kk, r * k2 * w["rkw"][...]]
    for _ in range(_PREP_MXU_DELAY):
        yield
    tril = w["tril"][...]
    cum = sum(jnp.dot(tril, p, preferred_element_type=F32) for p in logw_terms)
    kk_ss, rk_sum = _head_sums(sum_terms, w["bd"][...])
    yield
    kk = kk * lax.rsqrt(jnp.maximum(kk_ss, 1e-24))
    dst["bonus"][...] = rk_sum * v
    for ck in range(tile // c):
        sl = slice(ck * c, (ck + 1) * c)
        lw, cm = logw[sl], cum[sl]
        total = cm[c - 1:c, :]
        g_inv = jnp.exp(-cm)
        g_all = jnp.exp(total)
        g_tail = g_all * g_inv
        b_c = kk[sl] * asig[sl]
        dst["a"][sl, :] = (-kk[sl] * jnp.exp(cm - lw)).astype(BF16)
        dst["b"][sl, :] = (b_c * g_inv).astype(BF16)
        dst["k"][sl, :] = (k2[sl] * g_inv).astype(BF16)
        dst["r"][sl, :] = (r[sl] * jnp.exp(cm)).astype(BF16)
        dst["v"][sl, :] = v[sl].astype(BF16)
        dst["b_e"][sl, :] = (b_c * g_tail).astype(BF16)
        dst["k_e"][sl, :] = (k2[sl] * g_tail).astype(BF16)
        dst["g_all"][ck:ck + 1, :] = g_all
    yield


def _rwkv_main(src, w, consts, s_scr, y_scr, o_ref, row0):
    tile, c, pw = RW_TILE, RW_CHUNK, 2 * HEAD
    eye, masks, tri2, bdm4_16, bdm16, bdm = consts
    n_chunks, n_pairs = tile // c, N_HEADS // 2
    items = [(ck, p) for ck in range(n_chunks) for p in range(n_pairs)]

    def part(name, it):
        ck, p = it
        return src[name][ck * c:(ck + 1) * c, p * pw:(p + 1) * pw]

    a2 = [part("a", it) for it in items]
    r2 = [part("r", it) for it in items]
    l_ab, m_rb16, ak16 = [], [], []
    for a, x, it in zip(a2, r2, items):
        b, k_ = part("b", it), part("k", it)
        pr = _mm_nt(jnp.concatenate([a, x], axis=0),
                    jnp.concatenate([b, b, k_, k_], axis=0) * bdm4_16)
        ab = jnp.where(tri2, pr[:, :pw], 0.0)
        l_ab.append(ab[:c])
        m_rb16.append(ab[c:].astype(BF16))
        ak16.append(jnp.where(tri2, pr[:, pw:], 0.0).astype(BF16))
    yield
    qy = [_mm(x, _block_diag(part("v", it), bdm16))
          for x, it in zip(ak16, items)]
    qv16 = [q[:c].astype(BF16) for q in qy]
    yv = [q[c:] for q in qy]
    vk = [jnp.where(bdm, lax.dot_general(part("v", it), part("k_e", it), (((0,), (0,)), ((), ())),
                                         preferred_element_type=F32), 0.0) for it in items]
    yield
    tinv = yield from _unit_lower_inverse_many(l_ab, eye, masks, bdm16)
    wu = [_mm(t.astype(BF16),
              jnp.concatenate([_block_diag(a, bdm16), _block_diag(q, bdm16)], axis=1))
          for t, a, q in zip(tinv, a2, qv16)]
    wa16 = [x[:, :pw].astype(BF16) for x in wu]
    uv = [x[:, pw:] for x in wu]
    uvt = [x.T for x in uv]
    war = [jnp.concatenate([x, y], axis=0) for x, y in zip(wa16, r2)]
    yield

    state = [s_scr[p] for p in range(n_pairs)]
    for ck in range(n_chunks):
        idx = [ck * n_pairs + p for p in range(n_pairs)]
        s16 = [s.astype(BF16) for s in state]
        uy = [_mm_nt(war[i], s16[p]) for p, i in enumerate(idx)]
        ut = [_mm_nt(s16[p], wa16[i]) + uvt[i] for p, i in enumerate(idx)]
        for p, i in enumerate(idx):
            u16 = (uy[p][:c] + uv[i]).astype(BF16)
            y = uy[p][c:] + yv[i] + _mm(m_rb16[i], _block_diag(u16, bdm16))
            y_scr[ck * c:(ck + 1) * c, p * pw:(p + 1) * pw] = y
        state = [state[p] * src["g_all"][ck:ck + 1, p * pw:(p + 1) * pw] + vk[i]
                 + jnp.where(bdm, _mm(ut[p].astype(BF16), part("b_e", items[i])), 0.0)
                 for p, i in enumerate(idx)]
        yield
    for p in range(n_pairs):
        s_scr[p] = state[p]

    y = y_scr[...]
    bd = w["bd"][...]
    mu = _head_sums([y], bd)[0] * (1.0 / HEAD)
    d = y - mu
    var = _head_sums([d * d], bd)[0] * (1.0 / HEAD)
    out = d * lax.rsqrt(var + GN_EPS) * w["lnw"][...] + w["lnb"][...] + src["bonus"][...]
    o_ref[row0:row0 + tile, :] = out.astype(o_ref.dtype)
    yield


def _interleave(*streams):
    for _ in itertools.zip_longest(*streams):
        pass


_W_NAMES = ("w0", "wup", "a0", "aup", "kkw", "kaw", "rkw", "lnw", "lnb", "bd", "tril")


def _rwkv_kernel(*refs):
    cur, nxt = refs[0:4], refs[4:8]
    w = dict(zip(_W_NAMES, refs[8:8 + len(_W_NAMES)]))
    rest = refs[8 + len(_W_NAMES):]
    o_ref, s_scr, y_scr = rest[0], rest[1], rest[2]
    n_set = len(_SET_NAMES)
    set0 = dict(zip(_SET_NAMES, rest[3:3 + n_set]))
    set1 = dict(zip(_SET_NAMES, rest[3 + n_set:3 + 2 * n_set]))
    n = pl.program_id(1)

    @pl.when(n == 0)
    def _():
        s_scr[...] = jnp.zeros_like(s_scr)
        _interleave(_rwkv_prep(cur, 0, w, set0))

    c, pw = RW_CHUNK, 2 * HEAD
    ci = lax.broadcasted_iota(jnp.int32, (c, pw), 0)
    cj = lax.broadcasted_iota(jnp.int32, (c, pw), 1) % HEAD
    eye = (ci == cj).astype(F32)
    si = lax.broadcasted_iota(jnp.int32, (2 * c, pw), 0)
    sj = lax.broadcasted_iota(jnp.int32, (2 * c, pw), 1) % HEAD
    tri2 = ((si < c) & (si > sj)) | ((si >= c) & ((si - c) >= sj))
    bi = lax.broadcasted_iota(jnp.int32, (2 * pw, pw), 0)
    bj = lax.broadcasted_iota(jnp.int32, (2 * pw, pw), 1)
    bdm_bool = ((bi // HEAD) % 2) == (bj // HEAD)
    bdm4_16 = bdm_bool.astype(BF16)
    hi = lax.broadcasted_iota(jnp.int32, (pw, pw), 0)
    hj = lax.broadcasted_iota(jnp.int32, (pw, pw), 1)
    di, dj = hi % HEAD, hj % HEAD
    same_head = (hi // HEAD) == (hj // HEAD)
    near = lambda b: same_head & ((di // b) == (dj // b))
    ring = lambda b: same_head & ((di // (2 * b)) == (dj // (2 * b))) & ((di // b) != (dj // b))
    masks = ((ci // 8) == (cj // 8), near(8).astype(BF16),
             tuple(ring(b).astype(BF16) for b in (8, 16, 32)))
    consts = (eye, masks, tri2, bdm4_16, bdm4_16[:pw], bdm_bool[:pw])

    _interleave(_rwkv_prep(cur, RW_TILE, w, set1),
                _rwkv_main(set0, w, consts, s_scr, y_scr, o_ref, 0))
    _interleave(_rwkv_prep(nxt, 0, w, set0),
                _rwkv_main(set1, w, consts, s_scr, y_scr, o_ref, RW_TILE))


def _rwkv(proj, vecs, wup, aup, bd, tril, batch, seq):
    m = proj.shape[0]
    ns = seq // RW_STEP
    row = lambda b, n: b * ns + n
    nxt = lambda b, n: b * ns + jnp.minimum(n + 1, ns - 1)
    vec = lambda width: pl.BlockSpec((1, width), lambda b, n: (0, 0))
    full = lambda a: pl.BlockSpec(a.shape, lambda b, n: (0, 0))

    def token_specs(row_fn):
        wide = lambda cb: pl.BlockSpec((RW_STEP, D_BRANCH), lambda b, n: (row_fn(b, n), cb))
        return [wide(P_R // D_BRANCH), wide(P_K // D_BRANCH), wide(P_V // D_BRANCH),
                pl.BlockSpec((RW_STEP, 2 * LORA), lambda b, n: (row_fn(b, n), P_LORA // (2 * LORA)))]

    in_specs = token_specs(row) + token_specs(nxt) + [
        vec(D_BRANCH), full(wup), vec(D_BRANCH), full(aup),
        vec(D_BRANCH), vec(D_BRANCH), vec(D_BRANCH), vec(D_BRANCH), vec(D_BRANCH),
        full(bd), full(tril),
    ]
    prep_set = ([pltpu.VMEM((RW_TILE, D_BRANCH), BF16) for _ in _SET_BF16]
                + [pltpu.VMEM((8, D_BRANCH), F32), pltpu.VMEM((RW_TILE, D_BRANCH), F32)])
    return pl.pallas_call(
        _rwkv_kernel,
        grid=(batch, ns),
        in_specs=in_specs,
        out_specs=pl.BlockSpec((RW_STEP, D_BRANCH), lambda b, n: (row(b, n), 0)),
        out_shape=jax.ShapeDtypeStruct((m, D_BRANCH), ACT),
        scratch_shapes=[
            pltpu.VMEM((N_HEADS // 2, 2 * HEAD, 2 * HEAD), F32),
            pltpu.VMEM((RW_TILE, D_BRANCH), F32),
        ] + prep_set + prep_set,
        compiler_params=pltpu.CompilerParams(
            dimension_semantics=("parallel", "arbitrary"), vmem_limit_bytes=_VMEM_LIMIT),
        name="rwkv",
    )(*([proj] * 8), vecs[0], wup, vecs[1], aup, *vecs[2:], bd, tril)


_SLOPES = tuple(2.0 ** (-8.0 * (h + 1) / N_HEADS) for h in range(N_HEADS))
_SQRT_HALF = float(np.sqrt(0.5))


def _attention_bias():
    t = np.arange(BLK)[:, None]
    s = np.arange(2 * BLK)[None, :]
    dist = t + BLK - s
    out = np.empty((2, N_HEADS * BLK, 2 * BLK), np.float32)
    for first in (0, 1):
        valid = (dist >= 0) & (dist < BLK) & ((s >= BLK) | (first == 1))
        for h in range(N_HEADS):
            out[first, h * BLK:(h + 1) * BLK] = np.where(valid, -_SLOPES[h] * dist, NEG_INF)
    return out


def _mixers_kernel(sinks_ref, bias_ref, q_ref, kvc_ref, kvp_ref, zc_ref, zcp_ref, zd_ref,
                   poolw_ref, pscale_ref, nw_ref, sw_ref, sb_ref, o_ref, y_scr):
    n = pl.program_id(1)

    q = q_ref[...] * (HEAD ** -0.5)
    kv = jnp.concatenate([kvp_ref[...], kvc_ref[...]], axis=0)
    rows = ATT_GROUP * BLK
    head_in_group = lax.broadcasted_iota(jnp.int32, (rows, 1), 0) // BLK

    def per_row(values):
        col = jnp.full((rows, 1), values[-1], F32)
        for j in range(ATT_GROUP - 2, -1, -1):
            col = jnp.where(head_in_group == j, values[j], col)
        return col

    scores = []
    for g in range(KV_HEADS):
        qg = jnp.concatenate(
            [q[:, (g * ATT_GROUP + j) * HEAD:(g * ATT_GROUP + j + 1) * HEAD]
             for j in range(ATT_GROUP)], axis=0)
        scores.append(_bdot_nt(qg, kv[:, g * HEAD:(g + 1) * HEAD]))
    probs, dens = [], []
    for g in range(KV_HEADS):
        heads = range(g * ATT_GROUP, (g + 1) * ATT_GROUP)
        sink = per_row([sinks_ref[h] for h in heads])
        s = scores[g] + bias_ref[g * rows:(g + 1) * rows, :]
        mx = jnp.maximum(jnp.max(s, axis=-1, keepdims=True), sink)
        p = jnp.exp(s - mx)
        probs.append(p)
        dens.append(jnp.sum(p, axis=-1, keepdims=True) + jnp.exp(sink - mx))
    for g in range(KV_HEADS):
        vg = kv[:, (KV_HEADS + g) * HEAD:(KV_HEADS + g + 1) * HEAD]
        og = _bdot(probs[g], vg) / dens[g]
        for j in range(ATT_GROUP):
            h = g * ATT_GROUP + j
            y_scr[:, h * HEAD:(h + 1) * HEAD] = og[j * BLK:(j + 1) * BLK, :]

    zfull = jnp.concatenate([jnp.where(n > 0, zcp_ref[...], 0.0), zc_ref[...]],
                            axis=0).astype(F32)
    pos = n * BLK + lax.broadcasted_iota(jnp.int32, (BLK, 1), 0) + 1
    for g, w in enumerate(POOL_WINDOWS):
        gs = slice(g * BLK, (g + 1) * BLK)
        zg = zfull[:, gs]
        acc = zg
        step = 1
        while step < w:
            acc = acc + pltpu.roll(acc, step, axis=0)
            step *= 2
        cnt = jnp.minimum(pos, w).astype(F32)
        pooled = acc[POOL_HALO:, :] / cnt - zg[POOL_HALO:, :]
        yg = _bdot(pooled, poolw_ref[g]) * pscale_ref[:, gs]
        y_scr[:, D_BRANCH + g * BLK:D_BRANCH + (g + 1) * BLK] = yg

    zd = zd_ref[...].astype(F32)
    gz = 0.5 * zd * (1.0 + lax.erf(zd * _SQRT_HALF))
    u = gz[:, :D_BRANCH]
    vv = gz[:, D_BRANCH:]
    mu = jnp.mean(vv, axis=-1, keepdims=True)
    dv = vv - mu
    var = jnp.mean(dv * dv, axis=-1, keepdims=True)
    vn = dv * lax.rsqrt(var + LN_EPS) * nw_ref[...]
    ri = lax.broadcasted_iota(jnp.int32, (BLK, BLK), 0)
    rj = lax.broadcasted_iota(jnp.int32, (BLK, BLK), 1)
    causal = ri >= rj
    for g in range(4):
        gs = slice(g * BLK, (g + 1) * BLK)
        ws = jnp.where(causal, sw_ref[g], 0.0)
        sg = _bdot(ws, vn[:, gs]) + sb_ref[:, gs]
        y_scr[:, 2 * D_BRANCH + g * BLK:2 * D_BRANCH + (g + 1) * BLK] = u[:, gs] * sg
    o_ref[...] = y_scr[...].astype(o_ref.dtype)


def _mixers(proj, sinks, pool_w, pool_scale, norm_w, sgu_w, sgu_bias, batch, seq):
    m = proj.shape[0]
    nb = seq // BLK
    row = lambda b, n: b * nb + n
    halo = BLK // POOL_HALO
    bias = jnp.asarray(_attention_bias())
    in_specs = [
        pl.BlockSpec(memory_space=pltpu.SMEM),
        pl.BlockSpec((None,) + bias.shape[1:], lambda b, n: (jnp.minimum(n, 1), 0, 0)),
        pl.BlockSpec((BLK, D_BRANCH), lambda b, n: (row(b, n), P_Q // D_BRANCH)),
        pl.BlockSpec((BLK, 256), lambda b, n: (row(b, n), P_KV // 256)),
        pl.BlockSpec((BLK, 256), lambda b, n: (row(b, jnp.maximum(n - 1, 0)), P_KV // 256)),
        pl.BlockSpec((BLK, D_BRANCH), lambda b, n: (row(b, n), P_POOL // D_BRANCH)),
        pl.BlockSpec((POOL_HALO, D_BRANCH),
                     lambda b, n: (jnp.maximum(row(b, n) * halo - 1, 0), P_POOL // D_BRANCH)),
        pl.BlockSpec((BLK, 2 * D_BRANCH), lambda b, n: (row(b, n), P_SGU // (2 * D_BRANCH))),
        pl.BlockSpec((4, BLK, BLK), lambda b, n: (0, 0, 0)),
        pl.BlockSpec((1, D_BRANCH), lambda b, n: (0, 0)),
        pl.BlockSpec((1, D_BRANCH), lambda b, n: (0, 0)),
        pl.BlockSpec((4, BLK, BLK), lambda b, n: (0, 0, 0)),
        pl.BlockSpec((BLK, D_BRANCH), lambda b, n: (0, 0)),
    ]
    return pl.pallas_call(
        _mixers_kernel,
        grid=(batch, nb),
        in_specs=in_specs,
        out_specs=pl.BlockSpec((BLK, 3 * D_BRANCH), lambda b, n: (row(b, n), 0)),
        out_shape=jax.ShapeDtypeStruct((m, 3 * D_BRANCH), ACT),
        scratch_shapes=[pltpu.VMEM((BLK, 3 * D_BRANCH), F32)],
        compiler_params=pltpu.CompilerParams(
            dimension_semantics=("parallel", "parallel"), vmem_limit_bytes=_VMEM_LIMIT),
        name="mixers",
    )(sinks, bias, proj, proj, proj, proj, proj, proj, pool_w, pool_scale, norm_w, sgu_w,
      sgu_bias)


OUT_TM = 512
OUT_SUB = 256
OUT_STAGE = 512


def _outproj_kernel(ya_ref, yb_ref, g_ref, x_ref, w_hbm, pw_ref, o_ref, w_scr, stage, sem, *,
                    layer):
    @pl.when(pl.program_id(0) == 0)
    def _():
        def copy(idx):
            slot = idx % 2
            return pltpu.make_async_copy(w_hbm.at[layer, pl.ds(idx * OUT_STAGE, OUT_STAGE), :],
                                         stage.at[slot], sem.at[slot])

        n_pieces = D_MODEL // OUT_STAGE
        copy(0).start()
        for idx in range(n_pieces):
            if idx + 1 < n_pieces:
                copy(idx + 1).start()
            copy(idx).wait()
            w_scr[idx * OUT_STAGE:(idx + 1) * OUT_STAGE, :] = stage[idx % 2].astype(BF16)

    for r0 in range(0, OUT_TM, OUT_SUB):
        rows = slice(r0, r0 + OUT_SUB)
        g = g_ref[rows, :].astype(F32)
        y = jnp.concatenate([ya_ref[rows, :], yb_ref[rows, :]], axis=1).astype(F32)
        gated = (y * (g * jax.nn.sigmoid(g))).astype(BF16)
        acc = jnp.dot(gated, w_scr[...], preferred_element_type=F32)
        ms = jnp.mean(acc * acc, axis=-1, keepdims=True)
        o_ref[rows, :] = x_ref[rows, :] + acc * lax.rsqrt(ms + NORM_EPS) * pw_ref[...]


def _outproj(ya, ybcd, proj, x2d, w_out, layer, post_w):
    m = x2d.shape[0]
    return pl.pallas_call(
        functools.partial(_outproj_kernel, layer=layer),
        grid=(m // OUT_TM,),
        in_specs=[
            pl.BlockSpec((OUT_TM, D_BRANCH), lambda i: (i, 0)),
            pl.BlockSpec((OUT_TM, 3 * D_BRANCH), lambda i: (i, 0)),
            pl.BlockSpec((OUT_TM, D_MODEL), lambda i: (i, P_GATE // D_MODEL)),
            pl.BlockSpec((OUT_TM, D_MODEL), lambda i: (i, 0)),
            pl.BlockSpec(memory_space=pl.ANY),
            pl.BlockSpec((1, D_MODEL), lambda i: (0, 0)),
        ],
        out_specs=pl.BlockSpec((OUT_TM, D_MODEL), lambda i: (i, 0)),
        out_shape=jax.ShapeDtypeStruct((m, D_MODEL), F32),
        scratch_shapes=[
            pltpu.VMEM((D_MODEL, D_MODEL), BF16),
            pltpu.VMEM((2, OUT_STAGE, D_MODEL), F32),
            pltpu.SemaphoreType.DMA((2,)),
        ],
        compiler_params=pltpu.CompilerParams(
            dimension_semantics=("arbitrary",), vmem_limit_bytes=_VMEM_LIMIT),
        name="outproj",
    )(ya, ybcd, proj, x2d, w_out, post_w)


def kernel(x, pre_norm_w, post_norm_w, w_in, shift_mu, rwkv_w0, rwkv_w_up, rwkv_a0, rwkv_a_up,
           rwkv_k_k, rwkv_k_a, rwkv_r_k, rwkv_ln_w, rwkv_ln_b, attn_sinks, pool_w, pool_scale,
           sgu_norm_w, sgu_w, sgu_b, w_out):
    batch, seq, _ = x.shape
    assert x.shape == (batch, seq, D_MODEL) and seq % BLK == 0 and seq % RW_STEP == 0
    m = batch * seq
    head_id = np.arange(D_BRANCH // 2) // HEAD
    bd = jnp.asarray((head_id[:, None] == head_id[None, :]).astype(np.float32), dtype=BF16)
    t_id = np.arange(RW_TILE)
    tril = jnp.asarray(((t_id[:, None] >= t_id[None, :])
                        & (t_id[:, None] // RW_CHUNK == t_id[None, :] // RW_CHUNK)).astype(np.float32),
                       dtype=BF16)
    row_vec = lambda a: a.reshape(1, -1)

    h = x.reshape(m, D_MODEL)
    for l in range(DEPTH):
        mu = shift_mu[l]
        mu_p = jnp.concatenate([mu[:3 * D_BRANCH], jnp.zeros((P_LORA - P_KV,), mu.dtype),
                                mu[3 * D_BRANCH:], jnp.zeros((P_WIDTH - D_IN,), mu.dtype)])
        proj = _inproj(h, row_vec(pre_norm_w[l]), row_vec(mu_p), w_in, l, seq)

        vecs = [row_vec(rwkv_w0[l]), row_vec(rwkv_a0[l]), row_vec(rwkv_k_k[l]),
                row_vec(rwkv_k_a[l]), row_vec(rwkv_r_k[l]), row_vec(rwkv_ln_w[l]),
                row_vec(rwkv_ln_b[l])]
        ya = _rwkv(proj, vecs, rwkv_w_up[l], rwkv_a_up[l], bd, tril, batch, seq)

        sgu_bias = jnp.broadcast_to(sgu_b[l].T[:, :, None], (BLK, 4, BLK)).reshape(BLK, D_BRANCH)
        ybcd = _mixers(proj, attn_sinks[l], pool_w[l], row_vec(pool_scale[l]),
                       row_vec(sgu_norm_w[l]), sgu_w[l], sgu_bias, batch, seq)

        h = _outproj(ya, ybcd, proj, h, w_out, l, row_vec(post_norm_w[l]))
    return h.reshape(batch, seq, D_MODEL)
```

```python
import functools
import itertools

import jax
import jax.numpy as jnp
import numpy as np
from jax import lax
from jax.experimental import pallas as pl
from jax.experimental.pallas import tpu as pltpu

F32 = jnp.float32
BF16 = jnp.bfloat16
ACT = BF16

D_MODEL = 2048
DEPTH = 2
D_BRANCH = 512
HEAD = 64
N_HEADS = D_BRANCH // HEAD
LORA = 64
KV_HEADS = 2
ATT_GROUP = N_HEADS // KV_HEADS
BLK = 128
MIX_SUB = 4
POOL_WINDOWS = (2, 4, 8, 16)
NEG_INF = -1e30
NORM_EPS = 1e-6
LN_EPS = 1e-5
GN_EPS = 64e-5

A_COLS = 3 * D_BRANCH + 2 * LORA
B_COLS = D_BRANCH + 2 * KV_HEADS * HEAD
OFF_B = A_COLS
OFF_C = OFF_B + B_COLS
OFF_D = OFF_C + D_BRANCH
OFF_G = OFF_D + 2 * D_BRANCH
D_IN = OFF_G + D_MODEL

P_GATE = 0
P_SGU = 2048
P_POOL = 3072
P_Q = 3584
P_R = 4096
P_K = 4608
P_V = 5120
P_KV = 5632
P_LORA = 5888
P_WIDTH = 6144

RW_CHUNK = 64
RW_TILE = 256
RW_STEP = 2 * RW_TILE

_VMEM_LIMIT = 56 * 1024 * 1024


IN_TM = 512
IN_TN = 1024
IN_STAGE = 256
_W_RUNS = ((OFF_G, D_MODEL, P_GATE), (OFF_D, 2 * D_BRANCH, P_SGU), (OFF_C, D_BRANCH, P_POOL),
           (OFF_B, D_BRANCH, P_Q), (0, 3 * D_BRANCH, P_R),
           (OFF_B + D_BRANCH, 2 * KV_HEADS * HEAD, P_KV), (3 * D_BRANCH, 2 * LORA, P_LORA))
_W_PIECES = tuple((src + o, min(IN_STAGE, width - o), dst + o)
                  for src, width, dst in _W_RUNS for o in range(0, width, IN_STAGE))
assert sum(p[1] for p in _W_PIECES) == D_IN and P_LORA + 2 * LORA == D_IN


def _load_weight(w_hbm, layer, w_scr, stage, sem):
    def copy(idx):
        src, width, _ = _W_PIECES[idx]
        slot = idx % 2
        return pltpu.make_async_copy(w_hbm.at[layer, :, pl.ds(src, width)],
                                     stage.at[slot, :, pl.ds(0, width)], sem.at[slot])

    copy(0).start()
    for idx, (_, width, dst) in enumerate(_W_PIECES):
        if idx + 1 < len(_W_PIECES):
            copy(idx + 1).start()
        copy(idx).wait()
        w_scr[:, dst:dst + width] = stage[idx % 2, :, 0:width].astype(BF16)
    w_scr[:, D_IN:] = jnp.zeros((D_MODEL, P_WIDTH - D_IN), BF16)


def _inproj_kernel(x_ref, pw_ref, mu_ref, w_hbm, o_ref, w_scr, stage, sem, h_scr, carry_scr, *,
                   layer, tiles_per_seq):
    i = pl.program_id(0)

    @pl.when(i == 0)
    def _():
        carry_scr[...] = jnp.zeros_like(carry_scr)
        _load_weight(w_hbm, layer, w_scr, stage, sem)

    for r0 in range(0, IN_TM, 256):
        x = x_ref[r0:r0 + 256, :]
        ms = jnp.mean(x * x, axis=-1, keepdims=True)
        h_scr[r0:r0 + 256, :] = (x * lax.rsqrt(ms + NORM_EPS) * pw_ref[...]).astype(BF16)
    row = lax.broadcasted_iota(jnp.int32, (IN_TM, 1), 0)
    starts_sequence = (i % tiles_per_seq) == 0
    for c0 in range(0, P_WIDTH, IN_TN):
        acc = jnp.dot(h_scr[...], w_scr[:, c0:c0 + IN_TN], preferred_element_type=F32)
        if c0 >= P_R:
            cs = slice(c0 - P_R, c0 - P_R + IN_TN)
            before = jnp.where(starts_sequence, 0.0, carry_scr[0:1, cs])
            prev = jnp.where(row == 0, before, pltpu.roll(acc, 1, axis=0))
            carry_scr[0:1, cs] = acc[IN_TM - 1:IN_TM, :]
            acc = acc + mu_ref[:, cs] * (prev - acc)
        o_ref[:, c0:c0 + IN_TN] = acc.astype(o_ref.dtype)


def _inproj(x2d, pre_w, mu_p, w_in, layer, seq):
    m = x2d.shape[0]
    assert P_R % IN_TN == 0 and seq % IN_TM == 0 and mu_p.shape == (1, P_WIDTH - P_R)
    return pl.pallas_call(
        functools.partial(_inproj_kernel, layer=layer, tiles_per_seq=seq // IN_TM),
        grid=(m // IN_TM,),
        in_specs=[
            pl.BlockSpec((IN_TM, D_MODEL), lambda i: (i, 0)),
            pl.BlockSpec((1, D_MODEL), lambda i: (0, 0)),
            pl.BlockSpec((1, P_WIDTH - P_R), lambda i: (0, 0)),
            pl.BlockSpec(memory_space=pl.ANY),
        ],
        out_specs=pl.BlockSpec((IN_TM, P_WIDTH), lambda i: (i, 0)),
        out_shape=jax.ShapeDtypeStruct((m, P_WIDTH), ACT),
        scratch_shapes=[
            pltpu.VMEM((D_MODEL, P_WIDTH), BF16),
            pltpu.VMEM((2, D_MODEL, IN_STAGE), F32),
            pltpu.SemaphoreType.DMA((2,)),
            pltpu.VMEM((IN_TM, D_MODEL), BF16),
            pltpu.VMEM((8, P_WIDTH - P_R), F32),
        ],
        compiler_params=pltpu.CompilerParams(
            dimension_semantics=("arbitrary",), vmem_limit_bytes=_VMEM_LIMIT),
        name="inproj",
    )(x2d, pre_w, mu_p, w_in)


def _bdot(a, b):
    return jnp.dot(a.astype(BF16), b.astype(BF16), preferred_element_type=F32)


def _bdot_nt(a, b):
    return lax.dot_general(a.astype(BF16), b.astype(BF16), (((1,), (1,)), ((), ())),
                           preferred_element_type=F32)


def _split3(x):
    x1 = x.astype(BF16)
    r1 = x - x1.astype(F32)
    x2 = r1.astype(BF16)
    x3 = (r1 - x2.astype(F32)).astype(BF16)
    return x1, x2, x3


def _head_sums(xs, bd):
    rows = xs[0].shape[0]
    half = bd.shape[0]
    x = jnp.concatenate(xs, axis=0).astype(BF16)
    out = jnp.concatenate(
        [jnp.dot(x[:, :half], bd, preferred_element_type=F32),
         jnp.dot(x[:, half:], bd, preferred_element_type=F32)], axis=1)
    return [out[i * rows:(i + 1) * rows] for i in range(len(xs))]


def _mm(a16, b16):
    return jnp.dot(a16, b16, preferred_element_type=F32)


def _mm_nt(a16, b16):
    return lax.dot_general(a16, b16, (((1,), (1,)), ((), ())), preferred_element_type=F32)


def _block_diag(m16, bdm16):
    return jnp.concatenate([m16, m16], axis=0) * bdm16


def _unit_lower_inverse_many(lmats, eye, masks, bdm16):
    base_mask, base_bd16, level_bd16 = masks
    c = lmats[0].shape[0]
    l16s = [l.astype(BF16) for l in lmats]
    stack2 = lambda m16: jnp.concatenate([m16, m16], axis=0)
    lds = [jnp.where(base_mask, l, 0.0) for l in lmats]
    l2s = [_mm(ld.astype(BF16), stack2(l16) * base_bd16).astype(BF16)
           for ld, l16 in zip(lds, l16s)]
    yield
    xs = [eye + ld for ld in lds]
    both = [_mm(jnp.concatenate([x.astype(BF16), l2], axis=0), stack2(l2) * bdm16)
            for x, l2 in zip(xs, l2s)]
    xs = [x + b[:c] for x, b in zip(xs, both)]
    l4s = [b[c:].astype(BF16) for b in both]
    yield
    xs = [x + _mm(x.astype(BF16), stack2(l4) * bdm16) for x, l4 in zip(xs, l4s)]
    yield
    for lvl16 in level_bd16:
        ts = [_mm(x.astype(BF16), stack2(l16) * lvl16).astype(BF16) for x, l16 in zip(xs, l16s)]
        yield
        xs = [x + _mm(t, stack2(x.astype(BF16)) * bdm16) for x, t in zip(xs, ts)]
        yield
    return xs


_SET_BF16 = ("a", "b", "k", "r", "v", "b_e", "k_e")
_SET_NAMES = _SET_BF16 + ("g_all", "bonus")
_PREP_MXU_DELAY = 4
_EXP_NEG_HALF = float(np.exp(-0.5))


def _rwkv_prep(z_refs, row0, w, dst):
    tile, c = RW_TILE, RW_CHUNK
    rows = lambda z_ref: z_ref[row0:row0 + tile, :].astype(F32)

    lo = rows(z_refs[3])
    lora_w = _bdot(jnp.tanh(lo[:, :LORA]), w["wup"][...])
    lora_a = _bdot(lo[:, LORA:], w["aup"][...])
    yield
    r, k, v = rows(z_refs[0]), rows(z_refs[1]), rows(z_refs[2])
    logw = -_EXP_NEG_HALF * jax.nn.sigmoid(w["w0"][...] + lora_w)
    asig = jax.nn.sigmoid(w["a0"][...] + lora_a)
    logw_terms = _split3(logw)
    kk = k * w["kkw"][...]
    k2 = k * (1.0 + (asig - 1.0) * w["kaw"][...])
    sum_terms = [kk * kk, r * k2 * w["rkw"][...]]
    for _ in range(_PREP_MXU_DELAY):
        yield
    tril = w["tril"][...]
    cum = sum(jnp.dot(tril, p, preferred_element_type=F32) for p in logw_terms)
    kk_ss, rk_sum = _head_sums(sum_terms, w["bd"][...])
    yield
    kk = kk * lax.rsqrt(jnp.maximum(kk_ss, 1e-24))
    dst["bonus"][...] = rk_sum * v
    for ck in range(tile // c):
        sl = slice(ck * c, (ck + 1) * c)
        lw, cm = logw[sl], cum[sl]
        total = cm[c - 1:c, :]
        g_inv = jnp.exp(-cm)
        g_all = jnp.exp(total)
        g_tail = g_all * g_inv
        b_c = kk[sl] * asig[sl]
        dst["a"][sl, :] = (-kk[sl] * jnp.exp(cm - lw)).astype(BF16)
        dst["b"][sl, :] = (b_c * g_inv).astype(BF16)
        dst["k"][sl, :] = (k2[sl] * g_inv).astype(BF16)
        dst["r"][sl, :] = (r[sl] * jnp.exp(cm)).astype(BF16)
        dst["v"][sl, :] = v[sl].astype(BF16)
        dst["b_e"][sl, :] = (b_c * g_tail).astype(BF16)
        dst["k_e"][sl, :] = (k2[sl] * g_tail).astype(BF16)
        dst["g_all"][ck:ck + 1, :] = g_all
    yield


def _rwkv_main(src, w, consts, s_scr, y_scr, o_ref, row0):
    tile, c, pw = RW_TILE, RW_CHUNK, 2 * HEAD
    eye, masks, tri2, bdm4_16, bdm16, bdm = consts
    n_chunks, n_pairs = tile // c, N_HEADS // 2
    items = [(ck, p) for ck in range(n_chunks) for p in range(n_pairs)]

    def part(name, it):
        ck, p = it
        return src[name][ck * c:(ck + 1) * c, p * pw:(p + 1) * pw]

    a2 = [part("a", it) for it in items]
    r2 = [part("r", it) for it in items]
    l_ab, m_rb16, ak16 = [], [], []
    for a, x, it in zip(a2, r2, items):
        b, k_ = part("b", it), part("k", it)
        pr = _mm_nt(jnp.concatenate([a, x], axis=0),
                    jnp.concatenate([b, b, k_, k_], axis=0) * bdm4_16)
        ab = jnp.where(tri2, pr[:, :pw], 0.0)
        l_ab.append(ab[:c])
        m_rb16.append(ab[c:].astype(BF16))
        ak16.append(jnp.where(tri2, pr[:, pw:], 0.0).astype(BF16))
    yield
    qy = [_mm(x, _block_diag(part("v", it), bdm16))
          for x, it in zip(ak16, items)]
    qv16 = [q[:c].astype(BF16) for q in qy]
    yv = [q[c:] for q in qy]
    vk = [jnp.where(bdm, lax.dot_general(part("v", it), part("k_e", it), (((0,), (0,)), ((), ())),
                                         preferred_element_type=F32), 0.0) for it in items]
    yield
    tinv = yield from _unit_lower_inverse_many(l_ab, eye, masks, bdm16)
    wu = [_mm(t.astype(BF16),
              jnp.concatenate([_block_diag(a, bdm16), _block_diag(q, bdm16)], axis=1))
          for t, a, q in zip(tinv, a2, qv16)]
    wa16 = [x[:, :pw].astype(BF16) for x in wu]
    uv = [x[:, pw:] for x in wu]
    uvt = [x.T for x in uv]
    war = [jnp.concatenate([x, y], axis=0) for x, y in zip(wa16, r2)]
    yield

    state = [s_scr[p] for p in range(n_pairs)]
    for ck in range(n_chunks):
        idx = [ck * n_pairs + p for p in range(n_pairs)]
        s16 = [s.astype(BF16) for s in state]
        uy = [_mm_nt(war[i], s16[p]) for p, i in enumerate(idx)]
        ut = [_mm_nt(s16[p], wa16[i]) + uvt[i] for p, i in enumerate(idx)]
        for p, i in enumerate(idx):
            u16 = (uy[p][:c] + uv[i]).astype(BF16)
            y = uy[p][c:] + yv[i] + _mm(m_rb16[i], _block_diag(u16, bdm16))
            y_scr[ck * c:(ck + 1) * c, p * pw:(p + 1) * pw] = y
        state = [state[p] * src["g_all"][ck:ck + 1, p * pw:(p + 1) * pw] + vk[i]
                 + jnp.where(bdm, _mm(ut[p].astype(BF16), part("b_e", items[i])), 0.0)
                 for p, i in enumerate(idx)]
        yield
    for p in range(n_pairs):
        s_scr[p] = state[p]

    y = y_scr[...]
    bd = w["bd"][...]
    mu = _head_sums([y], bd)[0] * (1.0 / HEAD)
    d = y - mu
    var = _head_sums([d * d], bd)[0] * (1.0 / HEAD)
    out = d * lax.rsqrt(var + GN_EPS) * w["lnw"][...] + w["lnb"][...] + src["bonus"][...]
    o_ref[row0:row0 + tile, :] = out.astype(o_ref.dtype)
    yield


def _interleave(*streams):
    for _ in itertools.zip_longest(*streams):
        pass


_W_NAMES = ("w0", "wup", "a0", "aup", "kkw", "kaw", "rkw", "lnw", "lnb", "bd", "tril")


def _rwkv_kernel(*refs):
    cur, nxt = refs[0:4], refs[4:8]
    w = dict(zip(_W_NAMES, refs[8:8 + len(_W_NAMES)]))
    rest = refs[8 + len(_W_NAMES):]
    o_ref, s_scr, y_scr = rest[0], rest[1], rest[2]
    n_set = len(_SET_NAMES)
    set0 = dict(zip(_SET_NAMES, rest[3:3 + n_set]))
    set1 = dict(zip(_SET_NAMES, rest[3 + n_set:3 + 2 * n_set]))
    n = pl.program_id(1)

    @pl.when(n == 0)
    def _():
        s_scr[...] = jnp.zeros_like(s_scr)
        _interleave(_rwkv_prep(cur, 0, w, set0))

    c, pw = RW_CHUNK, 2 * HEAD
    ci = lax.broadcasted_iota(jnp.int32, (c, pw), 0)
    cj = lax.broadcasted_iota(jnp.int32, (c, pw), 1) % HEAD
    eye = (ci == cj).astype(F32)
    si = lax.broadcasted_iota(jnp.int32, (2 * c, pw), 0)
    sj = lax.broadcasted_iota(jnp.int32, (2 * c, pw), 1) % HEAD
    tri2 = ((si < c) & (si > sj)) | ((si >= c) & ((si - c) >= sj))
    bi = lax.broadcasted_iota(jnp.int32, (2 * pw, pw), 0)
    bj = lax.broadcasted_iota(jnp.int32, (2 * pw, pw), 1)
    bdm_bool = ((bi // HEAD) % 2) == (bj // HEAD)
    bdm4_16 = bdm_bool.astype(BF16)
    hi = lax.broadcasted_iota(jnp.int32, (pw, pw), 0)
    hj = lax.broadcasted_iota(jnp.int32, (pw, pw), 1)
    di, dj = hi % HEAD, hj % HEAD
    same_head = (hi // HEAD) == (hj // HEAD)
    near = lambda b: same_head & ((di // b) == (dj // b))
    ring = lambda b: same_head & ((di // (2 * b)) == (dj // (2 * b))) & ((di // b) != (dj // b))
    masks = ((ci // 8) == (cj // 8), near(8).astype(BF16),
             tuple(ring(b).astype(BF16) for b in (8, 16, 32)))
    consts = (eye, masks, tri2, bdm4_16, bdm4_16[:pw], bdm_bool[:pw])

    _interleave(_rwkv_prep(cur, RW_TILE, w, set1),
                _rwkv_main(set0, w, consts, s_scr, y_scr, o_ref, 0))
    _interleave(_rwkv_prep(nxt, 0, w, set0),
                _rwkv_main(set1, w, consts, s_scr, y_scr, o_ref, RW_TILE))


def _rwkv(proj, vecs, wup, aup, bd, tril, batch, seq):
    m = proj.shape[0]
    ns = seq // RW_STEP
    row = lambda b, n: b * ns + n
    nxt = lambda b, n: b * ns + jnp.minimum(n + 1, ns - 1)
    vec = lambda width: pl.BlockSpec((1, width), lambda b, n: (0, 0))
    full = lambda a: pl.BlockSpec(a.shape, lambda b, n: (0, 0))

    def token_specs(row_fn):
        wide = lambda cb: pl.BlockSpec((RW_STEP, D_BRANCH), lambda b, n: (row_fn(b, n), cb))
        return [wide(P_R // D_BRANCH), wide(P_K // D_BRANCH), wide(P_V // D_BRANCH),
                pl.BlockSpec((RW_STEP, 2 * LORA), lambda b, n: (row_fn(b, n), P_LORA // (2 * LORA)))]

    in_specs = token_specs(row) + token_specs(nxt) + [
        vec(D_BRANCH), full(wup), vec(D_BRANCH), full(aup),
        vec(D_BRANCH), vec(D_BRANCH), vec(D_BRANCH), vec(D_BRANCH), vec(D_BRANCH),
        full(bd), full(tril),
    ]
    prep_set = ([pltpu.VMEM((RW_TILE, D_BRANCH), BF16) for _ in _SET_BF16]
                + [pltpu.VMEM((8, D_BRANCH), F32), pltpu.VMEM((RW_TILE, D_BRANCH), F32)])
    return pl.pallas_call(
        _rwkv_kernel,
        grid=(batch, ns),
        in_specs=in_specs,
        out_specs=pl.BlockSpec((RW_STEP, D_BRANCH), lambda b, n: (row(b, n), 0)),
        out_shape=jax.ShapeDtypeStruct((m, D_BRANCH), ACT),
        scratch_shapes=[
            pltpu.VMEM((N_HEADS // 2, 2 * HEAD, 2 * HEAD), F32),
            pltpu.VMEM((RW_TILE, D_BRANCH), F32),
        ] + prep_set + prep_set,
        compiler_params=pltpu.CompilerParams(
            dimension_semantics=("parallel", "arbitrary"), vmem_limit_bytes=_VMEM_LIMIT),
        name="rwkv",
    )(*([proj] * 8), vecs[0], wup, vecs[1], aup, *vecs[2:], bd, tril)


_SLOPES = tuple(2.0 ** (-8.0 * (h + 1) / N_HEADS) for h in range(N_HEADS))
_SQRT_HALF = float(np.sqrt(0.5))


def _attention_bias():
    t = np.arange(BLK)[:, None]
    s = np.arange(2 * BLK)[None, :]
    dist = t + BLK - s
    out = np.empty((2, N_HEADS * BLK, 2 * BLK), np.float32)
    for first in (0, 1):
        valid = (dist >= 0) & (dist < BLK) & ((s >= BLK) | (first == 1))
        for h in range(N_HEADS):
            out[first, h * BLK:(h + 1) * BLK] = np.where(valid, -_SLOPES[h] * dist, NEG_INF)
    return out


def _pool_bands():
    t = np.arange(BLK)[:, None] + BLK
    s = np.arange(2 * BLK)[None, :]
    return np.stack([((s <= t) & (s > t - w)).astype(np.float32) for w in POOL_WINDOWS])


def _mixers_kernel(sinks_ref, bias_ref, band_ref, q_ref, kvc_ref, kvp_ref, zc_ref, zcp_ref, zd_ref,
                   poolw_ref, pscale_ref, nw_ref, sw_ref, sb_ref, o_ref, y_scr):
    n = pl.program_id(1)
    rows = ATT_GROUP * BLK
    head_in_group = lax.broadcasted_iota(jnp.int32, (rows, 1), 0) // BLK

    def per_row(values):
        col = jnp.full((rows, 1), values[-1], F32)
        for j in range(ATT_GROUP - 2, -1, -1):
            col = jnp.where(head_in_group == j, values[j], col)
        return col

    sinks = [per_row([sinks_ref[h] for h in range(g * ATT_GROUP, (g + 1) * ATT_GROUP)])
             for g in range(KV_HEADS)]
    ri = lax.broadcasted_iota(jnp.int32, (BLK, BLK), 0)
    rj = lax.broadcasted_iota(jnp.int32, (BLK, BLK), 1)
    sgu_w16 = [jnp.where(ri >= rj, sw_ref[g], 0.0).astype(BF16) for g in range(4)]
    refs = (bias_ref, band_ref, q_ref, kvc_ref, kvp_ref, zc_ref, zcp_ref, zd_ref, poolw_ref,
            pscale_ref, nw_ref, sb_ref, y_scr)
    for sub in range(MIX_SUB):
        _mixers_block(sub, n, refs, sinks, sgu_w16)
    o_ref[...] = y_scr[...].astype(o_ref.dtype)


def _mixers_block(sub, n, refs, sinks, sgu_w16):
    (bias_ref, band_ref, q_ref, kvc_ref, kvp_ref, zc_ref, zcp_ref, zd_ref, poolw_ref,
     pscale_ref, nw_ref, sb_ref, y_scr) = refs
    rs = slice(sub * BLK, (sub + 1) * BLK)
    before = slice((sub - 1) * BLK, sub * BLK)
    rows = ATT_GROUP * BLK
    if sub == 0:
        kv_prev = kvp_ref[...]
        z_prev = jnp.where(n > 0, zcp_ref[...], 0.0)
        bias_of = lambda g: bias_ref[jnp.minimum(n, 1), g * rows:(g + 1) * rows, :]
    else:
        kv_prev = kvc_ref[before, :]
        z_prev = zc_ref[before, :]
        bias_of = lambda g: bias_ref[1, g * rows:(g + 1) * rows, :]

    q = q_ref[rs, :] * (HEAD ** -0.5)
    kv = jnp.concatenate([kv_prev, kvc_ref[rs, :]], axis=0)

    scores = []
    for g in range(KV_HEADS):
        qg = jnp.concatenate(
            [q[:, (g * ATT_GROUP + j) * HEAD:(g * ATT_GROUP + j + 1) * HEAD]
             for j in range(ATT_GROUP)], axis=0)
        scores.append(_bdot_nt(qg, kv[:, g * HEAD:(g + 1) * HEAD]))

    zfull = jnp.concatenate([z_prev, zc_ref[rs, :]], axis=0)
    pos = (n * MIX_SUB + sub) * BLK + lax.broadcasted_iota(jnp.int32, (BLK, 1), 0) + 1
    sums = [jnp.dot(band_ref[g], zfull[:, g * BLK:(g + 1) * BLK], preferred_element_type=F32)
            for g in range(len(POOL_WINDOWS))]

    zd = zd_ref[rs, :].astype(F32)
    gz = 0.5 * zd * (1.0 + lax.erf(zd * _SQRT_HALF))
    u = gz[:, :D_BRANCH]
    vv = gz[:, D_BRANCH:]
    mu = jnp.mean(vv, axis=-1, keepdims=True)
    dv = vv - mu
    var = jnp.mean(dv * dv, axis=-1, keepdims=True)
    vn = dv * lax.rsqrt(var + LN_EPS) * nw_ref[...]
    for g in range(4):
        gs = slice(g * BLK, (g + 1) * BLK)
        sg = _bdot(sgu_w16[g], vn[:, gs]) + sb_ref[:, gs]
        y_scr[rs, 2 * D_BRANCH + g * BLK:2 * D_BRANCH + (g + 1) * BLK] = u[:, gs] * sg

    for g, w in enumerate(POOL_WINDOWS):
        gs = slice(g * BLK, (g + 1) * BLK)
        cnt = jnp.minimum(pos, w).astype(F32)
        pooled = sums[g] / cnt - zc_ref[rs, gs].astype(F32)
        yg = _bdot(pooled, poolw_ref[g]) * pscale_ref[:, gs]
        y_scr[rs, D_BRANCH + g * BLK:D_BRANCH + (g + 1) * BLK] = yg

    probs, dens = [], []
    for g in range(KV_HEADS):
        s = scores[g] + bias_of(g)
        mx = jnp.maximum(jnp.max(s, axis=-1, keepdims=True), sinks[g])
        p = jnp.exp(s - mx)
        probs.append(p)
        dens.append(jnp.sum(p, axis=-1, keepdims=True) + jnp.exp(sinks[g] - mx))
    for g in range(KV_HEADS):
        vg = kv[:, (KV_HEADS + g) * HEAD:(KV_HEADS + g + 1) * HEAD]
        og = _bdot(probs[g], vg) / dens[g]
        for j in range(ATT_GROUP):
            h = g * ATT_GROUP + j
            y_scr[rs, h * HEAD:(h + 1) * HEAD] = og[j * BLK:(j + 1) * BLK, :]


def _mixers(proj, sinks, pool_w, pool_scale, norm_w, sgu_w, sgu_bias, batch, seq):
    m = proj.shape[0]
    tm = MIX_SUB * BLK
    ns = seq // tm
    row = lambda b, n: b * ns + n
    before = lambda b, n: jnp.maximum(row(b, n) * MIX_SUB - 1, 0)
    bias = jnp.asarray(_attention_bias())
    assert proj.dtype == BF16
    band = jnp.asarray(_pool_bands(), dtype=BF16)
    in_specs = [
        pl.BlockSpec(memory_space=pltpu.SMEM),
        pl.BlockSpec(bias.shape, lambda b, n: (0, 0, 0)),
        pl.BlockSpec(band.shape, lambda b, n: (0, 0, 0)),
        pl.BlockSpec((tm, D_BRANCH), lambda b, n: (row(b, n), P_Q // D_BRANCH)),
        pl.BlockSpec((tm, 256), lambda b, n: (row(b, n), P_KV // 256)),
        pl.BlockSpec((BLK, 256), lambda b, n: (before(b, n), P_KV // 256)),
        pl.BlockSpec((tm, D_BRANCH), lambda b, n: (row(b, n), P_POOL // D_BRANCH)),
        pl.BlockSpec((BLK, D_BRANCH), lambda b, n: (before(b, n), P_POOL // D_BRANCH)),
        pl.BlockSpec((tm, 2 * D_BRANCH), lambda b, n: (row(b, n), P_SGU // (2 * D_BRANCH))),
        pl.BlockSpec((4, BLK, BLK), lambda b, n: (0, 0, 0)),
        pl.BlockSpec((1, D_BRANCH), lambda b, n: (0, 0)),
        pl.BlockSpec((1, D_BRANCH), lambda b, n: (0, 0)),
        pl.BlockSpec((4, BLK, BLK), lambda b, n: (0, 0, 0)),
        pl.BlockSpec((BLK, D_BRANCH), lambda b, n: (0, 0)),
    ]
    return pl.pallas_call(
        _mixers_kernel,
        grid=(batch, ns),
        in_specs=in_specs,
        out_specs=pl.BlockSpec((tm, 3 * D_BRANCH), lambda b, n: (row(b, n), 0)),
        out_shape=jax.ShapeDtypeStruct((m, 3 * D_BRANCH), ACT),
        scratch_shapes=[pltpu.VMEM((tm, 3 * D_BRANCH), F32)],
        compiler_params=pltpu.CompilerParams(
            dimension_semantics=("parallel", "parallel"), vmem_limit_bytes=_VMEM_LIMIT),
        name="mixers",
    )(sinks, bias, band, proj, proj, proj, proj, proj, proj, pool_w, pool_scale, norm_w, sgu_w,
      sgu_bias)


OUT_TM = 512
OUT_SUB = 256
OUT_STAGE = 512


def _outproj_kernel(ya_ref, yb_ref, g_ref, x_ref, w_hbm, pw_ref, o_ref, w_scr, stage, sem, *,
                    layer):
    @pl.when(pl.program_id(0) == 0)
    def _():
        def copy(idx):
            slot = idx % 2
            return pltpu.make_async_copy(w_hbm.at[layer, pl.ds(idx * OUT_STAGE, OUT_STAGE), :],
                                         stage.at[slot], sem.at[slot])

        n_pieces = D_MODEL // OUT_STAGE
        copy(0).start()
        for idx in range(n_pieces):
            if idx + 1 < n_pieces:
                copy(idx + 1).start()
            copy(idx).wait()
            w_scr[idx * OUT_STAGE:(idx + 1) * OUT_STAGE, :] = stage[idx % 2].astype(BF16)

    for r0 in range(0, OUT_TM, OUT_SUB):
        rows = slice(r0, r0 + OUT_SUB)
        g = g_ref[rows, :].astype(F32)
        y = jnp.concatenate([ya_ref[rows, :], yb_ref[rows, :]], axis=1).astype(F32)
        gated = (y * (g * jax.nn.sigmoid(g))).astype(BF16)
        acc = jnp.dot(gated, w_scr[...], preferred_element_type=F32)
        ms = jnp.mean(acc * acc, axis=-1, keepdims=True)
        o_ref[rows, :] = x_ref[rows, :] + acc * lax.rsqrt(ms + NORM_EPS) * pw_ref[...]


def _outproj(ya, ybcd, proj, x2d, w_out, layer, post_w):
    m = x2d.shape[0]
    return pl.pallas_call(
        functools.partial(_outproj_kernel, layer=layer),
        grid=(m // OUT_TM,),
        in_specs=[
            pl.BlockSpec((OUT_TM, D_BRANCH), lambda i: (i, 0)),
            pl.BlockSpec((OUT_TM, 3 * D_BRANCH), lambda i: (i, 0)),
            pl.BlockSpec((OUT_TM, D_MODEL), lambda i: (i, P_GATE // D_MODEL)),
            pl.BlockSpec((OUT_TM, D_MODEL), lambda i: (i, 0)),
            pl.BlockSpec(memory_space=pl.ANY),
            pl.BlockSpec((1, D_MODEL), lambda i: (0, 0)),
        ],
        out_specs=pl.BlockSpec((OUT_TM, D_MODEL), lambda i: (i, 0)),
        out_shape=jax.ShapeDtypeStruct((m, D_MODEL), F32),
        scratch_shapes=[
            pltpu.VMEM((D_MODEL, D_MODEL), BF16),
            pltpu.VMEM((2, OUT_STAGE, D_MODEL), F32),
            pltpu.SemaphoreType.DMA((2,)),
        ],
        compiler_params=pltpu.CompilerParams(
            dimension_semantics=("arbitrary",), vmem_limit_bytes=_VMEM_LIMIT),
        name="outproj",
    )(ya, ybcd, proj, x2d, w_out, post_w)


def kernel(x, pre_norm_w, post_norm_w, w_in, shift_mu, rwkv_w0, rwkv_w_up, rwkv_a0, rwkv_a_up,
           rwkv_k_k, rwkv_k_a, rwkv_r_k, rwkv_ln_w, rwkv_ln_b, attn_sinks, pool_w, pool_scale,
           sgu_norm_w, sgu_w, sgu_b, w_out):
    batch, seq, _ = x.shape
    assert x.shape == (batch, seq, D_MODEL) and seq % (MIX_SUB * BLK) == 0 and seq % RW_STEP == 0
    m = batch * seq
    head_id = np.arange(D_BRANCH // 2) // HEAD
    bd = jnp.asarray((head_id[:, None] == head_id[None, :]).astype(np.float32), dtype=BF16)
    t_id = np.arange(RW_TILE)
    tril = jnp.asarray(((t_id[:, None] >= t_id[None, :])
                        & (t_id[:, None] // RW_CHUNK == t_id[None, :] // RW_CHUNK)).astype(np.float32),
                       dtype=BF16)
    row_vec = lambda a: a.reshape(1, -1)

    h = x.reshape(m, D_MODEL)
    for l in range(DEPTH):
        mu = shift_mu[l]
        mu_p = jnp.concatenate([mu[:3 * D_BRANCH], jnp.zeros((P_LORA - P_KV,), mu.dtype),
                                mu[3 * D_BRANCH:], jnp.zeros((P_WIDTH - D_IN,), mu.dtype)])
        proj = _inproj(h, row_vec(pre_norm_w[l]), row_vec(mu_p), w_in, l, seq)

        vecs = [row_vec(rwkv_w0[l]), row_vec(rwkv_a0[l]), row_vec(rwkv_k_k[l]),
                row_vec(rwkv_k_a[l]), row_vec(rwkv_r_k[l]), row_vec(rwkv_ln_w[l]),
                row_vec(rwkv_ln_b[l])]
        ya = _rwkv(proj, vecs, rwkv_w_up[l], rwkv_a_up[l], bd, tril, batch, seq)

        sgu_bias = jnp.broadcast_to(sgu_b[l].T[:, :, None], (BLK, 4, BLK)).reshape(BLK, D_BRANCH)
        ybcd = _mixers(proj, attn_sinks[l], pool_w[l], row_vec(pool_scale[l]),
                       row_vec(sgu_norm_w[l]), sgu_w[l], sgu_bias, batch, seq)

        h = _outproj(ya, ybcd, proj, h, w_out, l, row_vec(post_norm_w[l]))
    return h.reshape(batch, seq, D_MODEL)
```

```python
import functools
import itertools

import jax
import jax.numpy as jnp
import numpy as np
from jax import lax
from jax.experimental import pallas as pl
from jax.experimental.pallas import tpu as pltpu

F32 = jnp.float32
BF16 = jnp.bfloat16
ACT = BF16

D_MODEL = 2048
DEPTH = 2
D_BRANCH = 512
HEAD = 64
N_HEADS = D_BRANCH // HEAD
LORA = 64
KV_HEADS = 2
ATT_GROUP = N_HEADS // KV_HEADS
BLK = 128
MIX_SUB = 4
POOL_WINDOWS = (2, 4, 8, 16)
NEG_INF = -1e30
NORM_EPS = 1e-6
LN_EPS = 1e-5
GN_EPS = 64e-5

A_COLS = 3 * D_BRANCH + 2 * LORA
B_COLS = D_BRANCH + 2 * KV_HEADS * HEAD
OFF_B = A_COLS
OFF_C = OFF_B + B_COLS
OFF_D = OFF_C + D_BRANCH
OFF_G = OFF_D + 2 * D_BRANCH
D_IN = OFF_G + D_MODEL

P_GATE = 0
P_SGU = 2048
P_POOL = 3072
P_Q = 3584
P_R = 4096
P_K = 4608
P_V = 5120
P_KV = 5632
P_LORA = 5888
P_WIDTH = 6144

RW_CHUNK = 64
RW_TILE = 256
RW_STEP = 2 * RW_TILE

_VMEM_LIMIT = 56 * 1024 * 1024


IN_TM = 512
IN_TN = 1024
IN_STAGE = 256
_W_RUNS = ((OFF_G, D_MODEL, P_GATE), (OFF_D, 2 * D_BRANCH, P_SGU), (OFF_C, D_BRANCH, P_POOL),
           (OFF_B, D_BRANCH, P_Q), (0, 3 * D_BRANCH, P_R),
           (OFF_B + D_BRANCH, 2 * KV_HEADS * HEAD, P_KV), (3 * D_BRANCH, 2 * LORA, P_LORA))
_W_PIECES = tuple((src + o, min(IN_STAGE, width - o), dst + o)
                  for src, width, dst in _W_RUNS for o in range(0, width, IN_STAGE))
assert sum(p[1] for p in _W_PIECES) == D_IN and P_LORA + 2 * LORA == D_IN


IN_SLOTS = 2
assert all(dst // IN_TN == (dst + width - 1) // IN_TN for _, width, dst in _W_PIECES)


def _weight_copy(w_hbm, layer, stage, sem, idx):
    src, width, _ = _W_PIECES[idx]
    slot = idx % IN_SLOTS
    return pltpu.make_async_copy(w_hbm.at[layer, :, pl.ds(src, width)],
                                 stage.at[slot, :, pl.ds(0, width)], sem.at[slot])


def _land_weight_chunk(c0, w_hbm, layer, w_scr, stage, sem):
    for idx, (_, width, dst) in enumerate(_W_PIECES):
        if c0 <= dst < c0 + IN_TN:
            _weight_copy(w_hbm, layer, stage, sem, idx).wait()
            w_scr[:, dst:dst + width] = stage[idx % IN_SLOTS, :, 0:width].astype(BF16)
            if idx + IN_SLOTS < len(_W_PIECES):
                _weight_copy(w_hbm, layer, stage, sem, idx + IN_SLOTS).start()


def _inproj_kernel(x_ref, pw_ref, mu_ref, w_hbm, o_ref, w_scr, stage, sem, h_scr, carry_scr, *,
                   layer, tiles_per_seq):
    i = pl.program_id(0)

    @pl.when(i == 0)
    def _():
        carry_scr[...] = jnp.zeros_like(carry_scr)
        w_scr[:, D_IN:] = jnp.zeros((D_MODEL, P_WIDTH - D_IN), BF16)
        for idx in range(IN_SLOTS):
            _weight_copy(w_hbm, layer, stage, sem, idx).start()

    for r0 in range(0, IN_TM, 256):
        x = x_ref[r0:r0 + 256, :]
        ms = jnp.mean(x * x, axis=-1, keepdims=True)
        h_scr[r0:r0 + 256, :] = (x * lax.rsqrt(ms + NORM_EPS) * pw_ref[...]).astype(BF16)
    row = lax.broadcasted_iota(jnp.int32, (IN_TM, 1), 0)
    starts_sequence = (i % tiles_per_seq) == 0
    for c0 in range(0, P_WIDTH, IN_TN):
        @pl.when(i == 0)
        def _(c0=c0):
            _land_weight_chunk(c0, w_hbm, layer, w_scr, stage, sem)

        acc = jnp.dot(h_scr[...], w_scr[:, c0:c0 + IN_TN], preferred_element_type=F32)
        if c0 >= P_R:
            cs = slice(c0 - P_R, c0 - P_R + IN_TN)
            before = jnp.where(starts_sequence, 0.0, carry_scr[0:1, cs])
            prev = jnp.where(row == 0, before, pltpu.roll(acc, 1, axis=0))
            carry_scr[0:1, cs] = acc[IN_TM - 1:IN_TM, :]
            acc = acc + mu_ref[:, cs] * (prev - acc)
        o_ref[:, c0:c0 + IN_TN] = acc.astype(o_ref.dtype)


def _inproj(x2d, pre_w, mu_p, w_in, layer, seq):
    m = x2d.shape[0]
    assert P_R % IN_TN == 0 and seq % IN_TM == 0 and mu_p.shape == (1, P_WIDTH - P_R)
    return pl.pallas_call(
        functools.partial(_inproj_kernel, layer=layer, tiles_per_seq=seq // IN_TM),
        grid=(m // IN_TM,),
        in_specs=[
            pl.BlockSpec((IN_TM, D_MODEL), lambda i: (i, 0)),
            pl.BlockSpec((1, D_MODEL), lambda i: (0, 0)),
            pl.BlockSpec((1, P_WIDTH - P_R), lambda i: (0, 0)),
            pl.BlockSpec(memory_space=pl.ANY),
        ],
        out_specs=pl.BlockSpec((IN_TM, P_WIDTH), lambda i: (i, 0)),
        out_shape=jax.ShapeDtypeStruct((m, P_WIDTH), ACT),
        scratch_shapes=[
            pltpu.VMEM((D_MODEL, P_WIDTH), BF16),
            pltpu.VMEM((IN_SLOTS, D_MODEL, IN_STAGE), F32),
            pltpu.SemaphoreType.DMA((IN_SLOTS,)),
            pltpu.VMEM((IN_TM, D_MODEL), BF16),
            pltpu.VMEM((8, P_WIDTH - P_R), F32),
        ],
        compiler_params=pltpu.CompilerParams(
            dimension_semantics=("arbitrary",), vmem_limit_bytes=_VMEM_LIMIT),
        name="inproj",
    )(x2d, pre_w, mu_p, w_in)


def _bdot(a, b):
    return jnp.dot(a.astype(BF16), b.astype(BF16), preferred_element_type=F32)


def _bdot_nt(a, b):
    return lax.dot_general(a.astype(BF16), b.astype(BF16), (((1,), (1,)), ((), ())),
                           preferred_element_type=F32)


def _split3(x):
    x1 = x.astype(BF16)
    r1 = x - x1.astype(F32)
    x2 = r1.astype(BF16)
    x3 = (r1 - x2.astype(F32)).astype(BF16)
    return x1, x2, x3


def _head_sums(xs, bd):
    rows = xs[0].shape[0]
    half = bd.shape[0]
    x = jnp.concatenate(xs, axis=0).astype(BF16)
    out = jnp.concatenate(
        [jnp.dot(x[:, :half], bd, preferred_element_type=F32),
         jnp.dot(x[:, half:], bd, preferred_element_type=F32)], axis=1)
    return [out[i * rows:(i + 1) * rows] for i in range(len(xs))]


def _mm(a16, b16):
    return jnp.dot(a16, b16, preferred_element_type=F32)


def _mm_nt(a16, b16):
    return lax.dot_general(a16, b16, (((1,), (1,)), ((), ())), preferred_element_type=F32)


def _block_diag(m16, bdm16):
    return jnp.concatenate([m16, m16], axis=0) * bdm16


def _unit_lower_inverse_many(lmats, eye, masks, bdm16):
    base_mask, base_bd16, level_bd16 = masks
    c = lmats[0].shape[0]
    l16s = [l.astype(BF16) for l in lmats]
    stack2 = lambda m16: jnp.concatenate([m16, m16], axis=0)
    lds = [jnp.where(base_mask, l, 0.0) for l in lmats]
    l2s = [_mm(ld.astype(BF16), stack2(l16) * base_bd16).astype(BF16)
           for ld, l16 in zip(lds, l16s)]
    yield
    xs = [eye + ld for ld in lds]
    both = [_mm(jnp.concatenate([x.astype(BF16), l2], axis=0), stack2(l2) * bdm16)
            for x, l2 in zip(xs, l2s)]
    xs = [x + b[:c] for x, b in zip(xs, both)]
    l4s = [b[c:].astype(BF16) for b in both]
    yield
    xs = [x + _mm(x.astype(BF16), stack2(l4) * bdm16) for x, l4 in zip(xs, l4s)]
    yield
    for lvl16 in level_bd16:
        ts = [_mm(x.astype(BF16), stack2(l16) * lvl16).astype(BF16) for x, l16 in zip(xs, l16s)]
        yield
        xs = [x + _mm(t, stack2(x.astype(BF16)) * bdm16) for x, t in zip(xs, ts)]
        yield
    return xs


_SET_BF16 = ("a", "b", "k", "r", "v", "b_e", "k_e")
_SET_NAMES = _SET_BF16 + ("g_all", "bonus")
_PREP_MXU_DELAY = 4
_EXP_NEG_HALF = float(np.exp(-0.5))


def _rwkv_prep(z_refs, row0, w, dst):
    tile, c = RW_TILE, RW_CHUNK
    rows = lambda z_ref: z_ref[row0:row0 + tile, :].astype(F32)

    lo = rows(z_refs[3])
    lora_w = _bdot(jnp.tanh(lo[:, :LORA]), w["wup"][...])
    lora_a = _bdot(lo[:, LORA:], w["aup"][...])
    yield
    r, k, v = rows(z_refs[0]), rows(z_refs[1]), rows(z_refs[2])
    logw = -_EXP_NEG_HALF * jax.nn.sigmoid(w["w0"][...] + lora_w)
    asig = jax.nn.sigmoid(w["a0"][...] + lora_a)
    logw_terms = _split3(logw)
    kk = k * w["kkw"][...]
    k2 = k * (1.0 + (asig - 1.0) * w["kaw"][...])
    sum_terms = [kk * kk, r * k2 * w["rkw"][...]]
    for _ in range(_PREP_MXU_DELAY):
        yield
    tril = w["tril"][...]
    cum = sum(jnp.dot(tril, p, preferred_element_type=F32) for p in logw_terms)
    kk_ss, rk_sum = _head_sums(sum_terms, w["bd"][...])
    yield
    kk = kk * lax.rsqrt(jnp.maximum(kk_ss, 1e-24))
    dst["bonus"][...] = rk_sum * v
    for ck in range(tile // c):
        sl = slice(ck * c, (ck + 1) * c)
        lw, cm = logw[sl], cum[sl]
        total = cm[c - 1:c, :]
        g_inv = jnp.exp(-cm)
        g_all = jnp.exp(total)
        g_tail = g_all * g_inv
        b_c = kk[sl] * asig[sl]
        dst["a"][sl, :] = (-kk[sl] * jnp.exp(cm - lw)).astype(BF16)
        dst["b"][sl, :] = (b_c * g_inv).astype(BF16)
        dst["k"][sl, :] = (k2[sl] * g_inv).astype(BF16)
        dst["r"][sl, :] = (r[sl] * jnp.exp(cm)).astype(BF16)
        dst["v"][sl, :] = v[sl].astype(BF16)
        dst["b_e"][sl, :] = (b_c * g_tail).astype(BF16)
        dst["k_e"][sl, :] = (k2[sl] * g_tail).astype(BF16)
        dst["g_all"][ck:ck + 1, :] = g_all
    yield


def _rwkv_main(src, w, consts, s_scr, y_scr, o_ref, row0):
    tile, c, pw = RW_TILE, RW_CHUNK, 2 * HEAD
    eye, masks, tri2, bdm4_16, bdm16, bdm = consts
    n_chunks, n_pairs = tile // c, N_HEADS // 2
    items = [(ck, p) for ck in range(n_chunks) for p in range(n_pairs)]

    def part(name, it):
        ck, p = it
        return src[name][ck * c:(ck + 1) * c, p * pw:(p + 1) * pw]

    a2 = [part("a", it) for it in items]
    r2 = [part("r", it) for it in items]
    l_ab, m_rb16, ak16 = [], [], []
    for a, x, it in zip(a2, r2, items):
        b, k_ = part("b", it), part("k", it)
        pr = _mm_nt(jnp.concatenate([a, x], axis=0),
                    jnp.concatenate([b, b, k_, k_], axis=0) * bdm4_16)
        ab = jnp.where(tri2, pr[:, :pw], 0.0)
        l_ab.append(ab[:c])
        m_rb16.append(ab[c:].astype(BF16))
        ak16.append(jnp.where(tri2, pr[:, pw:], 0.0).astype(BF16))
    yield
    qy = [_mm(x, _block_diag(part("v", it), bdm16))
          for x, it in zip(ak16, items)]
    qv16 = [q[:c].astype(BF16) for q in qy]
    yv = [q[c:] for q in qy]
    vk = [jnp.where(bdm, lax.dot_general(part("v", it), part("k_e", it), (((0,), (0,)), ((), ())),
                                         preferred_element_type=F32), 0.0) for it in items]
    yield
    tinv = yield from _unit_lower_inverse_many(l_ab, eye, masks, bdm16)
    wu = [_mm(t.astype(BF16),
              jnp.concatenate([_block_diag(a, bdm16), _block_diag(q, bdm16)], axis=1))
          for t, a, q in zip(tinv, a2, qv16)]
    wa16 = [x[:, :pw].astype(BF16) for x in wu]
    uv = [x[:, pw:] for x in wu]
    uvt = [x.T for x in uv]
    war = [jnp.concatenate([x, y], axis=0) for x, y in zip(wa16, r2)]
    yield

    state = [s_scr[p] for p in range(n_pairs)]
    for ck in range(n_chunks):
        idx = [ck * n_pairs + p for p in range(n_pairs)]
        s16 = [s.astype(BF16) for s in state]
        uy = [_mm_nt(war[i], s16[p]) for p, i in enumerate(idx)]
        ut = [_mm_nt(s16[p], wa16[i]) + uvt[i] for p, i in enumerate(idx)]
        for p, i in enumerate(idx):
            u16 = (uy[p][:c] + uv[i]).astype(BF16)
            y = uy[p][c:] + yv[i] + _mm(m_rb16[i], _block_diag(u16, bdm16))
            y_scr[ck * c:(ck + 1) * c, p * pw:(p + 1) * pw] = y
        state = [state[p] * src["g_all"][ck:ck + 1, p * pw:(p + 1) * pw] + vk[i]
                 + jnp.where(bdm, _mm(ut[p].astype(BF16), part("b_e", items[i])), 0.0)
                 for p, i in enumerate(idx)]
        yield
    for p in range(n_pairs):
        s_scr[p] = state[p]

    y = y_scr[...]
    bd = w["bd"][...]
    mu = _head_sums([y], bd)[0] * (1.0 / HEAD)
    d = y - mu
    var = _head_sums([d * d], bd)[0] * (1.0 / HEAD)
    out = d * lax.rsqrt(var + GN_EPS) * w["lnw"][...] + w["lnb"][...] + src["bonus"][...]
    o_ref[row0:row0 + tile, :] = out.astype(o_ref.dtype)
    yield


def _interleave(*streams):
    for _ in itertools.zip_longest(*streams):
        pass


_W_NAMES = ("w0", "wup", "a0", "aup", "kkw", "kaw", "rkw", "lnw", "lnb", "bd", "tril")


def _rwkv_kernel(*refs):
    cur, nxt = refs[0:4], refs[4:8]
    w = dict(zip(_W_NAMES, refs[8:8 + len(_W_NAMES)]))
    rest = refs[8 + len(_W_NAMES):]
    o_ref, s_scr, y_scr = rest[0], rest[1], rest[2]
    n_set = len(_SET_NAMES)
    set0 = dict(zip(_SET_NAMES, rest[3:3 + n_set]))
    set1 = dict(zip(_SET_NAMES, rest[3 + n_set:3 + 2 * n_set]))
    n = pl.program_id(1)

    @pl.when(n == 0)
    def _():
        s_scr[...] = jnp.zeros_like(s_scr)
        _interleave(_rwkv_prep(cur, 0, w, set0))

    c, pw = RW_CHUNK, 2 * HEAD
    ci = lax.broadcasted_iota(jnp.int32, (c, pw), 0)
    cj = lax.broadcasted_iota(jnp.int32, (c, pw), 1) % HEAD
    eye = (ci == cj).astype(F32)
    si = lax.broadcasted_iota(jnp.int32, (2 * c, pw), 0)
    sj = lax.broadcasted_iota(jnp.int32, (2 * c, pw), 1) % HEAD
    tri2 = ((si < c) & (si > sj)) | ((si >= c) & ((si - c) >= sj))
    bi = lax.broadcasted_iota(jnp.int32, (2 * pw, pw), 0)
    bj = lax.broadcasted_iota(jnp.int32, (2 * pw, pw), 1)
    bdm_bool = ((bi // HEAD) % 2) == (bj // HEAD)
    bdm4_16 = bdm_bool.astype(BF16)
    hi = lax.broadcasted_iota(jnp.int32, (pw, pw), 0)
    hj = lax.broadcasted_iota(jnp.int32, (pw, pw), 1)
    di, dj = hi % HEAD, hj % HEAD
    same_head = (hi // HEAD) == (hj // HEAD)
    near = lambda b: same_head & ((di // b) == (dj // b))
    ring = lambda b: same_head & ((di // (2 * b)) == (dj // (2 * b))) & ((di // b) != (dj // b))
    masks = ((ci // 8) == (cj // 8), near(8).astype(BF16),
             tuple(ring(b).astype(BF16) for b in (8, 16, 32)))
    consts = (eye, masks, tri2, bdm4_16, bdm4_16[:pw], bdm_bool[:pw])

    _interleave(_rwkv_prep(cur, RW_TILE, w, set1),
                _rwkv_main(set0, w, consts, s_scr, y_scr, o_ref, 0))
    _interleave(_rwkv_prep(nxt, 0, w, set0),
                _rwkv_main(set1, w, consts, s_scr, y_scr, o_ref, RW_TILE))


def _rwkv(proj, vecs, wup, aup, bd, tril, batch, seq):
    m = proj.shape[0]
    ns = seq // RW_STEP
    row = lambda b, n: b * ns + n
    nxt = lambda b, n: b * ns + jnp.minimum(n + 1, ns - 1)
    vec = lambda width: pl.BlockSpec((1, width), lambda b, n: (0, 0))
    full = lambda a: pl.BlockSpec(a.shape, lambda b, n: (0, 0))

    def token_specs(row_fn):
        wide = lambda cb: pl.BlockSpec((RW_STEP, D_BRANCH), lambda b, n: (row_fn(b, n), cb))
        return [wide(P_R // D_BRANCH), wide(P_K // D_BRANCH), wide(P_V // D_BRANCH),
                pl.BlockSpec((RW_STEP, 2 * LORA), lambda b, n: (row_fn(b, n), P_LORA // (2 * LORA)))]

    in_specs = token_specs(row) + token_specs(nxt) + [
        vec(D_BRANCH), full(wup), vec(D_BRANCH), full(aup),
        vec(D_BRANCH), vec(D_BRANCH), vec(D_BRANCH), vec(D_BRANCH), vec(D_BRANCH),
        full(bd), full(tril),
    ]
    prep_set = ([pltpu.VMEM((RW_TILE, D_BRANCH), BF16) for _ in _SET_BF16]
                + [pltpu.VMEM((8, D_BRANCH), F32), pltpu.VMEM((RW_TILE, D_BRANCH), F32)])
    return pl.pallas_call(
        _rwkv_kernel,
        grid=(batch, ns),
        in_specs=in_specs,
        out_specs=pl.BlockSpec((RW_STEP, D_BRANCH), lambda b, n: (row(b, n), 0)),
        out_shape=jax.ShapeDtypeStruct((m, D_BRANCH), ACT),
        scratch_shapes=[
            pltpu.VMEM((N_HEADS // 2, 2 * HEAD, 2 * HEAD), F32),
            pltpu.VMEM((RW_TILE, D_BRANCH), F32),
        ] + prep_set + prep_set,
        compiler_params=pltpu.CompilerParams(
            dimension_semantics=("parallel", "arbitrary"), vmem_limit_bytes=_VMEM_LIMIT),
        name="rwkv",
    )(*([proj] * 8), vecs[0], wup, vecs[1], aup, *vecs[2:], bd, tril)


_SLOPES = tuple(2.0 ** (-8.0 * (h + 1) / N_HEADS) for h in range(N_HEADS))
_SQRT_HALF = float(np.sqrt(0.5))


def _attention_bias():
    t = np.arange(BLK)[:, None]
    s = np.arange(2 * BLK)[None, :]
    dist = t + BLK - s
    out = np.empty((2, N_HEADS * BLK, 2 * BLK), np.float32)
    for first in (0, 1):
        valid = (dist >= 0) & (dist < BLK) & ((s >= BLK) | (first == 1))
        for h in range(N_HEADS):
            out[first, h * BLK:(h + 1) * BLK] = np.where(valid, -_SLOPES[h] * dist, NEG_INF)
    return out


def _pool_bands():
    t = np.arange(BLK)[:, None] + BLK
    s = np.arange(2 * BLK)[None, :]
    return np.stack([((s <= t) & (s > t - w)).astype(np.float32) for w in POOL_WINDOWS])


def _mixers_kernel(sinks_ref, bias_ref, band_ref, q_ref, kvc_ref, kvp_ref, zc_ref, zcp_ref, zd_ref,
                   poolw_ref, pscale_ref, nw_ref, sw_ref, sb_ref, o_ref, y_scr):
    n = pl.program_id(1)
    rows = ATT_GROUP * BLK
    head_in_group = lax.broadcasted_iota(jnp.int32, (rows, 1), 0) // BLK

    def per_row(values):
        col = jnp.full((rows, 1), values[-1], F32)
        for j in range(ATT_GROUP - 2, -1, -1):
            col = jnp.where(head_in_group == j, values[j], col)
        return col

    sinks = [per_row([sinks_ref[h] for h in range(g * ATT_GROUP, (g + 1) * ATT_GROUP)])
             for g in range(KV_HEADS)]
    ri = lax.broadcasted_iota(jnp.int32, (BLK, BLK), 0)
    rj = lax.broadcasted_iota(jnp.int32, (BLK, BLK), 1)
    sgu_w16 = [jnp.where(ri >= rj, sw_ref[g], 0.0).astype(BF16) for g in range(4)]
    refs = (bias_ref, band_ref, q_ref, kvc_ref, kvp_ref, zc_ref, zcp_ref, zd_ref, poolw_ref,
            pscale_ref, nw_ref, sb_ref, y_scr)
    for sub in range(MIX_SUB):
        _mixers_block(sub, n, refs, sinks, sgu_w16)
    o_ref[...] = y_scr[...].astype(o_ref.dtype)


def _mixers_block(sub, n, refs, sinks, sgu_w16):
    (bias_ref, band_ref, q_ref, kvc_ref, kvp_ref, zc_ref, zcp_ref, zd_ref, poolw_ref,
     pscale_ref, nw_ref, sb_ref, y_scr) = refs
    rs = slice(sub * BLK, (sub + 1) * BLK)
    before = slice((sub - 1) * BLK, sub * BLK)
    rows = ATT_GROUP * BLK
    if sub == 0:
        kv_prev = kvp_ref[...]
        z_prev = jnp.where(n > 0, zcp_ref[...], 0.0)
        bias_of = lambda g: bias_ref[jnp.minimum(n, 1), g * rows:(g + 1) * rows, :]
    else:
        kv_prev = kvc_ref[before, :]
        z_prev = zc_ref[before, :]
        bias_of = lambda g: bias_ref[1, g * rows:(g + 1) * rows, :]

    q = q_ref[rs, :] * (HEAD ** -0.5)
    kv = jnp.concatenate([kv_prev, kvc_ref[rs, :]], axis=0)

    scores = []
    for g in range(KV_HEADS):
        qg = jnp.concatenate(
            [q[:, (g * ATT_GROUP + j) * HEAD:(g * ATT_GROUP + j + 1) * HEAD]
             for j in range(ATT_GROUP)], axis=0)
        scores.append(_bdot_nt(qg, kv[:, g * HEAD:(g + 1) * HEAD]))

    zfull = jnp.concatenate([z_prev, zc_ref[rs, :]], axis=0)
    pos = (n * MIX_SUB + sub) * BLK + lax.broadcasted_iota(jnp.int32, (BLK, 1), 0) + 1
    sums = [jnp.dot(band_ref[g], zfull[:, g * BLK:(g + 1) * BLK], preferred_element_type=F32)
            for g in range(len(POOL_WINDOWS))]

    zd = zd_ref[rs, :].astype(F32)
    gz = 0.5 * zd * (1.0 + lax.erf(zd * _SQRT_HALF))
    u = gz[:, :D_BRANCH]
    vv = gz[:, D_BRANCH:]
    mu = jnp.mean(vv, axis=-1, keepdims=True)
    dv = vv - mu
    var = jnp.mean(dv * dv, axis=-1, keepdims=True)
    vn = dv * lax.rsqrt(var + LN_EPS) * nw_ref[...]
    for g in range(4):
        gs = slice(g * BLK, (g + 1) * BLK)
        sg = _bdot(sgu_w16[g], vn[:, gs]) + sb_ref[:, gs]
        y_scr[rs, 2 * D_BRANCH + g * BLK:2 * D_BRANCH + (g + 1) * BLK] = u[:, gs] * sg

    for g, w in enumerate(POOL_WINDOWS):
        gs = slice(g * BLK, (g + 1) * BLK)
        cnt = jnp.minimum(pos, w).astype(F32)
        pooled = sums[g] / cnt - zc_ref[rs, gs].astype(F32)
        yg = _bdot(pooled, poolw_ref[g]) * pscale_ref[:, gs]
        y_scr[rs, D_BRANCH + g * BLK:D_BRANCH + (g + 1) * BLK] = yg

    probs, dens = [], []
    for g in range(KV_HEADS):
        s = scores[g] + bias_of(g)
        mx = jnp.maximum(jnp.max(s, axis=-1, keepdims=True), sinks[g])
        p = jnp.exp(s - mx)
        probs.append(p)
        dens.append(jnp.sum(p, axis=-1, keepdims=True) + jnp.exp(sinks[g] - mx))
    for g in range(KV_HEADS):
        vg = kv[:, (KV_HEADS + g) * HEAD:(KV_HEADS + g + 1) * HEAD]
        og = _bdot(probs[g], vg) / dens[g]
        for j in range(ATT_GROUP):
            h = g * ATT_GROUP + j
            y_scr[rs, h * HEAD:(h + 1) * HEAD] = og[j * BLK:(j + 1) * BLK, :]


def _mixers(proj, sinks, pool_w, pool_scale, norm_w, sgu_w, sgu_bias, batch, seq):
    m = proj.shape[0]
    tm = MIX_SUB * BLK
    ns = seq // tm
    row = lambda b, n: b * ns + n
    before = lambda b, n: jnp.maximum(row(b, n) * MIX_SUB - 1, 0)
    bias = jnp.asarray(_attention_bias())
    assert proj.dtype == BF16
    band = jnp.asarray(_pool_bands(), dtype=BF16)
    in_specs = [
        pl.BlockSpec(memory_space=pltpu.SMEM),
        pl.BlockSpec(bias.shape, lambda b, n: (0, 0, 0)),
        pl.BlockSpec(band.shape, lambda b, n: (0, 0, 0)),
        pl.BlockSpec((tm, D_BRANCH), lambda b, n: (row(b, n), P_Q // D_BRANCH)),
        pl.BlockSpec((tm, 256), lambda b, n: (row(b, n), P_KV // 256)),
        pl.BlockSpec((BLK, 256), lambda b, n: (before(b, n), P_KV // 256)),
        pl.BlockSpec((tm, D_BRANCH), lambda b, n: (row(b, n), P_POOL // D_BRANCH)),
        pl.BlockSpec((BLK, D_BRANCH), lambda b, n: (before(b, n), P_POOL // D_BRANCH)),
        pl.BlockSpec((tm, 2 * D_BRANCH), lambda b, n: (row(b, n), P_SGU // (2 * D_BRANCH))),
        pl.BlockSpec((4, BLK, BLK), lambda b, n: (0, 0, 0)),
        pl.BlockSpec((1, D_BRANCH), lambda b, n: (0, 0)),
        pl.BlockSpec((1, D_BRANCH), lambda b, n: (0, 0)),
        pl.BlockSpec((4, BLK, BLK), lambda b, n: (0, 0, 0)),
        pl.BlockSpec((BLK, D_BRANCH), lambda b, n: (0, 0)),
    ]
    return pl.pallas_call(
        _mixers_kernel,
        grid=(batch, ns),
        in_specs=in_specs,
        out_specs=pl.BlockSpec((tm, 3 * D_BRANCH), lambda b, n: (row(b, n), 0)),
        out_shape=jax.ShapeDtypeStruct((m, 3 * D_BRANCH), ACT),
        scratch_shapes=[pltpu.VMEM((tm, 3 * D_BRANCH), F32)],
        compiler_params=pltpu.CompilerParams(
            dimension_semantics=("parallel", "parallel"), vmem_limit_bytes=_VMEM_LIMIT),
        name="mixers",
    )(sinks, bias, band, proj, proj, proj, proj, proj, proj, pool_w, pool_scale, norm_w, sgu_w,
      sgu_bias)


OUT_TM = 512
OUT_SUB = 256
OUT_STAGE = 512


def _outproj_kernel(ya_ref, yb_ref, g_ref, x_ref, w_hbm, pw_ref, o_ref, w_scr, stage, sem, *,
                    layer):
    @pl.when(pl.program_id(0) == 0)
    def _():
        def copy(idx):
            slot = idx % 2
            return pltpu.make_async_copy(w_hbm.at[layer, pl.ds(idx * OUT_STAGE, OUT_STAGE), :],
                                         stage.at[slot], sem.at[slot])

        n_pieces = D_MODEL // OUT_STAGE
        copy(0).start()
        for idx in range(n_pieces):
            if idx + 1 < n_pieces:
                copy(idx + 1).start()
            copy(idx).wait()
            w_scr[idx * OUT_STAGE:(idx + 1) * OUT_STAGE, :] = stage[idx % 2].astype(BF16)

    for r0 in range(0, OUT_TM, OUT_SUB):
        rows = slice(r0, r0 + OUT_SUB)
        g = g_ref[rows, :].astype(F32)
        y = jnp.concatenate([ya_ref[rows, :], yb_ref[rows, :]], axis=1).astype(F32)
        gated = (y * (g * jax.nn.sigmoid(g))).astype(BF16)
        acc = jnp.dot(gated, w_scr[...], preferred_element_type=F32)
        ms = jnp.mean(acc * acc, axis=-1, keepdims=True)
        o_ref[rows, :] = x_ref[rows, :] + acc * lax.rsqrt(ms + NORM_EPS) * pw_ref[...]


def _outproj(ya, ybcd, proj, x2d, w_out, layer, post_w):
    m = x2d.shape[0]
    return pl.pallas_call(
        functools.partial(_outproj_kernel, layer=layer),
        grid=(m // OUT_TM,),
        in_specs=[
            pl.BlockSpec((OUT_TM, D_BRANCH), lambda i: (i, 0)),
            pl.BlockSpec((OUT_TM, 3 * D_BRANCH), lambda i: (i, 0)),
            pl.BlockSpec((OUT_TM, D_MODEL), lambda i: (i, P_GATE // D_MODEL)),
            pl.BlockSpec((OUT_TM, D_MODEL), lambda i: (i, 0)),
            pl.BlockSpec(memory_space=pl.ANY),
            pl.BlockSpec((1, D_MODEL), lambda i: (0, 0)),
        ],
        out_specs=pl.BlockSpec((OUT_TM, D_MODEL), lambda i: (i, 0)),
        out_shape=jax.ShapeDtypeStruct((m, D_MODEL), F32),
        scratch_shapes=[
            pltpu.VMEM((D_MODEL, D_MODEL), BF16),
            pltpu.VMEM((2, OUT_STAGE, D_MODEL), F32),
            pltpu.SemaphoreType.DMA((2,)),
        ],
        compiler_params=pltpu.CompilerParams(
            dimension_semantics=("arbitrary",), vmem_limit_bytes=_VMEM_LIMIT),
        name="outproj",
    )(ya, ybcd, proj, x2d, w_out, post_w)


def kernel(x, pre_norm_w, post_norm_w, w_in, shift_mu, rwkv_w0, rwkv_w_up, rwkv_a0, rwkv_a_up,
           rwkv_k_k, rwkv_k_a, rwkv_r_k, rwkv_ln_w, rwkv_ln_b, attn_sinks, pool_w, pool_scale,
           sgu_norm_w, sgu_w, sgu_b, w_out):
    batch, seq, _ = x.shape
    assert x.shape == (batch, seq, D_MODEL) and seq % (MIX_SUB * BLK) == 0 and seq % RW_STEP == 0
    m = batch * seq
    head_id = np.arange(D_BRANCH // 2) // HEAD
    bd = jnp.asarray((head_id[:, None] == head_id[None, :]).astype(np.float32), dtype=BF16)
    t_id = np.arange(RW_TILE)
    tril = jnp.asarray(((t_id[:, None] >= t_id[None, :])
                        & (t_id[:, None] // RW_CHUNK == t_id[None, :] // RW_CHUNK)).astype(np.float32),
                       dtype=BF16)
    row_vec = lambda a: a.reshape(1, -1)

    h = x.reshape(m, D_MODEL)
    for l in range(DEPTH):
        mu = shift_mu[l]
        mu_p = jnp.concatenate([mu[:3 * D_BRANCH], jnp.zeros((P_LORA - P_KV,), mu.dtype),
                                mu[3 * D_BRANCH:], jnp.zeros((P_WIDTH - D_IN,), mu.dtype)])
        proj = _inproj(h, row_vec(pre_norm_w[l]), row_vec(mu_p), w_in, l, seq)

        vecs = [row_vec(rwkv_w0[l]), row_vec(rwkv_a0[l]), row_vec(rwkv_k_k[l]),
                row_vec(rwkv_k_a[l]), row_vec(rwkv_r_k[l]), row_vec(rwkv_ln_w[l]),
                row_vec(rwkv_ln_b[l])]
        ya = _rwkv(proj, vecs, rwkv_w_up[l], rwkv_a_up[l], bd, tril, batch, seq)

        sgu_bias = jnp.broadcast_to(sgu_b[l].T[:, :, None], (BLK, 4, BLK)).reshape(BLK, D_BRANCH)
        ybcd = _mixers(proj, attn_sinks[l], pool_w[l], row_vec(pool_scale[l]),
                       row_vec(sgu_norm_w[l]), sgu_w[l], sgu_bias, batch, seq)

        h = _outproj(ya, ybcd, proj, h, w_out, l, row_vec(post_norm_w[l]))
    return h.reshape(batch, seq, D_MODEL)
```

```python
import functools
import itertools

import jax
import jax.numpy as jnp
import numpy as np
from jax import lax
from jax.experimental import pallas as pl
from jax.experimental.pallas import tpu as pltpu

F32 = jnp.float32
BF16 = jnp.bfloat16
ACT = BF16

D_MODEL = 2048
DEPTH = 2
D_BRANCH = 512
HEAD = 64
N_HEADS = D_BRANCH // HEAD
LORA = 64
KV_HEADS = 2
ATT_GROUP = N_HEADS // KV_HEADS
BLK = 128
MIX_SUB = 4
POOL_WINDOWS = (2, 4, 8, 16)
NEG_INF = -1e30
NORM_EPS = 1e-6
LN_EPS = 1e-5
GN_EPS = 64e-5

A_COLS = 3 * D_BRANCH + 2 * LORA
B_COLS = D_BRANCH + 2 * KV_HEADS * HEAD
OFF_B = A_COLS
OFF_C = OFF_B + B_COLS
OFF_D = OFF_C + D_BRANCH
OFF_G = OFF_D + 2 * D_BRANCH
D_IN = OFF_G + D_MODEL

P_GATE = 0
P_SGU = 2048
P_POOL = 3072
P_Q = 3584
P_R = 4096
P_K = 4608
P_V = 5120
P_KV = 5632
P_LORA = 5888
P_WIDTH = 6144

RW_CHUNK = 64
RW_TILE = 256
RW_STEP = 2 * RW_TILE

_VMEM_LIMIT = 56 * 1024 * 1024


IN_TM = 512
IN_TN = 1024
IN_STAGE = 256
_W_RUNS = ((OFF_G, D_MODEL, P_GATE), (OFF_D, 2 * D_BRANCH, P_SGU), (OFF_C, D_BRANCH, P_POOL),
           (OFF_B, D_BRANCH, P_Q), (0, 3 * D_BRANCH, P_R),
           (OFF_B + D_BRANCH, 2 * KV_HEADS * HEAD, P_KV), (3 * D_BRANCH, 2 * LORA, P_LORA))
_W_PIECES = tuple((src + o, min(IN_STAGE, width - o), dst + o)
                  for src, width, dst in _W_RUNS for o in range(0, width, IN_STAGE))
assert sum(p[1] for p in _W_PIECES) == D_IN and P_LORA + 2 * LORA == D_IN


IN_SLOTS = 2


def _load_weight(w_hbm, layer, w_scr, stage, sem):
    def copy(idx):
        src, width, _ = _W_PIECES[idx]
        slot = idx % IN_SLOTS
        return pltpu.make_async_copy(w_hbm.at[layer, :, pl.ds(src, width)],
                                     stage.at[slot, :, pl.ds(0, width)], sem.at[slot])

    for idx in range(IN_SLOTS - 1):
        copy(idx).start()
    for idx, (_, width, dst) in enumerate(_W_PIECES):
        if idx + IN_SLOTS - 1 < len(_W_PIECES):
            copy(idx + IN_SLOTS - 1).start()
        copy(idx).wait()
        w_scr[:, dst:dst + width] = stage[idx % IN_SLOTS, :, 0:width].astype(BF16)
    w_scr[:, D_IN:] = jnp.zeros((D_MODEL, P_WIDTH - D_IN), BF16)


def _inproj_kernel(x_ref, pw_ref, mu_ref, w_hbm, o_ref, w_scr, stage, sem, h_scr, carry_scr, *,
                   layer, tiles_per_seq):
    i = pl.program_id(0)

    @pl.when(i == 0)
    def _():
        carry_scr[...] = jnp.zeros_like(carry_scr)
        _load_weight(w_hbm, layer, w_scr, stage, sem)

    for r0 in range(0, IN_TM, 256):
        x = x_ref[r0:r0 + 256, :]
        ms = jnp.mean(x * x, axis=-1, keepdims=True)
        h_scr[r0:r0 + 256, :] = (x * lax.rsqrt(ms + NORM_EPS) * pw_ref[...]).astype(BF16)
    row = lax.broadcasted_iota(jnp.int32, (IN_TM, 1), 0)
    starts_sequence = (i % tiles_per_seq) == 0
    for c0 in range(0, P_WIDTH, IN_TN):
        acc = jnp.dot(h_scr[...], w_scr[:, c0:c0 + IN_TN], preferred_element_type=F32)
        if c0 >= P_R:
            cs = slice(c0 - P_R, c0 - P_R + IN_TN)
            before = jnp.where(starts_sequence, 0.0, carry_scr[0:1, cs])
            prev = jnp.where(row == 0, before, pltpu.roll(acc, 1, axis=0))
            carry_scr[0:1, cs] = acc[IN_TM - 1:IN_TM, :]
            acc = acc + mu_ref[:, cs] * (prev - acc)
        o_ref[:, c0:c0 + IN_TN] = acc.astype(o_ref.dtype)


def _inproj(x2d, pre_w, mu_p, w_in, layer, seq):
    m = x2d.shape[0]
    assert P_R % IN_TN == 0 and seq % IN_TM == 0 and mu_p.shape == (1, P_WIDTH - P_R)
    return pl.pallas_call(
        functools.partial(_inproj_kernel, layer=layer, tiles_per_seq=seq // IN_TM),
        grid=(m // IN_TM,),
        in_specs=[
            pl.BlockSpec((IN_TM, D_MODEL), lambda i: (i, 0)),
            pl.BlockSpec((1, D_MODEL), lambda i: (0, 0)),
            pl.BlockSpec((1, P_WIDTH - P_R), lambda i: (0, 0)),
            pl.BlockSpec(memory_space=pl.ANY),
        ],
        out_specs=pl.BlockSpec((IN_TM, P_WIDTH), lambda i: (i, 0)),
        out_shape=jax.ShapeDtypeStruct((m, P_WIDTH), ACT),
        scratch_shapes=[
            pltpu.VMEM((D_MODEL, P_WIDTH), BF16),
            pltpu.VMEM((IN_SLOTS, D_MODEL, IN_STAGE), F32),
            pltpu.SemaphoreType.DMA((IN_SLOTS,)),
            pltpu.VMEM((IN_TM, D_MODEL), BF16),
            pltpu.VMEM((8, P_WIDTH - P_R), F32),
        ],
        compiler_params=pltpu.CompilerParams(
            dimension_semantics=("arbitrary",), vmem_limit_bytes=_VMEM_LIMIT),
        name="inproj",
    )(x2d, pre_w, mu_p, w_in)


def _bdot(a, b):
    return jnp.dot(a.astype(BF16), b.astype(BF16), preferred_element_type=F32)


def _bdot_nt(a, b):
    return lax.dot_general(a.astype(BF16), b.astype(BF16), (((1,), (1,)), ((), ())),
                           preferred_element_type=F32)


def _split3(x):
    x1 = x.astype(BF16)
    r1 = x - x1.astype(F32)
    x2 = r1.astype(BF16)
    x3 = (r1 - x2.astype(F32)).astype(BF16)
    return x1, x2, x3


def _head_sums(xs, bd):
    rows = xs[0].shape[0]
    half = bd.shape[0]
    x = jnp.concatenate(xs, axis=0).astype(BF16)
    out = jnp.concatenate(
        [jnp.dot(x[:, :half], bd, preferred_element_type=F32),
         jnp.dot(x[:, half:], bd, preferred_element_type=F32)], axis=1)
    return [out[i * rows:(i + 1) * rows] for i in range(len(xs))]


def _mm(a16, b16):
    return jnp.dot(a16, b16, preferred_element_type=F32)


def _mm_nt(a16, b16):
    return lax.dot_general(a16, b16, (((1,), (1,)), ((), ())), preferred_element_type=F32)


def _block_diag(m16, bdm16):
    return jnp.concatenate([m16, m16], axis=0) * bdm16


def _unit_lower_inverse_many(lmats, eye, masks, bdm16):
    base_mask, base_bd16, level_bd16 = masks
    c = lmats[0].shape[0]
    l16s = [l.astype(BF16) for l in lmats]
    stack2 = lambda m16: jnp.concatenate([m16, m16], axis=0)
    lds = [jnp.where(base_mask, l, 0.0) for l in lmats]
    l2s = [_mm(ld.astype(BF16), stack2(l16) * base_bd16).astype(BF16)
           for ld, l16 in zip(lds, l16s)]
    yield
    xs = [eye + ld for ld in lds]
    both = [_mm(jnp.concatenate([x.astype(BF16), l2], axis=0), stack2(l2) * bdm16)
            for x, l2 in zip(xs, l2s)]
    xs = [x + b[:c] for x, b in zip(xs, both)]
    l4s = [b[c:].astype(BF16) for b in both]
    yield
    xs = [x + _mm(x.astype(BF16), stack2(l4) * bdm16) for x, l4 in zip(xs, l4s)]
    yield
    for lvl16 in level_bd16:
        ts = [_mm(x.astype(BF16), stack2(l16) * lvl16).astype(BF16) for x, l16 in zip(xs, l16s)]
        yield
        xs = [x + _mm(t, stack2(x.astype(BF16)) * bdm16) for x, t in zip(xs, ts)]
        yield
    return xs


_SET_BF16 = ("a", "b", "k", "r", "v", "b_e", "k_e")
_SET_NAMES = _SET_BF16 + ("g_all", "bonus")
_PREP_MXU_DELAY = 12
_EXP_NEG_HALF = float(np.exp(-0.5))


def _rwkv_prep(z_refs, row0, w, dst):
    tile, c = RW_TILE, RW_CHUNK
    rows = lambda z_ref: z_ref[row0:row0 + tile, :].astype(F32)

    lo = rows(z_refs[3])
    lora_w = _bdot(jnp.tanh(lo[:, :LORA]), w["wup"][...])
    lora_a = _bdot(lo[:, LORA:], w["aup"][...])
    yield
    r, k, v = rows(z_refs[0]), rows(z_refs[1]), rows(z_refs[2])
    logw = -_EXP_NEG_HALF * jax.nn.sigmoid(w["w0"][...] + lora_w)
    asig = jax.nn.sigmoid(w["a0"][...] + lora_a)
    logw_terms = _split3(logw)
    kk = k * w["kkw"][...]
    k2 = k * (1.0 + (asig - 1.0) * w["kaw"][...])
    sum_terms = [kk * kk, r * k2 * w["rkw"][...]]
    for _ in range(_PREP_MXU_DELAY):
        yield
    tril = w["tril"][...]
    cum = sum(jnp.dot(tril, p, preferred_element_type=F32) for p in logw_terms)
    kk_ss, rk_sum = _head_sums(sum_terms, w["bd"][...])
    yield
    kk = kk * lax.rsqrt(jnp.maximum(kk_ss, 1e-24))
    dst["bonus"][...] = rk_sum * v
    for ck in range(tile // c):
        sl = slice(ck * c, (ck + 1) * c)
        lw, cm = logw[sl], cum[sl]
        total = cm[c - 1:c, :]
        g_inv = jnp.exp(-cm)
        g_all = jnp.exp(total)
        g_tail = g_all * g_inv
        b_c = kk[sl] * asig[sl]
        dst["a"][sl, :] = (-kk[sl] * jnp.exp(cm - lw)).astype(BF16)
        dst["b"][sl, :] = (b_c * g_inv).astype(BF16)
        dst["k"][sl, :] = (k2[sl] * g_inv).astype(BF16)
        dst["r"][sl, :] = (r[sl] * jnp.exp(cm)).astype(BF16)
        dst["v"][sl, :] = v[sl].astype(BF16)
        dst["b_e"][sl, :] = (b_c * g_tail).astype(BF16)
        dst["k_e"][sl, :] = (k2[sl] * g_tail).astype(BF16)
        dst["g_all"][ck:ck + 1, :] = g_all
    yield


def _rwkv_main(src, w, consts, s_scr, y_scr, o_ref, row0):
    tile, c, pw = RW_TILE, RW_CHUNK, 2 * HEAD
    eye, masks, tri2, bdm4_16, bdm16, bdm = consts
    n_chunks, n_pairs = tile // c, N_HEADS // 2
    items = [(ck, p) for ck in range(n_chunks) for p in range(n_pairs)]

    def part(name, it):
        ck, p = it
        return src[name][ck * c:(ck + 1) * c, p * pw:(p + 1) * pw]

    a2 = [part("a", it) for it in items]
    r2 = [part("r", it) for it in items]
    l_ab, m_rb16, ak16 = [], [], []
    for a, x, it in zip(a2, r2, items):
        b, k_ = part("b", it), part("k", it)
        pr = _mm_nt(jnp.concatenate([a, x], axis=0),
                    jnp.concatenate([b, b, k_, k_], axis=0) * bdm4_16)
        ab = jnp.where(tri2, pr[:, :pw], 0.0)
        l_ab.append(ab[:c])
        m_rb16.append(ab[c:].astype(BF16))
        ak16.append(jnp.where(tri2, pr[:, pw:], 0.0).astype(BF16))
    yield
    qy = [_mm(x, _block_diag(part("v", it), bdm16))
          for x, it in zip(ak16, items)]
    qv16 = [q[:c].astype(BF16) for q in qy]
    yv = [q[c:] for q in qy]
    vk = [jnp.where(bdm, lax.dot_general(part("v", it), part("k_e", it), (((0,), (0,)), ((), ())),
                                         preferred_element_type=F32), 0.0) for it in items]
    yield
    tinv = yield from _unit_lower_inverse_many(l_ab, eye, masks, bdm16)
    wu = [_mm(t.astype(BF16),
              jnp.concatenate([_block_diag(a, bdm16), _block_diag(q, bdm16)], axis=1))
          for t, a, q in zip(tinv, a2, qv16)]
    wa16 = [x[:, :pw].astype(BF16) for x in wu]
    uv = [x[:, pw:] for x in wu]
    uvt = [x.T for x in uv]
    war = [jnp.concatenate([x, y], axis=0) for x, y in zip(wa16, r2)]
    yield

    state = [s_scr[p] for p in range(n_pairs)]
    for ck in range(n_chunks):
        idx = [ck * n_pairs + p for p in range(n_pairs)]
        s16 = [s.astype(BF16) for s in state]
        uy = [_mm_nt(war[i], s16[p]) for p, i in enumerate(idx)]
        ut = [_mm_nt(s16[p], wa16[i]) + uvt[i] for p, i in enumerate(idx)]
        for p, i in enumerate(idx):
            u16 = (uy[p][:c] + uv[i]).astype(BF16)
            y = uy[p][c:] + yv[i] + _mm(m_rb16[i], _block_diag(u16, bdm16))
            y_scr[ck * c:(ck + 1) * c, p * pw:(p + 1) * pw] = y
        state = [state[p] * src["g_all"][ck:ck + 1, p * pw:(p + 1) * pw] + vk[i]
                 + jnp.where(bdm, _mm(ut[p].astype(BF16), part("b_e", items[i])), 0.0)
                 for p, i in enumerate(idx)]
        yield
    for p in range(n_pairs):
        s_scr[p] = state[p]

    y = y_scr[...]
    bd = w["bd"][...]
    mu = _head_sums([y], bd)[0] * (1.0 / HEAD)
    d = y - mu
    var = _head_sums([d * d], bd)[0] * (1.0 / HEAD)
    out = d * lax.rsqrt(var + GN_EPS) * w["lnw"][...] + w["lnb"][...] + src["bonus"][...]
    o_ref[row0:row0 + tile, :] = out.astype(o_ref.dtype)
    yield


def _interleave(*streams):
    for _ in itertools.zip_longest(*streams):
        pass


_W_NAMES = ("w0", "wup", "a0", "aup", "kkw", "kaw", "rkw", "lnw", "lnb", "bd", "tril")


def _rwkv_kernel(*refs):
    cur, nxt = refs[0:4], refs[4:8]
    w = dict(zip(_W_NAMES, refs[8:8 + len(_W_NAMES)]))
    rest = refs[8 + len(_W_NAMES):]
    o_ref, s_scr, y_scr = rest[0], rest[1], rest[2]
    n_set = len(_SET_NAMES)
    set0 = dict(zip(_SET_NAMES, rest[3:3 + n_set]))
    set1 = dict(zip(_SET_NAMES, rest[3 + n_set:3 + 2 * n_set]))
    n = pl.program_id(1)

    @pl.when(n == 0)
    def _():
        s_scr[...] = jnp.zeros_like(s_scr)
        _interleave(_rwkv_prep(cur, 0, w, set0))

    c, pw = RW_CHUNK, 2 * HEAD
    ci = lax.broadcasted_iota(jnp.int32, (c, pw), 0)
    cj = lax.broadcasted_iota(jnp.int32, (c, pw), 1) % HEAD
    eye = (ci == cj).astype(F32)
    si = lax.broadcasted_iota(jnp.int32, (2 * c, pw), 0)
    sj = lax.broadcasted_iota(jnp.int32, (2 * c, pw), 1) % HEAD
    tri2 = ((si < c) & (si > sj)) | ((si >= c) & ((si - c) >= sj))
    bi = lax.broadcasted_iota(jnp.int32, (2 * pw, pw), 0)
    bj = lax.broadcasted_iota(jnp.int32, (2 * pw, pw), 1)
    bdm_bool = ((bi // HEAD) % 2) == (bj // HEAD)
    bdm4_16 = bdm_bool.astype(BF16)
    hi = lax.broadcasted_iota(jnp.int32, (pw, pw), 0)
    hj = lax.broadcasted_iota(jnp.int32, (pw, pw), 1)
    di, dj = hi % HEAD, hj % HEAD
    same_head = (hi // HEAD) == (hj // HEAD)
    near = lambda b: same_head & ((di // b) == (dj // b))
    ring = lambda b: same_head & ((di // (2 * b)) == (dj // (2 * b))) & ((di // b) != (dj // b))
    masks = ((ci // 8) == (cj // 8), near(8).astype(BF16),
             tuple(ring(b).astype(BF16) for b in (8, 16, 32)))
    consts = (eye, masks, tri2, bdm4_16, bdm4_16[:pw], bdm_bool[:pw])

    _interleave(_rwkv_prep(cur, RW_TILE, w, set1),
                _rwkv_main(set0, w, consts, s_scr, y_scr, o_ref, 0))
    _interleave(_rwkv_prep(nxt, 0, w, set0),
                _rwkv_main(set1, w, consts, s_scr, y_scr, o_ref, RW_TILE))


def _rwkv(proj, vecs, wup, aup, bd, tril, batch, seq):
    m = proj.shape[0]
    ns = seq // RW_STEP
    row = lambda b, n: b * ns + n
    nxt = lambda b, n: b * ns + jnp.minimum(n + 1, ns - 1)
    vec = lambda width: pl.BlockSpec((1, width), lambda b, n: (0, 0))
    full = lambda a: pl.BlockSpec(a.shape, lambda b, n: (0, 0))

    def token_specs(row_fn):
        wide = lambda cb: pl.BlockSpec((RW_STEP, D_BRANCH), lambda b, n: (row_fn(b, n), cb))
        return [wide(P_R // D_BRANCH), wide(P_K // D_BRANCH), wide(P_V // D_BRANCH),
                pl.BlockSpec((RW_STEP, 2 * LORA), lambda b, n: (row_fn(b, n), P_LORA // (2 * LORA)))]

    in_specs = token_specs(row) + token_specs(nxt) + [
        vec(D_BRANCH), full(wup), vec(D_BRANCH), full(aup),
        vec(D_BRANCH), vec(D_BRANCH), vec(D_BRANCH), vec(D_BRANCH), vec(D_BRANCH),
        full(bd), full(tril),
    ]
    prep_set = ([pltpu.VMEM((RW_TILE, D_BRANCH), BF16) for _ in _SET_BF16]
                + [pltpu.VMEM((8, D_BRANCH), F32), pltpu.VMEM((RW_TILE, D_BRANCH), F32)])
    return pl.pallas_call(
        _rwkv_kernel,
        grid=(batch, ns),
        in_specs=in_specs,
        out_specs=pl.BlockSpec((RW_STEP, D_BRANCH), lambda b, n: (row(b, n), 0)),
        out_shape=jax.ShapeDtypeStruct((m, D_BRANCH), ACT),
        scratch_shapes=[
            pltpu.VMEM((N_HEADS // 2, 2 * HEAD, 2 * HEAD), F32),
            pltpu.VMEM((RW_TILE, D_BRANCH), F32),
        ] + prep_set + prep_set,
        compiler_params=pltpu.CompilerParams(
            dimension_semantics=("parallel", "arbitrary"), vmem_limit_bytes=_VMEM_LIMIT),
        name="rwkv",
    )(*([proj] * 8), vecs[0], wup, vecs[1], aup, *vecs[2:], bd, tril)


_SLOPES = tuple(2.0 ** (-8.0 * (h + 1) / N_HEADS) for h in range(N_HEADS))
_SQRT_HALF = float(np.sqrt(0.5))


def _attention_bias():
    t = np.arange(BLK)[:, None]
    s = np.arange(2 * BLK)[None, :]
    dist = t + BLK - s
    out = np.empty((2, N_HEADS * BLK, 2 * BLK), np.float32)
    for first in (0, 1):
        valid = (dist >= 0) & (dist < BLK) & ((s >= BLK) | (first == 1))
        for h in range(N_HEADS):
            out[first, h * BLK:(h + 1) * BLK] = np.where(valid, -_SLOPES[h] * dist, NEG_INF)
    return out


def _pool_bands():
    t = np.arange(BLK)[:, None] + BLK
    s = np.arange(2 * BLK)[None, :]
    return np.stack([((s <= t) & (s > t - w)).astype(np.float32) for w in POOL_WINDOWS])


def _mixers_kernel(sinks_ref, bias_ref, band_ref, q_ref, kvc_ref, kvp_ref, zc_ref, zcp_ref, zd_ref,
                   poolw_ref, pscale_ref, nw_ref, sw_ref, sb_ref, o_ref, y_scr):
    n = pl.program_id(1)
    rows = ATT_GROUP * BLK
    head_in_group = lax.broadcasted_iota(jnp.int32, (rows, 1), 0) // BLK

    def per_row(values):
        col = jnp.full((rows, 1), values[-1], F32)
        for j in range(ATT_GROUP - 2, -1, -1):
            col = jnp.where(head_in_group == j, values[j], col)
        return col

    sinks = [per_row([sinks_ref[h] for h in range(g * ATT_GROUP, (g + 1) * ATT_GROUP)])
             for g in range(KV_HEADS)]
    ri = lax.broadcasted_iota(jnp.int32, (BLK, BLK), 0)
    rj = lax.broadcasted_iota(jnp.int32, (BLK, BLK), 1)
    sgu_w16 = [jnp.where(ri >= rj, sw_ref[g], 0.0).astype(BF16) for g in range(4)]
    refs = (bias_ref, band_ref, q_ref, kvc_ref, kvp_ref, zc_ref, zcp_ref, zd_ref, poolw_ref,
            pscale_ref, nw_ref, sb_ref, y_scr)
    for sub in range(MIX_SUB):
        _mixers_block(sub, n, refs, sinks, sgu_w16)
    o_ref[...] = y_scr[...].astype(o_ref.dtype)


def _mixers_block(sub, n, refs, sinks, sgu_w16):
    (bias_ref, band_ref, q_ref, kvc_ref, kvp_ref, zc_ref, zcp_ref, zd_ref, poolw_ref,
     pscale_ref, nw_ref, sb_ref, y_scr) = refs
    rs = slice(sub * BLK, (sub + 1) * BLK)
    before = slice((sub - 1) * BLK, sub * BLK)
    rows = ATT_GROUP * BLK
    if sub == 0:
        kv_prev = kvp_ref[...]
        z_prev = jnp.where(n > 0, zcp_ref[...], 0.0)
        bias_of = lambda g: bias_ref[jnp.minimum(n, 1), g * rows:(g + 1) * rows, :]
    else:
        kv_prev = kvc_ref[before, :]
        z_prev = zc_ref[before, :]
        bias_of = lambda g: bias_ref[1, g * rows:(g + 1) * rows, :]

    q = q_ref[rs, :] * (HEAD ** -0.5)
    kv = jnp.concatenate([kv_prev, kvc_ref[rs, :]], axis=0)

    scores = []
    for g in range(KV_HEADS):
        qg = jnp.concatenate(
            [q[:, (g * ATT_GROUP + j) * HEAD:(g * ATT_GROUP + j + 1) * HEAD]
             for j in range(ATT_GROUP)], axis=0)
        scores.append(_bdot_nt(qg, kv[:, g * HEAD:(g + 1) * HEAD]))

    zfull = jnp.concatenate([z_prev, zc_ref[rs, :]], axis=0)
    pos = (n * MIX_SUB + sub) * BLK + lax.broadcasted_iota(jnp.int32, (BLK, 1), 0) + 1
    sums = [jnp.dot(band_ref[g], zfull[:, g * BLK:(g + 1) * BLK], preferred_element_type=F32)
            for g in range(len(POOL_WINDOWS))]

    zd = zd_ref[rs, :].astype(F32)
    gz = 0.5 * zd * (1.0 + lax.erf(zd * _SQRT_HALF))
    u = gz[:, :D_BRANCH]
    vv = gz[:, D_BRANCH:]
    mu = jnp.mean(vv, axis=-1, keepdims=True)
    dv = vv - mu
    var = jnp.mean(dv * dv, axis=-1, keepdims=True)
    vn = dv * lax.rsqrt(var + LN_EPS) * nw_ref[...]
    for g in range(4):
        gs = slice(g * BLK, (g + 1) * BLK)
        sg = _bdot(sgu_w16[g], vn[:, gs]) + sb_ref[:, gs]
        y_scr[rs, 2 * D_BRANCH + g * BLK:2 * D_BRANCH + (g + 1) * BLK] = u[:, gs] * sg

    for g, w in enumerate(POOL_WINDOWS):
        gs = slice(g * BLK, (g + 1) * BLK)
        cnt = jnp.minimum(pos, w).astype(F32)
        pooled = sums[g] / cnt - zc_ref[rs, gs].astype(F32)
        yg = _bdot(pooled, poolw_ref[g]) * pscale_ref[:, gs]
        y_scr[rs, D_BRANCH + g * BLK:D_BRANCH + (g + 1) * BLK] = yg

    probs, dens = [], []
    for g in range(KV_HEADS):
        s = scores[g] + bias_of(g)
        mx = jnp.maximum(jnp.max(s, axis=-1, keepdims=True), sinks[g])
        p = jnp.exp(s - mx)
        probs.append(p)
        dens.append(jnp.sum(p, axis=-1, keepdims=True) + jnp.exp(sinks[g] - mx))
    for g in range(KV_HEADS):
        vg = kv[:, (KV_HEADS + g) * HEAD:(KV_HEADS + g + 1) * HEAD]
        og = _bdot(probs[g], vg) / dens[g]
        for j in range(ATT_GROUP):
            h = g * ATT_GROUP + j
            y_scr[rs, h * HEAD:(h + 1) * HEAD] = og[j * BLK:(j + 1) * BLK, :]


def _mixers(proj, sinks, pool_w, pool_scale, norm_w, sgu_w, sgu_bias, batch, seq):
    m = proj.shape[0]
    tm = MIX_SUB * BLK
    ns = seq // tm
    row = lambda b, n: b * ns + n
    before = lambda b, n: jnp.maximum(row(b, n) * MIX_SUB - 1, 0)
    bias = jnp.asarray(_attention_bias())
    assert proj.dtype == BF16
    band = jnp.asarray(_pool_bands(), dtype=BF16)
    in_specs = [
        pl.BlockSpec(memory_space=pltpu.SMEM),
        pl.BlockSpec(bias.shape, lambda b, n: (0, 0, 0)),
        pl.BlockSpec(band.shape, lambda b, n: (0, 0, 0)),
        pl.BlockSpec((tm, D_BRANCH), lambda b, n: (row(b, n), P_Q // D_BRANCH)),
        pl.BlockSpec((tm, 256), lambda b, n: (row(b, n), P_KV // 256)),
        pl.BlockSpec((BLK, 256), lambda b, n: (before(b, n), P_KV // 256)),
        pl.BlockSpec((tm, D_BRANCH), lambda b, n: (row(b, n), P_POOL // D_BRANCH)),
        pl.BlockSpec((BLK, D_BRANCH), lambda b, n: (before(b, n), P_POOL // D_BRANCH)),
        pl.BlockSpec((tm, 2 * D_BRANCH), lambda b, n: (row(b, n), P_SGU // (2 * D_BRANCH))),
        pl.BlockSpec((4, BLK, BLK), lambda b, n: (0, 0, 0)),
        pl.BlockSpec((1, D_BRANCH), lambda b, n: (0, 0)),
        pl.BlockSpec((1, D_BRANCH), lambda b, n: (0, 0)),
        pl.BlockSpec((4, BLK, BLK), lambda b, n: (0, 0, 0)),
        pl.BlockSpec((BLK, D_BRANCH), lambda b, n: (0, 0)),
    ]
    return pl.pallas_call(
        _mixers_kernel,
        grid=(batch, ns),
        in_specs=in_specs,
        out_specs=pl.BlockSpec((tm, 3 * D_BRANCH), lambda b, n: (row(b, n), 0)),
        out_shape=jax.ShapeDtypeStruct((m, 3 * D_BRANCH), ACT),
        scratch_shapes=[pltpu.VMEM((tm, 3 * D_BRANCH), F32)],
        compiler_params=pltpu.CompilerParams(
            dimension_semantics=("parallel", "parallel"), vmem_limit_bytes=_VMEM_LIMIT),
        name="mixers",
    )(sinks, bias, band, proj, proj, proj, proj, proj, proj, pool_w, pool_scale, norm_w, sgu_w,
      sgu_bias)


OUT_TM = 512
OUT_SUB = 256
OUT_STAGE = 512


def _outproj_kernel(ya_ref, yb_ref, g_ref, x_ref, w_hbm, pw_ref, o_ref, w_scr, stage, sem, *,
                    layer):
    @pl.when(pl.program_id(0) == 0)
    def _():
        def copy(idx):
            slot = idx % 2
            return pltpu.make_async_copy(w_hbm.at[layer, pl.ds(idx * OUT_STAGE, OUT_STAGE), :],
                                         stage.at[slot], sem.at[slot])

        n_pieces = D_MODEL // OUT_STAGE
        copy(0).start()
        for idx in range(n_pieces):
            if idx + 1 < n_pieces:
                copy(idx + 1).start()
            copy(idx).wait()
            w_scr[idx * OUT_STAGE:(idx + 1) * OUT_STAGE, :] = stage[idx % 2].astype(BF16)

    for r0 in range(0, OUT_TM, OUT_SUB):
        rows = slice(r0, r0 + OUT_SUB)
        g = g_ref[rows, :].astype(F32)
        y = jnp.concatenate([ya_ref[rows, :], yb_ref[rows, :]], axis=1).astype(F32)
        gated = (y * (g * jax.nn.sigmoid(g))).astype(BF16)
        acc = jnp.dot(gated, w_scr[...], preferred_element_type=F32)
        ms = jnp.mean(acc * acc, axis=-1, keepdims=True)
        o_ref[rows, :] = x_ref[rows, :] + acc * lax.rsqrt(ms + NORM_EPS) * pw_ref[...]


def _outproj(ya, ybcd, proj, x2d, w_out, layer, post_w):
    m = x2d.shape[0]
    return pl.pallas_call(
        functools.partial(_outproj_kernel, layer=layer),
        grid=(m // OUT_TM,),
        in_specs=[
            pl.BlockSpec((OUT_TM, D_BRANCH), lambda i: (i, 0)),
            pl.BlockSpec((OUT_TM, 3 * D_BRANCH), lambda i: (i, 0)),
            pl.BlockSpec((OUT_TM, D_MODEL), lambda i: (i, P_GATE // D_MODEL)),
            pl.BlockSpec((OUT_TM, D_MODEL), lambda i: (i, 0)),
            pl.BlockSpec(memory_space=pl.ANY),
            pl.BlockSpec((1, D_MODEL), lambda i: (0, 0)),
        ],
        out_specs=pl.BlockSpec((OUT_TM, D_MODEL), lambda i: (i, 0)),
        out_shape=jax.ShapeDtypeStruct((m, D_MODEL), F32),
        scratch_shapes=[
            pltpu.VMEM((D_MODEL, D_MODEL), BF16),
            pltpu.VMEM((2, OUT_STAGE, D_MODEL), F32),
            pltpu.SemaphoreType.DMA((2,)),
        ],
        compiler_params=pltpu.CompilerParams(
            dimension_semantics=("arbitrary",), vmem_limit_bytes=_VMEM_LIMIT),
        name="outproj",
    )(ya, ybcd, proj, x2d, w_out, post_w)


def kernel(x, pre_norm_w, post_norm_w, w_in, shift_mu, rwkv_w0, rwkv_w_up, rwkv_a0, rwkv_a_up,
           rwkv_k_k, rwkv_k_a, rwkv_r_k, rwkv_ln_w, rwkv_ln_b, attn_sinks, pool_w, pool_scale,
           sgu_norm_w, sgu_w, sgu_b, w_out):
    batch, seq, _ = x.shape
    assert x.shape == (batch, seq, D_MODEL) and seq % (MIX_SUB * BLK) == 0 and seq % RW_STEP == 0
    m = batch * seq
    head_id = np.arange(D_BRANCH // 2) // HEAD
    bd = jnp.asarray((head_id[:, None] == head_id[None, :]).astype(np.float32), dtype=BF16)
    t_id = np.arange(RW_TILE)
    tril = jnp.asarray(((t_id[:, None] >= t_id[None, :])
                        & (t_id[:, None] // RW_CHUNK == t_id[None, :] // RW_CHUNK)).astype(np.float32),
                       dtype=BF16)
    row_vec = lambda a: a.reshape(1, -1)

    h = x.reshape(m, D_MODEL)
    for l in range(DEPTH):
        mu = shift_mu[l]
        mu_p = jnp.concatenate([mu[:3 * D_BRANCH], jnp.zeros((P_LORA - P_KV,), mu.dtype),
                                mu[3 * D_BRANCH:], jnp.zeros((P_WIDTH - D_IN,), mu.dtype)])
        proj = _inproj(h, row_vec(pre_norm_w[l]), row_vec(mu_p), w_in, l, seq)

        vecs = [row_vec(rwkv_w0[l]), row_vec(rwkv_a0[l]), row_vec(rwkv_k_k[l]),
                row_vec(rwkv_k_a[l]), row_vec(rwkv_r_k[l]), row_vec(rwkv_ln_w[l]),
                row_vec(rwkv_ln_b[l])]
        ya = _rwkv(proj, vecs, rwkv_w_up[l], rwkv_a_up[l], bd, tril, batch, seq)

        sgu_bias = jnp.broadcast_to(sgu_b[l].T[:, :, None], (BLK, 4, BLK)).reshape(BLK, D_BRANCH)
        ybcd = _mixers(proj, attn_sinks[l], pool_w[l], row_vec(pool_scale[l]),
                       row_vec(sgu_norm_w[l]), sgu_w[l], sgu_bias, batch, seq)

        h = _outproj(ya, ybcd, proj, h, w_out, l, row_vec(post_norm_w[l]))
    return h.reshape(batch, seq, D_MODEL)
```

```python
import functools
import itertools

import jax
import jax.numpy as jnp
import numpy as np
from jax import lax
from jax.experimental import pallas as pl
from jax.experimental.pallas import tpu as pltpu

F32 = jnp.float32
BF16 = jnp.bfloat16
ACT = BF16

D_MODEL = 2048
DEPTH = 2
D_BRANCH = 512
HEAD = 64
N_HEADS = D_BRANCH // HEAD
LORA = 64
KV_HEADS = 2
ATT_GROUP = N_HEADS // KV_HEADS
BLK = 128
MIX_SUB = 4
POOL_WINDOWS = (2, 4, 8, 16)
NEG_INF = -1e30
NORM_EPS = 1e-6
LN_EPS = 1e-5
GN_EPS = 64e-5

A_COLS = 3 * D_BRANCH + 2 * LORA
B_COLS = D_BRANCH + 2 * KV_HEADS * HEAD
OFF_B = A_COLS
OFF_C = OFF_B + B_COLS
OFF_D = OFF_C + D_BRANCH
OFF_G = OFF_D + 2 * D_BRANCH
D_IN = OFF_G + D_MODEL

P_GATE = 0
P_SGU = 2048
P_POOL = 3072
P_Q = 3584
P_R = 4096
P_K = 4608
P_V = 5120
P_KV = 5632
P_LORA = 5888
P_WIDTH = 6144

RW_CHUNK = 64
RW_TILE = 256
RW_STEP = 2 * RW_TILE

_VMEM_LIMIT = 56 * 1024 * 1024


IN_TM = 512
IN_TN = 1024
IN_STAGE = 256
_W_RUNS = ((OFF_G, D_MODEL, P_GATE), (OFF_D, 2 * D_BRANCH, P_SGU), (OFF_C, D_BRANCH, P_POOL),
           (OFF_B, D_BRANCH, P_Q), (0, 3 * D_BRANCH, P_R),
           (OFF_B + D_BRANCH, 2 * KV_HEADS * HEAD, P_KV), (3 * D_BRANCH, 2 * LORA, P_LORA))
_W_PIECES = tuple((src + o, min(IN_STAGE, width - o), dst + o)
                  for src, width, dst in _W_RUNS for o in range(0, width, IN_STAGE))
assert sum(p[1] for p in _W_PIECES) == D_IN and P_LORA + 2 * LORA == D_IN


IN_SLOTS = 2


def _load_weight(w_hbm, layer, w_scr, stage, sem):
    def copy(idx):
        src, width, _ = _W_PIECES[idx]
        slot = idx % IN_SLOTS
        return pltpu.make_async_copy(w_hbm.at[layer, :, pl.ds(src, width)],
                                     stage.at[slot, :, pl.ds(0, width)], sem.at[slot])

    for idx in range(IN_SLOTS - 1):
        copy(idx).start()
    for idx, (_, width, dst) in enumerate(_W_PIECES):
        if idx + IN_SLOTS - 1 < len(_W_PIECES):
            copy(idx + IN_SLOTS - 1).start()
        copy(idx).wait()
        w_scr[:, dst:dst + width] = stage[idx % IN_SLOTS, :, 0:width].astype(BF16)
    w_scr[:, D_IN:] = jnp.zeros((D_MODEL, P_WIDTH - D_IN), BF16)


def _inproj_kernel(x_ref, pw_ref, mu_ref, w_hbm, o_ref, w_scr, stage, sem, h_scr, carry_scr, *,
                   layer, tiles_per_seq):
    i = pl.program_id(0)

    @pl.when(i == 0)
    def _():
        carry_scr[...] = jnp.zeros_like(carry_scr)
        _load_weight(w_hbm, layer, w_scr, stage, sem)

    for r0 in range(0, IN_TM, 256):
        x = x_ref[r0:r0 + 256, :]
        ms = jnp.mean(x * x, axis=-1, keepdims=True)
        h_scr[r0:r0 + 256, :] = (x * lax.rsqrt(ms + NORM_EPS) * pw_ref[...]).astype(BF16)
    row = lax.broadcasted_iota(jnp.int32, (IN_TM, 1), 0)
    starts_sequence = (i % tiles_per_seq) == 0
    for c0 in range(0, P_WIDTH, IN_TN):
        acc = jnp.dot(h_scr[...], w_scr[:, c0:c0 + IN_TN], preferred_element_type=F32)
        if c0 >= P_R:
            cs = slice(c0 - P_R, c0 - P_R + IN_TN)
            before = jnp.where(starts_sequence, 0.0, carry_scr[0:1, cs])
            prev = jnp.where(row == 0, before, pltpu.roll(acc, 1, axis=0))
            carry_scr[0:1, cs] = acc[IN_TM - 1:IN_TM, :]
            acc = acc + mu_ref[:, cs] * (prev - acc)
        o_ref[:, c0:c0 + IN_TN] = acc.astype(o_ref.dtype)


def _inproj(x2d, pre_w, mu_p, w_in, layer, seq):
    m = x2d.shape[0]
    assert P_R % IN_TN == 0 and seq % IN_TM == 0 and mu_p.shape == (1, P_WIDTH - P_R)
    return pl.pallas_call(
        functools.partial(_inproj_kernel, layer=layer, tiles_per_seq=seq // IN_TM),
        grid=(m // IN_TM,),
        in_specs=[
            pl.BlockSpec((IN_TM, D_MODEL), lambda i: (i, 0)),
            pl.BlockSpec((1, D_MODEL), lambda i: (0, 0)),
            pl.BlockSpec((1, P_WIDTH - P_R), lambda i: (0, 0)),
            pl.BlockSpec(memory_space=pl.ANY),
        ],
        out_specs=pl.BlockSpec((IN_TM, P_WIDTH), lambda i: (i, 0)),
        out_shape=jax.ShapeDtypeStruct((m, P_WIDTH), ACT),
        scratch_shapes=[
            pltpu.VMEM((D_MODEL, P_WIDTH), BF16),
            pltpu.VMEM((IN_SLOTS, D_MODEL, IN_STAGE), F32),
            pltpu.SemaphoreType.DMA((IN_SLOTS,)),
            pltpu.VMEM((IN_TM, D_MODEL), BF16),
            pltpu.VMEM((8, P_WIDTH - P_R), F32),
        ],
        compiler_params=pltpu.CompilerParams(
            dimension_semantics=("arbitrary",), vmem_limit_bytes=_VMEM_LIMIT),
        name="inproj",
    )(x2d, pre_w, mu_p, w_in)


def _bdot(a, b):
    return jnp.dot(a.astype(BF16), b.astype(BF16), preferred_element_type=F32)


def _bdot_nt(a, b):
    return lax.dot_general(a.astype(BF16), b.astype(BF16), (((1,), (1,)), ((), ())),
                           preferred_element_type=F32)


def _split3(x):
    x1 = x.astype(BF16)
    r1 = x - x1.astype(F32)
    x2 = r1.astype(BF16)
    x3 = (r1 - x2.astype(F32)).astype(BF16)
    return x1, x2, x3


def _head_sums(xs, bd):
    rows = xs[0].shape[0]
    half = bd.shape[0]
    x = jnp.concatenate(xs, axis=0).astype(BF16)
    out = jnp.concatenate(
        [jnp.dot(x[:, :half], bd, preferred_element_type=F32),
         jnp.dot(x[:, half:], bd, preferred_element_type=F32)], axis=1)
    return [out[i * rows:(i + 1) * rows] for i in range(len(xs))]


def _mm(a16, b16):
    return jnp.dot(a16, b16, preferred_element_type=F32)


def _mm_nt(a16, b16):
    return lax.dot_general(a16, b16, (((1,), (1,)), ((), ())), preferred_element_type=F32)


def _block_diag(m16, bdm16):
    return jnp.concatenate([m16, m16], axis=0) * bdm16


def _unit_lower_inverse_many(lmats, eye, masks, bdm16):
    base_mask, base_bd16, level_bd16 = masks
    c = lmats[0].shape[0]
    l16s = [l.astype(BF16) for l in lmats]
    stack2 = lambda m16: jnp.concatenate([m16, m16], axis=0)
    lds = [jnp.where(base_mask, l, 0.0) for l in lmats]
    l2s = [_mm(ld.astype(BF16), stack2(l16) * base_bd16).astype(BF16)
           for ld, l16 in zip(lds, l16s)]
    yield
    xs = [eye + ld for ld in lds]
    both = [_mm(jnp.concatenate([x.astype(BF16), l2], axis=0), stack2(l2) * bdm16)
            for x, l2 in zip(xs, l2s)]
    xs = [x + b[:c] for x, b in zip(xs, both)]
    l4s = [b[c:].astype(BF16) for b in both]
    yield
    xs = [x + _mm(x.astype(BF16), stack2(l4) * bdm16) for x, l4 in zip(xs, l4s)]
    yield
    for lvl16 in level_bd16:
        ts = [_mm(x.astype(BF16), stack2(l16) * lvl16).astype(BF16) for x, l16 in zip(xs, l16s)]
        yield
        xs = [x + _mm(t, stack2(x.astype(BF16)) * bdm16) for x, t in zip(xs, ts)]
        yield
    return xs


_SET_BF16 = ("a", "b", "k", "r", "v", "b_e", "k_e")
_SET_NAMES = _SET_BF16 + ("g_all", "bonus")
_PREP_DELAY_BESIDE_A = 8
_PREP_DELAY_BESIDE_B = 4
_EXP_NEG_HALF = float(np.exp(-0.5))


def _rwkv_prep(z_refs, row0, w, dst, mxu_delay=0):
    tile, c = RW_TILE, RW_CHUNK
    rows = lambda z_ref: z_ref[row0:row0 + tile, :].astype(F32)

    lo = rows(z_refs[3])
    lora_w = _bdot(jnp.tanh(lo[:, :LORA]), w["wup"][...])
    lora_a = _bdot(lo[:, LORA:], w["aup"][...])
    yield
    r, k, v = rows(z_refs[0]), rows(z_refs[1]), rows(z_refs[2])
    logw = -_EXP_NEG_HALF * jax.nn.sigmoid(w["w0"][...] + lora_w)
    asig = jax.nn.sigmoid(w["a0"][...] + lora_a)
    logw_terms = _split3(logw)
    kk = k * w["kkw"][...]
    k2 = k * (1.0 + (asig - 1.0) * w["kaw"][...])
    sum_terms = [kk * kk, r * k2 * w["rkw"][...]]
    for _ in range(mxu_delay):
        yield
    tril = w["tril"][...]
    cum = sum(jnp.dot(tril, p, preferred_element_type=F32) for p in logw_terms)
    kk_ss, rk_sum = _head_sums(sum_terms, w["bd"][...])
    yield
    kk = kk * lax.rsqrt(jnp.maximum(kk_ss, 1e-24))
    dst["bonus"][...] = rk_sum * v
    for ck in range(tile // c):
        sl = slice(ck * c, (ck + 1) * c)
        lw, cm = logw[sl], cum[sl]
        total = cm[c - 1:c, :]
        g_inv = jnp.exp(-cm)
        g_all = jnp.exp(total)
        g_tail = g_all * g_inv
        b_c = kk[sl] * asig[sl]
        dst["a"][sl, :] = (-kk[sl] * jnp.exp(cm - lw)).astype(BF16)
        dst["b"][sl, :] = (b_c * g_inv).astype(BF16)
        dst["k"][sl, :] = (k2[sl] * g_inv).astype(BF16)
        dst["r"][sl, :] = (r[sl] * jnp.exp(cm)).astype(BF16)
        dst["v"][sl, :] = v[sl].astype(BF16)
        dst["b_e"][sl, :] = (b_c * g_tail).astype(BF16)
        dst["k_e"][sl, :] = (k2[sl] * g_tail).astype(BF16)
        dst["g_all"][ck:ck + 1, :] = g_all
    yield


def _rwkv_items():
    return [(ck, p) for ck in range(RW_TILE // RW_CHUNK) for p in range(N_HEADS // 2)]


def _rwkv_part(src, name, it):
    ck, p = it
    c, pw = RW_CHUNK, 2 * HEAD
    return src[name][ck * c:(ck + 1) * c, p * pw:(p + 1) * pw]


def _rwkv_main_a(src, consts, ctx):
    c, pw = RW_CHUNK, 2 * HEAD
    eye, masks, tri2, bdm4_16, bdm16, bdm = consts
    items = _rwkv_items()
    part = functools.partial(_rwkv_part, src)

    a2 = [part("a", it) for it in items]
    r2 = [part("r", it) for it in items]
    l_ab, m_rb16, ak16 = [], [], []
    for a, x, it in zip(a2, r2, items):
        b, k_ = part("b", it), part("k", it)
        pr = _mm_nt(jnp.concatenate([a, x], axis=0),
                    jnp.concatenate([b, b, k_, k_], axis=0) * bdm4_16)
        ab = jnp.where(tri2, pr[:, :pw], 0.0)
        l_ab.append(ab[:c])
        m_rb16.append(ab[c:].astype(BF16))
        ak16.append(jnp.where(tri2, pr[:, pw:], 0.0).astype(BF16))
    yield
    qy = [_mm(x, _block_diag(part("v", it), bdm16))
          for x, it in zip(ak16, items)]
    qv16 = [q[:c].astype(BF16) for q in qy]
    yv = [q[c:] for q in qy]
    vk = [jnp.where(bdm, lax.dot_general(part("v", it), part("k_e", it), (((0,), (0,)), ((), ())),
                                         preferred_element_type=F32), 0.0) for it in items]
    yield
    tinv = yield from _unit_lower_inverse_many(l_ab, eye, masks, bdm16)
    wu = [_mm(t.astype(BF16),
              jnp.concatenate([_block_diag(a, bdm16), _block_diag(q, bdm16)], axis=1))
          for t, a, q in zip(tinv, a2, qv16)]
    wa16 = [x[:, :pw].astype(BF16) for x in wu]
    uv = [x[:, pw:] for x in wu]
    uvt = [x.T for x in uv]
    war = [jnp.concatenate([x, y], axis=0) for x, y in zip(wa16, r2)]
    ctx.update(wa16=wa16, uv=uv, uvt=uvt, war=war, m_rb16=m_rb16, yv=yv, vk=vk)
    yield


def _rwkv_main_b(src, w, consts, ctx, s_scr, y_scr, o_ref, row0):
    tile, c, pw = RW_TILE, RW_CHUNK, 2 * HEAD
    bdm16, bdm = consts[4], consts[5]
    n_chunks, n_pairs = tile // c, N_HEADS // 2
    items = _rwkv_items()
    part = functools.partial(_rwkv_part, src)
    wa16, uv, uvt, war = ctx["wa16"], ctx["uv"], ctx["uvt"], ctx["war"]
    m_rb16, yv, vk = ctx["m_rb16"], ctx["yv"], ctx["vk"]

    state = [s_scr[p] for p in range(n_pairs)]
    for ck in range(n_chunks):
        idx = [ck * n_pairs + p for p in range(n_pairs)]
        s16 = [s.astype(BF16) for s in state]
        uy = [_mm_nt(war[i], s16[p]) for p, i in enumerate(idx)]
        ut = [_mm_nt(s16[p], wa16[i]) + uvt[i] for p, i in enumerate(idx)]
        yield
        for p, i in enumerate(idx):
            u16 = (uy[p][:c] + uv[i]).astype(BF16)
            y = uy[p][c:] + yv[i] + _mm(m_rb16[i], _block_diag(u16, bdm16))
            y_scr[ck * c:(ck + 1) * c, p * pw:(p + 1) * pw] = y
        state = [state[p] * src["g_all"][ck:ck + 1, p * pw:(p + 1) * pw] + vk[i]
                 + jnp.where(bdm, _mm(ut[p].astype(BF16), part("b_e", items[i])), 0.0)
                 for p, i in enumerate(idx)]
        yield
    for p in range(n_pairs):
        s_scr[p] = state[p]

    y = y_scr[...]
    bd = w["bd"][...]
    mu = _head_sums([y], bd)[0] * (1.0 / HEAD)
    d = y - mu
    var = _head_sums([d * d], bd)[0] * (1.0 / HEAD)
    out = d * lax.rsqrt(var + GN_EPS) * w["lnw"][...] + w["lnb"][...] + src["bonus"][...]
    o_ref[row0:row0 + tile, :] = out.astype(o_ref.dtype)
    yield


def _interleave(*streams):
    for _ in itertools.zip_longest(*streams):
        pass


_W_NAMES = ("w0", "wup", "a0", "aup", "kkw", "kaw", "rkw", "lnw", "lnb", "bd", "tril")


def _rwkv_kernel(*refs):
    cur, nxt = refs[0:4], refs[4:8]
    w = dict(zip(_W_NAMES, refs[8:8 + len(_W_NAMES)]))
    rest = refs[8 + len(_W_NAMES):]
    o_ref, s_scr, y_scr = rest[0], rest[1], rest[2]
    n_set = len(_SET_NAMES)
    set0 = dict(zip(_SET_NAMES, rest[3:3 + n_set]))
    set1 = dict(zip(_SET_NAMES, rest[3 + n_set:3 + 2 * n_set]))
    n = pl.program_id(1)

    @pl.when(n == 0)
    def _():
        s_scr[...] = jnp.zeros_like(s_scr)
        _interleave(_rwkv_prep(cur, 0, w, set0))

    c, pw = RW_CHUNK, 2 * HEAD
    ci = lax.broadcasted_iota(jnp.int32, (c, pw), 0)
    cj = lax.broadcasted_iota(jnp.int32, (c, pw), 1) % HEAD
    eye = (ci == cj).astype(F32)
    si = lax.broadcasted_iota(jnp.int32, (2 * c, pw), 0)
    sj = lax.broadcasted_iota(jnp.int32, (2 * c, pw), 1) % HEAD
    tri2 = ((si < c) & (si > sj)) | ((si >= c) & ((si - c) >= sj))
    bi = lax.broadcasted_iota(jnp.int32, (2 * pw, pw), 0)
    bj = lax.broadcasted_iota(jnp.int32, (2 * pw, pw), 1)
    bdm_bool = ((bi // HEAD) % 2) == (bj // HEAD)
    bdm4_16 = bdm_bool.astype(BF16)
    hi = lax.broadcasted_iota(jnp.int32, (pw, pw), 0)
    hj = lax.broadcasted_iota(jnp.int32, (pw, pw), 1)
    di, dj = hi % HEAD, hj % HEAD
    same_head = (hi // HEAD) == (hj // HEAD)
    near = lambda b: same_head & ((di // b) == (dj // b))
    ring = lambda b: same_head & ((di // (2 * b)) == (dj // (2 * b))) & ((di // b) != (dj // b))
    masks = ((ci // 8) == (cj // 8), near(8).astype(BF16),
             tuple(ring(b).astype(BF16) for b in (8, 16, 32)))
    consts = (eye, masks, tri2, bdm4_16, bdm4_16[:pw], bdm_bool[:pw])

    ctx0, ctx1 = {}, {}
    _interleave(_rwkv_prep(cur, RW_TILE, w, set1, _PREP_DELAY_BESIDE_A),
                _rwkv_main_a(set0, consts, ctx0))
    _interleave(_rwkv_main_b(set0, w, consts, ctx0, s_scr, y_scr, o_ref, 0),
                _rwkv_main_a(set1, consts, ctx1))
    _interleave(_rwkv_main_b(set1, w, consts, ctx1, s_scr, y_scr, o_ref, RW_TILE),
                _rwkv_prep(nxt, 0, w, set0, _PREP_DELAY_BESIDE_B))


def _rwkv(proj, vecs, wup, aup, bd, tril, batch, seq):
    m = proj.shape[0]
    ns = seq // RW_STEP
    row = lambda b, n: b * ns + n
    nxt = lambda b, n: b * ns + jnp.minimum(n + 1, ns - 1)
    vec = lambda width: pl.BlockSpec((1, width), lambda b, n: (0, 0))
    full = lambda a: pl.BlockSpec(a.shape, lambda b, n: (0, 0))

    def token_specs(row_fn):
        wide = lambda cb: pl.BlockSpec((RW_STEP, D_BRANCH), lambda b, n: (row_fn(b, n), cb))
        return [wide(P_R // D_BRANCH), wide(P_K // D_BRANCH), wide(P_V // D_BRANCH),
                pl.BlockSpec((RW_STEP, 2 * LORA), lambda b, n: (row_fn(b, n), P_LORA // (2 * LORA)))]

    in_specs = token_specs(row) + token_specs(nxt) + [
        vec(D_BRANCH), full(wup), vec(D_BRANCH), full(aup),
        vec(D_BRANCH), vec(D_BRANCH), vec(D_BRANCH), vec(D_BRANCH), vec(D_BRANCH),
        full(bd), full(tril),
    ]
    prep_set = ([pltpu.VMEM((RW_TILE, D_BRANCH), BF16) for _ in _SET_BF16]
                + [pltpu.VMEM((8, D_BRANCH), F32), pltpu.VMEM((RW_TILE, D_BRANCH), F32)])
    return pl.pallas_call(
        _rwkv_kernel,
        grid=(batch, ns),
        in_specs=in_specs,
        out_specs=pl.BlockSpec((RW_STEP, D_BRANCH), lambda b, n: (row(b, n), 0)),
        out_shape=jax.ShapeDtypeStruct((m, D_BRANCH), ACT),
        scratch_shapes=[
            pltpu.VMEM((N_HEADS // 2, 2 * HEAD, 2 * HEAD), F32),
            pltpu.VMEM((RW_TILE, D_BRANCH), F32),
        ] + prep_set + prep_set,
        compiler_params=pltpu.CompilerParams(
            dimension_semantics=("parallel", "arbitrary"), vmem_limit_bytes=_VMEM_LIMIT),
        name="rwkv",
    )(*([proj] * 8), vecs[0], wup, vecs[1], aup, *vecs[2:], bd, tril)


_SLOPES = tuple(2.0 ** (-8.0 * (h + 1) / N_HEADS) for h in range(N_HEADS))
_SQRT_HALF = float(np.sqrt(0.5))


def _attention_bias():
    t = np.arange(BLK)[:, None]
    s = np.arange(2 * BLK)[None, :]
    dist = t + BLK - s
    out = np.empty((2, N_HEADS * BLK, 2 * BLK), np.float32)
    for first in (0, 1):
        valid = (dist >= 0) & (dist < BLK) & ((s >= BLK) | (first == 1))
        for h in range(N_HEADS):
            out[first, h * BLK:(h + 1) * BLK] = np.where(valid, -_SLOPES[h] * dist, NEG_INF)
    return out


def _pool_bands():
    t = np.arange(BLK)[:, None] + BLK
    s = np.arange(2 * BLK)[None, :]
    return np.stack([((s <= t) & (s > t - w)).astype(np.float32) for w in POOL_WINDOWS])


def _mixers_kernel(sinks_ref, bias_ref, band_ref, q_ref, kvc_ref, kvp_ref, zc_ref, zcp_ref, zd_ref,
                   poolw_ref, pscale_ref, nw_ref, sw_ref, sb_ref, o_ref, y_scr):
    n = pl.program_id(1)
    rows = ATT_GROUP * BLK
    head_in_group = lax.broadcasted_iota(jnp.int32, (rows, 1), 0) // BLK

    def per_row(values):
        col = jnp.full((rows, 1), values[-1], F32)
        for j in range(ATT_GROUP - 2, -1, -1):
            col = jnp.where(head_in_group == j, values[j], col)
        return col

    sinks = [per_row([sinks_ref[h] for h in range(g * ATT_GROUP, (g + 1) * ATT_GROUP)])
             for g in range(KV_HEADS)]
    ri = lax.broadcasted_iota(jnp.int32, (BLK, BLK), 0)
    rj = lax.broadcasted_iota(jnp.int32, (BLK, BLK), 1)
    sgu_w16 = [jnp.where(ri >= rj, sw_ref[g], 0.0).astype(BF16) for g in range(4)]
    refs = (bias_ref, band_ref, q_ref, kvc_ref, kvp_ref, zc_ref, zcp_ref, zd_ref, poolw_ref,
            pscale_ref, nw_ref, sb_ref, y_scr)
    for sub in range(MIX_SUB):
        _mixers_block(sub, n, refs, sinks, sgu_w16)
    o_ref[...] = y_scr[...].astype(o_ref.dtype)


def _mixers_block(sub, n, refs, sinks, sgu_w16):
    (bias_ref, band_ref, q_ref, kvc_ref, kvp_ref, zc_ref, zcp_ref, zd_ref, poolw_ref,
     pscale_ref, nw_ref, sb_ref, y_scr) = refs
    rs = slice(sub * BLK, (sub + 1) * BLK)
    before = slice((sub - 1) * BLK, sub * BLK)
    rows = ATT_GROUP * BLK
    if sub == 0:
        kv_prev = kvp_ref[...]
        z_prev = jnp.where(n > 0, zcp_ref[...], 0.0)
        bias_of = lambda g: bias_ref[jnp.minimum(n, 1), g * rows:(g + 1) * rows, :]
    else:
        kv_prev = kvc_ref[before, :]
        z_prev = zc_ref[before, :]
        bias_of = lambda g: bias_ref[1, g * rows:(g + 1) * rows, :]

    q = q_ref[rs, :] * (HEAD ** -0.5)
    kv = jnp.concatenate([kv_prev, kvc_ref[rs, :]], axis=0)

    scores = []
    for g in range(KV_HEADS):
        qg = jnp.concatenate(
            [q[:, (g * ATT_GROUP + j) * HEAD:(g * ATT_GROUP + j + 1) * HEAD]
             for j in range(ATT_GROUP)], axis=0)
        scores.append(_bdot_nt(qg, kv[:, g * HEAD:(g + 1) * HEAD]))

    zfull = jnp.concatenate([z_prev, zc_ref[rs, :]], axis=0)
    pos = (n * MIX_SUB + sub) * BLK + lax.broadcasted_iota(jnp.int32, (BLK, 1), 0) + 1
    sums = [jnp.dot(band_ref[g], zfull[:, g * BLK:(g + 1) * BLK], preferred_element_type=F32)
            for g in range(len(POOL_WINDOWS))]

    zd = zd_ref[rs, :].astype(F32)
    gz = 0.5 * zd * (1.0 + lax.erf(zd * _SQRT_HALF))
    u = gz[:, :D_BRANCH]
    vv = gz[:, D_BRANCH:]
    mu = jnp.mean(vv, axis=-1, keepdims=True)
    dv = vv - mu
    var = jnp.mean(dv * dv, axis=-1, keepdims=True)
    vn = dv * lax.rsqrt(var + LN_EPS) * nw_ref[...]
    for g in range(4):
        gs = slice(g * BLK, (g + 1) * BLK)
        sg = _bdot(sgu_w16[g], vn[:, gs]) + sb_ref[:, gs]
        y_scr[rs, 2 * D_BRANCH + g * BLK:2 * D_BRANCH + (g + 1) * BLK] = u[:, gs] * sg

    for g, w in enumerate(POOL_WINDOWS):
        gs = slice(g * BLK, (g + 1) * BLK)
        cnt = jnp.minimum(pos, w).astype(F32)
        pooled = sums[g] / cnt - zc_ref[rs, gs].astype(F32)
        yg = _bdot(pooled, poolw_ref[g]) * pscale_ref[:, gs]
        y_scr[rs, D_BRANCH + g * BLK:D_BRANCH + (g + 1) * BLK] = yg

    probs, dens = [], []
    for g in range(KV_HEADS):
        s = scores[g] + bias_of(g)
        mx = jnp.maximum(jnp.max(s, axis=-1, keepdims=True), sinks[g])
        p = jnp.exp(s - mx)
        probs.append(p)
        dens.append(jnp.sum(p, axis=-1, keepdims=True) + jnp.exp(sinks[g] - mx))
    for g in range(KV_HEADS):
        vg = kv[:, (KV_HEADS + g) * HEAD:(KV_HEADS + g + 1) * HEAD]
        og = _bdot(probs[g], vg) / dens[g]
        for j in range(ATT_GROUP):
            h = g * ATT_GROUP + j
            y_scr[rs, h * HEAD:(h + 1) * HEAD] = og[j * BLK:(j + 1) * BLK, :]


def _mixers(proj, sinks, pool_w, pool_scale, norm_w, sgu_w, sgu_bias, batch, seq):
    m = proj.shape[0]
    tm = MIX_SUB * BLK
    ns = seq // tm
    row = lambda b, n: b * ns + n
    before = lambda b, n: jnp.maximum(row(b, n) * MIX_SUB - 1, 0)
    bias = jnp.asarray(_attention_bias())
    assert proj.dtype == BF16
    band = jnp.asarray(_pool_bands(), dtype=BF16)
    in_specs = [
        pl.BlockSpec(memory_space=pltpu.SMEM),
        pl.BlockSpec(bias.shape, lambda b, n: (0, 0, 0)),
        pl.BlockSpec(band.shape, lambda b, n: (0, 0, 0)),
        pl.BlockSpec((tm, D_BRANCH), lambda b, n: (row(b, n), P_Q // D_BRANCH)),
        pl.BlockSpec((tm, 256), lambda b, n: (row(b, n), P_KV // 256)),
        pl.BlockSpec((BLK, 256), lambda b, n: (before(b, n), P_KV // 256)),
        pl.BlockSpec((tm, D_BRANCH), lambda b, n: (row(b, n), P_POOL // D_BRANCH)),
        pl.BlockSpec((BLK, D_BRANCH), lambda b, n: (before(b, n), P_POOL // D_BRANCH)),
        pl.BlockSpec((tm, 2 * D_BRANCH), lambda b, n: (row(b, n), P_SGU // (2 * D_BRANCH))),
        pl.BlockSpec((4, BLK, BLK), lambda b, n: (0, 0, 0)),
        pl.BlockSpec((1, D_BRANCH), lambda b, n: (0, 0)),
        pl.BlockSpec((1, D_BRANCH), lambda b, n: (0, 0)),
        pl.BlockSpec((4, BLK, BLK), lambda b, n: (0, 0, 0)),
        pl.BlockSpec((BLK, D_BRANCH), lambda b, n: (0, 0)),
    ]
    return pl.pallas_call(
        _mixers_kernel,
        grid=(batch, ns),
        in_specs=in_specs,
        out_specs=pl.BlockSpec((tm, 3 * D_BRANCH), lambda b, n: (row(b, n), 0)),
        out_shape=jax.ShapeDtypeStruct((m, 3 * D_BRANCH), ACT),
        scratch_shapes=[pltpu.VMEM((tm, 3 * D_BRANCH), F32)],
        compiler_params=pltpu.CompilerParams(
            dimension_semantics=("parallel", "parallel"), vmem_limit_bytes=_VMEM_LIMIT),
        name="mixers",
    )(sinks, bias, band, proj, proj, proj, proj, proj, proj, pool_w, pool_scale, norm_w, sgu_w,
      sgu_bias)


OUT_TM = 512
OUT_SUB = 256
OUT_STAGE = 512


def _outproj_kernel(ya_ref, yb_ref, g_ref, x_ref, w_hbm, pw_ref, o_ref, w_scr, stage, sem, *,
                    layer):
    @pl.when(pl.program_id(0) == 0)
    def _():
        def copy(idx):
            slot = idx % 2
            return pltpu.make_async_copy(w_hbm.at[layer, pl.ds(idx * OUT_STAGE, OUT_STAGE), :],
                                         stage.at[slot], sem.at[slot])

        n_pieces = D_MODEL // OUT_STAGE
        copy(0).start()
        for idx in range(n_pieces):
            if idx + 1 < n_pieces:
                copy(idx + 1).start()
            copy(idx).wait()
            w_scr[idx * OUT_STAGE:(idx + 1) * OUT_STAGE, :] = stage[idx % 2].astype(BF16)

    for r0 in range(0, OUT_TM, OUT_SUB):
        rows = slice(r0, r0 + OUT_SUB)
        g = g_ref[rows, :].astype(F32)
        y = jnp.concatenate([ya_ref[rows, :], yb_ref[rows, :]], axis=1).astype(F32)
        gated = (y * (g * jax.nn.sigmoid(g))).astype(BF16)
        acc = jnp.dot(gated, w_scr[...], preferred_element_type=F32)
        ms = jnp.mean(acc * acc, axis=-1, keepdims=True)
        o_ref[rows, :] = x_ref[rows, :] + acc * lax.rsqrt(ms + NORM_EPS) * pw_ref[...]


def _outproj(ya, ybcd, proj, x2d, w_out, layer, post_w):
    m = x2d.shape[0]
    return pl.pallas_call(
        functools.partial(_outproj_kernel, layer=layer),
        grid=(m // OUT_TM,),
        in_specs=[
            pl.BlockSpec((OUT_TM, D_BRANCH), lambda i: (i, 0)),
            pl.BlockSpec((OUT_TM, 3 * D_BRANCH), lambda i: (i, 0)),
            pl.BlockSpec((OUT_TM, D_MODEL), lambda i: (i, P_GATE // D_MODEL)),
            pl.BlockSpec((OUT_TM, D_MODEL), lambda i: (i, 0)),
            pl.BlockSpec(memory_space=pl.ANY),
            pl.BlockSpec((1, D_MODEL), lambda i: (0, 0)),
        ],
        out_specs=pl.BlockSpec((OUT_TM, D_MODEL), lambda i: (i, 0)),
        out_shape=jax.ShapeDtypeStruct((m, D_MODEL), F32),
        scratch_shapes=[
            pltpu.VMEM((D_MODEL, D_MODEL), BF16),
            pltpu.VMEM((2, OUT_STAGE, D_MODEL), F32),
            pltpu.SemaphoreType.DMA((2,)),
        ],
        compiler_params=pltpu.CompilerParams(
            dimension_semantics=("arbitrary",), vmem_limit_bytes=_VMEM_LIMIT),
        name="outproj",
    )(ya, ybcd, proj, x2d, w_out, post_w)


def kernel(x, pre_norm_w, post_norm_w, w_in, shift_mu, rwkv_w0, rwkv_w_up, rwkv_a0, rwkv_a_up,
           rwkv_k_k, rwkv_k_a, rwkv_r_k, rwkv_ln_w, rwkv_ln_b, attn_sinks, pool_w, pool_scale,
           sgu_norm_w, sgu_w, sgu_b, w_out):
    batch, seq, _ = x.shape
    assert x.shape == (batch, seq, D_MODEL) and seq % (MIX_SUB * BLK) == 0 and seq % RW_STEP == 0
    m = batch * seq
    head_id = np.arange(D_BRANCH // 2) // HEAD
    bd = jnp.asarray((head_id[:, None] == head_id[None, :]).astype(np.float32), dtype=BF16)
    t_id = np.arange(RW_TILE)
    tril = jnp.asarray(((t_id[:, None] >= t_id[None, :])
                        & (t_id[:, None] // RW_CHUNK == t_id[None, :] // RW_CHUNK)).astype(np.float32),
                       dtype=BF16)
    row_vec = lambda a: a.reshape(1, -1)

    h = x.reshape(m, D_MODEL)
    for l in range(DEPTH):
        mu = shift_mu[l]
        mu_p = jnp.concatenate([mu[:3 * D_BRANCH], jnp.zeros((P_LORA - P_KV,), mu.dtype),
                                mu[3 * D_BRANCH:], jnp.zeros((P_WIDTH - D_IN,), mu.dtype)])
        proj = _inproj(h, row_vec(pre_norm_w[l]), row_vec(mu_p), w_in, l, seq)

        vecs = [row_vec(rwkv_w0[l]), row_vec(rwkv_a0[l]), row_vec(rwkv_k_k[l]),
                row_vec(rwkv_k_a[l]), row_vec(rwkv_r_k[l]), row_vec(rwkv_ln_w[l]),
                row_vec(rwkv_ln_b[l])]
        ya = _rwkv(proj, vecs, rwkv_w_up[l], rwkv_a_up[l], bd, tril, batch, seq)

        sgu_bias = jnp.broadcast_to(sgu_b[l].T[:, :, None], (BLK, 4, BLK)).reshape(BLK, D_BRANCH)
        ybcd = _mixers(proj, attn_sinks[l], pool_w[l], row_vec(pool_scale[l]),
                       row_vec(sgu_norm_w[l]), sgu_w[l], sgu_bias, batch, seq)

        h = _outproj(ya, ybcd, proj, h, w_out, l, row_vec(post_norm_w[l]))
    return h.reshape(batch, seq, D_MODEL)
```

```python
import functools
import itertools

import jax
import jax.numpy as jnp
import numpy as np
from jax import lax
from jax.experimental import pallas as pl
from jax.experimental.pallas import tpu as pltpu

F32 = jnp.float32
BF16 = jnp.bfloat16
ACT = BF16

D_MODEL = 2048
DEPTH = 2
D_BRANCH = 512
HEAD = 64
N_HEADS = D_BRANCH // HEAD
LORA = 64
KV_HEADS = 2
ATT_GROUP = N_HEADS // KV_HEADS
BLK = 128
MIX_SUB = 4
POOL_WINDOWS = (2, 4, 8, 16)
NEG_INF = -1e30
NORM_EPS = 1e-6
LN_EPS = 1e-5
GN_EPS = 64e-5

A_COLS = 3 * D_BRANCH + 2 * LORA
B_COLS = D_BRANCH + 2 * KV_HEADS * HEAD
OFF_B = A_COLS
OFF_C = OFF_B + B_COLS
OFF_D = OFF_C + D_BRANCH
OFF_G = OFF_D + 2 * D_BRANCH
D_IN = OFF_G + D_MODEL

P_GATE = 0
P_SGU = 2048
P_POOL = 3072
P_Q = 3584
P_R = 4096
P_K = 4608
P_V = 5120
P_KV = 5632
P_LORA = 5888
P_WIDTH = 6144

RW_CHUNK = 64
RW_TILE = 256
RW_STEP = 2 * RW_TILE

_VMEM_LIMIT = 56 * 1024 * 1024


IN_TM = 512
IN_TN = 1024
IN_STAGE = 256
_W_RUNS = ((OFF_G, D_MODEL, P_GATE), (OFF_D, 2 * D_BRANCH, P_SGU), (OFF_C, D_BRANCH, P_POOL),
           (OFF_B, D_BRANCH, P_Q), (0, 3 * D_BRANCH, P_R),
           (OFF_B + D_BRANCH, 2 * KV_HEADS * HEAD, P_KV), (3 * D_BRANCH, 2 * LORA, P_LORA))
_W_PIECES = tuple((src + o, min(IN_STAGE, width - o), dst + o)
                  for src, width, dst in _W_RUNS for o in range(0, width, IN_STAGE))
assert sum(p[1] for p in _W_PIECES) == D_IN and P_LORA + 2 * LORA == D_IN


IN_SLOTS = 2


def _load_weight(w_hbm, layer, w_scr, stage, sem):
    def copy(idx):
        src, width, _ = _W_PIECES[idx]
        slot = idx % IN_SLOTS
        return pltpu.make_async_copy(w_hbm.at[layer, :, pl.ds(src, width)],
                                     stage.at[slot, :, pl.ds(0, width)], sem.at[slot])

    for idx in range(IN_SLOTS - 1):
        copy(idx).start()
    for idx, (_, width, dst) in enumerate(_W_PIECES):
        if idx + IN_SLOTS - 1 < len(_W_PIECES):
            copy(idx + IN_SLOTS - 1).start()
        copy(idx).wait()
        w_scr[:, dst:dst + width] = stage[idx % IN_SLOTS, :, 0:width].astype(BF16)
    w_scr[:, D_IN:] = jnp.zeros((D_MODEL, P_WIDTH - D_IN), BF16)


def _inproj_kernel(x_ref, pw_ref, mu_ref, w_hbm, o_ref, w_scr, stage, sem, h_scr, carry_scr, *,
                   layer, tiles_per_seq):
    i = pl.program_id(0)

    @pl.when(i == 0)
    def _():
        carry_scr[...] = jnp.zeros_like(carry_scr)
        _load_weight(w_hbm, layer, w_scr, stage, sem)

    for r0 in range(0, IN_TM, 256):
        x = x_ref[r0:r0 + 256, :]
        ms = jnp.mean(x * x, axis=-1, keepdims=True)
        h_scr[r0:r0 + 256, :] = (x * lax.rsqrt(ms + NORM_EPS) * pw_ref[...]).astype(BF16)
    row = lax.broadcasted_iota(jnp.int32, (IN_TM, 1), 0)
    starts_sequence = (i % tiles_per_seq) == 0
    for c0 in range(0, P_WIDTH, IN_TN):
        acc = jnp.dot(h_scr[...], w_scr[:, c0:c0 + IN_TN], preferred_element_type=F32)
        if c0 >= P_R:
            cs = slice(c0 - P_R, c0 - P_R + IN_TN)
            before = jnp.where(starts_sequence, 0.0, carry_scr[0:1, cs])
            prev = jnp.where(row == 0, before, pltpu.roll(acc, 1, axis=0))
            carry_scr[0:1, cs] = acc[IN_TM - 1:IN_TM, :]
            acc = acc + mu_ref[:, cs] * (prev - acc)
        o_ref[:, c0:c0 + IN_TN] = acc.astype(o_ref.dtype)


def _inproj(x2d, pre_w, mu_p, w_in, layer, seq):
    m = x2d.shape[0]
    assert P_R % IN_TN == 0 and seq % IN_TM == 0 and mu_p.shape == (1, P_WIDTH - P_R)
    return pl.pallas_call(
        functools.partial(_inproj_kernel, layer=layer, tiles_per_seq=seq // IN_TM),
        grid=(m // IN_TM,),
        in_specs=[
            pl.BlockSpec((IN_TM, D_MODEL), lambda i: (i, 0)),
            pl.BlockSpec((1, D_MODEL), lambda i: (0, 0)),
            pl.BlockSpec((1, P_WIDTH - P_R), lambda i: (0, 0)),
            pl.BlockSpec(memory_space=pl.ANY),
        ],
        out_specs=pl.BlockSpec((IN_TM, P_WIDTH), lambda i: (i, 0)),
        out_shape=jax.ShapeDtypeStruct((m, P_WIDTH), ACT),
        scratch_shapes=[
            pltpu.VMEM((D_MODEL, P_WIDTH), BF16),
            pltpu.VMEM((IN_SLOTS, D_MODEL, IN_STAGE), F32),
            pltpu.SemaphoreType.DMA((IN_SLOTS,)),
            pltpu.VMEM((IN_TM, D_MODEL), BF16),
            pltpu.VMEM((8, P_WIDTH - P_R), F32),
        ],
        compiler_params=pltpu.CompilerParams(
            dimension_semantics=("arbitrary",), vmem_limit_bytes=_VMEM_LIMIT),
        name="inproj",
    )(x2d, pre_w, mu_p, w_in)


def _bdot(a, b):
    return jnp.dot(a.astype(BF16), b.astype(BF16), preferred_element_type=F32)


def _bdot_nt(a, b):
    return lax.dot_general(a.astype(BF16), b.astype(BF16), (((1,), (1,)), ((), ())),
                           preferred_element_type=F32)


def _split3(x):
    x1 = x.astype(BF16)
    r1 = x - x1.astype(F32)
    x2 = r1.astype(BF16)
    x3 = (r1 - x2.astype(F32)).astype(BF16)
    return x1, x2, x3


def _head_sums(xs, bd):
    rows = xs[0].shape[0]
    half = bd.shape[0]
    x = jnp.concatenate(xs, axis=0).astype(BF16)
    out = jnp.concatenate(
        [jnp.dot(x[:, :half], bd, preferred_element_type=F32),
         jnp.dot(x[:, half:], bd, preferred_element_type=F32)], axis=1)
    return [out[i * rows:(i + 1) * rows] for i in range(len(xs))]


def _mm(a16, b16):
    return jnp.dot(a16, b16, preferred_element_type=F32)


def _mm_nt(a16, b16):
    return lax.dot_general(a16, b16, (((1,), (1,)), ((), ())), preferred_element_type=F32)


def _block_diag(m16, bdm16):
    return jnp.concatenate([m16, m16], axis=0) * bdm16


def _unit_lower_inverse_many(lmats, eye, masks, bdm16):
    base_mask, base_bd16, level_bd16 = masks
    c = lmats[0].shape[0]
    l16s = [l.astype(BF16) for l in lmats]
    stack2 = lambda m16: jnp.concatenate([m16, m16], axis=0)
    lds = [jnp.where(base_mask, l, 0.0) for l in lmats]
    l2s = [_mm(ld.astype(BF16), stack2(l16) * base_bd16).astype(BF16)
           for ld, l16 in zip(lds, l16s)]
    yield
    xs = [eye + ld for ld in lds]
    both = [_mm(jnp.concatenate([x.astype(BF16), l2], axis=0), stack2(l2) * bdm16)
            for x, l2 in zip(xs, l2s)]
    xs = [x + b[:c] for x, b in zip(xs, both)]
    l4s = [b[c:].astype(BF16) for b in both]
    yield
    xs = [x + _mm(x.astype(BF16), stack2(l4) * bdm16) for x, l4 in zip(xs, l4s)]
    yield
    for lvl16 in level_bd16:
        ts = [_mm(x.astype(BF16), stack2(l16) * lvl16).astype(BF16) for x, l16 in zip(xs, l16s)]
        yield
        xs = [x + _mm(t, stack2(x.astype(BF16)) * bdm16) for x, t in zip(xs, ts)]
        yield
    return xs


_SET_BF16 = ("a", "b", "k", "r", "v", "b_e", "k_e")
_SET_NAMES = _SET_BF16 + ("g_all", "bonus")
_PREP_DELAY_BESIDE_A = 8
_PREP_DELAY_BESIDE_B = 4
_EXP_NEG_HALF = float(np.exp(-0.5))


def _rwkv_prep(z_refs, row0, w, dst, mxu_delay=0):
    tile, c = RW_TILE, RW_CHUNK
    rows = lambda z_ref: z_ref[row0:row0 + tile, :].astype(F32)

    lo = rows(z_refs[3])
    lora_w = _bdot(jnp.tanh(lo[:, :LORA]), w["wup"][...])
    lora_a = _bdot(lo[:, LORA:], w["aup"][...])
    yield
    r, k, v = rows(z_refs[0]), rows(z_refs[1]), rows(z_refs[2])
    logw = -_EXP_NEG_HALF * jax.nn.sigmoid(w["w0"][...] + lora_w)
    asig = jax.nn.sigmoid(w["a0"][...] + lora_a)
    logw_terms = _split3(logw)
    kk = k * w["kkw"][...]
    k2 = k * (1.0 + (asig - 1.0) * w["kaw"][...])
    sum_terms = [kk * kk, r * k2 * w["rkw"][...]]
    for _ in range(mxu_delay):
        yield
    tril = w["tril"][...]
    cum = sum(jnp.dot(tril, p, preferred_element_type=F32) for p in logw_terms)
    kk_ss, rk_sum = _head_sums(sum_terms, w["bd"][...])
    yield
    kk = kk * lax.rsqrt(jnp.maximum(kk_ss, 1e-24))
    dst["bonus"][...] = rk_sum * v
    for ck in range(tile // c):
        sl = slice(ck * c, (ck + 1) * c)
        lw, cm = logw[sl], cum[sl]
        total = cm[c - 1:c, :]
        g_inv = jnp.exp(-cm)
        g_all = jnp.exp(total)
        g_tail = g_all * g_inv
        b_c = kk[sl] * asig[sl]
        dst["a"][sl, :] = (-kk[sl] * jnp.exp(cm - lw)).astype(BF16)
        dst["b"][sl, :] = (b_c * g_inv).astype(BF16)
        dst["k"][sl, :] = (k2[sl] * g_inv).astype(BF16)
        dst["r"][sl, :] = (r[sl] * jnp.exp(cm)).astype(BF16)
        dst["v"][sl, :] = v[sl].astype(BF16)
        dst["b_e"][sl, :] = (b_c * g_tail).astype(BF16)
        dst["k_e"][sl, :] = (k2[sl] * g_tail).astype(BF16)
        dst["g_all"][ck:ck + 1, :] = g_all
    yield


def _rwkv_items():
    return [(ck, p) for ck in range(RW_TILE // RW_CHUNK) for p in range(N_HEADS // 2)]


def _rwkv_part(src, name, it):
    ck, p = it
    c, pw = RW_CHUNK, 2 * HEAD
    return src[name][ck * c:(ck + 1) * c, p * pw:(p + 1) * pw]


def _rwkv_main_a(src, consts, ctx):
    c, pw = RW_CHUNK, 2 * HEAD
    eye, masks, tri2, bdm4_16, bdm16, bdm = consts
    items = _rwkv_items()
    part = functools.partial(_rwkv_part, src)

    a2 = [part("a", it) for it in items]
    r2 = [part("r", it) for it in items]
    l_ab, m_rb16, ak16 = [], [], []
    for a, x, it in zip(a2, r2, items):
        b, k_ = part("b", it), part("k", it)
        pr = _mm_nt(jnp.concatenate([a, x], axis=0),
                    jnp.concatenate([b, b, k_, k_], axis=0) * bdm4_16)
        ab = jnp.where(tri2, pr[:, :pw], 0.0)
        l_ab.append(ab[:c])
        m_rb16.append(ab[c:].astype(BF16))
        ak16.append(jnp.where(tri2, pr[:, pw:], 0.0).astype(BF16))
    yield
    qy = [_mm(x, _block_diag(part("v", it), bdm16))
          for x, it in zip(ak16, items)]
    qv16 = [q[:c].astype(BF16) for q in qy]
    yv = [q[c:] for q in qy]
    vk = [jnp.where(bdm, lax.dot_general(part("v", it), part("k_e", it), (((0,), (0,)), ((), ())),
                                         preferred_element_type=F32), 0.0) for it in items]
    yield
    tinv = yield from _unit_lower_inverse_many(l_ab, eye, masks, bdm16)
    wu = [_mm(t.astype(BF16),
              jnp.concatenate([_block_diag(a, bdm16), _block_diag(q, bdm16)], axis=1))
          for t, a, q in zip(tinv, a2, qv16)]
    wa16 = [x[:, :pw].astype(BF16) for x in wu]
    uv = [x[:, pw:] for x in wu]
    uvt = [x.T for x in uv]
    war = [jnp.concatenate([x, y], axis=0) for x, y in zip(wa16, r2)]
    ctx.update(wa16=wa16, uv=uv, uvt=uvt, war=war, m_rb16=m_rb16, yv=yv, vk=vk)
    yield


def _rwkv_main_b(src, w, consts, ctx, s_scr, y_scr, o_ref, row0):
    tile, c, pw = RW_TILE, RW_CHUNK, 2 * HEAD
    bdm16, bdm = consts[4], consts[5]
    n_chunks, n_pairs = tile // c, N_HEADS // 2
    items = _rwkv_items()
    part = functools.partial(_rwkv_part, src)
    wa16, uv, uvt, war = ctx["wa16"], ctx["uv"], ctx["uvt"], ctx["war"]
    m_rb16, yv, vk = ctx["m_rb16"], ctx["yv"], ctx["vk"]

    state = [s_scr[p] for p in range(n_pairs)]
    for ck in range(n_chunks):
        idx = [ck * n_pairs + p for p in range(n_pairs)]
        s16 = [s.astype(BF16) for s in state]
        uy = [_mm_nt(war[i], s16[p]) for p, i in enumerate(idx)]
        ut = [_mm_nt(s16[p], wa16[i]) + uvt[i] for p, i in enumerate(idx)]
        yield
        for p, i in enumerate(idx):
            u16 = (uy[p][:c] + uv[i]).astype(BF16)
            y = uy[p][c:] + yv[i] + _mm(m_rb16[i], _block_diag(u16, bdm16))
            y_scr[ck * c:(ck + 1) * c, p * pw:(p + 1) * pw] = y
        state = [state[p] * src["g_all"][ck:ck + 1, p * pw:(p + 1) * pw] + vk[i]
                 + jnp.where(bdm, _mm(ut[p].astype(BF16), part("b_e", items[i])), 0.0)
                 for p, i in enumerate(idx)]
        yield
    for p in range(n_pairs):
        s_scr[p] = state[p]

    y = y_scr[...]
    bd = w["bd"][...]
    mu = _head_sums([y], bd)[0] * (1.0 / HEAD)
    d = y - mu
    var = _head_sums([d * d], bd)[0] * (1.0 / HEAD)
    out = d * lax.rsqrt(var + GN_EPS) * w["lnw"][...] + w["lnb"][...] + src["bonus"][...]
    o_ref[row0:row0 + tile, :] = out.astype(o_ref.dtype)
    yield


def _interleave(*streams):
    for _ in itertools.zip_longest(*streams):
        pass


_W_NAMES = ("w0", "wup", "a0", "aup", "kkw", "kaw", "rkw", "lnw", "lnb", "bd", "tril")


def _rwkv_kernel(*refs):
    cur, nxt = refs[0:4], refs[4:8]
    w = dict(zip(_W_NAMES, refs[8:8 + len(_W_NAMES)]))
    rest = refs[8 + len(_W_NAMES):]
    o_ref, s_scr, y_scr = rest[0], rest[1], rest[2]
    n_set = len(_SET_NAMES)
    set0 = dict(zip(_SET_NAMES, rest[3:3 + n_set]))
    set1 = dict(zip(_SET_NAMES, rest[3 + n_set:3 + 2 * n_set]))
    n = pl.program_id(1)

    @pl.when(n == 0)
    def _():
        s_scr[...] = jnp.zeros_like(s_scr)

    @pl.when((pl.program_id(0) == 0) & (n == 0))
    def _():
        _interleave(_rwkv_prep(cur, 0, w, set0))

    c, pw = RW_CHUNK, 2 * HEAD
    ci = lax.broadcasted_iota(jnp.int32, (c, pw), 0)
    cj = lax.broadcasted_iota(jnp.int32, (c, pw), 1) % HEAD
    eye = (ci == cj).astype(F32)
    si = lax.broadcasted_iota(jnp.int32, (2 * c, pw), 0)
    sj = lax.broadcasted_iota(jnp.int32, (2 * c, pw), 1) % HEAD
    tri2 = ((si < c) & (si > sj)) | ((si >= c) & ((si - c) >= sj))
    bi = lax.broadcasted_iota(jnp.int32, (2 * pw, pw), 0)
    bj = lax.broadcasted_iota(jnp.int32, (2 * pw, pw), 1)
    bdm_bool = ((bi // HEAD) % 2) == (bj // HEAD)
    bdm4_16 = bdm_bool.astype(BF16)
    hi = lax.broadcasted_iota(jnp.int32, (pw, pw), 0)
    hj = lax.broadcasted_iota(jnp.int32, (pw, pw), 1)
    di, dj = hi % HEAD, hj % HEAD
    same_head = (hi // HEAD) == (hj // HEAD)
    near = lambda b: same_head & ((di // b) == (dj // b))
    ring = lambda b: same_head & ((di // (2 * b)) == (dj // (2 * b))) & ((di // b) != (dj // b))
    masks = ((ci // 8) == (cj // 8), near(8).astype(BF16),
             tuple(ring(b).astype(BF16) for b in (8, 16, 32)))
    consts = (eye, masks, tri2, bdm4_16, bdm4_16[:pw], bdm_bool[:pw])

    ctx0, ctx1 = {}, {}
    _interleave(_rwkv_prep(cur, RW_TILE, w, set1, _PREP_DELAY_BESIDE_A),
                _rwkv_main_a(set0, consts, ctx0))
    _interleave(_rwkv_main_b(set0, w, consts, ctx0, s_scr, y_scr, o_ref, 0),
                _rwkv_main_a(set1, consts, ctx1))
    _interleave(_rwkv_main_b(set1, w, consts, ctx1, s_scr, y_scr, o_ref, RW_TILE),
                _rwkv_prep(nxt, 0, w, set0, _PREP_DELAY_BESIDE_B))


def _rwkv(proj, vecs, wup, aup, bd, tril, batch, seq):
    m = proj.shape[0]
    ns = seq // RW_STEP
    row = lambda b, n: b * ns + n
    nxt = lambda b, n: jnp.minimum(b * ns + n + 1, batch * ns - 1)
    vec = lambda width: pl.BlockSpec((1, width), lambda b, n: (0, 0))
    full = lambda a: pl.BlockSpec(a.shape, lambda b, n: (0, 0))

    def token_specs(row_fn):
        wide = lambda cb: pl.BlockSpec((RW_STEP, D_BRANCH), lambda b, n: (row_fn(b, n), cb))
        return [wide(P_R // D_BRANCH), wide(P_K // D_BRANCH), wide(P_V // D_BRANCH),
                pl.BlockSpec((RW_STEP, 2 * LORA), lambda b, n: (row_fn(b, n), P_LORA // (2 * LORA)))]

    in_specs = token_specs(row) + token_specs(nxt) + [
        vec(D_BRANCH), full(wup), vec(D_BRANCH), full(aup),
        vec(D_BRANCH), vec(D_BRANCH), vec(D_BRANCH), vec(D_BRANCH), vec(D_BRANCH),
        full(bd), full(tril),
    ]
    prep_set = ([pltpu.VMEM((RW_TILE, D_BRANCH), BF16) for _ in _SET_BF16]
                + [pltpu.VMEM((8, D_BRANCH), F32), pltpu.VMEM((RW_TILE, D_BRANCH), F32)])
    return pl.pallas_call(
        _rwkv_kernel,
        grid=(batch, ns),
        in_specs=in_specs,
        out_specs=pl.BlockSpec((RW_STEP, D_BRANCH), lambda b, n: (row(b, n), 0)),
        out_shape=jax.ShapeDtypeStruct((m, D_BRANCH), ACT),
        scratch_shapes=[
            pltpu.VMEM((N_HEADS // 2, 2 * HEAD, 2 * HEAD), F32),
            pltpu.VMEM((RW_TILE, D_BRANCH), F32),
        ] + prep_set + prep_set,
        compiler_params=pltpu.CompilerParams(
            dimension_semantics=("arbitrary", "arbitrary"), vmem_limit_bytes=_VMEM_LIMIT),
        name="rwkv",
    )(*([proj] * 8), vecs[0], wup, vecs[1], aup, *vecs[2:], bd, tril)


_SLOPES = tuple(2.0 ** (-8.0 * (h + 1) / N_HEADS) for h in range(N_HEADS))
_SQRT_HALF = float(np.sqrt(0.5))


def _attention_bias():
    t = np.arange(BLK)[:, None]
    s = np.arange(2 * BLK)[None, :]
    dist = t + BLK - s
    out = np.empty((2, N_HEADS * BLK, 2 * BLK), np.float32)
    for first in (0, 1):
        valid = (dist >= 0) & (dist < BLK) & ((s >= BLK) | (first == 1))
        for h in range(N_HEADS):
            out[first, h * BLK:(h + 1) * BLK] = np.where(valid, -_SLOPES[h] * dist, NEG_INF)
    return out


def _pool_bands():
    t = np.arange(BLK)[:, None] + BLK
    s = np.arange(2 * BLK)[None, :]
    return np.stack([((s <= t) & (s > t - w)).astype(np.float32) for w in POOL_WINDOWS])


def _mixers_kernel(sinks_ref, bias_ref, band_ref, q_ref, kvc_ref, kvp_ref, zc_ref, zcp_ref, zd_ref,
                   poolw_ref, pscale_ref, nw_ref, sw_ref, sb_ref, o_ref, y_scr):
    n = pl.program_id(1)
    rows = ATT_GROUP * BLK
    head_in_group = lax.broadcasted_iota(jnp.int32, (rows, 1), 0) // BLK

    def per_row(values):
        col = jnp.full((rows, 1), values[-1], F32)
        for j in range(ATT_GROUP - 2, -1, -1):
            col = jnp.where(head_in_group == j, values[j], col)
        return col

    sinks = [per_row([sinks_ref[h] for h in range(g * ATT_GROUP, (g + 1) * ATT_GROUP)])
             for g in range(KV_HEADS)]
    ri = lax.broadcasted_iota(jnp.int32, (BLK, BLK), 0)
    rj = lax.broadcasted_iota(jnp.int32, (BLK, BLK), 1)
    sgu_w16 = [jnp.where(ri >= rj, sw_ref[g], 0.0).astype(BF16) for g in range(4)]
    refs = (bias_ref, band_ref, q_ref, kvc_ref, kvp_ref, zc_ref, zcp_ref, zd_ref, poolw_ref,
            pscale_ref, nw_ref, sb_ref, y_scr)
    for sub in range(MIX_SUB):
        _mixers_block(sub, n, refs, sinks, sgu_w16)
    o_ref[...] = y_scr[...].astype(o_ref.dtype)


def _mixers_block(sub, n, refs, sinks, sgu_w16):
    (bias_ref, band_ref, q_ref, kvc_ref, kvp_ref, zc_ref, zcp_ref, zd_ref, poolw_ref,
     pscale_ref, nw_ref, sb_ref, y_scr) = refs
    rs = slice(sub * BLK, (sub + 1) * BLK)
    before = slice((sub - 1) * BLK, sub * BLK)
    rows = ATT_GROUP * BLK
    if sub == 0:
        kv_prev = kvp_ref[...]
        z_prev = jnp.where(n > 0, zcp_ref[...], 0.0)
        bias_of = lambda g: bias_ref[jnp.minimum(n, 1), g * rows:(g + 1) * rows, :]
    else:
        kv_prev = kvc_ref[before, :]
        z_prev = zc_ref[before, :]
        bias_of = lambda g: bias_ref[1, g * rows:(g + 1) * rows, :]

    q = q_ref[rs, :] * (HEAD ** -0.5)
    kv = jnp.concatenate([kv_prev, kvc_ref[rs, :]], axis=0)

    scores = []
    for g in range(KV_HEADS):
        qg = jnp.concatenate(
            [q[:, (g * ATT_GROUP + j) * HEAD:(g * ATT_GROUP + j + 1) * HEAD]
             for j in range(ATT_GROUP)], axis=0)
        scores.append(_bdot_nt(qg, kv[:, g * HEAD:(g + 1) * HEAD]))

    zfull = jnp.concatenate([z_prev, zc_ref[rs, :]], axis=0)
    pos = (n * MIX_SUB + sub) * BLK + lax.broadcasted_iota(jnp.int32, (BLK, 1), 0) + 1
    sums = [jnp.dot(band_ref[g], zfull[:, g * BLK:(g + 1) * BLK], preferred_element_type=F32)
            for g in range(len(POOL_WINDOWS))]

    zd = zd_ref[rs, :].astype(F32)
    gz = 0.5 * zd * (1.0 + lax.erf(zd * _SQRT_HALF))
    u = gz[:, :D_BRANCH]
    vv = gz[:, D_BRANCH:]
    mu = jnp.mean(vv, axis=-1, keepdims=True)
    dv = vv - mu
    var = jnp.mean(dv * dv, axis=-1, keepdims=True)
    vn = dv * lax.rsqrt(var + LN_EPS) * nw_ref[...]
    for g in range(4):
        gs = slice(g * BLK, (g + 1) * BLK)
        sg = _bdot(sgu_w16[g], vn[:, gs]) + sb_ref[:, gs]
        y_scr[rs, 2 * D_BRANCH + g * BLK:2 * D_BRANCH + (g + 1) * BLK] = u[:, gs] * sg

    for g, w in enumerate(POOL_WINDOWS):
        gs = slice(g * BLK, (g + 1) * BLK)
        cnt = jnp.minimum(pos, w).astype(F32)
        pooled = sums[g] / cnt - zc_ref[rs, gs].astype(F32)
        yg = _bdot(pooled, poolw_ref[g]) * pscale_ref[:, gs]
        y_scr[rs, D_BRANCH + g * BLK:D_BRANCH + (g + 1) * BLK] = yg

    probs, dens = [], []
    for g in range(KV_HEADS):
        s = scores[g] + bias_of(g)
        mx = jnp.maximum(jnp.max(s, axis=-1, keepdims=True), sinks[g])
        p = jnp.exp(s - mx)
        probs.append(p)
        dens.append(jnp.sum(p, axis=-1, keepdims=True) + jnp.exp(sinks[g] - mx))
    for g in range(KV_HEADS):
        vg = kv[:, (KV_HEADS + g) * HEAD:(KV_HEADS + g + 1) * HEAD]
        og = _bdot(probs[g], vg) / dens[g]
        for j in range(ATT_GROUP):
            h = g * ATT_GROUP + j
            y_scr[rs, h * HEAD:(h + 1) * HEAD] = og[j * BLK:(j + 1) * BLK, :]


def _mixers(proj, sinks, pool_w, pool_scale, norm_w, sgu_w, sgu_bias, batch, seq):
    m = proj.shape[0]
    tm = MIX_SUB * BLK
    ns = seq // tm
    row = lambda b, n: b * ns + n
    before = lambda b, n: jnp.maximum(row(b, n) * MIX_SUB - 1, 0)
    bias = jnp.asarray(_attention_bias())
    assert proj.dtype == BF16
    band = jnp.asarray(_pool_bands(), dtype=BF16)
    in_specs = [
        pl.BlockSpec(memory_space=pltpu.SMEM),
        pl.BlockSpec(bias.shape, lambda b, n: (0, 0, 0)),
        pl.BlockSpec(band.shape, lambda b, n: (0, 0, 0)),
        pl.BlockSpec((tm, D_BRANCH), lambda b, n: (row(b, n), P_Q // D_BRANCH)),
        pl.BlockSpec((tm, 256), lambda b, n: (row(b, n), P_KV // 256)),
        pl.BlockSpec((BLK, 256), lambda b, n: (before(b, n), P_KV // 256)),
        pl.BlockSpec((tm, D_BRANCH), lambda b, n: (row(b, n), P_POOL // D_BRANCH)),
        pl.BlockSpec((BLK, D_BRANCH), lambda b, n: (before(b, n), P_POOL // D_BRANCH)),
        pl.BlockSpec((tm, 2 * D_BRANCH), lambda b, n: (row(b, n), P_SGU // (2 * D_BRANCH))),
        pl.BlockSpec((4, BLK, BLK), lambda b, n: (0, 0, 0)),
        pl.BlockSpec((1, D_BRANCH), lambda b, n: (0, 0)),
        pl.BlockSpec((1, D_BRANCH), lambda b, n: (0, 0)),
        pl.BlockSpec((4, BLK, BLK), lambda b, n: (0, 0, 0)),
        pl.BlockSpec((BLK, D_BRANCH), lambda b, n: (0, 0)),
    ]
    return pl.pallas_call(
        _mixers_kernel,
        grid=(batch, ns),
        in_specs=in_specs,
        out_specs=pl.BlockSpec((tm, 3 * D_BRANCH), lambda b, n: (row(b, n), 0)),
        out_shape=jax.ShapeDtypeStruct((m, 3 * D_BRANCH), ACT),
        scratch_shapes=[pltpu.VMEM((tm, 3 * D_BRANCH), F32)],
        compiler_params=pltpu.CompilerParams(
            dimension_semantics=("parallel", "parallel"), vmem_limit_bytes=_VMEM_LIMIT),
        name="mixers",
    )(sinks, bias, band, proj, proj, proj, proj, proj, proj, pool_w, pool_scale, norm_w, sgu_w,
      sgu_bias)


OUT_TM = 512
OUT_SUB = 256
OUT_STAGE = 512


def _outproj_kernel(ya_ref, yb_ref, g_ref, x_ref, w_hbm, pw_ref, o_ref, w_scr, stage, sem, *,
                    layer):
    @pl.when(pl.program_id(0) == 0)
    def _():
        def copy(idx):
            slot = idx % 2
            return pltpu.make_async_copy(w_hbm.at[layer, pl.ds(idx * OUT_STAGE, OUT_STAGE), :],
                                         stage.at[slot], sem.at[slot])

        n_pieces = D_MODEL // OUT_STAGE
        copy(0).start()
        for idx in range(n_pieces):
            if idx + 1 < n_pieces:
                copy(idx + 1).start()
            copy(idx).wait()
            w_scr[idx * OUT_STAGE:(idx + 1) * OUT_STAGE, :] = stage[idx % 2].astype(BF16)

    for r0 in range(0, OUT_TM, OUT_SUB):
        rows = slice(r0, r0 + OUT_SUB)
        g = g_ref[rows, :].astype(F32)
        y = jnp.concatenate([ya_ref[rows, :], yb_ref[rows, :]], axis=1).astype(F32)
        gated = (y * (g * jax.nn.sigmoid(g))).astype(BF16)
        acc = jnp.dot(gated, w_scr[...], preferred_element_type=F32)
        ms = jnp.mean(acc * acc, axis=-1, keepdims=True)
        o_ref[rows, :] = x_ref[rows, :] + acc * lax.rsqrt(ms + NORM_EPS) * pw_ref[...]


def _outproj(ya, ybcd, proj, x2d, w_out, layer, post_w):
    m = x2d.shape[0]
    return pl.pallas_call(
        functools.partial(_outproj_kernel, layer=layer),
        grid=(m // OUT_TM,),
        in_specs=[
            pl.BlockSpec((OUT_TM, D_BRANCH), lambda i: (i, 0)),
            pl.BlockSpec((OUT_TM, 3 * D_BRANCH), lambda i: (i, 0)),
            pl.BlockSpec((OUT_TM, D_MODEL), lambda i: (i, P_GATE // D_MODEL)),
            pl.BlockSpec((OUT_TM, D_MODEL), lambda i: (i, 0)),
            pl.BlockSpec(memory_space=pl.ANY),
            pl.BlockSpec((1, D_MODEL), lambda i: (0, 0)),
        ],
        out_specs=pl.BlockSpec((OUT_TM, D_MODEL), lambda i: (i, 0)),
        out_shape=jax.ShapeDtypeStruct((m, D_MODEL), F32),
        scratch_shapes=[
            pltpu.VMEM((D_MODEL, D_MODEL), BF16),
            pltpu.VMEM((2, OUT_STAGE, D_MODEL), F32),
            pltpu.SemaphoreType.DMA((2,)),
        ],
        compiler_params=pltpu.CompilerParams(
            dimension_semantics=("arbitrary",), vmem_limit_bytes=_VMEM_LIMIT),
        name="outproj",
    )(ya, ybcd, proj, x2d, w_out, post_w)


def kernel(x, pre_norm_w, post_norm_w, w_in, shift_mu, rwkv_w0, rwkv_w_up, rwkv_a0, rwkv_a_up,
           rwkv_k_k, rwkv_k_a, rwkv_r_k, rwkv_ln_w, rwkv_ln_b, attn_sinks, pool_w, pool_scale,
           sgu_norm_w, sgu_w, sgu_b, w_out):
    batch, seq, _ = x.shape
    assert x.shape == (batch, seq, D_MODEL) and seq % (MIX_SUB * BLK) == 0 and seq % RW_STEP == 0
    m = batch * seq
    head_id = np.arange(D_BRANCH // 2) // HEAD
    bd = jnp.asarray((head_id[:, None] == head_id[None, :]).astype(np.float32), dtype=BF16)
    t_id = np.arange(RW_TILE)
    tril = jnp.asarray(((t_id[:, None] >= t_id[None, :])
                        & (t_id[:, None] // RW_CHUNK == t_id[None, :] // RW_CHUNK)).astype(np.float32),
                       dtype=BF16)
    row_vec = lambda a: a.reshape(1, -1)

    h = x.reshape(m, D_MODEL)
    for l in range(DEPTH):
        mu = shift_mu[l]
        mu_p = jnp.concatenate([mu[:3 * D_BRANCH], jnp.zeros((P_LORA - P_KV,), mu.dtype),
                                mu[3 * D_BRANCH:], jnp.zeros((P_WIDTH - D_IN,), mu.dtype)])
        proj = _inproj(h, row_vec(pre_norm_w[l]), row_vec(mu_p), w_in, l, seq)

        vecs = [row_vec(rwkv_w0[l]), row_vec(rwkv_a0[l]), row_vec(rwkv_k_k[l]),
                row_vec(rwkv_k_a[l]), row_vec(rwkv_r_k[l]), row_vec(rwkv_ln_w[l]),
                row_vec(rwkv_ln_b[l])]
        ya = _rwkv(proj, vecs, rwkv_w_up[l], rwkv_a_up[l], bd, tril, batch, seq)

        sgu_bias = jnp.broadcast_to(sgu_b[l].T[:, :, None], (BLK, 4, BLK)).reshape(BLK, D_BRANCH)
        ybcd = _mixers(proj, attn_sinks[l], pool_w[l], row_vec(pool_scale[l]),
                       row_vec(sgu_norm_w[l]), sgu_w[l], sgu_bias, batch, seq)

        h = _outproj(ya, ybcd, proj, h, w_out, l, row_vec(post_norm_w[l]))
    return h.reshape(batch, seq, D_MODEL)
```

```python
import functools
import itertools

import jax
import jax.numpy as jnp
import numpy as np
from jax import lax
from jax.experimental import pallas as pl
from jax.experimental.pallas import tpu as pltpu

F32 = jnp.float32
BF16 = jnp.bfloat16
ACT = BF16

D_MODEL = 2048
DEPTH = 2
D_BRANCH = 512
HEAD = 64
N_HEADS = D_BRANCH // HEAD
LORA = 64
KV_HEADS = 2
ATT_GROUP = N_HEADS // KV_HEADS
BLK = 128
MIX_SUB = 4
POOL_WINDOWS = (2, 4, 8, 16)
NEG_INF = -1e30
NORM_EPS = 1e-6
LN_EPS = 1e-5
GN_EPS = 64e-5

A_COLS = 3 * D_BRANCH + 2 * LORA
B_COLS = D_BRANCH + 2 * KV_HEADS * HEAD
OFF_B = A_COLS
OFF_C = OFF_B + B_COLS
OFF_D = OFF_C + D_BRANCH
OFF_G = OFF_D + 2 * D_BRANCH
D_IN = OFF_G + D_MODEL

P_GATE = 0
P_SGU = 2048
P_POOL = 3072
P_Q = 3584
P_R = 4096
P_K = 4608
P_V = 5120
P_KV = 5632
P_LORA = 5888
P_WIDTH = 6144

RW_CHUNK = 64
RW_TILE = 256
RW_STEP = 2 * RW_TILE

_VMEM_LIMIT = 56 * 1024 * 1024


IN_TM = 512
IN_TN = 1024
IN_STAGE = 256
_W_RUNS = ((OFF_G, D_MODEL, P_GATE), (OFF_D, 2 * D_BRANCH, P_SGU), (OFF_C, D_BRANCH, P_POOL),
           (OFF_B, D_BRANCH, P_Q), (0, 3 * D_BRANCH, P_R),
           (OFF_B + D_BRANCH, 2 * KV_HEADS * HEAD, P_KV), (3 * D_BRANCH, 2 * LORA, P_LORA))
_W_PIECES = tuple((src + o, min(IN_STAGE, width - o), dst + o)
                  for src, width, dst in _W_RUNS for o in range(0, width, IN_STAGE))
assert sum(p[1] for p in _W_PIECES) == D_IN and P_LORA + 2 * LORA == D_IN


IN_SLOTS = 2


def _load_weight(w_hbm, layer, w_scr, stage, sem):
    def copy(idx):
        src, width, _ = _W_PIECES[idx]
        slot = idx % IN_SLOTS
        return pltpu.make_async_copy(w_hbm.at[layer, :, pl.ds(src, width)],
                                     stage.at[slot, :, pl.ds(0, width)], sem.at[slot])

    for idx in range(IN_SLOTS - 1):
        copy(idx).start()
    for idx, (_, width, dst) in enumerate(_W_PIECES):
        if idx + IN_SLOTS - 1 < len(_W_PIECES):
            copy(idx + IN_SLOTS - 1).start()
        copy(idx).wait()
        w_scr[:, dst:dst + width] = stage[idx % IN_SLOTS, :, 0:width].astype(BF16)
    w_scr[:, D_IN:] = jnp.zeros((D_MODEL, P_WIDTH - D_IN), BF16)


def _inproj_kernel(x_ref, pw_ref, mu_ref, w_hbm, o_ref, w_scr, stage, sem, h_scr, carry_scr, *,
                   layer, tiles_per_seq):
    i = pl.program_id(0)

    @pl.when(i == 0)
    def _():
        carry_scr[...] = jnp.zeros_like(carry_scr)
        _load_weight(w_hbm, layer, w_scr, stage, sem)

    for r0 in range(0, IN_TM, 256):
        x = x_ref[r0:r0 + 256, :]
        ms = jnp.mean(x * x, axis=-1, keepdims=True)
        h_scr[r0:r0 + 256, :] = (x * lax.rsqrt(ms + NORM_EPS) * pw_ref[...]).astype(BF16)
    row = lax.broadcasted_iota(jnp.int32, (IN_TM, 1), 0)
    starts_sequence = (i % tiles_per_seq) == 0
    for c0 in range(0, P_WIDTH, IN_TN):
        acc = jnp.dot(h_scr[...], w_scr[:, c0:c0 + IN_TN], preferred_element_type=F32)
        if c0 >= P_R:
            cs = slice(c0 - P_R, c0 - P_R + IN_TN)
            before = jnp.where(starts_sequence, 0.0, carry_scr[0:1, cs])
            prev = jnp.where(row == 0, before, pltpu.roll(acc, 1, axis=0))
            carry_scr[0:1, cs] = acc[IN_TM - 1:IN_TM, :]
            acc = acc + mu_ref[:, cs] * (prev - acc)
        o_ref[:, c0:c0 + IN_TN] = acc.astype(o_ref.dtype)


def _inproj(x2d, pre_w, mu_p, w_in, layer, seq):
    m = x2d.shape[0]
    assert P_R % IN_TN == 0 and seq % IN_TM == 0 and mu_p.shape == (1, P_WIDTH - P_R)
    return pl.pallas_call(
        functools.partial(_inproj_kernel, layer=layer, tiles_per_seq=seq // IN_TM),
        grid=(m // IN_TM,),
        in_specs=[
            pl.BlockSpec((IN_TM, D_MODEL), lambda i: (i, 0)),
            pl.BlockSpec((1, D_MODEL), lambda i: (0, 0)),
            pl.BlockSpec((1, P_WIDTH - P_R), lambda i: (0, 0)),
            pl.BlockSpec(memory_space=pl.ANY),
        ],
        out_specs=pl.BlockSpec((IN_TM, P_WIDTH), lambda i: (i, 0)),
        out_shape=jax.ShapeDtypeStruct((m, P_WIDTH), ACT),
        scratch_shapes=[
            pltpu.VMEM((D_MODEL, P_WIDTH), BF16),
            pltpu.VMEM((IN_SLOTS, D_MODEL, IN_STAGE), F32),
            pltpu.SemaphoreType.DMA((IN_SLOTS,)),
            pltpu.VMEM((IN_TM, D_MODEL), BF16),
            pltpu.VMEM((8, P_WIDTH - P_R), F32),
        ],
        compiler_params=pltpu.CompilerParams(
            dimension_semantics=("arbitrary",), vmem_limit_bytes=_VMEM_LIMIT),
        name="inproj",
    )(x2d, pre_w, mu_p, w_in)


def _bdot(a, b):
    return jnp.dot(a.astype(BF16), b.astype(BF16), preferred_element_type=F32)


def _bdot_nt(a, b):
    return lax.dot_general(a.astype(BF16), b.astype(BF16), (((1,), (1,)), ((), ())),
                           preferred_element_type=F32)


def _split3(x):
    x1 = x.astype(BF16)
    r1 = x - x1.astype(F32)
    x2 = r1.astype(BF16)
    x3 = (r1 - x2.astype(F32)).astype(BF16)
    return x1, x2, x3


def _head_sums(xs, bd):
    rows = xs[0].shape[0]
    half = bd.shape[0]
    x = jnp.concatenate(xs, axis=0).astype(BF16)
    out = jnp.concatenate(
        [jnp.dot(x[:, :half], bd, preferred_element_type=F32),
         jnp.dot(x[:, half:], bd, preferred_element_type=F32)], axis=1)
    return [out[i * rows:(i + 1) * rows] for i in range(len(xs))]


def _mm(a16, b16):
    return jnp.dot(a16, b16, preferred_element_type=F32)


def _mm_nt(a16, b16):
    return lax.dot_general(a16, b16, (((1,), (1,)), ((), ())), preferred_element_type=F32)


def _block_diag(m16, bdm16):
    return jnp.concatenate([m16, m16], axis=0) * bdm16


def _unit_lower_inverse_many(lmats, eye, masks, bdm16):
    base_mask, base_bd16, level_bd16 = masks
    c = lmats[0].shape[0]
    l16s = [l.astype(BF16) for l in lmats]
    stack2 = lambda m16: jnp.concatenate([m16, m16], axis=0)
    lds = [jnp.where(base_mask, l, 0.0) for l in lmats]
    l2s = [_mm(ld.astype(BF16), stack2(l16) * base_bd16).astype(BF16)
           for ld, l16 in zip(lds, l16s)]
    yield
    xs = [eye + ld for ld in lds]
    both = [_mm(jnp.concatenate([x.astype(BF16), l2], axis=0), stack2(l2) * bdm16)
            for x, l2 in zip(xs, l2s)]
    xs = [x + b[:c] for x, b in zip(xs, both)]
    l4s = [b[c:].astype(BF16) for b in both]
    yield
    xs = [x + _mm(x.astype(BF16), stack2(l4) * bdm16) for x, l4 in zip(xs, l4s)]
    yield
    for lvl16 in level_bd16:
        ts = [_mm(x.astype(BF16), stack2(l16) * lvl16).astype(BF16) for x, l16 in zip(xs, l16s)]
        yield
        xs = [x + _mm(t, stack2(x.astype(BF16)) * bdm16) for x, t in zip(xs, ts)]
        yield
    return xs


_SET_BF16 = ("a", "b", "k", "r", "v", "b_e", "k_e")
_SET_NAMES = _SET_BF16 + ("g_all", "bonus")
_PREP_DELAY_BESIDE_A = 8
_PREP_DELAY_BESIDE_B = 4
_EXP_NEG_HALF = float(np.exp(-0.5))


def _rwkv_prep(z_refs, row0, w, dst, mxu_delay=0):
    tile, c = RW_TILE, RW_CHUNK
    rows = lambda z_ref: z_ref[row0:row0 + tile, :].astype(F32)

    lo = rows(z_refs[3])
    lora_w = _bdot(jnp.tanh(lo[:, :LORA]), w["wup"][...])
    lora_a = _bdot(lo[:, LORA:], w["aup"][...])
    yield
    r, k, v = rows(z_refs[0]), rows(z_refs[1]), rows(z_refs[2])
    logw = -_EXP_NEG_HALF * jax.nn.sigmoid(w["w0"][...] + lora_w)
    asig = jax.nn.sigmoid(w["a0"][...] + lora_a)
    logw_terms = _split3(logw)
    kk = k * w["kkw"][...]
    k2 = k * (1.0 + (asig - 1.0) * w["kaw"][...])
    sum_terms = [kk * kk, r * k2 * w["rkw"][...]]
    for _ in range(mxu_delay):
        yield
    tril = w["tril"][...]
    cum = sum(jnp.dot(tril, p, preferred_element_type=F32) for p in logw_terms)
    kk_ss, rk_sum = _head_sums(sum_terms, w["bd"][...])
    yield
    kk = kk * lax.rsqrt(jnp.maximum(kk_ss, 1e-24))
    dst["bonus"][...] = rk_sum * v
    for ck in range(tile // c):
        sl = slice(ck * c, (ck + 1) * c)
        lw, cm = logw[sl], cum[sl]
        total = cm[c - 1:c, :]
        g_inv = jnp.exp(-cm)
        g_all = jnp.exp(total)
        g_tail = g_all * g_inv
        b_c = kk[sl] * asig[sl]
        dst["a"][sl, :] = (-kk[sl] * jnp.exp(cm - lw)).astype(BF16)
        dst["b"][sl, :] = (b_c * g_inv).astype(BF16)
        dst["k"][sl, :] = (k2[sl] * g_inv).astype(BF16)
        dst["r"][sl, :] = (r[sl] * jnp.exp(cm)).astype(BF16)
        dst["v"][sl, :] = v[sl].astype(BF16)
        dst["b_e"][sl, :] = (b_c * g_tail).astype(BF16)
        dst["k_e"][sl, :] = (k2[sl] * g_tail).astype(BF16)
        dst["g_all"][ck:ck + 1, :] = g_all
    yield


def _rwkv_items():
    return [(ck, p) for ck in range(RW_TILE // RW_CHUNK) for p in range(N_HEADS // 2)]


def _rwkv_part(src, name, it):
    ck, p = it
    c, pw = RW_CHUNK, 2 * HEAD
    return src[name][ck * c:(ck + 1) * c, p * pw:(p + 1) * pw]


def _rwkv_main_a(src, consts, ctx):
    c, pw = RW_CHUNK, 2 * HEAD
    eye, masks, tri2, bdm4_16, bdm16, bdm = consts
    items = _rwkv_items()
    part = functools.partial(_rwkv_part, src)

    a2 = [part("a", it) for it in items]
    r2 = [part("r", it) for it in items]
    l_ab, m_rb16, ak16 = [], [], []
    for a, x, it in zip(a2, r2, items):
        b, k_ = part("b", it), part("k", it)
        pr = _mm_nt(jnp.concatenate([a, x], axis=0),
                    jnp.concatenate([b, b, k_, k_], axis=0) * bdm4_16)
        ab = jnp.where(tri2, pr[:, :pw], 0.0)
        l_ab.append(ab[:c])
        m_rb16.append(ab[c:].astype(BF16))
        ak16.append(jnp.where(tri2, pr[:, pw:], 0.0).astype(BF16))
    yield
    qy = [_mm(x, _block_diag(part("v", it), bdm16))
          for x, it in zip(ak16, items)]
    qv16 = [q[:c].astype(BF16) for q in qy]
    yv = [q[c:] for q in qy]
    vk = [jnp.where(bdm, lax.dot_general(part("v", it), part("k_e", it), (((0,), (0,)), ((), ())),
                                         preferred_element_type=F32), 0.0) for it in items]
    yield
    tinv = yield from _unit_lower_inverse_many(l_ab, eye, masks, bdm16)
    wu = [_mm(t.astype(BF16),
              jnp.concatenate([_block_diag(a, bdm16), _block_diag(q, bdm16)], axis=1))
          for t, a, q in zip(tinv, a2, qv16)]
    wa16 = [x[:, :pw].astype(BF16) for x in wu]
    uv = [x[:, pw:] for x in wu]
    uvt = [x.T for x in uv]
    war = [jnp.concatenate([x, y], axis=0) for x, y in zip(wa16, r2)]
    ctx.update(wa16=wa16, uv=uv, uvt=uvt, war=war, m_rb16=m_rb16, yv=yv, vk=vk)
    yield


def _rwkv_main_b(src, w, consts, ctx, s_scr, y_scr, o_ref, row0):
    tile, c, pw = RW_TILE, RW_CHUNK, 2 * HEAD
    bdm16, bdm = consts[4], consts[5]
    n_chunks, n_pairs = tile // c, N_HEADS // 2
    items = _rwkv_items()
    part = functools.partial(_rwkv_part, src)
    wa16, uv, uvt, war = ctx["wa16"], ctx["uv"], ctx["uvt"], ctx["war"]
    m_rb16, yv, vk = ctx["m_rb16"], ctx["yv"], ctx["vk"]

    state = [s_scr[p] for p in range(n_pairs)]
    for ck in range(n_chunks):
        idx = [ck * n_pairs + p for p in range(n_pairs)]
        s16 = [s.astype(BF16) for s in state]
        uy = [_mm_nt(war[i], s16[p]) for p, i in enumerate(idx)]
        ut = [_mm_nt(s16[p], wa16[i]) + uvt[i] for p, i in enumerate(idx)]
        yield
        for p, i in enumerate(idx):
            u16 = (uy[p][:c] + uv[i]).astype(BF16)
            y = uy[p][c:] + yv[i] + _mm(m_rb16[i], _block_diag(u16, bdm16))
            y_scr[ck * c:(ck + 1) * c, p * pw:(p + 1) * pw] = y
        state = [state[p] * src["g_all"][ck:ck + 1, p * pw:(p + 1) * pw] + vk[i]
                 + jnp.where(bdm, _mm(ut[p].astype(BF16), part("b_e", items[i])), 0.0)
                 for p, i in enumerate(idx)]
        yield
    for p in range(n_pairs):
        s_scr[p] = state[p]

    y = y_scr[...]
    bd = w["bd"][...]
    mu = _head_sums([y], bd)[0] * (1.0 / HEAD)
    d = y - mu
    var = _head_sums([d * d], bd)[0] * (1.0 / HEAD)
    out = d * lax.rsqrt(var + GN_EPS) * w["lnw"][...] + w["lnb"][...] + src["bonus"][...]
    o_ref[row0:row0 + tile, :] = out.astype(o_ref.dtype)
    yield


def _interleave(*streams):
    for _ in itertools.zip_longest(*streams):
        pass


_W_NAMES = ("w0", "wup", "a0", "aup", "kkw", "kaw", "rkw", "lnw", "lnb", "bd", "tril")


def _rwkv_kernel(*refs):
    cur, nxt = refs[0:4], refs[4:8]
    w = dict(zip(_W_NAMES, refs[8:8 + len(_W_NAMES)]))
    rest = refs[8 + len(_W_NAMES):]
    o_ref, s_scr, y_scr = rest[0], rest[1], rest[2]
    n_set = len(_SET_NAMES)
    tiles = RW_STEP // RW_TILE
    sets = [dict(zip(_SET_NAMES, rest[3 + t * n_set:3 + (t + 1) * n_set])) for t in range(tiles)]
    n = pl.program_id(1)

    @pl.when(n == 0)
    def _():
        s_scr[...] = jnp.zeros_like(s_scr)

    @pl.when((pl.program_id(0) == 0) & (n == 0))
    def _():
        _interleave(_rwkv_prep(cur, 0, w, sets[0]))

    c, pw = RW_CHUNK, 2 * HEAD
    ci = lax.broadcasted_iota(jnp.int32, (c, pw), 0)
    cj = lax.broadcasted_iota(jnp.int32, (c, pw), 1) % HEAD
    eye = (ci == cj).astype(F32)
    si = lax.broadcasted_iota(jnp.int32, (2 * c, pw), 0)
    sj = lax.broadcasted_iota(jnp.int32, (2 * c, pw), 1) % HEAD
    tri2 = ((si < c) & (si > sj)) | ((si >= c) & ((si - c) >= sj))
    bi = lax.broadcasted_iota(jnp.int32, (2 * pw, pw), 0)
    bj = lax.broadcasted_iota(jnp.int32, (2 * pw, pw), 1)
    bdm_bool = ((bi // HEAD) % 2) == (bj // HEAD)
    bdm4_16 = bdm_bool.astype(BF16)
    hi = lax.broadcasted_iota(jnp.int32, (pw, pw), 0)
    hj = lax.broadcasted_iota(jnp.int32, (pw, pw), 1)
    di, dj = hi % HEAD, hj % HEAD
    same_head = (hi // HEAD) == (hj // HEAD)
    near = lambda b: same_head & ((di // b) == (dj // b))
    ring = lambda b: same_head & ((di // (2 * b)) == (dj // (2 * b))) & ((di // b) != (dj // b))
    masks = ((ci // 8) == (cj // 8), near(8).astype(BF16),
             tuple(ring(b).astype(BF16) for b in (8, 16, 32)))
    consts = (eye, masks, tri2, bdm4_16, bdm4_16[:pw], bdm_bool[:pw])

    ctx = [{} for _ in range(tiles)]
    stage_a = lambda t: _rwkv_main_a(sets[t], consts, ctx[t])
    stage_b = lambda t: _rwkv_main_b(sets[t], w, consts, ctx[t], s_scr, y_scr, o_ref, t * RW_TILE)
    prep = lambda t, delay: _rwkv_prep(cur, t * RW_TILE, w, sets[t], delay)
    _interleave(prep(1, _PREP_DELAY_BESIDE_A), stage_a(0))
    for t in range(1, tiles):
        streams = [stage_b(t - 1), stage_a(t)]
        if t + 1 < tiles:
            streams.append(prep(t + 1, _PREP_DELAY_BESIDE_A))
        _interleave(*streams)
    _interleave(stage_b(tiles - 1), _rwkv_prep(nxt, 0, w, sets[0], _PREP_DELAY_BESIDE_B))


def _rwkv(proj, vecs, wup, aup, bd, tril, batch, seq):
    m = proj.shape[0]
    ns = seq // RW_STEP
    row = lambda b, n: b * ns + n
    nxt = lambda b, n: jnp.minimum(b * ns + n + 1, batch * ns - 1)
    vec = lambda width: pl.BlockSpec((1, width), lambda b, n: (0, 0))
    full = lambda a: pl.BlockSpec(a.shape, lambda b, n: (0, 0))

    def token_specs(row_fn):
        wide = lambda cb: pl.BlockSpec((RW_STEP, D_BRANCH), lambda b, n: (row_fn(b, n), cb))
        return [wide(P_R // D_BRANCH), wide(P_K // D_BRANCH), wide(P_V // D_BRANCH),
                pl.BlockSpec((RW_STEP, 2 * LORA), lambda b, n: (row_fn(b, n), P_LORA // (2 * LORA)))]

    in_specs = token_specs(row) + token_specs(nxt) + [
        vec(D_BRANCH), full(wup), vec(D_BRANCH), full(aup),
        vec(D_BRANCH), vec(D_BRANCH), vec(D_BRANCH), vec(D_BRANCH), vec(D_BRANCH),
        full(bd), full(tril),
    ]
    prep_set = ([pltpu.VMEM((RW_TILE, D_BRANCH), BF16) for _ in _SET_BF16]
                + [pltpu.VMEM((8, D_BRANCH), F32), pltpu.VMEM((RW_TILE, D_BRANCH), F32)])
    return pl.pallas_call(
        _rwkv_kernel,
        grid=(batch, ns),
        in_specs=in_specs,
        out_specs=pl.BlockSpec((RW_STEP, D_BRANCH), lambda b, n: (row(b, n), 0)),
        out_shape=jax.ShapeDtypeStruct((m, D_BRANCH), ACT),
        scratch_shapes=[
            pltpu.VMEM((N_HEADS // 2, 2 * HEAD, 2 * HEAD), F32),
            pltpu.VMEM((RW_TILE, D_BRANCH), F32),
        ] + prep_set * (RW_STEP // RW_TILE),
        compiler_params=pltpu.CompilerParams(
            dimension_semantics=("arbitrary", "arbitrary"), vmem_limit_bytes=_VMEM_LIMIT),
        name="rwkv",
    )(*([proj] * 8), vecs[0], wup, vecs[1], aup, *vecs[2:], bd, tril)


_SLOPES = tuple(2.0 ** (-8.0 * (h + 1) / N_HEADS) for h in range(N_HEADS))
_SQRT_HALF = float(np.sqrt(0.5))


def _attention_bias():
    t = np.arange(BLK)[:, None]
    s = np.arange(2 * BLK)[None, :]
    dist = t + BLK - s
    out = np.empty((2, N_HEADS * BLK, 2 * BLK), np.float32)
    for first in (0, 1):
        valid = (dist >= 0) & (dist < BLK) & ((s >= BLK) | (first == 1))
        for h in range(N_HEADS):
            out[first, h * BLK:(h + 1) * BLK] = np.where(valid, -_SLOPES[h] * dist, NEG_INF)
    return out


def _pool_bands():
    t = np.arange(BLK)[:, None] + BLK
    s = np.arange(2 * BLK)[None, :]
    return np.stack([((s <= t) & (s > t - w)).astype(np.float32) for w in POOL_WINDOWS])


def _mixers_kernel(sinks_ref, bias_ref, band_ref, q_ref, kvc_ref, kvp_ref, zc_ref, zcp_ref, zd_ref,
                   poolw_ref, pscale_ref, nw_ref, sw_ref, sb_ref, o_ref, y_scr):
    n = pl.program_id(1)
    rows = ATT_GROUP * BLK
    head_in_group = lax.broadcasted_iota(jnp.int32, (rows, 1), 0) // BLK

    def per_row(values):
        col = jnp.full((rows, 1), values[-1], F32)
        for j in range(ATT_GROUP - 2, -1, -1):
            col = jnp.where(head_in_group == j, values[j], col)
        return col

    sinks = [per_row([sinks_ref[h] for h in range(g * ATT_GROUP, (g + 1) * ATT_GROUP)])
             for g in range(KV_HEADS)]
    ri = lax.broadcasted_iota(jnp.int32, (BLK, BLK), 0)
    rj = lax.broadcasted_iota(jnp.int32, (BLK, BLK), 1)
    sgu_w16 = [jnp.where(ri >= rj, sw_ref[g], 0.0).astype(BF16) for g in range(4)]
    refs = (bias_ref, band_ref, q_ref, kvc_ref, kvp_ref, zc_ref, zcp_ref, zd_ref, poolw_ref,
            pscale_ref, nw_ref, sb_ref, y_scr)
    for sub in range(MIX_SUB):
        _mixers_block(sub, n, refs, sinks, sgu_w16)
    o_ref[...] = y_scr[...].astype(o_ref.dtype)


def _mixers_block(sub, n, refs, sinks, sgu_w16):
    (bias_ref, band_ref, q_ref, kvc_ref, kvp_ref, zc_ref, zcp_ref, zd_ref, poolw_ref,
     pscale_ref, nw_ref, sb_ref, y_scr) = refs
    rs = slice(sub * BLK, (sub + 1) * BLK)
    before = slice((sub - 1) * BLK, sub * BLK)
    rows = ATT_GROUP * BLK
    if sub == 0:
        kv_prev = kvp_ref[...]
        z_prev = jnp.where(n > 0, zcp_ref[...], 0.0)
        bias_of = lambda g: bias_ref[jnp.minimum(n, 1), g * rows:(g + 1) * rows, :]
    else:
        kv_prev = kvc_ref[before, :]
        z_prev = zc_ref[before, :]
        bias_of = lambda g: bias_ref[1, g * rows:(g + 1) * rows, :]

    q = q_ref[rs, :] * (HEAD ** -0.5)
    kv = jnp.concatenate([kv_prev, kvc_ref[rs, :]], axis=0)

    scores = []
    for g in range(KV_HEADS):
        qg = jnp.concatenate(
            [q[:, (g * ATT_GROUP + j) * HEAD:(g * ATT_GROUP + j + 1) * HEAD]
             for j in range(ATT_GROUP)], axis=0)
        scores.append(_bdot_nt(qg, kv[:, g * HEAD:(g + 1) * HEAD]))

    zfull = jnp.concatenate([z_prev, zc_ref[rs, :]], axis=0)
    pos = (n * MIX_SUB + sub) * BLK + lax.broadcasted_iota(jnp.int32, (BLK, 1), 0) + 1
    sums = [jnp.dot(band_ref[g], zfull[:, g * BLK:(g + 1) * BLK], preferred_element_type=F32)
            for g in range(len(POOL_WINDOWS))]

    zd = zd_ref[rs, :].astype(F32)
    gz = 0.5 * zd * (1.0 + lax.erf(zd * _SQRT_HALF))
    u = gz[:, :D_BRANCH]
    vv = gz[:, D_BRANCH:]
    mu = jnp.mean(vv, axis=-1, keepdims=True)
    dv = vv - mu
    var = jnp.mean(dv * dv, axis=-1, keepdims=True)
    vn = dv * lax.rsqrt(var + LN_EPS) * nw_ref[...]
    for g in range(4):
        gs = slice(g * BLK, (g + 1) * BLK)
        sg = _bdot(sgu_w16[g], vn[:, gs]) + sb_ref[:, gs]
        y_scr[rs, 2 * D_BRANCH + g * BLK:2 * D_BRANCH + (g + 1) * BLK] = u[:, gs] * sg

    for g, w in enumerate(POOL_WINDOWS):
        gs = slice(g * BLK, (g + 1) * BLK)
        cnt = jnp.minimum(pos, w).astype(F32)
        pooled = sums[g] / cnt - zc_ref[rs, gs].astype(F32)
        yg = _bdot(pooled, poolw_ref[g]) * pscale_ref[:, gs]
        y_scr[rs, D_BRANCH + g * BLK:D_BRANCH + (g + 1) * BLK] = yg

    probs, dens = [], []
    for g in range(KV_HEADS):
        s = scores[g] + bias_of(g)
        mx = jnp.maximum(jnp.max(s, axis=-1, keepdims=True), sinks[g])
        p = jnp.exp(s - mx)
        probs.append(p)
        dens.append(jnp.sum(p, axis=-1, keepdims=True) + jnp.exp(sinks[g] - mx))
    for g in range(KV_HEADS):
        vg = kv[:, (KV_HEADS + g) * HEAD:(KV_HEADS + g + 1) * HEAD]
        og = _bdot(probs[g], vg) / dens[g]
        for j in range(ATT_GROUP):
            h = g * ATT_GROUP + j
            y_scr[rs, h * HEAD:(h + 1) * HEAD] = og[j * BLK:(j + 1) * BLK, :]


def _mixers(proj, sinks, pool_w, pool_scale, norm_w, sgu_w, sgu_bias, batch, seq):
    m = proj.shape[0]
    tm = MIX_SUB * BLK
    ns = seq // tm
    row = lambda b, n: b * ns + n
    before = lambda b, n: jnp.maximum(row(b, n) * MIX_SUB - 1, 0)
    bias = jnp.asarray(_attention_bias())
    assert proj.dtype == BF16
    band = jnp.asarray(_pool_bands(), dtype=BF16)
    in_specs = [
        pl.BlockSpec(memory_space=pltpu.SMEM),
        pl.BlockSpec(bias.shape, lambda b, n: (0, 0, 0)),
        pl.BlockSpec(band.shape, lambda b, n: (0, 0, 0)),
        pl.BlockSpec((tm, D_BRANCH), lambda b, n: (row(b, n), P_Q // D_BRANCH)),
        pl.BlockSpec((tm, 256), lambda b, n: (row(b, n), P_KV // 256)),
        pl.BlockSpec((BLK, 256), lambda b, n: (before(b, n), P_KV // 256)),
        pl.BlockSpec((tm, D_BRANCH), lambda b, n: (row(b, n), P_POOL // D_BRANCH)),
        pl.BlockSpec((BLK, D_BRANCH), lambda b, n: (before(b, n), P_POOL // D_BRANCH)),
        pl.BlockSpec((tm, 2 * D_BRANCH), lambda b, n: (row(b, n), P_SGU // (2 * D_BRANCH))),
        pl.BlockSpec((4, BLK, BLK), lambda b, n: (0, 0, 0)),
        pl.BlockSpec((1, D_BRANCH), lambda b, n: (0, 0)),
        pl.BlockSpec((1, D_BRANCH), lambda b, n: (0, 0)),
        pl.BlockSpec((4, BLK, BLK), lambda b, n: (0, 0, 0)),
        pl.BlockSpec((BLK, D_BRANCH), lambda b, n: (0, 0)),
    ]
    return pl.pallas_call(
        _mixers_kernel,
        grid=(batch, ns),
        in_specs=in_specs,
        out_specs=pl.BlockSpec((tm, 3 * D_BRANCH), lambda b, n: (row(b, n), 0)),
        out_shape=jax.ShapeDtypeStruct((m, 3 * D_BRANCH), ACT),
        scratch_shapes=[pltpu.VMEM((tm, 3 * D_BRANCH), F32)],
        compiler_params=pltpu.CompilerParams(
            dimension_semantics=("parallel", "parallel"), vmem_limit_bytes=_VMEM_LIMIT),
        name="mixers",
    )(sinks, bias, band, proj, proj, proj, proj, proj, proj, pool_w, pool_scale, norm_w, sgu_w,
      sgu_bias)


OUT_TM = 512
OUT_SUB = 256
OUT_STAGE = 512


def _outproj_kernel(ya_ref, yb_ref, g_ref, x_ref, w_hbm, pw_ref, o_ref, w_scr, stage, sem, *,
                    layer):
    @pl.when(pl.program_id(0) == 0)
    def _():
        def copy(idx):
            slot = idx % 2
            return pltpu.make_async_copy(w_hbm.at[layer, pl.ds(idx * OUT_STAGE, OUT_STAGE), :],
                                         stage.at[slot], sem.at[slot])

        n_pieces = D_MODEL // OUT_STAGE
        copy(0).start()
        for idx in range(n_pieces):
            if idx + 1 < n_pieces:
                copy(idx + 1).start()
            copy(idx).wait()
            w_scr[idx * OUT_STAGE:(idx + 1) * OUT_STAGE, :] = stage[idx % 2].astype(BF16)

    for r0 in range(0, OUT_TM, OUT_SUB):
        rows = slice(r0, r0 + OUT_SUB)
        g = g_ref[rows, :].astype(F32)
        y = jnp.concatenate([ya_ref[rows, :], yb_ref[rows, :]], axis=1).astype(F32)
        gated = (y * (g * jax.nn.sigmoid(g))).astype(BF16)
        acc = jnp.dot(gated, w_scr[...], preferred_element_type=F32)
        ms = jnp.mean(acc * acc, axis=-1, keepdims=True)
        o_ref[rows, :] = x_ref[rows, :] + acc * lax.rsqrt(ms + NORM_EPS) * pw_ref[...]


def _outproj(ya, ybcd, proj, x2d, w_out, layer, post_w):
    m = x2d.shape[0]
    return pl.pallas_call(
        functools.partial(_outproj_kernel, layer=layer),
        grid=(m // OUT_TM,),
        in_specs=[
            pl.BlockSpec((OUT_TM, D_BRANCH), lambda i: (i, 0)),
            pl.BlockSpec((OUT_TM, 3 * D_BRANCH), lambda i: (i, 0)),
            pl.BlockSpec((OUT_TM, D_MODEL), lambda i: (i, P_GATE // D_MODEL)),
            pl.BlockSpec((OUT_TM, D_MODEL), lambda i: (i, 0)),
            pl.BlockSpec(memory_space=pl.ANY),
            pl.BlockSpec((1, D_MODEL), lambda i: (0, 0)),
        ],
        out_specs=pl.BlockSpec((OUT_TM, D_MODEL), lambda i: (i, 0)),
        out_shape=jax.ShapeDtypeStruct((m, D_MODEL), F32),
        scratch_shapes=[
            pltpu.VMEM((D_MODEL, D_MODEL), BF16),
            pltpu.VMEM((2, OUT_STAGE, D_MODEL), F32),
            pltpu.SemaphoreType.DMA((2,)),
        ],
        compiler_params=pltpu.CompilerParams(
            dimension_semantics=("arbitrary",), vmem_limit_bytes=_VMEM_LIMIT),
        name="outproj",
    )(ya, ybcd, proj, x2d, w_out, post_w)


def kernel(x, pre_norm_w, post_norm_w, w_in, shift_mu, rwkv_w0, rwkv_w_up, rwkv_a0, rwkv_a_up,
           rwkv_k_k, rwkv_k_a, rwkv_r_k, rwkv_ln_w, rwkv_ln_b, attn_sinks, pool_w, pool_scale,
           sgu_norm_w, sgu_w, sgu_b, w_out):
    batch, seq, _ = x.shape
    assert x.shape == (batch, seq, D_MODEL) and seq % (MIX_SUB * BLK) == 0 and seq % RW_STEP == 0
    m = batch * seq
    head_id = np.arange(D_BRANCH // 2) // HEAD
    bd = jnp.asarray((head_id[:, None] == head_id[None, :]).astype(np.float32), dtype=BF16)
    t_id = np.arange(RW_TILE)
    tril = jnp.asarray(((t_id[:, None] >= t_id[None, :])
                        & (t_id[:, None] // RW_CHUNK == t_id[None, :] // RW_CHUNK)).astype(np.float32),
                       dtype=BF16)
    row_vec = lambda a: a.reshape(1, -1)

    h = x.reshape(m, D_MODEL)
    for l in range(DEPTH):
        mu = shift_mu[l]
        mu_p = jnp.concatenate([mu[:3 * D_BRANCH], jnp.zeros((P_LORA - P_KV,), mu.dtype),
                                mu[3 * D_BRANCH:], jnp.zeros((P_WIDTH - D_IN,), mu.dtype)])
        proj = _inproj(h, row_vec(pre_norm_w[l]), row_vec(mu_p), w_in, l, seq)

        vecs = [row_vec(rwkv_w0[l]), row_vec(rwkv_a0[l]), row_vec(rwkv_k_k[l]),
                row_vec(rwkv_k_a[l]), row_vec(rwkv_r_k[l]), row_vec(rwkv_ln_w[l]),
                row_vec(rwkv_ln_b[l])]
        ya = _rwkv(proj, vecs, rwkv_w_up[l], rwkv_a_up[l], bd, tril, batch, seq)

        sgu_bias = jnp.broadcast_to(sgu_b[l].T[:, :, None], (BLK, 4, BLK)).reshape(BLK, D_BRANCH)
        ybcd = _mixers(proj, attn_sinks[l], pool_w[l], row_vec(pool_scale[l]),
                       row_vec(sgu_norm_w[l]), sgu_w[l], sgu_bias, batch, seq)

        h = _outproj(ya, ybcd, proj, h, w_out, l, row_vec(post_norm_w[l]))
    return h.reshape(batch, seq, D_MODEL)
```

```python
import functools
import itertools

import jax
import jax.numpy as jnp
import numpy as np
from jax import lax
from jax.experimental import pallas as pl
from jax.experimental.pallas import tpu as pltpu

F32 = jnp.float32
BF16 = jnp.bfloat16
ACT = BF16

D_MODEL = 2048
DEPTH = 2
D_BRANCH = 512
HEAD = 64
N_HEADS = D_BRANCH // HEAD
LORA = 64
KV_HEADS = 2
ATT_GROUP = N_HEADS // KV_HEADS
BLK = 128
MIX_SUB = 4
POOL_WINDOWS = (2, 4, 8, 16)
NEG_INF = -1e30
NORM_EPS = 1e-6
LN_EPS = 1e-5
GN_EPS = 64e-5

A_COLS = 3 * D_BRANCH + 2 * LORA
B_COLS = D_BRANCH + 2 * KV_HEADS * HEAD
OFF_B = A_COLS
OFF_C = OFF_B + B_COLS
OFF_D = OFF_C + D_BRANCH
OFF_G = OFF_D + 2 * D_BRANCH
D_IN = OFF_G + D_MODEL

P_GATE = 0
P_SGU = 2048
P_POOL = 3072
P_Q = 3584
P_R = 4096
P_K = 4608
P_V = 5120
P_KV = 5632
P_LORA = 5888
P_WIDTH = 6144

RW_CHUNK = 64
RW_TILE = 256
RW_STEP = 2 * RW_TILE

_VMEM_LIMIT = 56 * 1024 * 1024


IN_TM = 512
IN_TN = 1024
IN_STAGE = 256
IN_NORM_ROWS = 256
_W_RUNS = ((OFF_G, D_MODEL, P_GATE), (OFF_D, 2 * D_BRANCH, P_SGU), (OFF_C, D_BRANCH, P_POOL),
           (OFF_B, D_BRANCH, P_Q), (0, 3 * D_BRANCH, P_R),
           (OFF_B + D_BRANCH, 2 * KV_HEADS * HEAD, P_KV), (3 * D_BRANCH, 2 * LORA, P_LORA))
_W_PIECES = tuple((src + o, min(IN_STAGE, width - o), dst + o)
                  for src, width, dst in _W_RUNS for o in range(0, width, IN_STAGE))
assert sum(p[1] for p in _W_PIECES) == D_IN and P_LORA + 2 * LORA == D_IN


IN_SLOTS = 2


def _load_weight(w_hbm, layer, w_scr, stage, sem):
    def copy(idx):
        src, width, _ = _W_PIECES[idx]
        slot = idx % IN_SLOTS
        return pltpu.make_async_copy(w_hbm.at[layer, :, pl.ds(src, width)],
                                     stage.at[slot, :, pl.ds(0, width)], sem.at[slot])

    for idx in range(IN_SLOTS - 1):
        copy(idx).start()
    for idx, (_, width, dst) in enumerate(_W_PIECES):
        if idx + IN_SLOTS - 1 < len(_W_PIECES):
            copy(idx + IN_SLOTS - 1).start()
        copy(idx).wait()
        w_scr[:, dst:dst + width] = stage[idx % IN_SLOTS, :, 0:width].astype(BF16)
    w_scr[:, D_IN:] = jnp.zeros((D_MODEL, P_WIDTH - D_IN), BF16)


def _inproj_kernel(x_ref, pw_ref, mu_ref, w_hbm, o_ref, w_scr, stage, sem, h_scr, carry_scr, *,
                   layer, tiles_per_seq):
    i = pl.program_id(0)

    @pl.when(i == 0)
    def _():
        carry_scr[...] = jnp.zeros_like(carry_scr)
        _load_weight(w_hbm, layer, w_scr, stage, sem)

    for r0 in range(0, IN_TM, IN_NORM_ROWS):
        x = x_ref[r0:r0 + IN_NORM_ROWS, :]
        ms = jnp.mean(x * x, axis=-1, keepdims=True)
        h_scr[r0:r0 + IN_NORM_ROWS, :] = (x * lax.rsqrt(ms + NORM_EPS)
                                          * pw_ref[...]).astype(BF16)
    row = lax.broadcasted_iota(jnp.int32, (IN_TM, 1), 0)
    starts_sequence = (i % tiles_per_seq) == 0
    for c0 in range(0, P_WIDTH, IN_TN):
        acc = jnp.dot(h_scr[...], w_scr[:, c0:c0 + IN_TN], preferred_element_type=F32)
        if c0 >= P_R:
            cs = slice(c0 - P_R, c0 - P_R + IN_TN)
            before = jnp.where(starts_sequence, 0.0, carry_scr[0:1, cs])
            prev = jnp.where(row == 0, before, pltpu.roll(acc, 1, axis=0))
            carry_scr[0:1, cs] = acc[IN_TM - 1:IN_TM, :]
            acc = acc + mu_ref[:, cs] * (prev - acc)
        o_ref[:, c0:c0 + IN_TN] = acc.astype(o_ref.dtype)


def _inproj(x2d, pre_w, mu_p, w_in, layer, seq):
    m = x2d.shape[0]
    assert P_R % IN_TN == 0 and seq % IN_TM == 0 and mu_p.shape == (1, P_WIDTH - P_R)
    return pl.pallas_call(
        functools.partial(_inproj_kernel, layer=layer, tiles_per_seq=seq // IN_TM),
        grid=(m // IN_TM,),
        in_specs=[
            pl.BlockSpec((IN_TM, D_MODEL), lambda i: (i, 0)),
            pl.BlockSpec((1, D_MODEL), lambda i: (0, 0)),
            pl.BlockSpec((1, P_WIDTH - P_R), lambda i: (0, 0)),
            pl.BlockSpec(memory_space=pl.ANY),
        ],
        out_specs=pl.BlockSpec((IN_TM, P_WIDTH), lambda i: (i, 0)),
        out_shape=jax.ShapeDtypeStruct((m, P_WIDTH), ACT),
        scratch_shapes=[
            pltpu.VMEM((D_MODEL, P_WIDTH), BF16),
            pltpu.VMEM((IN_SLOTS, D_MODEL, IN_STAGE), F32),
            pltpu.SemaphoreType.DMA((IN_SLOTS,)),
            pltpu.VMEM((IN_TM, D_MODEL), BF16),
            pltpu.VMEM((8, P_WIDTH - P_R), F32),
        ],
        compiler_params=pltpu.CompilerParams(
            dimension_semantics=("arbitrary",), vmem_limit_bytes=_VMEM_LIMIT),
        name="inproj",
    )(x2d, pre_w, mu_p, w_in)


def _bdot(a, b):
    return jnp.dot(a.astype(BF16), b.astype(BF16), preferred_element_type=F32)


def _bdot_nt(a, b):
    return lax.dot_general(a.astype(BF16), b.astype(BF16), (((1,), (1,)), ((), ())),
                           preferred_element_type=F32)


def _split3(x):
    x1 = x.astype(BF16)
    r1 = x - x1.astype(F32)
    x2 = r1.astype(BF16)
    x3 = (r1 - x2.astype(F32)).astype(BF16)
    return x1, x2, x3


def _head_sums(xs, bd):
    rows = xs[0].shape[0]
    half = bd.shape[0]
    x = jnp.concatenate(xs, axis=0).astype(BF16)
    out = jnp.concatenate(
        [jnp.dot(x[:, :half], bd, preferred_element_type=F32),
         jnp.dot(x[:, half:], bd, preferred_element_type=F32)], axis=1)
    return [out[i * rows:(i + 1) * rows] for i in range(len(xs))]


def _mm(a16, b16):
    return jnp.dot(a16, b16, preferred_element_type=F32)


def _mm_nt(a16, b16):
    return lax.dot_general(a16, b16, (((1,), (1,)), ((), ())), preferred_element_type=F32)


def _block_diag(m16, bdm16):
    return jnp.concatenate([m16, m16], axis=0) * bdm16


def _unit_lower_inverse_many(lmats, eye, masks, bdm16):
    base_mask, base_bd16, level_bd16 = masks
    c = lmats[0].shape[0]
    l16s = [l.astype(BF16) for l in lmats]
    stack2 = lambda m16: jnp.concatenate([m16, m16], axis=0)
    lds = [jnp.where(base_mask, l, 0.0) for l in lmats]
    l2s = [_mm(ld.astype(BF16), stack2(l16) * base_bd16).astype(BF16)
           for ld, l16 in zip(lds, l16s)]
    yield
    xs = [eye + ld for ld in lds]
    both = [_mm(jnp.concatenate([x.astype(BF16), l2], axis=0), stack2(l2) * bdm16)
            for x, l2 in zip(xs, l2s)]
    xs = [x + b[:c] for x, b in zip(xs, both)]
    l4s = [b[c:].astype(BF16) for b in both]
    yield
    xs = [x + _mm(x.astype(BF16), stack2(l4) * bdm16) for x, l4 in zip(xs, l4s)]
    yield
    for lvl16 in level_bd16:
        ts = [_mm(x.astype(BF16), stack2(l16) * lvl16).astype(BF16) for x, l16 in zip(xs, l16s)]
        yield
        xs = [x + _mm(t, stack2(x.astype(BF16)) * bdm16) for x, t in zip(xs, ts)]
        yield
    return xs


_SET_BF16 = ("a", "b", "k", "r", "v", "b_e", "k_e")
_SET_NAMES = _SET_BF16 + ("g_all", "bonus")
_PREP_DELAY_BESIDE_A = 8
_PREP_DELAY_BESIDE_B = 4
_EXP_NEG_HALF = float(np.exp(-0.5))


def _rwkv_prep(z_refs, row0, w, dst, mxu_delay=0):
    tile, c = RW_TILE, RW_CHUNK
    rows = lambda z_ref: z_ref[row0:row0 + tile, :].astype(F32)

    lo = rows(z_refs[3])
    lora_w = _bdot(jnp.tanh(lo[:, :LORA]), w["wup"][...])
    lora_a = _bdot(lo[:, LORA:], w["aup"][...])
    yield
    r, k, v = rows(z_refs[0]), rows(z_refs[1]), rows(z_refs[2])
    logw = -_EXP_NEG_HALF * jax.nn.sigmoid(w["w0"][...] + lora_w)
    asig = jax.nn.sigmoid(w["a0"][...] + lora_a)
    logw_terms = _split3(logw)
    kk = k * w["kkw"][...]
    k2 = k * (1.0 + (asig - 1.0) * w["kaw"][...])
    sum_terms = [kk * kk, r * k2 * w["rkw"][...]]
    for _ in range(mxu_delay):
        yield
    tril = w["tril"][...]
    cum = sum(jnp.dot(tril, p, preferred_element_type=F32) for p in logw_terms)
    kk_ss, rk_sum = _head_sums(sum_terms, w["bd"][...])
    yield
    kk = kk * lax.rsqrt(jnp.maximum(kk_ss, 1e-24))
    dst["bonus"][...] = rk_sum * v
    for ck in range(tile // c):
        sl = slice(ck * c, (ck + 1) * c)
        lw, cm = logw[sl], cum[sl]
        total = cm[c - 1:c, :]
        g_inv = jnp.exp(-cm)
        g_all = jnp.exp(total)
        g_tail = g_all * g_inv
        b_c = kk[sl] * asig[sl]
        dst["a"][sl, :] = (-kk[sl] * jnp.exp(cm - lw)).astype(BF16)
        dst["b"][sl, :] = (b_c * g_inv).astype(BF16)
        dst["k"][sl, :] = (k2[sl] * g_inv).astype(BF16)
        dst["r"][sl, :] = (r[sl] * jnp.exp(cm)).astype(BF16)
        dst["v"][sl, :] = v[sl].astype(BF16)
        dst["b_e"][sl, :] = (b_c * g_tail).astype(BF16)
        dst["k_e"][sl, :] = (k2[sl] * g_tail).astype(BF16)
        dst["g_all"][ck:ck + 1, :] = g_all
    yield


def _rwkv_items():
    return [(ck, p) for ck in range(RW_TILE // RW_CHUNK) for p in range(N_HEADS // 2)]


def _rwkv_part(src, name, it):
    ck, p = it
    c, pw = RW_CHUNK, 2 * HEAD
    return src[name][ck * c:(ck + 1) * c, p * pw:(p + 1) * pw]


def _rwkv_main_a(src, consts, ctx):
    c, pw = RW_CHUNK, 2 * HEAD
    eye, masks, tri2, bdm4_16, bdm16, bdm = consts
    items = _rwkv_items()
    part = functools.partial(_rwkv_part, src)

    a2 = [part("a", it) for it in items]
    r2 = [part("r", it) for it in items]
    l_ab, m_rb16, ak16 = [], [], []
    for a, x, it in zip(a2, r2, items):
        b, k_ = part("b", it), part("k", it)
        pr = _mm_nt(jnp.concatenate([a, x], axis=0),
                    jnp.concatenate([b, b, k_, k_], axis=0) * bdm4_16)
        ab = jnp.where(tri2, pr[:, :pw], 0.0)
        l_ab.append(ab[:c])
        m_rb16.append(ab[c:].astype(BF16))
        ak16.append(jnp.where(tri2, pr[:, pw:], 0.0).astype(BF16))
    yield
    qy = [_mm(x, _block_diag(part("v", it), bdm16))
          for x, it in zip(ak16, items)]
    qv16 = [q[:c].astype(BF16) for q in qy]
    yv = [q[c:] for q in qy]
    vk = [jnp.where(bdm, lax.dot_general(part("v", it), part("k_e", it), (((0,), (0,)), ((), ())),
                                         preferred_element_type=F32), 0.0) for it in items]
    yield
    tinv = yield from _unit_lower_inverse_many(l_ab, eye, masks, bdm16)
    wu = [_mm(t.astype(BF16),
              jnp.concatenate([_block_diag(a, bdm16), _block_diag(q, bdm16)], axis=1))
          for t, a, q in zip(tinv, a2, qv16)]
    wa16 = [x[:, :pw].astype(BF16) for x in wu]
    uv = [x[:, pw:] for x in wu]
    uvt = [x.T for x in uv]
    war = [jnp.concatenate([x, y], axis=0) for x, y in zip(wa16, r2)]
    ctx.update(wa16=wa16, uv=uv, uvt=uvt, war=war, m_rb16=m_rb16, yv=yv, vk=vk)
    yield


def _rwkv_main_b(src, w, consts, ctx, s_scr, y_scr, o_ref, row0):
    tile, c, pw = RW_TILE, RW_CHUNK, 2 * HEAD
    bdm16, bdm = consts[4], consts[5]
    n_chunks, n_pairs = tile // c, N_HEADS // 2
    items = _rwkv_items()
    part = functools.partial(_rwkv_part, src)
    wa16, uv, uvt, war = ctx["wa16"], ctx["uv"], ctx["uvt"], ctx["war"]
    m_rb16, yv, vk = ctx["m_rb16"], ctx["yv"], ctx["vk"]

    state = [s_scr[p] for p in range(n_pairs)]
    for ck in range(n_chunks):
        idx = [ck * n_pairs + p for p in range(n_pairs)]
        s16 = [s.astype(BF16) for s in state]
        uy = [_mm_nt(war[i], s16[p]) for p, i in enumerate(idx)]
        ut = [_mm_nt(s16[p], wa16[i]) + uvt[i] for p, i in enumerate(idx)]
        yield
        for p, i in enumerate(idx):
            u16 = (uy[p][:c] + uv[i]).astype(BF16)
            y = uy[p][c:] + yv[i] + _mm(m_rb16[i], _block_diag(u16, bdm16))
            y_scr[ck * c:(ck + 1) * c, p * pw:(p + 1) * pw] = y
        state = [state[p] * src["g_all"][ck:ck + 1, p * pw:(p + 1) * pw] + vk[i]
                 + jnp.where(bdm, _mm(ut[p].astype(BF16), part("b_e", items[i])), 0.0)
                 for p, i in enumerate(idx)]
        yield
    for p in range(n_pairs):
        s_scr[p] = state[p]

    y = y_scr[...]
    bd = w["bd"][...]
    mu = _head_sums([y], bd)[0] * (1.0 / HEAD)
    d = y - mu
    var = _head_sums([d * d], bd)[0] * (1.0 / HEAD)
    out = d * lax.rsqrt(var + GN_EPS) * w["lnw"][...] + w["lnb"][...] + src["bonus"][...]
    o_ref[row0:row0 + tile, :] = out.astype(o_ref.dtype)
    yield


def _interleave(*streams):
    for _ in itertools.zip_longest(*streams):
        pass


_W_NAMES = ("w0", "wup", "a0", "aup", "kkw", "kaw", "rkw", "lnw", "lnb", "bd", "tril")


def _rwkv_kernel(*refs):
    cur, nxt = refs[0:4], refs[4:8]
    w = dict(zip(_W_NAMES, refs[8:8 + len(_W_NAMES)]))
    rest = refs[8 + len(_W_NAMES):]
    o_ref, s_scr, y_scr = rest[0], rest[1], rest[2]
    n_set = len(_SET_NAMES)
    tiles = RW_STEP // RW_TILE
    sets = [dict(zip(_SET_NAMES, rest[3 + t * n_set:3 + (t + 1) * n_set])) for t in range(tiles)]
    n = pl.program_id(1)

    @pl.when(n == 0)
    def _():
        s_scr[...] = jnp.zeros_like(s_scr)

    @pl.when((pl.program_id(0) == 0) & (n == 0))
    def _():
        _interleave(_rwkv_prep(cur, 0, w, sets[0]))

    c, pw = RW_CHUNK, 2 * HEAD
    ci = lax.broadcasted_iota(jnp.int32, (c, pw), 0)
    cj = lax.broadcasted_iota(jnp.int32, (c, pw), 1) % HEAD
    eye = (ci == cj).astype(F32)
    si = lax.broadcasted_iota(jnp.int32, (2 * c, pw), 0)
    sj = lax.broadcasted_iota(jnp.int32, (2 * c, pw), 1) % HEAD
    tri2 = ((si < c) & (si > sj)) | ((si >= c) & ((si - c) >= sj))
    bi = lax.broadcasted_iota(jnp.int32, (2 * pw, pw), 0)
    bj = lax.broadcasted_iota(jnp.int32, (2 * pw, pw), 1)
    bdm_bool = ((bi // HEAD) % 2) == (bj // HEAD)
    bdm4_16 = bdm_bool.astype(BF16)
    hi = lax.broadcasted_iota(jnp.int32, (pw, pw), 0)
    hj = lax.broadcasted_iota(jnp.int32, (pw, pw), 1)
    di, dj = hi % HEAD, hj % HEAD
    same_head = (hi // HEAD) == (hj // HEAD)
    near = lambda b: same_head & ((di // b) == (dj // b))
    ring = lambda b: same_head & ((di // (2 * b)) == (dj // (2 * b))) & ((di // b) != (dj // b))
    masks = ((ci // 8) == (cj // 8), near(8).astype(BF16),
             tuple(ring(b).astype(BF16) for b in (8, 16, 32)))
    consts = (eye, masks, tri2, bdm4_16, bdm4_16[:pw], bdm_bool[:pw])

    ctx = [{} for _ in range(tiles)]
    stage_a = lambda t: _rwkv_main_a(sets[t], consts, ctx[t])
    stage_b = lambda t: _rwkv_main_b(sets[t], w, consts, ctx[t], s_scr, y_scr, o_ref, t * RW_TILE)
    prep = lambda t, delay: _rwkv_prep(cur, t * RW_TILE, w, sets[t], delay)
    _interleave(prep(1, _PREP_DELAY_BESIDE_A), stage_a(0))
    for t in range(1, tiles):
        streams = [stage_b(t - 1), stage_a(t)]
        if t + 1 < tiles:
            streams.append(prep(t + 1, _PREP_DELAY_BESIDE_A))
        _interleave(*streams)
    _interleave(stage_b(tiles - 1), _rwkv_prep(nxt, 0, w, sets[0], _PREP_DELAY_BESIDE_B))


def _rwkv(proj, vecs, wup, aup, bd, tril, batch, seq):
    m = proj.shape[0]
    ns = seq // RW_STEP
    row = lambda b, n: b * ns + n
    nxt = lambda b, n: jnp.minimum(b * ns + n + 1, batch * ns - 1)
    vec = lambda width: pl.BlockSpec((1, width), lambda b, n: (0, 0))
    full = lambda a: pl.BlockSpec(a.shape, lambda b, n: (0, 0))

    def token_specs(row_fn):
        wide = lambda cb: pl.BlockSpec((RW_STEP, D_BRANCH), lambda b, n: (row_fn(b, n), cb))
        return [wide(P_R // D_BRANCH), wide(P_K // D_BRANCH), wide(P_V // D_BRANCH),
                pl.BlockSpec((RW_STEP, 2 * LORA), lambda b, n: (row_fn(b, n), P_LORA // (2 * LORA)))]

    in_specs = token_specs(row) + token_specs(nxt) + [
        vec(D_BRANCH), full(wup), vec(D_BRANCH), full(aup),
        vec(D_BRANCH), vec(D_BRANCH), vec(D_BRANCH), vec(D_BRANCH), vec(D_BRANCH),
        full(bd), full(tril),
    ]
    prep_set = ([pltpu.VMEM((RW_TILE, D_BRANCH), BF16) for _ in _SET_BF16]
                + [pltpu.VMEM((8, D_BRANCH), F32), pltpu.VMEM((RW_TILE, D_BRANCH), F32)])
    return pl.pallas_call(
        _rwkv_kernel,
        grid=(batch, ns),
        in_specs=in_specs,
        out_specs=pl.BlockSpec((RW_STEP, D_BRANCH), lambda b, n: (row(b, n), 0)),
        out_shape=jax.ShapeDtypeStruct((m, D_BRANCH), ACT),
        scratch_shapes=[
            pltpu.VMEM((N_HEADS // 2, 2 * HEAD, 2 * HEAD), F32),
            pltpu.VMEM((RW_TILE, D_BRANCH), F32),
        ] + prep_set * (RW_STEP // RW_TILE),
        compiler_params=pltpu.CompilerParams(
            dimension_semantics=("arbitrary", "arbitrary"), vmem_limit_bytes=_VMEM_LIMIT),
        name="rwkv",
    )(*([proj] * 8), vecs[0], wup, vecs[1], aup, *vecs[2:], bd, tril)


_SLOPES = tuple(2.0 ** (-8.0 * (h + 1) / N_HEADS) for h in range(N_HEADS))
_SQRT_HALF = float(np.sqrt(0.5))


def _attention_bias():
    t = np.arange(BLK)[:, None]
    s = np.arange(2 * BLK)[None, :]
    dist = t + BLK - s
    out = np.empty((2, N_HEADS * BLK, 2 * BLK), np.float32)
    for first in (0, 1):
        valid = (dist >= 0) & (dist < BLK) & ((s >= BLK) | (first == 1))
        for h in range(N_HEADS):
            out[first, h * BLK:(h + 1) * BLK] = np.where(valid, -_SLOPES[h] * dist, NEG_INF)
    return out


def _pool_bands():
    t = np.arange(BLK)[:, None] + BLK
    s = np.arange(2 * BLK)[None, :]
    return np.stack([((s <= t) & (s > t - w)).astype(np.float32) for w in POOL_WINDOWS])


def _mixers_kernel(sinks_ref, bias_ref, band_ref, q_ref, kvc_ref, kvp_ref, zc_ref, zcp_ref, zd_ref,
                   poolw_ref, pscale_ref, nw_ref, sw_ref, sb_ref, o_ref, y_scr):
    n = pl.program_id(1)
    rows = ATT_GROUP * BLK
    head_in_group = lax.broadcasted_iota(jnp.int32, (rows, 1), 0) // BLK

    def per_row(values):
        col = jnp.full((rows, 1), values[-1], F32)
        for j in range(ATT_GROUP - 2, -1, -1):
            col = jnp.where(head_in_group == j, values[j], col)
        return col

    sinks = [per_row([sinks_ref[h] for h in range(g * ATT_GROUP, (g + 1) * ATT_GROUP)])
             for g in range(KV_HEADS)]
    ri = lax.broadcasted_iota(jnp.int32, (BLK, BLK), 0)
    rj = lax.broadcasted_iota(jnp.int32, (BLK, BLK), 1)
    sgu_w16 = [jnp.where(ri >= rj, sw_ref[g], 0.0).astype(BF16) for g in range(4)]
    refs = (bias_ref, band_ref, q_ref, kvc_ref, kvp_ref, zc_ref, zcp_ref, zd_ref, poolw_ref,
            pscale_ref, nw_ref, sb_ref, y_scr)
    for sub in range(MIX_SUB):
        _mixers_block(sub, n, refs, sinks, sgu_w16)
    o_ref[...] = y_scr[...].astype(o_ref.dtype)


def _mixers_block(sub, n, refs, sinks, sgu_w16):
    (bias_ref, band_ref, q_ref, kvc_ref, kvp_ref, zc_ref, zcp_ref, zd_ref, poolw_ref,
     pscale_ref, nw_ref, sb_ref, y_scr) = refs
    rs = slice(sub * BLK, (sub + 1) * BLK)
    before = slice((sub - 1) * BLK, sub * BLK)
    rows = ATT_GROUP * BLK
    if sub == 0:
        kv_prev = kvp_ref[...]
        z_prev = jnp.where(n > 0, zcp_ref[...], 0.0)
        bias_of = lambda g: bias_ref[jnp.minimum(n, 1), g * rows:(g + 1) * rows, :]
    else:
        kv_prev = kvc_ref[before, :]
        z_prev = zc_ref[before, :]
        bias_of = lambda g: bias_ref[1, g * rows:(g + 1) * rows, :]

    q = q_ref[rs, :] * (HEAD ** -0.5)
    kv = jnp.concatenate([kv_prev, kvc_ref[rs, :]], axis=0)

    scores = []
    for g in range(KV_HEADS):
        qg = jnp.concatenate(
            [q[:, (g * ATT_GROUP + j) * HEAD:(g * ATT_GROUP + j + 1) * HEAD]
             for j in range(ATT_GROUP)], axis=0)
        scores.append(_bdot_nt(qg, kv[:, g * HEAD:(g + 1) * HEAD]))

    zfull = jnp.concatenate([z_prev, zc_ref[rs, :]], axis=0)
    pos = (n * MIX_SUB + sub) * BLK + lax.broadcasted_iota(jnp.int32, (BLK, 1), 0) + 1
    sums = [jnp.dot(band_ref[g], zfull[:, g * BLK:(g + 1) * BLK], preferred_element_type=F32)
            for g in range(len(POOL_WINDOWS))]

    zd = zd_ref[rs, :].astype(F32)
    gz = 0.5 * zd * (1.0 + lax.erf(zd * _SQRT_HALF))
    u = gz[:, :D_BRANCH]
    vv = gz[:, D_BRANCH:]
    mu = jnp.mean(vv, axis=-1, keepdims=True)
    dv = vv - mu
    var = jnp.mean(dv * dv, axis=-1, keepdims=True)
    vn = dv * lax.rsqrt(var + LN_EPS) * nw_ref[...]
    for g in range(4):
        gs = slice(g * BLK, (g + 1) * BLK)
        sg = _bdot(sgu_w16[g], vn[:, gs]) + sb_ref[:, gs]
        y_scr[rs, 2 * D_BRANCH + g * BLK:2 * D_BRANCH + (g + 1) * BLK] = u[:, gs] * sg

    for g, w in enumerate(POOL_WINDOWS):
        gs = slice(g * BLK, (g + 1) * BLK)
        cnt = jnp.minimum(pos, w).astype(F32)
        pooled = sums[g] / cnt - zc_ref[rs, gs].astype(F32)
        yg = _bdot(pooled, poolw_ref[g]) * pscale_ref[:, gs]
        y_scr[rs, D_BRANCH + g * BLK:D_BRANCH + (g + 1) * BLK] = yg

    probs, dens = [], []
    for g in range(KV_HEADS):
        s = scores[g] + bias_of(g)
        mx = jnp.maximum(jnp.max(s, axis=-1, keepdims=True), sinks[g])
        p = jnp.exp(s - mx)
        probs.append(p)
        dens.append(jnp.sum(p, axis=-1, keepdims=True) + jnp.exp(sinks[g] - mx))
    for g in range(KV_HEADS):
        vg = kv[:, (KV_HEADS + g) * HEAD:(KV_HEADS + g + 1) * HEAD]
        og = _bdot(probs[g], vg) / dens[g]
        for j in range(ATT_GROUP):
            h = g * ATT_GROUP + j
            y_scr[rs, h * HEAD:(h + 1) * HEAD] = og[j * BLK:(j + 1) * BLK, :]


def _mixers(proj, sinks, pool_w, pool_scale, norm_w, sgu_w, sgu_bias, batch, seq):
    m = proj.shape[0]
    tm = MIX_SUB * BLK
    kv_w = 2 * KV_HEADS * HEAD
    ns = seq // tm
    row = lambda b, n: b * ns + n
    before = lambda b, n: jnp.maximum(row(b, n) * MIX_SUB - 1, 0)
    bias = jnp.asarray(_attention_bias())
    assert proj.dtype == BF16
    band = jnp.asarray(_pool_bands(), dtype=BF16)
    in_specs = [
        pl.BlockSpec(memory_space=pltpu.SMEM),
        pl.BlockSpec(bias.shape, lambda b, n: (0, 0, 0)),
        pl.BlockSpec(band.shape, lambda b, n: (0, 0, 0)),
        pl.BlockSpec((tm, D_BRANCH), lambda b, n: (row(b, n), P_Q // D_BRANCH)),
        pl.BlockSpec((tm, kv_w), lambda b, n: (row(b, n), P_KV // kv_w)),
        pl.BlockSpec((BLK, kv_w), lambda b, n: (before(b, n), P_KV // kv_w)),
        pl.BlockSpec((tm, D_BRANCH), lambda b, n: (row(b, n), P_POOL // D_BRANCH)),
        pl.BlockSpec((BLK, D_BRANCH), lambda b, n: (before(b, n), P_POOL // D_BRANCH)),
        pl.BlockSpec((tm, 2 * D_BRANCH), lambda b, n: (row(b, n), P_SGU // (2 * D_BRANCH))),
        pl.BlockSpec((4, BLK, BLK), lambda b, n: (0, 0, 0)),
        pl.BlockSpec((1, D_BRANCH), lambda b, n: (0, 0)),
        pl.BlockSpec((1, D_BRANCH), lambda b, n: (0, 0)),
        pl.BlockSpec((4, BLK, BLK), lambda b, n: (0, 0, 0)),
        pl.BlockSpec((BLK, D_BRANCH), lambda b, n: (0, 0)),
    ]
    return pl.pallas_call(
        _mixers_kernel,
        grid=(batch, ns),
        in_specs=in_specs,
        out_specs=pl.BlockSpec((tm, 3 * D_BRANCH), lambda b, n: (row(b, n), 0)),
        out_shape=jax.ShapeDtypeStruct((m, 3 * D_BRANCH), ACT),
        scratch_shapes=[pltpu.VMEM((tm, 3 * D_BRANCH), F32)],
        compiler_params=pltpu.CompilerParams(
            dimension_semantics=("parallel", "parallel"), vmem_limit_bytes=_VMEM_LIMIT),
        name="mixers",
    )(sinks, bias, band, proj, proj, proj, proj, proj, proj, pool_w, pool_scale, norm_w, sgu_w,
      sgu_bias)


OUT_TM = 512
OUT_SUB = 256
OUT_STAGE = 512


def _outproj_kernel(ya_ref, yb_ref, g_ref, x_ref, w_hbm, pw_ref, o_ref, w_scr, stage, sem, *,
                    layer):
    @pl.when(pl.program_id(0) == 0)
    def _():
        def copy(idx):
            slot = idx % 2
            return pltpu.make_async_copy(w_hbm.at[layer, pl.ds(idx * OUT_STAGE, OUT_STAGE), :],
                                         stage.at[slot], sem.at[slot])

        n_pieces = D_MODEL // OUT_STAGE
        copy(0).start()
        for idx in range(n_pieces):
            if idx + 1 < n_pieces:
                copy(idx + 1).start()
            copy(idx).wait()
            w_scr[idx * OUT_STAGE:(idx + 1) * OUT_STAGE, :] = stage[idx % 2].astype(BF16)

    for r0 in range(0, OUT_TM, OUT_SUB):
        rows = slice(r0, r0 + OUT_SUB)
        g = g_ref[rows, :].astype(F32)
        y = jnp.concatenate([ya_ref[rows, :], yb_ref[rows, :]], axis=1).astype(F32)
        gated = (y * (g * jax.nn.sigmoid(g))).astype(BF16)
        acc = jnp.dot(gated, w_scr[...], preferred_element_type=F32)
        ms = jnp.mean(acc * acc, axis=-1, keepdims=True)
        o_ref[rows, :] = x_ref[rows, :] + acc * lax.rsqrt(ms + NORM_EPS) * pw_ref[...]


def _outproj(ya, ybcd, proj, x2d, w_out, layer, post_w):
    m = x2d.shape[0]
    return pl.pallas_call(
        functools.partial(_outproj_kernel, layer=layer),
        grid=(m // OUT_TM,),
        in_specs=[
            pl.BlockSpec((OUT_TM, D_BRANCH), lambda i: (i, 0)),
            pl.BlockSpec((OUT_TM, 3 * D_BRANCH), lambda i: (i, 0)),
            pl.BlockSpec((OUT_TM, D_MODEL), lambda i: (i, P_GATE // D_MODEL)),
            pl.BlockSpec((OUT_TM, D_MODEL), lambda i: (i, 0)),
            pl.BlockSpec(memory_space=pl.ANY),
            pl.BlockSpec((1, D_MODEL), lambda i: (0, 0)),
        ],
        out_specs=pl.BlockSpec((OUT_TM, D_MODEL), lambda i: (i, 0)),
        out_shape=jax.ShapeDtypeStruct((m, D_MODEL), F32),
        scratch_shapes=[
            pltpu.VMEM((D_MODEL, D_MODEL), BF16),
            pltpu.VMEM((2, OUT_STAGE, D_MODEL), F32),
            pltpu.SemaphoreType.DMA((2,)),
        ],
        compiler_params=pltpu.CompilerParams(
            dimension_semantics=("arbitrary",), vmem_limit_bytes=_VMEM_LIMIT),
        name="outproj",
    )(ya, ybcd, proj, x2d, w_out, post_w)


def kernel(x, pre_norm_w, post_norm_w, w_in, shift_mu, rwkv_w0, rwkv_w_up, rwkv_a0, rwkv_a_up,
           rwkv_k_k, rwkv_k_a, rwkv_r_k, rwkv_ln_w, rwkv_ln_b, attn_sinks, pool_w, pool_scale,
           sgu_norm_w, sgu_w, sgu_b, w_out):
    batch, seq, _ = x.shape
    assert x.shape == (batch, seq, D_MODEL) and seq % (MIX_SUB * BLK) == 0 and seq % RW_STEP == 0
    m = batch * seq
    head_id = np.arange(D_BRANCH // 2) // HEAD
    bd = jnp.asarray((head_id[:, None] == head_id[None, :]).astype(np.float32), dtype=BF16)
    t_id = np.arange(RW_TILE)
    tril = jnp.asarray(((t_id[:, None] >= t_id[None, :])
                        & (t_id[:, None] // RW_CHUNK == t_id[None, :] // RW_CHUNK)).astype(np.float32),
                       dtype=BF16)
    row_vec = lambda a: a.reshape(1, -1)

    h = x.reshape(m, D_MODEL)
    for l in range(DEPTH):
        mu = shift_mu[l]
        mu_p = jnp.concatenate([mu[:3 * D_BRANCH], jnp.zeros((P_LORA - P_KV,), mu.dtype),
                                mu[3 * D_BRANCH:], jnp.zeros((P_WIDTH - D_IN,), mu.dtype)])
        proj = _inproj(h, row_vec(pre_norm_w[l]), row_vec(mu_p), w_in, l, seq)

        vecs = [row_vec(rwkv_w0[l]), row_vec(rwkv_a0[l]), row_vec(rwkv_k_k[l]),
                row_vec(rwkv_k_a[l]), row_vec(rwkv_r_k[l]), row_vec(rwkv_ln_w[l]),
                row_vec(rwkv_ln_b[l])]
        ya = _rwkv(proj, vecs, rwkv_w_up[l], rwkv_a_up[l], bd, tril, batch, seq)

        sgu_bias = jnp.broadcast_to(sgu_b[l].T[:, :, None], (BLK, 4, BLK)).reshape(BLK, D_BRANCH)
        ybcd = _mixers(proj, attn_sinks[l], pool_w[l], row_vec(pool_scale[l]),
                       row_vec(sgu_norm_w[l]), sgu_w[l], sgu_bias, batch, seq)

        h = _outproj(ya, ybcd, proj, h, w_out, l, row_vec(post_norm_w[l]))
    return h.reshape(batch, seq, D_MODEL)
```

```python
import functools
import itertools

import jax
import jax.numpy as jnp
import numpy as np
from jax import lax
from jax.experimental import pallas as pl
from jax.experimental.pallas import tpu as pltpu

F32 = jnp.float32
BF16 = jnp.bfloat16
ACT = BF16

D_MODEL = 2048
DEPTH = 2
D_BRANCH = 512
HEAD = 64
N_HEADS = D_BRANCH // HEAD
LORA = 64
KV_HEADS = 2
ATT_GROUP = N_HEADS // KV_HEADS
BLK = 128
MIX_SUB = 4
POOL_WINDOWS = (2, 4, 8, 16)
NEG_INF = -1e30
NORM_EPS = 1e-6
LN_EPS = 1e-5
GN_EPS = 64e-5

A_COLS = 3 * D_BRANCH + 2 * LORA
B_COLS = D_BRANCH + 2 * KV_HEADS * HEAD
OFF_B = A_COLS
OFF_C = OFF_B + B_COLS
OFF_D = OFF_C + D_BRANCH
OFF_G = OFF_D + 2 * D_BRANCH
D_IN = OFF_G + D_MODEL

P_GATE = 0
P_SGU = 2048
P_POOL = 3072
P_Q = 3584
P_R = 4096
P_K = 4608
P_V = 5120
P_KV = 5632
P_LORA = 5888
P_WIDTH = 6144

RW_CHUNK = 64
RW_TILE = 256
RW_STEP = 2 * RW_TILE

_VMEM_LIMIT = 56 * 1024 * 1024


IN_TM = 512
IN_TN = 1024
IN_STAGE = 256
IN_NORM_ROWS = 256
_W_RUNS = ((OFF_G, D_MODEL, P_GATE), (OFF_D, 2 * D_BRANCH, P_SGU), (OFF_C, D_BRANCH, P_POOL),
           (OFF_B, D_BRANCH, P_Q), (0, 3 * D_BRANCH, P_R),
           (OFF_B + D_BRANCH, 2 * KV_HEADS * HEAD, P_KV), (3 * D_BRANCH, 2 * LORA, P_LORA))
_W_PIECES = tuple((src + o, min(IN_STAGE, width - o), dst + o)
                  for src, width, dst in _W_RUNS for o in range(0, width, IN_STAGE))
assert sum(p[1] for p in _W_PIECES) == D_IN and P_LORA + 2 * LORA == D_IN


IN_SLOTS = 2


def _load_weight(w_hbm, layer, w_scr, stage, sem):
    def copy(idx):
        src, width, _ = _W_PIECES[idx]
        slot = idx % IN_SLOTS
        return pltpu.make_async_copy(w_hbm.at[layer, :, pl.ds(src, width)],
                                     stage.at[slot, :, pl.ds(0, width)], sem.at[slot])

    for idx in range(IN_SLOTS - 1):
        copy(idx).start()
    for idx, (_, width, dst) in enumerate(_W_PIECES):
        if idx + IN_SLOTS - 1 < len(_W_PIECES):
            copy(idx + IN_SLOTS - 1).start()
        copy(idx).wait()
        w_scr[:, dst:dst + width] = stage[idx % IN_SLOTS, :, 0:width].astype(BF16)
    w_scr[:, D_IN:] = jnp.zeros((D_MODEL, P_WIDTH - D_IN), BF16)


def _inproj_kernel(x_ref, pw_ref, mu_ref, w_hbm, o_ref, w_scr, stage, sem, h_scr, carry_scr, *,
                   layer, tiles_per_seq):
    i = pl.program_id(0)

    @pl.when(i == 0)
    def _():
        carry_scr[...] = jnp.zeros_like(carry_scr)
        _load_weight(w_hbm, layer, w_scr, stage, sem)

    for r0 in range(0, IN_TM, IN_NORM_ROWS):
        x = x_ref[r0:r0 + IN_NORM_ROWS, :]
        ms = jnp.mean(x * x, axis=-1, keepdims=True)
        h_scr[r0:r0 + IN_NORM_ROWS, :] = (x * lax.rsqrt(ms + NORM_EPS)
                                          * pw_ref[...]).astype(BF16)
    row = lax.broadcasted_iota(jnp.int32, (IN_TM, 1), 0)
    starts_sequence = (i % tiles_per_seq) == 0
    for c0 in range(0, P_WIDTH, IN_TN):
        acc = jnp.dot(h_scr[...], w_scr[:, c0:c0 + IN_TN], preferred_element_type=F32)
        if c0 >= P_R:
            cs = slice(c0 - P_R, c0 - P_R + IN_TN)
            before = jnp.where(starts_sequence, 0.0, carry_scr[0:1, cs])
            prev = jnp.where(row == 0, before, pltpu.roll(acc, 1, axis=0))
            carry_scr[0:1, cs] = acc[IN_TM - 1:IN_TM, :]
            acc = acc + mu_ref[:, cs] * (prev - acc)
        o_ref[:, c0:c0 + IN_TN] = acc.astype(o_ref.dtype)


def _inproj(x2d, pre_w, mu_p, w_in, layer, seq):
    m = x2d.shape[0]
    assert P_R % IN_TN == 0 and seq % IN_TM == 0 and mu_p.shape == (1, P_WIDTH - P_R)
    return pl.pallas_call(
        functools.partial(_inproj_kernel, layer=layer, tiles_per_seq=seq // IN_TM),
        grid=(m // IN_TM,),
        in_specs=[
            pl.BlockSpec((IN_TM, D_MODEL), lambda i: (i, 0)),
            pl.BlockSpec((1, D_MODEL), lambda i: (0, 0)),
            pl.BlockSpec((1, P_WIDTH - P_R), lambda i: (0, 0)),
            pl.BlockSpec(memory_space=pl.ANY),
        ],
        out_specs=pl.BlockSpec((IN_TM, P_WIDTH), lambda i: (i, 0)),
        out_shape=jax.ShapeDtypeStruct((m, P_WIDTH), ACT),
        scratch_shapes=[
            pltpu.VMEM((D_MODEL, P_WIDTH), BF16),
            pltpu.VMEM((IN_SLOTS, D_MODEL, IN_STAGE), F32),
            pltpu.SemaphoreType.DMA((IN_SLOTS,)),
            pltpu.VMEM((IN_TM, D_MODEL), BF16),
            pltpu.VMEM((8, P_WIDTH - P_R), F32),
        ],
        compiler_params=pltpu.CompilerParams(
            dimension_semantics=("arbitrary",), vmem_limit_bytes=_VMEM_LIMIT),
        name="inproj",
    )(x2d, pre_w, mu_p, w_in)


def _bdot(a, b):
    return jnp.dot(a.astype(BF16), b.astype(BF16), preferred_element_type=F32)


def _bdot_nt(a, b):
    return lax.dot_general(a.astype(BF16), b.astype(BF16), (((1,), (1,)), ((), ())),
                           preferred_element_type=F32)


def _split3(x):
    x1 = x.astype(BF16)
    r1 = x - x1.astype(F32)
    x2 = r1.astype(BF16)
    x3 = (r1 - x2.astype(F32)).astype(BF16)
    return x1, x2, x3


def _head_sums(xs, bd):
    rows = xs[0].shape[0]
    half = bd.shape[0]
    x = jnp.concatenate(xs, axis=0).astype(BF16)
    out = jnp.concatenate(
        [jnp.dot(x[:, :half], bd, preferred_element_type=F32),
         jnp.dot(x[:, half:], bd, preferred_element_type=F32)], axis=1)
    return [out[i * rows:(i + 1) * rows] for i in range(len(xs))]


def _mm(a16, b16):
    return jnp.dot(a16, b16, preferred_element_type=F32)


def _mm_nt(a16, b16):
    return lax.dot_general(a16, b16, (((1,), (1,)), ((), ())), preferred_element_type=F32)


def _block_diag(m16, bdm16):
    return jnp.concatenate([m16, m16], axis=0) * bdm16


def _unit_lower_inverse_many(lmats, eye, masks, bdm16):
    base_mask, base_bd16, level_bd16 = masks
    c = lmats[0].shape[0]
    l16s = [l.astype(BF16) for l in lmats]
    stack2 = lambda m16: jnp.concatenate([m16, m16], axis=0)
    lds = [jnp.where(base_mask, l, 0.0) for l in lmats]
    l2s = [_mm(ld.astype(BF16), stack2(l16) * base_bd16).astype(BF16)
           for ld, l16 in zip(lds, l16s)]
    yield
    xs = [eye + ld for ld in lds]
    both = [_mm(jnp.concatenate([x.astype(BF16), l2], axis=0), stack2(l2) * bdm16)
            for x, l2 in zip(xs, l2s)]
    xs = [x + b[:c] for x, b in zip(xs, both)]
    l4s = [b[c:].astype(BF16) for b in both]
    yield
    xs = [x + _mm(x.astype(BF16), stack2(l4) * bdm16) for x, l4 in zip(xs, l4s)]
    yield
    for lvl16 in level_bd16:
        ts = [_mm(x.astype(BF16), stack2(l16) * lvl16).astype(BF16) for x, l16 in zip(xs, l16s)]
        yield
        xs = [x + _mm(t, stack2(x.astype(BF16)) * bdm16) for x, t in zip(xs, ts)]
        yield
    return xs


_SET_BF16 = ("a", "b", "k", "r", "v", "b_e", "k_e")
_SET_NAMES = _SET_BF16 + ("g_all", "bonus")
_PREP_DELAY_BESIDE_A = 8
_PREP_DELAY_BESIDE_B = 4
_EXP_NEG_HALF = float(np.exp(-0.5))


def _rwkv_prep(z_refs, row0, w, dst, mxu_delay=0):
    tile, c = RW_TILE, RW_CHUNK
    rows = lambda z_ref: z_ref[row0:row0 + tile, :].astype(F32)

    lo = rows(z_refs[3])
    lora_w = _bdot(jnp.tanh(lo[:, :LORA]), w["wup"][...])
    lora_a = _bdot(lo[:, LORA:], w["aup"][...])
    yield
    r, k, v = rows(z_refs[0]), rows(z_refs[1]), rows(z_refs[2])
    logw = -_EXP_NEG_HALF * jax.nn.sigmoid(w["w0"][...] + lora_w)
    asig = jax.nn.sigmoid(w["a0"][...] + lora_a)
    logw_terms = _split3(logw)
    kk = k * w["kkw"][...]
    k2 = k * (1.0 + (asig - 1.0) * w["kaw"][...])
    sum_terms = [kk * kk, r * k2 * w["rkw"][...]]
    for _ in range(mxu_delay):
        yield
    tril = w["tril"][...]
    cum = sum(jnp.dot(tril, p, preferred_element_type=F32) for p in logw_terms)
    kk_ss, rk_sum = _head_sums(sum_terms, w["bd"][...])
    yield
    kk = kk * lax.rsqrt(jnp.maximum(kk_ss, 1e-24))
    dst["bonus"][...] = rk_sum * v
    for ck in range(tile // c):
        sl = slice(ck * c, (ck + 1) * c)
        lw, cm = logw[sl], cum[sl]
        total = cm[c - 1:c, :]
        g_inv = jnp.exp(-cm)
        g_all = jnp.exp(total)
        g_tail = g_all * g_inv
        b_c = kk[sl] * asig[sl]
        dst["a"][sl, :] = (-kk[sl] * jnp.exp(cm - lw)).astype(BF16)
        dst["b"][sl, :] = (b_c * g_inv).astype(BF16)
        dst["k"][sl, :] = (k2[sl] * g_inv).astype(BF16)
        dst["r"][sl, :] = (r[sl] * jnp.exp(cm)).astype(BF16)
        dst["v"][sl, :] = v[sl].astype(BF16)
        dst["b_e"][sl, :] = (b_c * g_tail).astype(BF16)
        dst["k_e"][sl, :] = (k2[sl] * g_tail).astype(BF16)
        dst["g_all"][ck:ck + 1, :] = g_all
    yield


def _rwkv_items():
    return [(ck, p) for ck in range(RW_TILE // RW_CHUNK) for p in range(N_HEADS // 2)]


def _rwkv_part(src, name, it):
    ck, p = it
    c, pw = RW_CHUNK, 2 * HEAD
    return src[name][ck * c:(ck + 1) * c, p * pw:(p + 1) * pw]


def _rwkv_main_a(src, consts, ctx):
    c, pw = RW_CHUNK, 2 * HEAD
    eye, masks, tri2, bdm4_16, bdm16, bdm = consts
    items = _rwkv_items()
    part = functools.partial(_rwkv_part, src)

    a2 = [part("a", it) for it in items]
    r2 = [part("r", it) for it in items]
    l_ab, m_rb16, ak16 = [], [], []
    for a, x, it in zip(a2, r2, items):
        b, k_ = part("b", it), part("k", it)
        pr = _mm_nt(jnp.concatenate([a, x], axis=0),
                    jnp.concatenate([b, b, k_, k_], axis=0) * bdm4_16)
        ab = jnp.where(tri2, pr[:, :pw], 0.0)
        l_ab.append(ab[:c])
        m_rb16.append(ab[c:].astype(BF16))
        ak16.append(jnp.where(tri2, pr[:, pw:], 0.0).astype(BF16))
    yield
    qy = [_mm(x, _block_diag(part("v", it), bdm16))
          for x, it in zip(ak16, items)]
    qv16 = [q[:c].astype(BF16) for q in qy]
    yv = [q[c:] for q in qy]
    vk = [jnp.where(bdm, lax.dot_general(part("v", it), part("k_e", it), (((0,), (0,)), ((), ())),
                                         preferred_element_type=F32), 0.0) for it in items]
    yield
    tinv = yield from _unit_lower_inverse_many(l_ab, eye, masks, bdm16)
    wu = [_mm(t.astype(BF16),
              jnp.concatenate([_block_diag(a, bdm16), _block_diag(q, bdm16)], axis=1))
          for t, a, q in zip(tinv, a2, qv16)]
    uv = [x[:, pw:] for x in wu]
    war = [jnp.concatenate([x[:, :pw].astype(BF16), y], axis=0) for x, y in zip(wu, r2)]
    ctx.update(uv=uv, war=war, m_rb16=m_rb16, yv=yv, vk=vk)
    yield


def _rwkv_main_b(src, w, consts, ctx, s_scr, y_scr, o_ref, row0):
    tile, c, pw = RW_TILE, RW_CHUNK, 2 * HEAD
    bdm16, bdm = consts[4], consts[5]
    n_chunks, n_pairs = tile // c, N_HEADS // 2
    items = _rwkv_items()
    part = functools.partial(_rwkv_part, src)
    uv, war, m_rb16, yv, vk = ctx["uv"], ctx["war"], ctx["m_rb16"], ctx["yv"], ctx["vk"]

    state = [s_scr[p] for p in range(n_pairs)]
    for ck in range(n_chunks):
        idx = [ck * n_pairs + p for p in range(n_pairs)]
        s16 = [s.astype(BF16) for s in state]
        uy = [_mm_nt(war[i], s16[p]) for p, i in enumerate(idx)]
        yield
        u16 = [(uy[p][:c] + uv[i]).astype(BF16) for p, i in enumerate(idx)]
        for p, i in enumerate(idx):
            y = uy[p][c:] + yv[i] + _mm(m_rb16[i], _block_diag(u16[p], bdm16))
            y_scr[ck * c:(ck + 1) * c, p * pw:(p + 1) * pw] = y
        state = [state[p] * src["g_all"][ck:ck + 1, p * pw:(p + 1) * pw] + vk[i]
                 + jnp.where(bdm, lax.dot_general(u16[p], part("b_e", items[i]),
                                                  (((0,), (0,)), ((), ())),
                                                  preferred_element_type=F32), 0.0)
                 for p, i in enumerate(idx)]
        yield
    for p in range(n_pairs):
        s_scr[p] = state[p]

    y = y_scr[...]
    bd = w["bd"][...]
    mu = _head_sums([y], bd)[0] * (1.0 / HEAD)
    d = y - mu
    var = _head_sums([d * d], bd)[0] * (1.0 / HEAD)
    out = d * lax.rsqrt(var + GN_EPS) * w["lnw"][...] + w["lnb"][...] + src["bonus"][...]
    o_ref[row0:row0 + tile, :] = out.astype(o_ref.dtype)
    yield


def _interleave(*streams):
    for _ in itertools.zip_longest(*streams):
        pass


_W_NAMES = ("w0", "wup", "a0", "aup", "kkw", "kaw", "rkw", "lnw", "lnb", "bd", "tril")


def _rwkv_kernel(*refs):
    cur, nxt = refs[0:4], refs[4:8]
    w = dict(zip(_W_NAMES, refs[8:8 + len(_W_NAMES)]))
    rest = refs[8 + len(_W_NAMES):]
    o_ref, s_scr, y_scr = rest[0], rest[1], rest[2]
    n_set = len(_SET_NAMES)
    tiles = RW_STEP // RW_TILE
    sets = [dict(zip(_SET_NAMES, rest[3 + t * n_set:3 + (t + 1) * n_set])) for t in range(tiles)]
    n = pl.program_id(1)

    @pl.when(n == 0)
    def _():
        s_scr[...] = jnp.zeros_like(s_scr)

    @pl.when((pl.program_id(0) == 0) & (n == 0))
    def _():
        _interleave(_rwkv_prep(cur, 0, w, sets[0]))

    c, pw = RW_CHUNK, 2 * HEAD
    ci = lax.broadcasted_iota(jnp.int32, (c, pw), 0)
    cj = lax.broadcasted_iota(jnp.int32, (c, pw), 1) % HEAD
    eye = (ci == cj).astype(F32)
    si = lax.broadcasted_iota(jnp.int32, (2 * c, pw), 0)
    sj = lax.broadcasted_iota(jnp.int32, (2 * c, pw), 1) % HEAD
    tri2 = ((si < c) & (si > sj)) | ((si >= c) & ((si - c) >= sj))
    bi = lax.broadcasted_iota(jnp.int32, (2 * pw, pw), 0)
    bj = lax.broadcasted_iota(jnp.int32, (2 * pw, pw), 1)
    bdm_bool = ((bi // HEAD) % 2) == (bj // HEAD)
    bdm4_16 = bdm_bool.astype(BF16)
    hi = lax.broadcasted_iota(jnp.int32, (pw, pw), 0)
    hj = lax.broadcasted_iota(jnp.int32, (pw, pw), 1)
    di, dj = hi % HEAD, hj % HEAD
    same_head = (hi // HEAD) == (hj // HEAD)
    near = lambda b: same_head & ((di // b) == (dj // b))
    ring = lambda b: same_head & ((di // (2 * b)) == (dj // (2 * b))) & ((di // b) != (dj // b))
    masks = ((ci // 8) == (cj // 8), near(8).astype(BF16),
             tuple(ring(b).astype(BF16) for b in (8, 16, 32)))
    consts = (eye, masks, tri2, bdm4_16, bdm4_16[:pw], bdm_bool[:pw])

    ctx = [{} for _ in range(tiles)]
    stage_a = lambda t: _rwkv_main_a(sets[t], consts, ctx[t])
    stage_b = lambda t: _rwkv_main_b(sets[t], w, consts, ctx[t], s_scr, y_scr, o_ref, t * RW_TILE)
    prep = lambda t, delay: _rwkv_prep(cur, t * RW_TILE, w, sets[t], delay)
    _interleave(prep(1, _PREP_DELAY_BESIDE_A), stage_a(0))
    for t in range(1, tiles):
        streams = [stage_b(t - 1), stage_a(t)]
        if t + 1 < tiles:
            streams.append(prep(t + 1, _PREP_DELAY_BESIDE_A))
        _interleave(*streams)
    _interleave(stage_b(tiles - 1), _rwkv_prep(nxt, 0, w, sets[0], _PREP_DELAY_BESIDE_B))


def _rwkv(proj, vecs, wup, aup, bd, tril, batch, seq):
    m = proj.shape[0]
    ns = seq // RW_STEP
    row = lambda b, n: b * ns + n
    nxt = lambda b, n: jnp.minimum(b * ns + n + 1, batch * ns - 1)
    vec = lambda width: pl.BlockSpec((1, width), lambda b, n: (0, 0))
    full = lambda a: pl.BlockSpec(a.shape, lambda b, n: (0, 0))

    def token_specs(row_fn):
        wide = lambda cb: pl.BlockSpec((RW_STEP, D_BRANCH), lambda b, n: (row_fn(b, n), cb))
        return [wide(P_R // D_BRANCH), wide(P_K // D_BRANCH), wide(P_V // D_BRANCH),
                pl.BlockSpec((RW_STEP, 2 * LORA), lambda b, n: (row_fn(b, n), P_LORA // (2 * LORA)))]

    in_specs = token_specs(row) + token_specs(nxt) + [
        vec(D_BRANCH), full(wup), vec(D_BRANCH), full(aup),
        vec(D_BRANCH), vec(D_BRANCH), vec(D_BRANCH), vec(D_BRANCH), vec(D_BRANCH),
        full(bd), full(tril),
    ]
    prep_set = ([pltpu.VMEM((RW_TILE, D_BRANCH), BF16) for _ in _SET_BF16]
                + [pltpu.VMEM((8, D_BRANCH), F32), pltpu.VMEM((RW_TILE, D_BRANCH), F32)])
    return pl.pallas_call(
        _rwkv_kernel,
        grid=(batch, ns),
        in_specs=in_specs,
        out_specs=pl.BlockSpec((RW_STEP, D_BRANCH), lambda b, n: (row(b, n), 0)),
        out_shape=jax.ShapeDtypeStruct((m, D_BRANCH), ACT),
        scratch_shapes=[
            pltpu.VMEM((N_HEADS // 2, 2 * HEAD, 2 * HEAD), F32),
            pltpu.VMEM((RW_TILE, D_BRANCH), F32),
        ] + prep_set * (RW_STEP // RW_TILE),
        compiler_params=pltpu.CompilerParams(
            dimension_semantics=("arbitrary", "arbitrary"), vmem_limit_bytes=_VMEM_LIMIT),
        name="rwkv",
    )(*([proj] * 8), vecs[0], wup, vecs[1], aup, *vecs[2:], bd, tril)


_SLOPES = tuple(2.0 ** (-8.0 * (h + 1) / N_HEADS) for h in range(N_HEADS))
_SQRT_HALF = float(np.sqrt(0.5))


def _attention_bias():
    t = np.arange(BLK)[:, None]
    s = np.arange(2 * BLK)[None, :]
    dist = t + BLK - s
    out = np.empty((2, N_HEADS * BLK, 2 * BLK), np.float32)
    for first in (0, 1):
        valid = (dist >= 0) & (dist < BLK) & ((s >= BLK) | (first == 1))
        for h in range(N_HEADS):
            out[first, h * BLK:(h + 1) * BLK] = np.where(valid, -_SLOPES[h] * dist, NEG_INF)
    return out


def _pool_bands():
    t = np.arange(BLK)[:, None] + BLK
    s = np.arange(2 * BLK)[None, :]
    return np.stack([((s <= t) & (s > t - w)).astype(np.float32) for w in POOL_WINDOWS])


def _mixers_kernel(sinks_ref, bias_ref, band_ref, q_ref, kvc_ref, kvp_ref, zc_ref, zcp_ref, zd_ref,
                   poolw_ref, pscale_ref, nw_ref, sw_ref, sb_ref, o_ref, y_scr):
    n = pl.program_id(1)
    rows = ATT_GROUP * BLK
    head_in_group = lax.broadcasted_iota(jnp.int32, (rows, 1), 0) // BLK

    def per_row(values):
        col = jnp.full((rows, 1), values[-1], F32)
        for j in range(ATT_GROUP - 2, -1, -1):
            col = jnp.where(head_in_group == j, values[j], col)
        return col

    sinks = [per_row([sinks_ref[h] for h in range(g * ATT_GROUP, (g + 1) * ATT_GROUP)])
             for g in range(KV_HEADS)]
    ri = lax.broadcasted_iota(jnp.int32, (BLK, BLK), 0)
    rj = lax.broadcasted_iota(jnp.int32, (BLK, BLK), 1)
    sgu_w16 = [jnp.where(ri >= rj, sw_ref[g], 0.0).astype(BF16) for g in range(4)]
    refs = (bias_ref, band_ref, q_ref, kvc_ref, kvp_ref, zc_ref, zcp_ref, zd_ref, poolw_ref,
            pscale_ref, nw_ref, sb_ref, y_scr)
    for sub in range(MIX_SUB):
        _mixers_block(sub, n, refs, sinks, sgu_w16)
    o_ref[...] = y_scr[...].astype(o_ref.dtype)


def _mixers_block(sub, n, refs, sinks, sgu_w16):
    (bias_ref, band_ref, q_ref, kvc_ref, kvp_ref, zc_ref, zcp_ref, zd_ref, poolw_ref,
     pscale_ref, nw_ref, sb_ref, y_scr) = refs
    rs = slice(sub * BLK, (sub + 1) * BLK)
    before = slice((sub - 1) * BLK, sub * BLK)
    rows = ATT_GROUP * BLK
    if sub == 0:
        kv_prev = kvp_ref[...]
        z_prev = jnp.where(n > 0, zcp_ref[...], 0.0)
        bias_of = lambda g: bias_ref[jnp.minimum(n, 1), g * rows:(g + 1) * rows, :]
    else:
        kv_prev = kvc_ref[before, :]
        z_prev = zc_ref[before, :]
        bias_of = lambda g: bias_ref[1, g * rows:(g + 1) * rows, :]

    q = q_ref[rs, :] * (HEAD ** -0.5)
    kv = jnp.concatenate([kv_prev, kvc_ref[rs, :]], axis=0)

    scores = []
    for g in range(KV_HEADS):
        qg = jnp.concatenate(
            [q[:, (g * ATT_GROUP + j) * HEAD:(g * ATT_GROUP + j + 1) * HEAD]
             for j in range(ATT_GROUP)], axis=0)
        scores.append(_bdot_nt(qg, kv[:, g * HEAD:(g + 1) * HEAD]))

    zfull = jnp.concatenate([z_prev, zc_ref[rs, :]], axis=0)
    pos = (n * MIX_SUB + sub) * BLK + lax.broadcasted_iota(jnp.int32, (BLK, 1), 0) + 1
    sums = [jnp.dot(band_ref[g], zfull[:, g * BLK:(g + 1) * BLK], preferred_element_type=F32)
            for g in range(len(POOL_WINDOWS))]

    zd = zd_ref[rs, :].astype(F32)
    gz = 0.5 * zd * (1.0 + lax.erf(zd * _SQRT_HALF))
    u = gz[:, :D_BRANCH]
    vv = gz[:, D_BRANCH:]
    mu = jnp.mean(vv, axis=-1, keepdims=True)
    dv = vv - mu
    var = jnp.mean(dv * dv, axis=-1, keepdims=True)
    vn = dv * lax.rsqrt(var + LN_EPS) * nw_ref[...]
    for g in range(4):
        gs = slice(g * BLK, (g + 1) * BLK)
        sg = _bdot(sgu_w16[g], vn[:, gs]) + sb_ref[:, gs]
        y_scr[rs, 2 * D_BRANCH + g * BLK:2 * D_BRANCH + (g + 1) * BLK] = u[:, gs] * sg

    for g, w in enumerate(POOL_WINDOWS):
        gs = slice(g * BLK, (g + 1) * BLK)
        cnt = jnp.minimum(pos, w).astype(F32)
        pooled = sums[g] / cnt - zc_ref[rs, gs].astype(F32)
        yg = _bdot(pooled, poolw_ref[g]) * pscale_ref[:, gs]
        y_scr[rs, D_BRANCH + g * BLK:D_BRANCH + (g + 1) * BLK] = yg

    probs, dens = [], []
    for g in range(KV_HEADS):
        s = scores[g] + bias_of(g)
        mx = jnp.maximum(jnp.max(s, axis=-1, keepdims=True), sinks[g])
        p = jnp.exp(s - mx)
        probs.append(p)
        dens.append(jnp.sum(p, axis=-1, keepdims=True) + jnp.exp(sinks[g] - mx))
    for g in range(KV_HEADS):
        vg = kv[:, (KV_HEADS + g) * HEAD:(KV_HEADS + g + 1) * HEAD]
        og = _bdot(probs[g], vg) / dens[g]
        for j in range(ATT_GROUP):
            h = g * ATT_GROUP + j
            y_scr[rs, h * HEAD:(h + 1) * HEAD] = og[j * BLK:(j + 1) * BLK, :]


def _mixers(proj, sinks, pool_w, pool_scale, norm_w, sgu_w, sgu_bias, batch, seq):
    m = proj.shape[0]
    tm = MIX_SUB * BLK
    kv_w = 2 * KV_HEADS * HEAD
    ns = seq // tm
    row = lambda b, n: b * ns + n
    before = lambda b, n: jnp.maximum(row(b, n) * MIX_SUB - 1, 0)
    bias = jnp.asarray(_attention_bias())
    assert proj.dtype == BF16
    band = jnp.asarray(_pool_bands(), dtype=BF16)
    in_specs = [
        pl.BlockSpec(memory_space=pltpu.SMEM),
        pl.BlockSpec(bias.shape, lambda b, n: (0, 0, 0)),
        pl.BlockSpec(band.shape, lambda b, n: (0, 0, 0)),
        pl.BlockSpec((tm, D_BRANCH), lambda b, n: (row(b, n), P_Q // D_BRANCH)),
        pl.BlockSpec((tm, kv_w), lambda b, n: (row(b, n), P_KV // kv_w)),
        pl.BlockSpec((BLK, kv_w), lambda b, n: (before(b, n), P_KV // kv_w)),
        pl.BlockSpec((tm, D_BRANCH), lambda b, n: (row(b, n), P_POOL // D_BRANCH)),
        pl.BlockSpec((BLK, D_BRANCH), lambda b, n: (before(b, n), P_POOL // D_BRANCH)),
        pl.BlockSpec((tm, 2 * D_BRANCH), lambda b, n: (row(b, n), P_SGU // (2 * D_BRANCH))),
        pl.BlockSpec((4, BLK, BLK), lambda b, n: (0, 0, 0)),
        pl.BlockSpec((1, D_BRANCH), lambda b, n: (0, 0)),
        pl.BlockSpec((1, D_BRANCH), lambda b, n: (0, 0)),
        pl.BlockSpec((4, BLK, BLK), lambda b, n: (0, 0, 0)),
        pl.BlockSpec((BLK, D_BRANCH), lambda b, n: (0, 0)),
    ]
    return pl.pallas_call(
        _mixers_kernel,
        grid=(batch, ns),
        in_specs=in_specs,
        out_specs=pl.BlockSpec((tm, 3 * D_BRANCH), lambda b, n: (row(b, n), 0)),
        out_shape=jax.ShapeDtypeStruct((m, 3 * D_BRANCH), ACT),
        scratch_shapes=[pltpu.VMEM((tm, 3 * D_BRANCH), F32)],
        compiler_params=pltpu.CompilerParams(
            dimension_semantics=("parallel", "parallel"), vmem_limit_bytes=_VMEM_LIMIT),
        name="mixers",
    )(sinks, bias, band, proj, proj, proj, proj, proj, proj, pool_w, pool_scale, norm_w, sgu_w,
      sgu_bias)


OUT_TM = 512
OUT_SUB = 256
OUT_STAGE = 512


def _outproj_kernel(ya_ref, yb_ref, g_ref, x_ref, w_hbm, pw_ref, o_ref, w_scr, stage, sem, *,
                    layer):
    @pl.when(pl.program_id(0) == 0)
    def _():
        def copy(idx):
            slot = idx % 2
            return pltpu.make_async_copy(w_hbm.at[layer, pl.ds(idx * OUT_STAGE, OUT_STAGE), :],
                                         stage.at[slot], sem.at[slot])

        n_pieces = D_MODEL // OUT_STAGE
        copy(0).start()
        for idx in range(n_pieces):
            if idx + 1 < n_pieces:
                copy(idx + 1).start()
            copy(idx).wait()
            w_scr[idx * OUT_STAGE:(idx + 1) * OUT_STAGE, :] = stage[idx % 2].astype(BF16)

    for r0 in range(0, OUT_TM, OUT_SUB):
        rows = slice(r0, r0 + OUT_SUB)
        g = g_ref[rows, :].astype(F32)
        y = jnp.concatenate([ya_ref[rows, :], yb_ref[rows, :]], axis=1).astype(F32)
        gated = (y * (g * jax.nn.sigmoid(g))).astype(BF16)
        acc = jnp.dot(gated, w_scr[...], preferred_element_type=F32)
        ms = jnp.mean(acc * acc, axis=-1, keepdims=True)
        o_ref[rows, :] = x_ref[rows, :] + acc * lax.rsqrt(ms + NORM_EPS) * pw_ref[...]


def _outproj(ya, ybcd, proj, x2d, w_out, layer, post_w):
    m = x2d.shape[0]
    return pl.pallas_call(
        functools.partial(_outproj_kernel, layer=layer),
        grid=(m // OUT_TM,),
        in_specs=[
            pl.BlockSpec((OUT_TM, D_BRANCH), lambda i: (i, 0)),
            pl.BlockSpec((OUT_TM, 3 * D_BRANCH), lambda i: (i, 0)),
            pl.BlockSpec((OUT_TM, D_MODEL), lambda i: (i, P_GATE // D_MODEL)),
            pl.BlockSpec((OUT_TM, D_MODEL), lambda i: (i, 0)),
            pl.BlockSpec(memory_space=pl.ANY),
            pl.BlockSpec((1, D_MODEL), lambda i: (0, 0)),
        ],
        out_specs=pl.BlockSpec((OUT_TM, D_MODEL), lambda i: (i, 0)),
        out_shape=jax.ShapeDtypeStruct((m, D_MODEL), F32),
        scratch_shapes=[
            pltpu.VMEM((D_MODEL, D_MODEL), BF16),
            pltpu.VMEM((2, OUT_STAGE, D_MODEL), F32),
            pltpu.SemaphoreType.DMA((2,)),
        ],
        compiler_params=pltpu.CompilerParams(
            dimension_semantics=("arbitrary",), vmem_limit_bytes=_VMEM_LIMIT),
        name="outproj",
    )(ya, ybcd, proj, x2d, w_out, post_w)


def kernel(x, pre_norm_w, post_norm_w, w_in, shift_mu, rwkv_w0, rwkv_w_up, rwkv_a0, rwkv_a_up,
           rwkv_k_k, rwkv_k_a, rwkv_r_k, rwkv_ln_w, rwkv_ln_b, attn_sinks, pool_w, pool_scale,
           sgu_norm_w, sgu_w, sgu_b, w_out):
    batch, seq, _ = x.shape
    assert x.shape == (batch, seq, D_MODEL) and seq % (MIX_SUB * BLK) == 0 and seq % RW_STEP == 0
    m = batch * seq
    head_id = np.arange(D_BRANCH // 2) // HEAD
    bd = jnp.asarray((head_id[:, None] == head_id[None, :]).astype(np.float32), dtype=BF16)
    t_id = np.arange(RW_TILE)
    tril = jnp.asarray(((t_id[:, None] >= t_id[None, :])
                        & (t_id[:, None] // RW_CHUNK == t_id[None, :] // RW_CHUNK)).astype(np.float32),
                       dtype=BF16)
    row_vec = lambda a: a.reshape(1, -1)

    h = x.reshape(m, D_MODEL)
    for l in range(DEPTH):
        mu = shift_mu[l]
        mu_p = jnp.concatenate([mu[:3 * D_BRANCH], jnp.zeros((P_LORA - P_KV,), mu.dtype),
                                mu[3 * D_BRANCH:], jnp.zeros((P_WIDTH - D_IN,), mu.dtype)])
        proj = _inproj(h, row_vec(pre_norm_w[l]), row_vec(mu_p), w_in, l, seq)

        vecs = [row_vec(rwkv_w0[l]), row_vec(rwkv_a0[l]), row_vec(rwkv_k_k[l]),
                row_vec(rwkv_k_a[l]), row_vec(rwkv_r_k[l]), row_vec(rwkv_ln_w[l]),
                row_vec(rwkv_ln_b[l])]
        ya = _rwkv(proj, vecs, rwkv_w_up[l], rwkv_a_up[l], bd, tril, batch, seq)

        sgu_bias = jnp.broadcast_to(sgu_b[l].T[:, :, None], (BLK, 4, BLK)).reshape(BLK, D_BRANCH)
        ybcd = _mixers(proj, attn_sinks[l], pool_w[l], row_vec(pool_scale[l]),
                       row_vec(sgu_norm_w[l]), sgu_w[l], sgu_bias, batch, seq)

        h = _outproj(ya, ybcd, proj, h, w_out, l, row_vec(post_norm_w[l]))
    return h.reshape(batch, seq, D_MODEL)
```

```python
import functools
import itertools

import jax
import jax.numpy as jnp
import numpy as np
from jax import lax
from jax.experimental import pallas as pl
from jax.experimental.pallas import tpu as pltpu

F32 = jnp.float32
BF16 = jnp.bfloat16
ACT = BF16

D_MODEL = 2048
DEPTH = 2
D_BRANCH = 512
HEAD = 64
N_HEADS = D_BRANCH // HEAD
LORA = 64
KV_HEADS = 2
ATT_GROUP = N_HEADS // KV_HEADS
BLK = 128
MIX_SUB = 4
POOL_WINDOWS = (2, 4, 8, 16)
NEG_INF = -1e30
NORM_EPS = 1e-6
LN_EPS = 1e-5
GN_EPS = 64e-5

A_COLS = 3 * D_BRANCH + 2 * LORA
B_COLS = D_BRANCH + 2 * KV_HEADS * HEAD
OFF_B = A_COLS
OFF_C = OFF_B + B_COLS
OFF_D = OFF_C + D_BRANCH
OFF_G = OFF_D + 2 * D_BRANCH
D_IN = OFF_G + D_MODEL

P_GATE = 0
P_SGU = 2048
P_POOL = 3072
P_Q = 3584
P_R = 4096
P_K = 4608
P_V = 5120
P_KV = 5632
P_LORA = 5888
P_WIDTH = 6144

RW_CHUNK = 64
RW_TILE = 256
RW_STEP = 2 * RW_TILE

_VMEM_LIMIT = 56 * 1024 * 1024


IN_TM = 512
IN_TN = 1024
IN_STAGE = 256
IN_NORM_ROWS = 256
_W_RUNS = ((OFF_G, D_MODEL, P_GATE), (OFF_D, 2 * D_BRANCH, P_SGU), (OFF_C, D_BRANCH, P_POOL),
           (OFF_B, D_BRANCH, P_Q), (0, 3 * D_BRANCH, P_R),
           (OFF_B + D_BRANCH, 2 * KV_HEADS * HEAD, P_KV), (3 * D_BRANCH, 2 * LORA, P_LORA))
_W_PIECES = tuple((src + o, min(IN_STAGE, width - o), dst + o)
                  for src, width, dst in _W_RUNS for o in range(0, width, IN_STAGE))
assert sum(p[1] for p in _W_PIECES) == D_IN and P_LORA + 2 * LORA == D_IN


IN_SLOTS = 2


def _load_weight(w_hbm, layer, w_scr, stage, sem):
    def copy(idx):
        src, width, _ = _W_PIECES[idx]
        slot = idx % IN_SLOTS
        return pltpu.make_async_copy(w_hbm.at[layer, :, pl.ds(src, width)],
                                     stage.at[slot, :, pl.ds(0, width)], sem.at[slot])

    for idx in range(IN_SLOTS - 1):
        copy(idx).start()
    for idx, (_, width, dst) in enumerate(_W_PIECES):
        if idx + IN_SLOTS - 1 < len(_W_PIECES):
            copy(idx + IN_SLOTS - 1).start()
        copy(idx).wait()
        w_scr[:, dst:dst + width] = stage[idx % IN_SLOTS, :, 0:width].astype(BF16)
    w_scr[:, D_IN:] = jnp.zeros((D_MODEL, P_WIDTH - D_IN), BF16)


def _inproj_kernel(x_ref, pw_ref, mu_ref, w_hbm, o_ref, w_scr, stage, sem, h_scr, carry_scr, *,
                   layer, tiles_per_seq):
    i = pl.program_id(0)

    @pl.when(i == 0)
    def _():
        carry_scr[...] = jnp.zeros_like(carry_scr)
        _load_weight(w_hbm, layer, w_scr, stage, sem)

    for r0 in range(0, IN_TM, IN_NORM_ROWS):
        x = x_ref[r0:r0 + IN_NORM_ROWS, :]
        ms = jnp.mean(x * x, axis=-1, keepdims=True)
        h_scr[r0:r0 + IN_NORM_ROWS, :] = (x * lax.rsqrt(ms + NORM_EPS)
                                          * pw_ref[...]).astype(BF16)
    row = lax.broadcasted_iota(jnp.int32, (IN_TM, 1), 0)
    starts_sequence = (i % tiles_per_seq) == 0
    for c0 in range(0, P_WIDTH, IN_TN):
        acc = jnp.dot(h_scr[...], w_scr[:, c0:c0 + IN_TN], preferred_element_type=F32)
        if c0 >= P_R:
            cs = slice(c0 - P_R, c0 - P_R + IN_TN)
            before = jnp.where(starts_sequence, 0.0, carry_scr[0:1, cs])
            prev = jnp.where(row == 0, before, pltpu.roll(acc, 1, axis=0))
            carry_scr[0:1, cs] = acc[IN_TM - 1:IN_TM, :]
            acc = acc + mu_ref[:, cs] * (prev - acc)
        o_ref[:, c0:c0 + IN_TN] = acc.astype(o_ref.dtype)


def _inproj(x2d, pre_w, mu_p, w_in, layer, seq):
    m = x2d.shape[0]
    assert P_R % IN_TN == 0 and seq % IN_TM == 0 and mu_p.shape[1:] == (1, P_WIDTH - P_R)
    return pl.pallas_call(
        functools.partial(_inproj_kernel, layer=layer, tiles_per_seq=seq // IN_TM),
        grid=(m // IN_TM,),
        in_specs=[
            pl.BlockSpec((IN_TM, D_MODEL), lambda i: (i, 0)),
            pl.BlockSpec((None, 1, D_MODEL), lambda i: (layer, 0, 0)),
            pl.BlockSpec((None, 1, P_WIDTH - P_R), lambda i: (layer, 0, 0)),
            pl.BlockSpec(memory_space=pl.ANY),
        ],
        out_specs=pl.BlockSpec((IN_TM, P_WIDTH), lambda i: (i, 0)),
        out_shape=jax.ShapeDtypeStruct((m, P_WIDTH), ACT),
        scratch_shapes=[
            pltpu.VMEM((D_MODEL, P_WIDTH), BF16),
            pltpu.VMEM((IN_SLOTS, D_MODEL, IN_STAGE), F32),
            pltpu.SemaphoreType.DMA((IN_SLOTS,)),
            pltpu.VMEM((IN_TM, D_MODEL), BF16),
            pltpu.VMEM((8, P_WIDTH - P_R), F32),
        ],
        compiler_params=pltpu.CompilerParams(
            dimension_semantics=("arbitrary",), vmem_limit_bytes=_VMEM_LIMIT),
        name="inproj",
    )(x2d, pre_w, mu_p, w_in)


def _bdot(a, b):
    return jnp.dot(a.astype(BF16), b.astype(BF16), preferred_element_type=F32)


def _bdot_nt(a, b):
    return lax.dot_general(a.astype(BF16), b.astype(BF16), (((1,), (1,)), ((), ())),
                           preferred_element_type=F32)


def _split3(x):
    x1 = x.astype(BF16)
    r1 = x - x1.astype(F32)
    x2 = r1.astype(BF16)
    x3 = (r1 - x2.astype(F32)).astype(BF16)
    return x1, x2, x3


def _head_sums(xs, bd):
    rows = xs[0].shape[0]
    half = bd.shape[0]
    x = jnp.concatenate(xs, axis=0).astype(BF16)
    out = jnp.concatenate(
        [jnp.dot(x[:, :half], bd, preferred_element_type=F32),
         jnp.dot(x[:, half:], bd, preferred_element_type=F32)], axis=1)
    return [out[i * rows:(i + 1) * rows] for i in range(len(xs))]


def _mm(a16, b16):
    return jnp.dot(a16, b16, preferred_element_type=F32)


def _mm_nt(a16, b16):
    return lax.dot_general(a16, b16, (((1,), (1,)), ((), ())), preferred_element_type=F32)


def _block_diag(m16, bdm16):
    return jnp.concatenate([m16, m16], axis=0) * bdm16


def _unit_lower_inverse_many(lmats, eye, masks, bdm16):
    base_mask, base_bd16, level_bd16 = masks
    c = lmats[0].shape[0]
    l16s = [l.astype(BF16) for l in lmats]
    stack2 = lambda m16: jnp.concatenate([m16, m16], axis=0)
    lds = [jnp.where(base_mask, l, 0.0) for l in lmats]
    l2s = [_mm(ld.astype(BF16), stack2(l16) * base_bd16).astype(BF16)
           for ld, l16 in zip(lds, l16s)]
    yield
    xs = [eye + ld for ld in lds]
    both = [_mm(jnp.concatenate([x.astype(BF16), l2], axis=0), stack2(l2) * bdm16)
            for x, l2 in zip(xs, l2s)]
    xs = [x + b[:c] for x, b in zip(xs, both)]
    l4s = [b[c:].astype(BF16) for b in both]
    yield
    xs = [x + _mm(x.astype(BF16), stack2(l4) * bdm16) for x, l4 in zip(xs, l4s)]
    yield
    for lvl16 in level_bd16:
        ts = [_mm(x.astype(BF16), stack2(l16) * lvl16).astype(BF16) for x, l16 in zip(xs, l16s)]
        yield
        xs = [x + _mm(t, stack2(x.astype(BF16)) * bdm16) for x, t in zip(xs, ts)]
        yield
    return xs


_SET_BF16 = ("a", "b", "k", "r", "v", "b_e", "k_e")
_SET_NAMES = _SET_BF16 + ("g_all", "bonus")
_PREP_DELAY_BESIDE_A = 8
_PREP_DELAY_BESIDE_B = 4
_EXP_NEG_HALF = float(np.exp(-0.5))


def _rwkv_prep(z_refs, row0, w, dst, mxu_delay=0):
    tile, c = RW_TILE, RW_CHUNK
    rows = lambda z_ref: z_ref[row0:row0 + tile, :].astype(F32)

    lo = rows(z_refs[3])
    lora_w = _bdot(jnp.tanh(lo[:, :LORA]), w["wup"][...])
    lora_a = _bdot(lo[:, LORA:], w["aup"][...])
    yield
    r, k, v = rows(z_refs[0]), rows(z_refs[1]), rows(z_refs[2])
    logw = -_EXP_NEG_HALF * jax.nn.sigmoid(w["w0"][...] + lora_w)
    asig = jax.nn.sigmoid(w["a0"][...] + lora_a)
    logw_terms = _split3(logw)
    kk = k * w["kkw"][...]
    k2 = k * (1.0 + (asig - 1.0) * w["kaw"][...])
    sum_terms = [kk * kk, r * k2 * w["rkw"][...]]
    for _ in range(mxu_delay):
        yield
    tril = w["tril"][...]
    cum = sum(jnp.dot(tril, p, preferred_element_type=F32) for p in logw_terms)
    kk_ss, rk_sum = _head_sums(sum_terms, w["bd"][...])
    yield
    kk = kk * lax.rsqrt(jnp.maximum(kk_ss, 1e-24))
    dst["bonus"][...] = rk_sum * v
    for ck in range(tile // c):
        sl = slice(ck * c, (ck + 1) * c)
        lw, cm = logw[sl], cum[sl]
        total = cm[c - 1:c, :]
        g_inv = jnp.exp(-cm)
        g_all = jnp.exp(total)
        g_tail = g_all * g_inv
        b_c = kk[sl] * asig[sl]
        dst["a"][sl, :] = (-kk[sl] * jnp.exp(cm - lw)).astype(BF16)
        dst["b"][sl, :] = (b_c * g_inv).astype(BF16)
        dst["k"][sl, :] = (k2[sl] * g_inv).astype(BF16)
        dst["r"][sl, :] = (r[sl] * jnp.exp(cm)).astype(BF16)
        dst["v"][sl, :] = v[sl].astype(BF16)
        dst["b_e"][sl, :] = (b_c * g_tail).astype(BF16)
        dst["k_e"][sl, :] = (k2[sl] * g_tail).astype(BF16)
        dst["g_all"][ck:ck + 1, :] = g_all
    yield


def _rwkv_items():
    return [(ck, p) for ck in range(RW_TILE // RW_CHUNK) for p in range(N_HEADS // 2)]


def _rwkv_part(src, name, it):
    ck, p = it
    c, pw = RW_CHUNK, 2 * HEAD
    return src[name][ck * c:(ck + 1) * c, p * pw:(p + 1) * pw]


def _rwkv_main_a(src, consts, ctx):
    c, pw = RW_CHUNK, 2 * HEAD
    eye, masks, tri2, bdm4_16, bdm16, bdm = consts
    items = _rwkv_items()
    part = functools.partial(_rwkv_part, src)

    a2 = [part("a", it) for it in items]
    r2 = [part("r", it) for it in items]
    l_ab, m_rb16, ak16 = [], [], []
    for a, x, it in zip(a2, r2, items):
        b, k_ = part("b", it), part("k", it)
        pr = _mm_nt(jnp.concatenate([a, x], axis=0),
                    jnp.concatenate([b, b, k_, k_], axis=0) * bdm4_16)
        ab = jnp.where(tri2, pr[:, :pw], 0.0)
        l_ab.append(ab[:c])
        m_rb16.append(ab[c:].astype(BF16))
        ak16.append(jnp.where(tri2, pr[:, pw:], 0.0).astype(BF16))
    yield
    qy = [_mm(x, _block_diag(part("v", it), bdm16))
          for x, it in zip(ak16, items)]
    qv16 = [q[:c].astype(BF16) for q in qy]
    yv = [q[c:] for q in qy]
    vk = [jnp.where(bdm, lax.dot_general(part("v", it), part("k_e", it), (((0,), (0,)), ((), ())),
                                         preferred_element_type=F32), 0.0) for it in items]
    yield
    tinv = yield from _unit_lower_inverse_many(l_ab, eye, masks, bdm16)
    wu = [_mm(t.astype(BF16),
              jnp.concatenate([_block_diag(a, bdm16), _block_diag(q, bdm16)], axis=1))
          for t, a, q in zip(tinv, a2, qv16)]
    uv = [x[:, pw:] for x in wu]
    war = [jnp.concatenate([x[:, :pw].astype(BF16), y], axis=0) for x, y in zip(wu, r2)]
    ctx.update(uv=uv, war=war, m_rb16=m_rb16, yv=yv, vk=vk)
    yield


def _rwkv_main_b(src, w, consts, ctx, s_scr, y_scr, o_ref, row0):
    tile, c, pw = RW_TILE, RW_CHUNK, 2 * HEAD
    bdm16, bdm = consts[4], consts[5]
    n_chunks, n_pairs = tile // c, N_HEADS // 2
    items = _rwkv_items()
    part = functools.partial(_rwkv_part, src)
    uv, war, m_rb16, yv, vk = ctx["uv"], ctx["war"], ctx["m_rb16"], ctx["yv"], ctx["vk"]

    state = [s_scr[p] for p in range(n_pairs)]
    for ck in range(n_chunks):
        idx = [ck * n_pairs + p for p in range(n_pairs)]
        s16 = [s.astype(BF16) for s in state]
        uy = [_mm_nt(war[i], s16[p]) for p, i in enumerate(idx)]
        yield
        u16 = [(uy[p][:c] + uv[i]).astype(BF16) for p, i in enumerate(idx)]
        for p, i in enumerate(idx):
            y = uy[p][c:] + yv[i] + _mm(m_rb16[i], _block_diag(u16[p], bdm16))
            y_scr[ck * c:(ck + 1) * c, p * pw:(p + 1) * pw] = y
        state = [state[p] * src["g_all"][ck:ck + 1, p * pw:(p + 1) * pw] + vk[i]
                 + jnp.where(bdm, lax.dot_general(u16[p], part("b_e", items[i]),
                                                  (((0,), (0,)), ((), ())),
                                                  preferred_element_type=F32), 0.0)
                 for p, i in enumerate(idx)]
        yield
    for p in range(n_pairs):
        s_scr[p] = state[p]

    y = y_scr[...]
    bd = w["bd"][...]
    mu = _head_sums([y], bd)[0] * (1.0 / HEAD)
    d = y - mu
    var = _head_sums([d * d], bd)[0] * (1.0 / HEAD)
    out = d * lax.rsqrt(var + GN_EPS) * w["lnw"][...] + w["lnb"][...] + src["bonus"][...]
    o_ref[row0:row0 + tile, :] = out.astype(o_ref.dtype)
    yield


def _interleave(*streams):
    for _ in itertools.zip_longest(*streams):
        pass


_W_NAMES = ("w0", "wup", "a0", "aup", "kkw", "kaw", "rkw", "lnw", "lnb", "bd", "tril")


def _rwkv_kernel(*refs):
    cur, nxt = refs[0:4], refs[4:8]
    w = dict(zip(_W_NAMES, refs[8:8 + len(_W_NAMES)]))
    rest = refs[8 + len(_W_NAMES):]
    o_ref, s_scr, y_scr = rest[0], rest[1], rest[2]
    n_set = len(_SET_NAMES)
    tiles = RW_STEP // RW_TILE
    sets = [dict(zip(_SET_NAMES, rest[3 + t * n_set:3 + (t + 1) * n_set])) for t in range(tiles)]
    n = pl.program_id(1)

    @pl.when(n == 0)
    def _():
        s_scr[...] = jnp.zeros_like(s_scr)

    @pl.when((pl.program_id(0) == 0) & (n == 0))
    def _():
        _interleave(_rwkv_prep(cur, 0, w, sets[0]))

    c, pw = RW_CHUNK, 2 * HEAD
    ci = lax.broadcasted_iota(jnp.int32, (c, pw), 0)
    cj = lax.broadcasted_iota(jnp.int32, (c, pw), 1) % HEAD
    eye = (ci == cj).astype(F32)
    si = lax.broadcasted_iota(jnp.int32, (2 * c, pw), 0)
    sj = lax.broadcasted_iota(jnp.int32, (2 * c, pw), 1) % HEAD
    tri2 = ((si < c) & (si > sj)) | ((si >= c) & ((si - c) >= sj))
    bi = lax.broadcasted_iota(jnp.int32, (2 * pw, pw), 0)
    bj = lax.broadcasted_iota(jnp.int32, (2 * pw, pw), 1)
    bdm_bool = ((bi // HEAD) % 2) == (bj // HEAD)
    bdm4_16 = bdm_bool.astype(BF16)
    hi = lax.broadcasted_iota(jnp.int32, (pw, pw), 0)
    hj = lax.broadcasted_iota(jnp.int32, (pw, pw), 1)
    di, dj = hi % HEAD, hj % HEAD
    same_head = (hi // HEAD) == (hj // HEAD)
    near = lambda b: same_head & ((di // b) == (dj // b))
    ring = lambda b: same_head & ((di // (2 * b)) == (dj // (2 * b))) & ((di // b) != (dj // b))
    masks = ((ci // 8) == (cj // 8), near(8).astype(BF16),
             tuple(ring(b).astype(BF16) for b in (8, 16, 32)))
    consts = (eye, masks, tri2, bdm4_16, bdm4_16[:pw], bdm_bool[:pw])

    ctx = [{} for _ in range(tiles)]
    stage_a = lambda t: _rwkv_main_a(sets[t], consts, ctx[t])
    stage_b = lambda t: _rwkv_main_b(sets[t], w, consts, ctx[t], s_scr, y_scr, o_ref, t * RW_TILE)
    prep = lambda t, delay: _rwkv_prep(cur, t * RW_TILE, w, sets[t], delay)
    _interleave(prep(1, _PREP_DELAY_BESIDE_A), stage_a(0))
    for t in range(1, tiles):
        streams = [stage_b(t - 1), stage_a(t)]
        if t + 1 < tiles:
            streams.append(prep(t + 1, _PREP_DELAY_BESIDE_A))
        _interleave(*streams)
    _interleave(stage_b(tiles - 1), _rwkv_prep(nxt, 0, w, sets[0], _PREP_DELAY_BESIDE_B))


def _rwkv(proj, vecs, wup, aup, bd, tril, layer, batch, seq):
    m = proj.shape[0]
    ns = seq // RW_STEP
    row = lambda b, n: b * ns + n
    nxt = lambda b, n: jnp.minimum(b * ns + n + 1, batch * ns - 1)
    vec = lambda width: pl.BlockSpec((None, 1, width), lambda b, n: (layer, 0, 0))
    mat = lambda a: pl.BlockSpec((None,) + a.shape[1:], lambda b, n: (layer, 0, 0))
    full = lambda a: pl.BlockSpec(a.shape, lambda b, n: (0, 0))

    def token_specs(row_fn):
        wide = lambda cb: pl.BlockSpec((RW_STEP, D_BRANCH), lambda b, n: (row_fn(b, n), cb))
        return [wide(P_R // D_BRANCH), wide(P_K // D_BRANCH), wide(P_V // D_BRANCH),
                pl.BlockSpec((RW_STEP, 2 * LORA), lambda b, n: (row_fn(b, n), P_LORA // (2 * LORA)))]

    in_specs = token_specs(row) + token_specs(nxt) + [
        vec(D_BRANCH), mat(wup), vec(D_BRANCH), mat(aup),
        vec(D_BRANCH), vec(D_BRANCH), vec(D_BRANCH), vec(D_BRANCH), vec(D_BRANCH),
        full(bd), full(tril),
    ]
    prep_set = ([pltpu.VMEM((RW_TILE, D_BRANCH), BF16) for _ in _SET_BF16]
                + [pltpu.VMEM((8, D_BRANCH), F32), pltpu.VMEM((RW_TILE, D_BRANCH), F32)])
    return pl.pallas_call(
        _rwkv_kernel,
        grid=(batch, ns),
        in_specs=in_specs,
        out_specs=pl.BlockSpec((RW_STEP, D_BRANCH), lambda b, n: (row(b, n), 0)),
        out_shape=jax.ShapeDtypeStruct((m, D_BRANCH), ACT),
        scratch_shapes=[
            pltpu.VMEM((N_HEADS // 2, 2 * HEAD, 2 * HEAD), F32),
            pltpu.VMEM((RW_TILE, D_BRANCH), F32),
        ] + prep_set * (RW_STEP // RW_TILE),
        compiler_params=pltpu.CompilerParams(
            dimension_semantics=("arbitrary", "arbitrary"), vmem_limit_bytes=_VMEM_LIMIT),
        name="rwkv",
    )(*([proj] * 8), vecs[0], wup, vecs[1], aup, *vecs[2:], bd, tril)


_SLOPES = tuple(2.0 ** (-8.0 * (h + 1) / N_HEADS) for h in range(N_HEADS))
_SQRT_HALF = float(np.sqrt(0.5))


def _attention_bias():
    t = np.arange(BLK)[:, None]
    s = np.arange(2 * BLK)[None, :]
    dist = t + BLK - s
    out = np.empty((2, N_HEADS * BLK, 2 * BLK), np.float32)
    for first in (0, 1):
        valid = (dist >= 0) & (dist < BLK) & ((s >= BLK) | (first == 1))
        for h in range(N_HEADS):
            out[first, h * BLK:(h + 1) * BLK] = np.where(valid, -_SLOPES[h] * dist, NEG_INF)
    return out


def _pool_bands():
    t = np.arange(BLK)[:, None] + BLK
    s = np.arange(2 * BLK)[None, :]
    return np.stack([((s <= t) & (s > t - w)).astype(np.float32) for w in POOL_WINDOWS])


def _mixers_kernel(sinks_ref, bias_ref, band_ref, q_ref, kvc_ref, kvp_ref, zc_ref, zcp_ref, zd_ref,
                   poolw_ref, pscale_ref, nw_ref, sw_ref, sb_ref, o_ref, y_scr, *, layer):
    n = pl.program_id(1)
    rows = ATT_GROUP * BLK
    head_in_group = lax.broadcasted_iota(jnp.int32, (rows, 1), 0) // BLK

    def per_row(values):
        col = jnp.full((rows, 1), values[-1], F32)
        for j in range(ATT_GROUP - 2, -1, -1):
            col = jnp.where(head_in_group == j, values[j], col)
        return col

    sinks = [per_row([sinks_ref[layer, h] for h in range(g * ATT_GROUP, (g + 1) * ATT_GROUP)])
             for g in range(KV_HEADS)]
    ri = lax.broadcasted_iota(jnp.int32, (BLK, BLK), 0)
    rj = lax.broadcasted_iota(jnp.int32, (BLK, BLK), 1)
    sgu_w16 = [jnp.where(ri >= rj, sw_ref[g], 0.0).astype(BF16) for g in range(4)]
    refs = (bias_ref, band_ref, q_ref, kvc_ref, kvp_ref, zc_ref, zcp_ref, zd_ref, poolw_ref,
            pscale_ref, nw_ref, sb_ref, y_scr)
    for sub in range(MIX_SUB):
        _mixers_block(sub, n, refs, sinks, sgu_w16)
    o_ref[...] = y_scr[...].astype(o_ref.dtype)


def _mixers_block(sub, n, refs, sinks, sgu_w16):
    (bias_ref, band_ref, q_ref, kvc_ref, kvp_ref, zc_ref, zcp_ref, zd_ref, poolw_ref,
     pscale_ref, nw_ref, sb_ref, y_scr) = refs
    rs = slice(sub * BLK, (sub + 1) * BLK)
    before = slice((sub - 1) * BLK, sub * BLK)
    rows = ATT_GROUP * BLK
    if sub == 0:
        kv_prev = kvp_ref[...]
        z_prev = jnp.where(n > 0, zcp_ref[...], 0.0)
        bias_of = lambda g: bias_ref[jnp.minimum(n, 1), g * rows:(g + 1) * rows, :]
    else:
        kv_prev = kvc_ref[before, :]
        z_prev = zc_ref[before, :]
        bias_of = lambda g: bias_ref[1, g * rows:(g + 1) * rows, :]

    q = q_ref[rs, :] * (HEAD ** -0.5)
    kv = jnp.concatenate([kv_prev, kvc_ref[rs, :]], axis=0)

    scores = []
    for g in range(KV_HEADS):
        qg = jnp.concatenate(
            [q[:, (g * ATT_GROUP + j) * HEAD:(g * ATT_GROUP + j + 1) * HEAD]
             for j in range(ATT_GROUP)], axis=0)
        scores.append(_bdot_nt(qg, kv[:, g * HEAD:(g + 1) * HEAD]))

    zfull = jnp.concatenate([z_prev, zc_ref[rs, :]], axis=0)
    pos = (n * MIX_SUB + sub) * BLK + lax.broadcasted_iota(jnp.int32, (BLK, 1), 0) + 1
    sums = [jnp.dot(band_ref[g], zfull[:, g * BLK:(g + 1) * BLK], preferred_element_type=F32)
            for g in range(len(POOL_WINDOWS))]

    zd = zd_ref[rs, :].astype(F32)
    gz = 0.5 * zd * (1.0 + lax.erf(zd * _SQRT_HALF))
    u = gz[:, :D_BRANCH]
    vv = gz[:, D_BRANCH:]
    mu = jnp.mean(vv, axis=-1, keepdims=True)
    dv = vv - mu
    var = jnp.mean(dv * dv, axis=-1, keepdims=True)
    vn = dv * lax.rsqrt(var + LN_EPS) * nw_ref[...]
    for g in range(4):
        gs = slice(g * BLK, (g + 1) * BLK)
        sg = _bdot(sgu_w16[g], vn[:, gs]) + sb_ref[:, gs]
        y_scr[rs, 2 * D_BRANCH + g * BLK:2 * D_BRANCH + (g + 1) * BLK] = u[:, gs] * sg

    for g, w in enumerate(POOL_WINDOWS):
        gs = slice(g * BLK, (g + 1) * BLK)
        cnt = jnp.minimum(pos, w).astype(F32)
        pooled = sums[g] / cnt - zc_ref[rs, gs].astype(F32)
        yg = _bdot(pooled, poolw_ref[g]) * pscale_ref[:, gs]
        y_scr[rs, D_BRANCH + g * BLK:D_BRANCH + (g + 1) * BLK] = yg

    probs, dens = [], []
    for g in range(KV_HEADS):
        s = scores[g] + bias_of(g)
        mx = jnp.maximum(jnp.max(s, axis=-1, keepdims=True), sinks[g])
        p = jnp.exp(s - mx)
        probs.append(p)
        dens.append(jnp.sum(p, axis=-1, keepdims=True) + jnp.exp(sinks[g] - mx))
    for g in range(KV_HEADS):
        vg = kv[:, (KV_HEADS + g) * HEAD:(KV_HEADS + g + 1) * HEAD]
        og = _bdot(probs[g], vg) / dens[g]
        for j in range(ATT_GROUP):
            h = g * ATT_GROUP + j
            y_scr[rs, h * HEAD:(h + 1) * HEAD] = og[j * BLK:(j + 1) * BLK, :]


def _mixers(proj, sinks, pool_w, pool_scale, norm_w, sgu_w, sgu_bias, layer, batch, seq):
    m = proj.shape[0]
    per_layer = lambda a: pl.BlockSpec((None,) + a.shape[1:],
                                       lambda b, n: (layer,) + (0,) * (a.ndim - 1))
    tm = MIX_SUB * BLK
    kv_w = 2 * KV_HEADS * HEAD
    ns = seq // tm
    row = lambda b, n: b * ns + n
    before = lambda b, n: jnp.maximum(row(b, n) * MIX_SUB - 1, 0)
    bias = jnp.asarray(_attention_bias())
    assert proj.dtype == BF16
    band = jnp.asarray(_pool_bands(), dtype=BF16)
    in_specs = [
        pl.BlockSpec(memory_space=pltpu.SMEM),
        pl.BlockSpec(bias.shape, lambda b, n: (0, 0, 0)),
        pl.BlockSpec(band.shape, lambda b, n: (0, 0, 0)),
        pl.BlockSpec((tm, D_BRANCH), lambda b, n: (row(b, n), P_Q // D_BRANCH)),
        pl.BlockSpec((tm, kv_w), lambda b, n: (row(b, n), P_KV // kv_w)),
        pl.BlockSpec((BLK, kv_w), lambda b, n: (before(b, n), P_KV // kv_w)),
        pl.BlockSpec((tm, D_BRANCH), lambda b, n: (row(b, n), P_POOL // D_BRANCH)),
        pl.BlockSpec((BLK, D_BRANCH), lambda b, n: (before(b, n), P_POOL // D_BRANCH)),
        pl.BlockSpec((tm, 2 * D_BRANCH), lambda b, n: (row(b, n), P_SGU // (2 * D_BRANCH))),
        per_layer(pool_w), per_layer(pool_scale), per_layer(norm_w), per_layer(sgu_w),
        per_layer(sgu_bias),
    ]
    return pl.pallas_call(
        functools.partial(_mixers_kernel, layer=layer),
        grid=(batch, ns),
        in_specs=in_specs,
        out_specs=pl.BlockSpec((tm, 3 * D_BRANCH), lambda b, n: (row(b, n), 0)),
        out_shape=jax.ShapeDtypeStruct((m, 3 * D_BRANCH), ACT),
        scratch_shapes=[pltpu.VMEM((tm, 3 * D_BRANCH), F32)],
        compiler_params=pltpu.CompilerParams(
            dimension_semantics=("parallel", "parallel"), vmem_limit_bytes=_VMEM_LIMIT),
        name="mixers",
    )(sinks, bias, band, proj, proj, proj, proj, proj, proj, pool_w, pool_scale, norm_w, sgu_w,
      sgu_bias)


OUT_TM = 512
OUT_SUB = 256
OUT_STAGE = 512


def _outproj_kernel(ya_ref, yb_ref, g_ref, x_ref, w_hbm, pw_ref, o_ref, w_scr, stage, sem, *,
                    layer):
    @pl.when(pl.program_id(0) == 0)
    def _():
        def copy(idx):
            slot = idx % 2
            return pltpu.make_async_copy(w_hbm.at[layer, pl.ds(idx * OUT_STAGE, OUT_STAGE), :],
                                         stage.at[slot], sem.at[slot])

        n_pieces = D_MODEL // OUT_STAGE
        copy(0).start()
        for idx in range(n_pieces):
            if idx + 1 < n_pieces:
                copy(idx + 1).start()
            copy(idx).wait()
            w_scr[idx * OUT_STAGE:(idx + 1) * OUT_STAGE, :] = stage[idx % 2].astype(BF16)

    for r0 in range(0, OUT_TM, OUT_SUB):
        rows = slice(r0, r0 + OUT_SUB)
        g = g_ref[rows, :].astype(F32)
        y = jnp.concatenate([ya_ref[rows, :], yb_ref[rows, :]], axis=1).astype(F32)
        gated = (y * (g * jax.nn.sigmoid(g))).astype(BF16)
        acc = jnp.dot(gated, w_scr[...], preferred_element_type=F32)
        ms = jnp.mean(acc * acc, axis=-1, keepdims=True)
        o_ref[rows, :] = x_ref[rows, :] + acc * lax.rsqrt(ms + NORM_EPS) * pw_ref[...]


def _outproj(ya, ybcd, proj, x2d, w_out, layer, post_w):
    m = x2d.shape[0]
    return pl.pallas_call(
        functools.partial(_outproj_kernel, layer=layer),
        grid=(m // OUT_TM,),
        in_specs=[
            pl.BlockSpec((OUT_TM, D_BRANCH), lambda i: (i, 0)),
            pl.BlockSpec((OUT_TM, 3 * D_BRANCH), lambda i: (i, 0)),
            pl.BlockSpec((OUT_TM, D_MODEL), lambda i: (i, P_GATE // D_MODEL)),
            pl.BlockSpec((OUT_TM, D_MODEL), lambda i: (i, 0)),
            pl.BlockSpec(memory_space=pl.ANY),
            pl.BlockSpec((None, 1, D_MODEL), lambda i: (layer, 0, 0)),
        ],
        out_specs=pl.BlockSpec((OUT_TM, D_MODEL), lambda i: (i, 0)),
        out_shape=jax.ShapeDtypeStruct((m, D_MODEL), F32),
        scratch_shapes=[
            pltpu.VMEM((D_MODEL, D_MODEL), BF16),
            pltpu.VMEM((2, OUT_STAGE, D_MODEL), F32),
            pltpu.SemaphoreType.DMA((2,)),
        ],
        compiler_params=pltpu.CompilerParams(
            dimension_semantics=("arbitrary",), vmem_limit_bytes=_VMEM_LIMIT),
        name="outproj",
    )(ya, ybcd, proj, x2d, w_out, post_w)


def kernel(x, pre_norm_w, post_norm_w, w_in, shift_mu, rwkv_w0, rwkv_w_up, rwkv_a0, rwkv_a_up,
           rwkv_k_k, rwkv_k_a, rwkv_r_k, rwkv_ln_w, rwkv_ln_b, attn_sinks, pool_w, pool_scale,
           sgu_norm_w, sgu_w, sgu_b, w_out):
    batch, seq, _ = x.shape
    assert x.shape == (batch, seq, D_MODEL) and seq % (MIX_SUB * BLK) == 0 and seq % RW_STEP == 0
    m = batch * seq
    head_id = np.arange(D_BRANCH // 2) // HEAD
    bd = jnp.asarray((head_id[:, None] == head_id[None, :]).astype(np.float32), dtype=BF16)
    t_id = np.arange(RW_TILE)
    tril = jnp.asarray(((t_id[:, None] >= t_id[None, :])
                        & (t_id[:, None] // RW_CHUNK == t_id[None, :] // RW_CHUNK)).astype(np.float32),
                       dtype=BF16)
    rows = lambda a: a.reshape(DEPTH, 1, -1)
    zeros = lambda width: jnp.zeros((DEPTH, width), shift_mu.dtype)
    mu_p = rows(jnp.concatenate([shift_mu[:, :3 * D_BRANCH], zeros(P_LORA - P_KV),
                                 shift_mu[:, 3 * D_BRANCH:], zeros(P_WIDTH - D_IN)], axis=1))
    sgu_bias = jnp.broadcast_to(jnp.swapaxes(sgu_b, 1, 2)[:, :, :, None],
                                (DEPTH, BLK, 4, BLK)).reshape(DEPTH, BLK, D_BRANCH)
    rwkv_vecs = [rows(a) for a in (rwkv_w0, rwkv_a0, rwkv_k_k, rwkv_k_a, rwkv_r_k, rwkv_ln_w,
                                   rwkv_ln_b)]
    pre_w, post_w = rows(pre_norm_w), rows(post_norm_w)
    pool_scale3, sgu_norm3 = rows(pool_scale), rows(sgu_norm_w)

    h = x.reshape(m, D_MODEL)
    for l in range(DEPTH):
        proj = _inproj(h, pre_w, mu_p, w_in, l, seq)
        ya = _rwkv(proj, rwkv_vecs, rwkv_w_up, rwkv_a_up, bd, tril, l, batch, seq)
        ybcd = _mixers(proj, attn_sinks, pool_w, pool_scale3, sgu_norm3, sgu_w, sgu_bias, l,
                       batch, seq)
        h = _outproj(ya, ybcd, proj, h, w_out, l, post_w)
    return h.reshape(batch, seq, D_MODEL)
```

```python
import functools
import itertools

import jax
import jax.numpy as jnp
import numpy as np
from jax import lax
from jax.experimental import pallas as pl
from jax.experimental.pallas import tpu as pltpu

F32 = jnp.float32
BF16 = jnp.bfloat16
ACT = BF16

D_MODEL = 2048
DEPTH = 2
D_BRANCH = 512
HEAD = 64
N_HEADS = D_BRANCH // HEAD
LORA = 64
KV_HEADS = 2
ATT_GROUP = N_HEADS // KV_HEADS
BLK = 128
MIX_SUB = 4
POOL_WINDOWS = (2, 4, 8, 16)
NEG_INF = -1e30
NORM_EPS = 1e-6
LN_EPS = 1e-5
GN_EPS = 64e-5

A_COLS = 3 * D_BRANCH + 2 * LORA
B_COLS = D_BRANCH + 2 * KV_HEADS * HEAD
OFF_B = A_COLS
OFF_C = OFF_B + B_COLS
OFF_D = OFF_C + D_BRANCH
OFF_G = OFF_D + 2 * D_BRANCH
D_IN = OFF_G + D_MODEL

P_GATE = 0
P_SGU = 2048
P_POOL = 3072
P_Q = 3584
P_R = 4096
P_K = 4608
P_V = 5120
P_KV = 5632
P_LORA = 5888
P_WIDTH = 6144

RW_CHUNK = 64
RW_TILE = 256
RW_STEP = 2 * RW_TILE

_VMEM_LIMIT = 56 * 1024 * 1024


IN_TM = 512
IN_TN = 1024
IN_STAGE = 256
IN_NORM_ROWS = 256
_W_RUNS = ((OFF_G, D_MODEL, P_GATE), (OFF_D, 2 * D_BRANCH, P_SGU), (OFF_C, D_BRANCH, P_POOL),
           (OFF_B, D_BRANCH, P_Q), (0, 3 * D_BRANCH, P_R),
           (OFF_B + D_BRANCH, 2 * KV_HEADS * HEAD, P_KV), (3 * D_BRANCH, 2 * LORA, P_LORA))
_W_PIECES = tuple((src + o, min(IN_STAGE, width - o), dst + o)
                  for src, width, dst in _W_RUNS for o in range(0, width, IN_STAGE))
assert sum(p[1] for p in _W_PIECES) == D_IN and P_LORA + 2 * LORA == D_IN


IN_SLOTS = 2


def _load_weight(w_hbm, layer, w_scr, stage, sem):
    def copy(idx):
        src, width, _ = _W_PIECES[idx]
        slot = idx % IN_SLOTS
        return pltpu.make_async_copy(w_hbm.at[layer, :, pl.ds(src, width)],
                                     stage.at[slot, :, pl.ds(0, width)], sem.at[slot])

    for idx in range(IN_SLOTS - 1):
        copy(idx).start()
    for idx, (_, width, dst) in enumerate(_W_PIECES):
        if idx + IN_SLOTS - 1 < len(_W_PIECES):
            copy(idx + IN_SLOTS - 1).start()
        copy(idx).wait()
        w_scr[:, dst:dst + width] = stage[idx % IN_SLOTS, :, 0:width].astype(BF16)
    w_scr[:, D_IN:] = jnp.zeros((D_MODEL, P_WIDTH - D_IN), BF16)


def _inproj_kernel(x_ref, pw_ref, mu_ref, w_hbm, o_ref, w_scr, stage, sem, h_scr, carry_scr, *,
                   layer, tiles_per_seq):
    i = pl.program_id(0)

    @pl.when(i == 0)
    def _():
        carry_scr[...] = jnp.zeros_like(carry_scr)
        _load_weight(w_hbm, layer, w_scr, stage, sem)

    for r0 in range(0, IN_TM, IN_NORM_ROWS):
        x = x_ref[r0:r0 + IN_NORM_ROWS, :]
        ms = jnp.mean(x * x, axis=-1, keepdims=True)
        h_scr[r0:r0 + IN_NORM_ROWS, :] = (x * lax.rsqrt(ms + NORM_EPS)
                                          * pw_ref[layer:layer + 1, :]).astype(BF16)
    row = lax.broadcasted_iota(jnp.int32, (IN_TM, 1), 0)
    starts_sequence = (i % tiles_per_seq) == 0
    for c0 in range(0, P_WIDTH, IN_TN):
        acc = jnp.dot(h_scr[...], w_scr[:, c0:c0 + IN_TN], preferred_element_type=F32)
        if c0 >= P_R:
            cs = slice(c0 - P_R, c0 - P_R + IN_TN)
            before = jnp.where(starts_sequence, 0.0, carry_scr[0:1, cs])
            prev = jnp.where(row == 0, before, pltpu.roll(acc, 1, axis=0))
            carry_scr[0:1, cs] = acc[IN_TM - 1:IN_TM, :]
            acc = acc + mu_ref[layer:layer + 1, cs] * (prev - acc)
        o_ref[:, c0:c0 + IN_TN] = acc.astype(o_ref.dtype)


def _inproj(x2d, pre_w, mu_p, w_in, layer, seq):
    m = x2d.shape[0]
    assert P_R % IN_TN == 0 and seq % IN_TM == 0 and mu_p.shape == (DEPTH, P_WIDTH - P_R)
    return pl.pallas_call(
        functools.partial(_inproj_kernel, layer=layer, tiles_per_seq=seq // IN_TM),
        grid=(m // IN_TM,),
        in_specs=[
            pl.BlockSpec((IN_TM, D_MODEL), lambda i: (i, 0)),
            pl.BlockSpec((DEPTH, D_MODEL), lambda i: (0, 0)),
            pl.BlockSpec((DEPTH, P_WIDTH - P_R), lambda i: (0, 0)),
            pl.BlockSpec(memory_space=pl.ANY),
        ],
        out_specs=pl.BlockSpec((IN_TM, P_WIDTH), lambda i: (i, 0)),
        out_shape=jax.ShapeDtypeStruct((m, P_WIDTH), ACT),
        scratch_shapes=[
            pltpu.VMEM((D_MODEL, P_WIDTH), BF16),
            pltpu.VMEM((IN_SLOTS, D_MODEL, IN_STAGE), F32),
            pltpu.SemaphoreType.DMA((IN_SLOTS,)),
            pltpu.VMEM((IN_TM, D_MODEL), BF16),
            pltpu.VMEM((8, P_WIDTH - P_R), F32),
        ],
        compiler_params=pltpu.CompilerParams(
            dimension_semantics=("arbitrary",), vmem_limit_bytes=_VMEM_LIMIT),
        name="inproj",
    )(x2d, pre_w, mu_p, w_in)


class _LayerRow:
    def __init__(self, ref, layer):
        self.ref, self.layer = ref, layer

    def __getitem__(self, idx):
        cols = slice(None) if idx is Ellipsis else idx[1]
        return self.ref[self.layer:self.layer + 1, cols]


def _bdot(a, b):
    return jnp.dot(a.astype(BF16), b.astype(BF16), preferred_element_type=F32)


def _bdot_nt(a, b):
    return lax.dot_general(a.astype(BF16), b.astype(BF16), (((1,), (1,)), ((), ())),
                           preferred_element_type=F32)


def _split3(x):
    x1 = x.astype(BF16)
    r1 = x - x1.astype(F32)
    x2 = r1.astype(BF16)
    x3 = (r1 - x2.astype(F32)).astype(BF16)
    return x1, x2, x3


def _head_sums(xs, bd):
    rows = xs[0].shape[0]
    half = bd.shape[0]
    x = jnp.concatenate(xs, axis=0).astype(BF16)
    out = jnp.concatenate(
        [jnp.dot(x[:, :half], bd, preferred_element_type=F32),
         jnp.dot(x[:, half:], bd, preferred_element_type=F32)], axis=1)
    return [out[i * rows:(i + 1) * rows] for i in range(len(xs))]


def _mm(a16, b16):
    return jnp.dot(a16, b16, preferred_element_type=F32)


def _mm_nt(a16, b16):
    return lax.dot_general(a16, b16, (((1,), (1,)), ((), ())), preferred_element_type=F32)


def _block_diag(m16, bdm16):
    return jnp.concatenate([m16, m16], axis=0) * bdm16


def _unit_lower_inverse_many(lmats, eye, masks, bdm16):
    base_mask, base_bd16, level_bd16 = masks
    c = lmats[0].shape[0]
    l16s = [l.astype(BF16) for l in lmats]
    stack2 = lambda m16: jnp.concatenate([m16, m16], axis=0)
    lds = [jnp.where(base_mask, l, 0.0) for l in lmats]
    l2s = [_mm(ld.astype(BF16), stack2(l16) * base_bd16).astype(BF16)
           for ld, l16 in zip(lds, l16s)]
    yield
    xs = [eye + ld for ld in lds]
    both = [_mm(jnp.concatenate([x.astype(BF16), l2], axis=0), stack2(l2) * bdm16)
            for x, l2 in zip(xs, l2s)]
    xs = [x + b[:c] for x, b in zip(xs, both)]
    l4s = [b[c:].astype(BF16) for b in both]
    yield
    xs = [x + _mm(x.astype(BF16), stack2(l4) * bdm16) for x, l4 in zip(xs, l4s)]
    yield
    for lvl16 in level_bd16:
        ts = [_mm(x.astype(BF16), stack2(l16) * lvl16).astype(BF16) for x, l16 in zip(xs, l16s)]
        yield
        xs = [x + _mm(t, stack2(x.astype(BF16)) * bdm16) for x, t in zip(xs, ts)]
        yield
    return xs


_SET_BF16 = ("a", "b", "k", "r", "v", "b_e", "k_e")
_SET_NAMES = _SET_BF16 + ("g_all", "bonus")
_PREP_DELAY_BESIDE_A = 8
_PREP_DELAY_BESIDE_B = 4
_EXP_NEG_HALF = float(np.exp(-0.5))


def _rwkv_prep(z_refs, row0, w, dst, mxu_delay=0):
    tile, c = RW_TILE, RW_CHUNK
    rows = lambda z_ref: z_ref[row0:row0 + tile, :].astype(F32)

    lo = rows(z_refs[3])
    lora_w = _bdot(jnp.tanh(lo[:, :LORA]), w["wup"][...])
    lora_a = _bdot(lo[:, LORA:], w["aup"][...])
    yield
    r, k, v = rows(z_refs[0]), rows(z_refs[1]), rows(z_refs[2])
    logw = -_EXP_NEG_HALF * jax.nn.sigmoid(w["w0"][...] + lora_w)
    asig = jax.nn.sigmoid(w["a0"][...] + lora_a)
    logw_terms = _split3(logw)
    kk = k * w["kkw"][...]
    k2 = k * (1.0 + (asig - 1.0) * w["kaw"][...])
    sum_terms = [kk * kk, r * k2 * w["rkw"][...]]
    for _ in range(mxu_delay):
        yield
    tril = w["tril"][...]
    cum = sum(jnp.dot(tril, p, preferred_element_type=F32) for p in logw_terms)
    kk_ss, rk_sum = _head_sums(sum_terms, w["bd"][...])
    yield
    kk = kk * lax.rsqrt(jnp.maximum(kk_ss, 1e-24))
    dst["bonus"][...] = rk_sum * v
    for ck in range(tile // c):
        sl = slice(ck * c, (ck + 1) * c)
        lw, cm = logw[sl], cum[sl]
        total = cm[c - 1:c, :]
        g_inv = jnp.exp(-cm)
        g_all = jnp.exp(total)
        g_tail = g_all * g_inv
        b_c = kk[sl] * asig[sl]
        dst["a"][sl, :] = (-kk[sl] * jnp.exp(cm - lw)).astype(BF16)
        dst["b"][sl, :] = (b_c * g_inv).astype(BF16)
        dst["k"][sl, :] = (k2[sl] * g_inv).astype(BF16)
        dst["r"][sl, :] = (r[sl] * jnp.exp(cm)).astype(BF16)
        dst["v"][sl, :] = v[sl].astype(BF16)
        dst["b_e"][sl, :] = (b_c * g_tail).astype(BF16)
        dst["k_e"][sl, :] = (k2[sl] * g_tail).astype(BF16)
        dst["g_all"][ck:ck + 1, :] = g_all
    yield


def _rwkv_items():
    return [(ck, p) for ck in range(RW_TILE // RW_CHUNK) for p in range(N_HEADS // 2)]


def _rwkv_part(src, name, it):
    ck, p = it
    c, pw = RW_CHUNK, 2 * HEAD
    return src[name][ck * c:(ck + 1) * c, p * pw:(p + 1) * pw]


def _rwkv_main_a(src, consts, ctx):
    c, pw = RW_CHUNK, 2 * HEAD
    eye, masks, tri2, bdm4_16, bdm16, bdm = consts
    items = _rwkv_items()
    part = functools.partial(_rwkv_part, src)

    a2 = [part("a", it) for it in items]
    r2 = [part("r", it) for it in items]
    l_ab, m_rb16, ak16 = [], [], []
    for a, x, it in zip(a2, r2, items):
        b, k_ = part("b", it), part("k", it)
        pr = _mm_nt(jnp.concatenate([a, x], axis=0),
                    jnp.concatenate([b, b, k_, k_], axis=0) * bdm4_16)
        ab = jnp.where(tri2, pr[:, :pw], 0.0)
        l_ab.append(ab[:c])
        m_rb16.append(ab[c:].astype(BF16))
        ak16.append(jnp.where(tri2, pr[:, pw:], 0.0).astype(BF16))
    yield
    qy = [_mm(x, _block_diag(part("v", it), bdm16))
          for x, it in zip(ak16, items)]
    qv16 = [q[:c].astype(BF16) for q in qy]
    yv = [q[c:] for q in qy]
    vk = [jnp.where(bdm, lax.dot_general(part("v", it), part("k_e", it), (((0,), (0,)), ((), ())),
                                         preferred_element_type=F32), 0.0) for it in items]
    yield
    tinv = yield from _unit_lower_inverse_many(l_ab, eye, masks, bdm16)
    wu = [_mm(t.astype(BF16),
              jnp.concatenate([_block_diag(a, bdm16), _block_diag(q, bdm16)], axis=1))
          for t, a, q in zip(tinv, a2, qv16)]
    uv = [x[:, pw:] for x in wu]
    war = [jnp.concatenate([x[:, :pw].astype(BF16), y], axis=0) for x, y in zip(wu, r2)]
    ctx.update(uv=uv, war=war, m_rb16=m_rb16, yv=yv, vk=vk)
    yield


def _rwkv_main_b(src, w, consts, ctx, s_scr, y_scr, o_ref, row0):
    tile, c, pw = RW_TILE, RW_CHUNK, 2 * HEAD
    bdm16, bdm = consts[4], consts[5]
    n_chunks, n_pairs = tile // c, N_HEADS // 2
    items = _rwkv_items()
    part = functools.partial(_rwkv_part, src)
    uv, war, m_rb16, yv, vk = ctx["uv"], ctx["war"], ctx["m_rb16"], ctx["yv"], ctx["vk"]

    state = [s_scr[p] for p in range(n_pairs)]
    for ck in range(n_chunks):
        idx = [ck * n_pairs + p for p in range(n_pairs)]
        s16 = [s.astype(BF16) for s in state]
        uy = [_mm_nt(war[i], s16[p]) for p, i in enumerate(idx)]
        yield
        u16 = [(uy[p][:c] + uv[i]).astype(BF16) for p, i in enumerate(idx)]
        for p, i in enumerate(idx):
            y = uy[p][c:] + yv[i] + _mm(m_rb16[i], _block_diag(u16[p], bdm16))
            y_scr[ck * c:(ck + 1) * c, p * pw:(p + 1) * pw] = y
        state = [state[p] * src["g_all"][ck:ck + 1, p * pw:(p + 1) * pw] + vk[i]
                 + jnp.where(bdm, lax.dot_general(u16[p], part("b_e", items[i]),
                                                  (((0,), (0,)), ((), ())),
                                                  preferred_element_type=F32), 0.0)
                 for p, i in enumerate(idx)]
        yield
    for p in range(n_pairs):
        s_scr[p] = state[p]

    y = y_scr[...]
    bd = w["bd"][...]
    mu = _head_sums([y], bd)[0] * (1.0 / HEAD)
    d = y - mu
    var = _head_sums([d * d], bd)[0] * (1.0 / HEAD)
    out = d * lax.rsqrt(var + GN_EPS) * w["lnw"][...] + w["lnb"][...] + src["bonus"][...]
    o_ref[row0:row0 + tile, :] = out.astype(o_ref.dtype)
    yield


def _interleave(*streams):
    for _ in itertools.zip_longest(*streams):
        pass


_W_NAMES = ("w0", "wup", "a0", "aup", "kkw", "kaw", "rkw", "lnw", "lnb", "bd", "tril")
_W_VECTORS = ("w0", "a0", "kkw", "kaw", "rkw", "lnw", "lnb")


def _rwkv_kernel(*refs, layer):
    cur, nxt = refs[0:4], refs[4:8]
    w = dict(zip(_W_NAMES, refs[8:8 + len(_W_NAMES)]))
    for name in _W_VECTORS:
        w[name] = _LayerRow(w[name], layer)
    rest = refs[8 + len(_W_NAMES):]
    o_ref, s_scr, y_scr = rest[0], rest[1], rest[2]
    n_set = len(_SET_NAMES)
    tiles = RW_STEP // RW_TILE
    sets = [dict(zip(_SET_NAMES, rest[3 + t * n_set:3 + (t + 1) * n_set])) for t in range(tiles)]
    n = pl.program_id(1)

    @pl.when(n == 0)
    def _():
        s_scr[...] = jnp.zeros_like(s_scr)

    @pl.when((pl.program_id(0) == 0) & (n == 0))
    def _():
        _interleave(_rwkv_prep(cur, 0, w, sets[0]))

    c, pw = RW_CHUNK, 2 * HEAD
    ci = lax.broadcasted_iota(jnp.int32, (c, pw), 0)
    cj = lax.broadcasted_iota(jnp.int32, (c, pw), 1) % HEAD
    eye = (ci == cj).astype(F32)
    si = lax.broadcasted_iota(jnp.int32, (2 * c, pw), 0)
    sj = lax.broadcasted_iota(jnp.int32, (2 * c, pw), 1) % HEAD
    tri2 = ((si < c) & (si > sj)) | ((si >= c) & ((si - c) >= sj))
    bi = lax.broadcasted_iota(jnp.int32, (2 * pw, pw), 0)
    bj = lax.broadcasted_iota(jnp.int32, (2 * pw, pw), 1)
    bdm_bool = ((bi // HEAD) % 2) == (bj // HEAD)
    bdm4_16 = bdm_bool.astype(BF16)
    hi = lax.broadcasted_iota(jnp.int32, (pw, pw), 0)
    hj = lax.broadcasted_iota(jnp.int32, (pw, pw), 1)
    di, dj = hi % HEAD, hj % HEAD
    same_head = (hi // HEAD) == (hj // HEAD)
    near = lambda b: same_head & ((di // b) == (dj // b))
    ring = lambda b: same_head & ((di // (2 * b)) == (dj // (2 * b))) & ((di // b) != (dj // b))
    masks = ((ci // 8) == (cj // 8), near(8).astype(BF16),
             tuple(ring(b).astype(BF16) for b in (8, 16, 32)))
    consts = (eye, masks, tri2, bdm4_16, bdm4_16[:pw], bdm_bool[:pw])

    ctx = [{} for _ in range(tiles)]
    stage_a = lambda t: _rwkv_main_a(sets[t], consts, ctx[t])
    stage_b = lambda t: _rwkv_main_b(sets[t], w, consts, ctx[t], s_scr, y_scr, o_ref, t * RW_TILE)
    prep = lambda t, delay: _rwkv_prep(cur, t * RW_TILE, w, sets[t], delay)
    _interleave(prep(1, _PREP_DELAY_BESIDE_A), stage_a(0))
    for t in range(1, tiles):
        streams = [stage_b(t - 1), stage_a(t)]
        if t + 1 < tiles:
            streams.append(prep(t + 1, _PREP_DELAY_BESIDE_A))
        _interleave(*streams)
    _interleave(stage_b(tiles - 1), _rwkv_prep(nxt, 0, w, sets[0], _PREP_DELAY_BESIDE_B))


def _rwkv(proj, vecs, wup, aup, bd, tril, layer, batch, seq):
    m = proj.shape[0]
    ns = seq // RW_STEP
    row = lambda b, n: b * ns + n
    nxt = lambda b, n: jnp.minimum(b * ns + n + 1, batch * ns - 1)
    vec = lambda width: pl.BlockSpec((DEPTH, width), lambda b, n: (0, 0))
    mat = lambda a: pl.BlockSpec((None,) + a.shape[1:], lambda b, n: (layer, 0, 0))
    full = lambda a: pl.BlockSpec(a.shape, lambda b, n: (0, 0))

    def token_specs(row_fn):
        wide = lambda cb: pl.BlockSpec((RW_STEP, D_BRANCH), lambda b, n: (row_fn(b, n), cb))
        return [wide(P_R // D_BRANCH), wide(P_K // D_BRANCH), wide(P_V // D_BRANCH),
                pl.BlockSpec((RW_STEP, 2 * LORA), lambda b, n: (row_fn(b, n), P_LORA // (2 * LORA)))]

    in_specs = token_specs(row) + token_specs(nxt) + [
        vec(D_BRANCH), mat(wup), vec(D_BRANCH), mat(aup),
        vec(D_BRANCH), vec(D_BRANCH), vec(D_BRANCH), vec(D_BRANCH), vec(D_BRANCH),
        full(bd), full(tril),
    ]
    prep_set = ([pltpu.VMEM((RW_TILE, D_BRANCH), BF16) for _ in _SET_BF16]
                + [pltpu.VMEM((8, D_BRANCH), F32), pltpu.VMEM((RW_TILE, D_BRANCH), F32)])
    return pl.pallas_call(
        functools.partial(_rwkv_kernel, layer=layer),
        grid=(batch, ns),
        in_specs=in_specs,
        out_specs=pl.BlockSpec((RW_STEP, D_BRANCH), lambda b, n: (row(b, n), 0)),
        out_shape=jax.ShapeDtypeStruct((m, D_BRANCH), ACT),
        scratch_shapes=[
            pltpu.VMEM((N_HEADS // 2, 2 * HEAD, 2 * HEAD), F32),
            pltpu.VMEM((RW_TILE, D_BRANCH), F32),
        ] + prep_set * (RW_STEP // RW_TILE),
        compiler_params=pltpu.CompilerParams(
            dimension_semantics=("arbitrary", "arbitrary"), vmem_limit_bytes=_VMEM_LIMIT),
        name="rwkv",
    )(*([proj] * 8), vecs[0], wup, vecs[1], aup, *vecs[2:], bd, tril)


_SLOPES = tuple(2.0 ** (-8.0 * (h + 1) / N_HEADS) for h in range(N_HEADS))
_SQRT_HALF = float(np.sqrt(0.5))


def _attention_bias():
    t = np.arange(BLK)[:, None]
    s = np.arange(2 * BLK)[None, :]
    dist = t + BLK - s
    out = np.empty((2, N_HEADS * BLK, 2 * BLK), np.float32)
    for first in (0, 1):
        valid = (dist >= 0) & (dist < BLK) & ((s >= BLK) | (first == 1))
        for h in range(N_HEADS):
            out[first, h * BLK:(h + 1) * BLK] = np.where(valid, -_SLOPES[h] * dist, NEG_INF)
    return out


def _pool_bands():
    t = np.arange(BLK)[:, None] + BLK
    s = np.arange(2 * BLK)[None, :]
    return np.stack([((s <= t) & (s > t - w)).astype(np.float32) for w in POOL_WINDOWS])


def _mixers_kernel(sinks_ref, bias_ref, band_ref, q_ref, kvc_ref, kvp_ref, zc_ref, zcp_ref, zd_ref,
                   poolw_ref, pscale_ref, nw_ref, sw_ref, sb_ref, o_ref, y_scr, *, layer):
    n = pl.program_id(1)
    rows = ATT_GROUP * BLK
    head_in_group = lax.broadcasted_iota(jnp.int32, (rows, 1), 0) // BLK

    def per_row(values):
        col = jnp.full((rows, 1), values[-1], F32)
        for j in range(ATT_GROUP - 2, -1, -1):
            col = jnp.where(head_in_group == j, values[j], col)
        return col

    sinks = [per_row([sinks_ref[layer, h] for h in range(g * ATT_GROUP, (g + 1) * ATT_GROUP)])
             for g in range(KV_HEADS)]
    ri = lax.broadcasted_iota(jnp.int32, (BLK, BLK), 0)
    rj = lax.broadcasted_iota(jnp.int32, (BLK, BLK), 1)
    sgu_w16 = [jnp.where(ri >= rj, sw_ref[g], 0.0).astype(BF16) for g in range(4)]
    refs = (bias_ref, band_ref, q_ref, kvc_ref, kvp_ref, zc_ref, zcp_ref, zd_ref, poolw_ref,
            _LayerRow(pscale_ref, layer), _LayerRow(nw_ref, layer), sb_ref, y_scr)
    for sub in range(MIX_SUB):
        _mixers_block(sub, n, refs, sinks, sgu_w16)
    o_ref[...] = y_scr[...].astype(o_ref.dtype)


def _mixers_block(sub, n, refs, sinks, sgu_w16):
    (bias_ref, band_ref, q_ref, kvc_ref, kvp_ref, zc_ref, zcp_ref, zd_ref, poolw_ref,
     pscale_ref, nw_ref, sb_ref, y_scr) = refs
    rs = slice(sub * BLK, (sub + 1) * BLK)
    before = slice((sub - 1) * BLK, sub * BLK)
    rows = ATT_GROUP * BLK
    if sub == 0:
        kv_prev = kvp_ref[...]
        z_prev = jnp.where(n > 0, zcp_ref[...], 0.0)
        bias_of = lambda g: bias_ref[jnp.minimum(n, 1), g * rows:(g + 1) * rows, :]
    else:
        kv_prev = kvc_ref[before, :]
        z_prev = zc_ref[before, :]
        bias_of = lambda g: bias_ref[1, g * rows:(g + 1) * rows, :]

    q = q_ref[rs, :] * (HEAD ** -0.5)
    kv = jnp.concatenate([kv_prev, kvc_ref[rs, :]], axis=0)

    scores = []
    for g in range(KV_HEADS):
        qg = jnp.concatenate(
            [q[:, (g * ATT_GROUP + j) * HEAD:(g * ATT_GROUP + j + 1) * HEAD]
             for j in range(ATT_GROUP)], axis=0)
        scores.append(_bdot_nt(qg, kv[:, g * HEAD:(g + 1) * HEAD]))

    zfull = jnp.concatenate([z_prev, zc_ref[rs, :]], axis=0)
    pos = (n * MIX_SUB + sub) * BLK + lax.broadcasted_iota(jnp.int32, (BLK, 1), 0) + 1
    sums = [jnp.dot(band_ref[g], zfull[:, g * BLK:(g + 1) * BLK], preferred_element_type=F32)
            for g in range(len(POOL_WINDOWS))]

    zd = zd_ref[rs, :].astype(F32)
    gz = 0.5 * zd * (1.0 + lax.erf(zd * _SQRT_HALF))
    u = gz[:, :D_BRANCH]
    vv = gz[:, D_BRANCH:]
    mu = jnp.mean(vv, axis=-1, keepdims=True)
    dv = vv - mu
    var = jnp.mean(dv * dv, axis=-1, keepdims=True)
    vn = dv * lax.rsqrt(var + LN_EPS) * nw_ref[...]
    for g in range(4):
        gs = slice(g * BLK, (g + 1) * BLK)
        sg = _bdot(sgu_w16[g], vn[:, gs]) + sb_ref[:, gs]
        y_scr[rs, 2 * D_BRANCH + g * BLK:2 * D_BRANCH + (g + 1) * BLK] = u[:, gs] * sg

    for g, w in enumerate(POOL_WINDOWS):
        gs = slice(g * BLK, (g + 1) * BLK)
        cnt = jnp.minimum(pos, w).astype(F32)
        pooled = sums[g] / cnt - zc_ref[rs, gs].astype(F32)
        yg = _bdot(pooled, poolw_ref[g]) * pscale_ref[:, gs]
        y_scr[rs, D_BRANCH + g * BLK:D_BRANCH + (g + 1) * BLK] = yg

    probs, dens = [], []
    for g in range(KV_HEADS):
        s = scores[g] + bias_of(g)
        mx = jnp.maximum(jnp.max(s, axis=-1, keepdims=True), sinks[g])
        p = jnp.exp(s - mx)
        probs.append(p)
        dens.append(jnp.sum(p, axis=-1, keepdims=True) + jnp.exp(sinks[g] - mx))
    for g in range(KV_HEADS):
        vg = kv[:, (KV_HEADS + g) * HEAD:(KV_HEADS + g + 1) * HEAD]
        og = _bdot(probs[g], vg) / dens[g]
        for j in range(ATT_GROUP):
            h = g * ATT_GROUP + j
            y_scr[rs, h * HEAD:(h + 1) * HEAD] = og[j * BLK:(j + 1) * BLK, :]


def _mixers(proj, sinks, pool_w, pool_scale, norm_w, sgu_w, sgu_bias, layer, batch, seq):
    m = proj.shape[0]
    per_layer = lambda a: pl.BlockSpec((None,) + a.shape[1:],
                                       lambda b, n: (layer,) + (0,) * (a.ndim - 1))
    tm = MIX_SUB * BLK
    kv_w = 2 * KV_HEADS * HEAD
    ns = seq // tm
    row = lambda b, n: b * ns + n
    before = lambda b, n: jnp.maximum(row(b, n) * MIX_SUB - 1, 0)
    bias = jnp.asarray(_attention_bias())
    assert proj.dtype == BF16
    band = jnp.asarray(_pool_bands(), dtype=BF16)
    in_specs = [
        pl.BlockSpec(memory_space=pltpu.SMEM),
        pl.BlockSpec(bias.shape, lambda b, n: (0, 0, 0)),
        pl.BlockSpec(band.shape, lambda b, n: (0, 0, 0)),
        pl.BlockSpec((tm, D_BRANCH), lambda b, n: (row(b, n), P_Q // D_BRANCH)),
        pl.BlockSpec((tm, kv_w), lambda b, n: (row(b, n), P_KV // kv_w)),
        pl.BlockSpec((BLK, kv_w), lambda b, n: (before(b, n), P_KV // kv_w)),
        pl.BlockSpec((tm, D_BRANCH), lambda b, n: (row(b, n), P_POOL // D_BRANCH)),
        pl.BlockSpec((BLK, D_BRANCH), lambda b, n: (before(b, n), P_POOL // D_BRANCH)),
        pl.BlockSpec((tm, 2 * D_BRANCH), lambda b, n: (row(b, n), P_SGU // (2 * D_BRANCH))),
        per_layer(pool_w), pl.BlockSpec(pool_scale.shape, lambda b, n: (0, 0)),
        pl.BlockSpec(norm_w.shape, lambda b, n: (0, 0)), per_layer(sgu_w),
        per_layer(sgu_bias),
    ]
    return pl.pallas_call(
        functools.partial(_mixers_kernel, layer=layer),
        grid=(batch, ns),
        in_specs=in_specs,
        out_specs=pl.BlockSpec((tm, 3 * D_BRANCH), lambda b, n: (row(b, n), 0)),
        out_shape=jax.ShapeDtypeStruct((m, 3 * D_BRANCH), ACT),
        scratch_shapes=[pltpu.VMEM((tm, 3 * D_BRANCH), F32)],
        compiler_params=pltpu.CompilerParams(
            dimension_semantics=("parallel", "parallel"), vmem_limit_bytes=_VMEM_LIMIT),
        name="mixers",
    )(sinks, bias, band, proj, proj, proj, proj, proj, proj, pool_w, pool_scale, norm_w, sgu_w,
      sgu_bias)


OUT_TM = 512
OUT_SUB = 256
OUT_STAGE = 512


def _outproj_kernel(ya_ref, yb_ref, g_ref, x_ref, w_hbm, pw_ref, o_ref, w_scr, stage, sem, *,
                    layer):
    @pl.when(pl.program_id(0) == 0)
    def _():
        def copy(idx):
            slot = idx % 2
            return pltpu.make_async_copy(w_hbm.at[layer, pl.ds(idx * OUT_STAGE, OUT_STAGE), :],
                                         stage.at[slot], sem.at[slot])

        n_pieces = D_MODEL // OUT_STAGE
        copy(0).start()
        for idx in range(n_pieces):
            if idx + 1 < n_pieces:
                copy(idx + 1).start()
            copy(idx).wait()
            w_scr[idx * OUT_STAGE:(idx + 1) * OUT_STAGE, :] = stage[idx % 2].astype(BF16)

    for r0 in range(0, OUT_TM, OUT_SUB):
        rows = slice(r0, r0 + OUT_SUB)
        g = g_ref[rows, :].astype(F32)
        y = jnp.concatenate([ya_ref[rows, :], yb_ref[rows, :]], axis=1).astype(F32)
        gated = (y * (g * jax.nn.sigmoid(g))).astype(BF16)
        acc = jnp.dot(gated, w_scr[...], preferred_element_type=F32)
        ms = jnp.mean(acc * acc, axis=-1, keepdims=True)
        o_ref[rows, :] = (x_ref[rows, :]
                          + acc * lax.rsqrt(ms + NORM_EPS) * pw_ref[layer:layer + 1, :])


def _outproj(ya, ybcd, proj, x2d, w_out, layer, post_w):
    m = x2d.shape[0]
    return pl.pallas_call(
        functools.partial(_outproj_kernel, layer=layer),
        grid=(m // OUT_TM,),
        in_specs=[
            pl.BlockSpec((OUT_TM, D_BRANCH), lambda i: (i, 0)),
            pl.BlockSpec((OUT_TM, 3 * D_BRANCH), lambda i: (i, 0)),
            pl.BlockSpec((OUT_TM, D_MODEL), lambda i: (i, P_GATE // D_MODEL)),
            pl.BlockSpec((OUT_TM, D_MODEL), lambda i: (i, 0)),
            pl.BlockSpec(memory_space=pl.ANY),
            pl.BlockSpec((DEPTH, D_MODEL), lambda i: (0, 0)),
        ],
        out_specs=pl.BlockSpec((OUT_TM, D_MODEL), lambda i: (i, 0)),
        out_shape=jax.ShapeDtypeStruct((m, D_MODEL), F32),
        scratch_shapes=[
            pltpu.VMEM((D_MODEL, D_MODEL), BF16),
            pltpu.VMEM((2, OUT_STAGE, D_MODEL), F32),
            pltpu.SemaphoreType.DMA((2,)),
        ],
        compiler_params=pltpu.CompilerParams(
            dimension_semantics=("arbitrary",), vmem_limit_bytes=_VMEM_LIMIT),
        name="outproj",
    )(ya, ybcd, proj, x2d, w_out, post_w)


def kernel(x, pre_norm_w, post_norm_w, w_in, shift_mu, rwkv_w0, rwkv_w_up, rwkv_a0, rwkv_a_up,
           rwkv_k_k, rwkv_k_a, rwkv_r_k, rwkv_ln_w, rwkv_ln_b, attn_sinks, pool_w, pool_scale,
           sgu_norm_w, sgu_w, sgu_b, w_out):
    batch, seq, _ = x.shape
    assert x.shape == (batch, seq, D_MODEL) and seq % (MIX_SUB * BLK) == 0 and seq % RW_STEP == 0
    m = batch * seq
    head_id = np.arange(D_BRANCH // 2) // HEAD
    bd = jnp.asarray((head_id[:, None] == head_id[None, :]).astype(np.float32), dtype=BF16)
    t_id = np.arange(RW_TILE)
    tril = jnp.asarray(((t_id[:, None] >= t_id[None, :])
                        & (t_id[:, None] // RW_CHUNK == t_id[None, :] // RW_CHUNK)).astype(np.float32),
                       dtype=BF16)
    rows = lambda a: a.reshape(DEPTH, -1)
    zeros = lambda width: jnp.zeros((DEPTH, width), shift_mu.dtype)
    mu_p = rows(jnp.concatenate([shift_mu[:, :3 * D_BRANCH], zeros(P_LORA - P_KV),
                                 shift_mu[:, 3 * D_BRANCH:], zeros(P_WIDTH - D_IN)], axis=1))
    sgu_bias = jnp.broadcast_to(jnp.swapaxes(sgu_b, 1, 2)[:, :, :, None],
                                (DEPTH, BLK, 4, BLK)).reshape(DEPTH, BLK, D_BRANCH)
    rwkv_vecs = [rows(a) for a in (rwkv_w0, rwkv_a0, rwkv_k_k, rwkv_k_a, rwkv_r_k, rwkv_ln_w,
                                   rwkv_ln_b)]
    pre_w, post_w = rows(pre_norm_w), rows(post_norm_w)
    pool_scale3, sgu_norm3 = rows(pool_scale), rows(sgu_norm_w)

    h = x.reshape(m, D_MODEL)
    for l in range(DEPTH):
        proj = _inproj(h, pre_w, mu_p, w_in, l, seq)
        ya = _rwkv(proj, rwkv_vecs, rwkv_w_up, rwkv_a_up, bd, tril, l, batch, seq)
        ybcd = _mixers(proj, attn_sinks, pool_w, pool_scale3, sgu_norm3, sgu_w, sgu_bias, l,
                       batch, seq)
        h = _outproj(ya, ybcd, proj, h, w_out, l, post_w)
    return h.reshape(batch, seq, D_MODEL)
```
